```python
import math
import jax
import jax.numpy as jnp
from jax import lax
import numpy as np

D_MODEL = 1024
BATCH = 8
SEQ = 2048
DEPTH = 2

HEAD_DIM = 64
N_HEADS = D_MODEL // HEAD_DIM
MOBA_HEADS = N_HEADS // 2
DIL_HEADS = N_HEADS - MOBA_HEADS
MOBA_BLOCK = 256
MOBA_TOPK = 3
MOBA_QCHUNK = 16
DIL_PAIRS = ((128, 1), (512, 4), (2048, 16))
DIL_BLOCK = 128
NSA_HEADS = N_HEADS
NSA_GROUPS = 4
NSA_HPG = NSA_HEADS // NSA_GROUPS
NSA_CMP_LEN = 32
NSA_CMP_STRIDE = 16
NSA_CMP_HIDDEN = 2 * HEAD_DIM
NSA_SEL_BLOCK = 64
NSA_SEL_TOPN = 16
NSA_WINDOW = 512
NSA_WIN_BLOCK = 128
NSA_QCHUNK = 32
NSA_FORCE = 1.0e6
REL_BUCKETS = 32
REL_MAX_DIST = 2048
FFN_HIDDEN = -((-8 * D_MODEL) // (3 * 256)) * 256
PLE_DIM = 256
N_EVEN = (DEPTH + 1) // 2
N_ODD = DEPTH // 2
RMS_EPS = 1e-6
ATTN_SCALE = HEAD_DIM ** -0.5
F32 = jnp.float32

kernel_name = 'hybrid_moba_dilated_nsa_block'


def _rms(x, g):
    xf = x.astype(F32)
    y = xf * lax.rsqrt(jnp.mean(xf * xf, axis=-1, keepdims=True) + RMS_EPS)
    return (y * g.astype(F32)).astype(x.dtype)


def _heads(t, n):
    B, S, _ = t.shape
    return t.reshape(B, S, n, HEAD_DIM).transpose(0, 2, 1, 3)


def _rel_bucket(dist):
    n = jnp.maximum(dist, 0)
    exact = REL_BUCKETS // 2
    nf = jnp.maximum(n, 1).astype(F32)
    large = exact + (jnp.log(nf / exact) / math.log(REL_MAX_DIST / exact) * (REL_BUCKETS - exact)).astype(jnp.int32)
    return jnp.where(n < exact, n, jnp.minimum(large, REL_BUCKETS - 1))


def _masked_softmax(s, mask):
    s = jnp.where(mask, s.astype(F32), -jnp.inf)
    m = jnp.max(s, axis=-1, keepdims=True)
    m = jnp.where(jnp.isfinite(m), m, 0.0)
    e = jnp.exp(s - m)
    l = jnp.sum(e, axis=-1, keepdims=True)
    p = e / jnp.where(l > 0, l, 1.0)
    return p, (m + jnp.log(l))[..., 0]


def _moba(q, k, v, tab):
    B, H, S, hd = q.shape
    nb = -(-S // MOBA_BLOCK)
    padw = ((0, 0), (0, 0), (0, nb * MOBA_BLOCK - S), (0, 0))
    kb = jnp.pad(k, padw).reshape(B, H, nb, MOBA_BLOCK, hd)
    vb = jnp.pad(v, padw).reshape(B, H, nb, MOBA_BLOCK, hd)
    kmean = jnp.mean(kb.astype(F32), axis=3)
    own = jnp.arange(S) // MOBA_BLOCK
    gate = jnp.einsum('bhsd,bhnd->bhsn', q.astype(F32), kmean)
    gate = jnp.where(jnp.arange(nb)[None, :] < own[:, None], gate, -jnp.inf)
    n_sel = min(MOBA_TOPK, nb)
    _, sel = lax.top_k(gate, n_sel)
    sel_ok = sel < own[:, None]
    nc = S // MOBA_QCHUNK

    def to_chunks(t):
        return jnp.moveaxis(t.reshape(B, H, nc, MOBA_QCHUNK, *t.shape[3:]), 2, 0)

    b_ix = jnp.arange(B)[:, None, None, None]
    h_ix = jnp.arange(H)[None, :, None, None]
    a_blk = jnp.arange(MOBA_BLOCK)

    def chunk(args):
        c, qc, sc, okc = args
        t = c * MOBA_QCHUNK + jnp.arange(MOBA_QCHUNK)
        j = (c * MOBA_QCHUNK) // MOBA_BLOCK
        k_own = lax.dynamic_index_in_dim(kb, j, axis=2, keepdims=False)
        v_own = lax.dynamic_index_in_dim(vb, j, axis=2, keepdims=False)
        d_own = t[:, None] - (j * MOBA_BLOCK + a_blk)[None, :]
        s_own = jnp.einsum('bhqd,bhkd->bhqk', qc, k_own).astype(F32) * ATTN_SCALE + tab[:, _rel_bucket(d_own)].astype(F32)
        m_own = jnp.broadcast_to(d_own >= 0, s_own.shape)
        k_sel = kb[b_ix, h_ix, sc].reshape(B, H, MOBA_QCHUNK, n_sel * MOBA_BLOCK, hd)
        v_sel = vb[b_ix, h_ix, sc].reshape(B, H, MOBA_QCHUNK, n_sel * MOBA_BLOCK, hd)
        kpos = (sc[..., None] * MOBA_BLOCK + a_blk).reshape(B, H, MOBA_QCHUNK, n_sel * MOBA_BLOCK)
        s_sel = jnp.einsum('bhqd,bhqkd->bhqk', qc, k_sel).astype(F32) * ATTN_SCALE + tab[h_ix, _rel_bucket(t[:, None] - kpos)].astype(F32)
        m_sel = jnp.repeat(okc, MOBA_BLOCK, axis=-1)
        pr, _ = _masked_softmax(jnp.concatenate([s_own, s_sel], axis=-1), jnp.concatenate([m_own, m_sel], axis=-1))
        p_own = pr[..., :MOBA_BLOCK].astype(v.dtype)
        p_sel = pr[..., MOBA_BLOCK:].astype(v.dtype)
        return jnp.einsum('bhqk,bhkd->bhqd', p_own, v_own) + jnp.einsum('bhqk,bhqkd->bhqd', p_sel, v_sel)

    out = lax.map(chunk, (jnp.arange(nc), to_chunks(q), to_chunks(sel), to_chunks(sel_ok)))
    return jnp.moveaxis(out, 0, 2).reshape(B, H, S, hd)


def _stride_split(t, dil, L):
    B, H, S, hd = t.shape
    t = jnp.pad(t, ((0, 0), (0, 0), (0, L * dil - S), (0, 0)))
    return t.reshape(B, H, L, dil, hd).transpose(0, 1, 3, 2, 4)


def _band(t):
    prev = jnp.pad(t, ((0, 0), (0, 0), (0, 0), (1, 0), (0, 0), (0, 0)))[:, :, :, :-1]
    return jnp.concatenate([prev, t], axis=4)


def _dilated(q, k, v, tab):
    B, H, S, hd = q.shape
    blk = DIL_BLOCK
    outs, lses = [], []
    for window, dil in DIL_PAIRS:
        steps = window // dil
        L = -(-S // (dil * blk)) * blk
        nbk = L // blk
        qs = _stride_split(q, dil, L).reshape(B, H, dil, nbk, blk, hd)
        ks = _stride_split(k, dil, L).reshape(B, H, dil, nbk, blk, hd)
        vs = _stride_split(v, dil, L).reshape(B, H, dil, nbk, blk, hd)
        step = jnp.arange(blk)[:, None] + blk - jnp.arange(2 * blk)[None, :]
        key_ok = (jnp.arange(nbk)[:, None] * blk - blk + jnp.arange(2 * blk)[None, :]) >= 0
        mask = ((step >= 0) & (step <= steps))[None] & key_ok[:, None, :]
        bias = tab[:, _rel_bucket(step * dil)].astype(F32)
        s = jnp.einsum('bhrjqd,bhrjkd->bhrjqk', qs, _band(ks)).astype(F32) * ATTN_SCALE + bias[None, :, None, None]
        pr, lse = _masked_softmax(s, mask)
        o = jnp.einsum('bhrjqk,bhrjkd->bhrjqd', pr.astype(v.dtype), _band(vs))
        outs.append(o.reshape(B, H, dil, L, hd).transpose(0, 1, 3, 2, 4).reshape(B, H, L * dil, hd)[:, :, :S])
        lses.append(lse.reshape(B, H, dil, L).transpose(0, 1, 3, 2).reshape(B, H, L * dil)[:, :, :S])
    w = jax.nn.softmax(jnp.stack(lses), axis=0)
    o = jnp.einsum('nbhs,nbhsd->bhsd', w, jnp.stack(outs).astype(F32))
    return o.astype(q.dtype)


def _compress(t, pos_emb, w1, b1, w2, b2):
    B, G, S, hd = t.shape
    nc = (S - NSA_CMP_LEN) // NSA_CMP_STRIDE + 1
    idx = jnp.arange(nc)[:, None] * NSA_CMP_STRIDE + jnp.arange(NSA_CMP_LEN)[None, :]
    blocks = (t[:, :, idx] + pos_emb).reshape(B, G, nc, NSA_CMP_LEN * hd)
    return jax.nn.gelu(blocks @ w1 + b1) @ w2 + b2


def _nsa(q, kc, vc, ks, vs, kw, vw, gates, tab):
    B, H, S, hd = q.shape
    G, hpg = NSA_GROUPS, NSA_HPG
    qg = q.reshape(B, G, hpg, S, hd)
    tabg = tab.reshape(G, hpg, REL_BUCKETS)
    t = jnp.arange(S)
    nc = kc.shape[2]
    dc = t[:, None] - (jnp.arange(nc) * NSA_CMP_STRIDE + NSA_CMP_LEN - 1)[None, :]
    s = jnp.einsum('bghsd,bgcd->bghsc', qg, kc).astype(F32) * ATTN_SCALE + tabg[:, :, _rel_bucket(dc)].astype(F32)
    p_cmp, _ = _masked_softmax(s, dc >= 0)
    o_cmp = jnp.einsum('bghsc,bgcd->bghsd', p_cmp.astype(vc.dtype), vc)
    n_sel = -(-S // NSA_SEL_BLOCK)
    cstart = jnp.arange(nc) * NSA_CMP_STRIDE
    sstart = jnp.arange(n_sel) * NSA_SEL_BLOCK
    overlap = jnp.maximum(jnp.minimum(cstart[:, None] + NSA_CMP_LEN, sstart[None, :] + NSA_SEL_BLOCK) - jnp.maximum(cstart[:, None], sstart[None, :]), 0).astype(F32)
    imp = jnp.einsum('bghsc,cn->bgsn', p_cmp, overlap)
    cur = (t // NSA_SEL_BLOCK)[:, None]
    blk_id = jnp.arange(n_sel)[None, :]
    forced = (blk_id == 0) | (blk_id == cur) | (blk_id == cur - 1)
    imp = jnp.where(blk_id <= cur, imp + jnp.where(forced, NSA_FORCE, 0.0), -jnp.inf)
    n_top = min(NSA_SEL_TOPN, n_sel)
    _, sel = lax.top_k(imp, n_top)
    padw = ((0, 0), (0, 0), (0, n_sel * NSA_SEL_BLOCK - S), (0, 0))
    ksb = jnp.pad(ks, padw).reshape(B, G, n_sel, NSA_SEL_BLOCK, hd)
    vsb = jnp.pad(vs, padw).reshape(B, G, n_sel, NSA_SEL_BLOCK, hd)
    nq = S // NSA_QCHUNK
    b_ix = jnp.arange(B)[:, None, None, None]
    g_ix = jnp.arange(G)[None, :, None, None]
    h_ix = jnp.arange(hpg)[None, None, :, None, None]
    a_blk = jnp.arange(NSA_SEL_BLOCK)

    def sel_chunk(args):
        c, qc, sc = args
        tq = c * NSA_QCHUNK + jnp.arange(NSA_QCHUNK)
        kk = ksb[b_ix, g_ix, sc].reshape(B, G, NSA_QCHUNK, n_top * NSA_SEL_BLOCK, hd)
        vv = vsb[b_ix, g_ix, sc].reshape(B, G, NSA_QCHUNK, n_top * NSA_SEL_BLOCK, hd)
        kpos = (sc[..., None] * NSA_SEL_BLOCK + a_blk).reshape(B, G, NSA_QCHUNK, n_top * NSA_SEL_BLOCK)
        dist = tq[:, None] - kpos
        bias = tabg[g_ix[..., None], h_ix, _rel_bucket(dist)[:, :, None]].astype(F32)
        s = jnp.einsum('bghqd,bgqkd->bghqk', qc, kk).astype(F32) * ATTN_SCALE + bias
        pr, _ = _masked_softmax(s, (dist >= 0)[:, :, None])
        return jnp.einsum('bghqk,bgqkd->bghqd', pr.astype(vv.dtype), vv)

    qch = jnp.moveaxis(qg.reshape(B, G, hpg, nq, NSA_QCHUNK, hd), 3, 0)
    sch = jnp.moveaxis(sel.reshape(B, G, nq, NSA_QCHUNK, n_top), 2, 0)
    o_slc = jnp.moveaxis(lax.map(sel_chunk, (jnp.arange(nq), qch, sch)), 0, 3).reshape(B, G, hpg, S, hd)
    nw = S // NSA_WIN_BLOCK
    span = NSA_WINDOW + NSA_WIN_BLOCK
    kwp = jnp.pad(kw, ((0, 0), (0, 0), (NSA_WINDOW, 0), (0, 0)))
    vwp = jnp.pad(vw, ((0, 0), (0, 0), (NSA_WINDOW, 0), (0, 0)))
    kidx = jnp.arange(span)[None, :]
    dw = jnp.arange(NSA_WIN_BLOCK)[:, None] + NSA_WINDOW - kidx
    bias_w = tabg[:, :, _rel_bucket(dw)].astype(F32)

    def win_block(args):
        j, qb = args
        kb_ = lax.dynamic_slice_in_dim(kwp, j * NSA_WIN_BLOCK, span, axis=2)
        vb_ = lax.dynamic_slice_in_dim(vwp, j * NSA_WIN_BLOCK, span, axis=2)
        ok = (dw >= 0) & (dw < NSA_WINDOW) & (j * NSA_WIN_BLOCK - NSA_WINDOW + kidx >= 0)
        s = jnp.einsum('bghqd,bgkd->bghqk', qb, kb_).astype(F32) * ATTN_SCALE + bias_w
        pr, _ = _masked_softmax(s, ok)
        return jnp.einsum('bghqk,bgkd->bghqd', pr.astype(vb_.dtype), vb_)

    qwb = jnp.moveaxis(qg.reshape(B, G, hpg, nw, NSA_WIN_BLOCK, hd), 3, 0)
    o_win = jnp.moveaxis(lax.map(win_block, (jnp.arange(nw), qwb)), 0, 3).reshape(B, G, hpg, S, hd)
    o = (gates[..., 0:1] * o_cmp.reshape(B, H, S, hd) + gates[..., 1:2] * o_slc.reshape(B, H, S, hd)
         + gates[..., 2:3] * o_win.reshape(B, H, S, hd))
    return o.astype(q.dtype)


def _mixer_ab(xn, w_in, w_out, qn_a, kn_a, qn_b, kn_b, rel_bias):
    B, S, _ = xn.shape
    wa = MOBA_HEADS * HEAD_DIM
    wb = DIL_HEADS * HEAD_DIM
    qa, ka, va, qb, kb, vb = jnp.split(xn @ w_in, [wa, 2 * wa, 3 * wa, 3 * wa + wb, 3 * wa + 2 * wb], axis=-1)
    oa = _moba(_rms(_heads(qa, MOBA_HEADS), qn_a), _rms(_heads(ka, MOBA_HEADS), kn_a), _heads(va, MOBA_HEADS), rel_bias[:, :MOBA_HEADS].T)
    ob = _dilated(_rms(_heads(qb, DIL_HEADS), qn_b), _rms(_heads(kb, DIL_HEADS), kn_b), _heads(vb, DIL_HEADS), rel_bias[:, MOBA_HEADS:].T)
    o = jnp.concatenate([oa, ob], axis=1).transpose(0, 2, 1, 3).reshape(B, S, D_MODEL)
    return o @ w_out


def _mixer_nsa(xn, w_in, w_out, qn, kn_c, kn_s, kn_w, cmp_k, cmp_v, rel_bias):
    B, S, _ = xn.shape
    qw = NSA_HEADS * HEAD_DIM
    kvw = NSA_GROUPS * HEAD_DIM
    q, kc, vc, ks, vs, kw, vw, g = jnp.split(xn @ w_in, [qw + i * kvw for i in range(7)], axis=-1)
    q = _rms(_heads(q, NSA_HEADS), qn)
    kc = _rms(_compress(_heads(kc, NSA_GROUPS), *cmp_k), kn_c)
    vc = _compress(_heads(vc, NSA_GROUPS), *cmp_v)
    ks = _rms(_heads(ks, NSA_GROUPS), kn_s)
    kw = _rms(_heads(kw, NSA_GROUPS), kn_w)
    gates = jax.nn.sigmoid(g.astype(F32)).reshape(B, S, NSA_HEADS, 3).transpose(0, 2, 1, 3)
    o = _nsa(q, kc, vc, ks, _heads(vs, NSA_GROUPS), kw, _heads(vw, NSA_GROUPS), gates, rel_bias.T)
    return o.transpose(0, 2, 1, 3).reshape(B, S, D_MODEL) @ w_out


def _swiglu(x, wg, wu, wd):
    return (jax.nn.silu(x @ wg) * (x @ wu)) @ wd


def setup_inputs(seed: int = 0) -> dict:
    key = jax.random.key(seed)
    keys = iter(jax.random.split(key, 40))

    def nrm(shape, scale):
        return jax.random.normal(next(keys), shape, F32) * scale

    def gain(shape):
        return 1.0 + nrm(shape, 0.02)

    D, hd, F, CH = D_MODEL, HEAD_DIM, FFN_HIDDEN, NSA_CMP_HIDDEN
    res = (2 * DEPTH) ** -0.5
    w_ab = 3 * (MOBA_HEADS + DIL_HEADS) * hd
    w_nsa = NSA_HEADS * hd + 6 * NSA_GROUPS * hd + 3 * NSA_HEADS
    cin = NSA_CMP_LEN * hd
    return {
        'x': nrm((BATCH, SEQ, D), 1.0),
        'p': nrm((DEPTH, BATCH, SEQ, PLE_DIM), 1.0),
        'rel_bias': nrm((REL_BUCKETS, N_HEADS), 0.5),
        'norm_mix': gain((DEPTH, D)),
        'norm_ffn': gain((DEPTH, D)),
        'norm_ple': gain((DEPTH, D)),
        'w_ffn_gate': nrm((DEPTH, D, F), D ** -0.5),
        'w_ffn_up': nrm((DEPTH, D, F), D ** -0.5),
        'w_ffn_down': nrm((DEPTH, F, D), F ** -0.5 * res),
        'w_ple_proj': nrm((DEPTH, PLE_DIM, D), PLE_DIM ** -0.5 * res),
        'w_ple_gate': nrm((DEPTH, D, D), D ** -0.5),
        'w_in_ab': nrm((N_EVEN, D, w_ab), D ** -0.5),
        'w_out_ab': nrm((N_EVEN, D, D), D ** -0.5 * res),
        'qn_moba': gain((N_EVEN, hd)),
        'kn_moba': gain((N_EVEN, hd)),
        'qn_dil': gain((N_EVEN, hd)),
        'kn_dil': gain((N_EVEN, hd)),
        'w_in_nsa': nrm((N_ODD, D, w_nsa), D ** -0.5),
        'w_out_nsa': nrm((N_ODD, D, D), D ** -0.5 * res),
        'qn_nsa': gain((N_ODD, hd)),
        'kn_cmp': gain((N_ODD, hd)),
        'kn_slc': gain((N_ODD, hd)),
        'kn_win': gain((N_ODD, hd)),
        'cmp_k_pos': nrm((N_ODD, NSA_CMP_LEN, hd), 0.1),
        'cmp_k_w1': nrm((N_ODD, cin, CH), cin ** -0.5),
        'cmp_k_b1': nrm((N_ODD, CH), 0.01),
        'cmp_k_w2': nrm((N_ODD, CH, hd), CH ** -0.5),
        'cmp_k_b2': nrm((N_ODD, hd), 0.01),
        'cmp_v_pos': nrm((N_ODD, NSA_CMP_LEN, hd), 0.1),
        'cmp_v_w1': nrm((N_ODD, cin, CH), cin ** -0.5),
        'cmp_v_b1': nrm((N_ODD, CH), 0.01),
        'cmp_v_w2': nrm((N_ODD, CH, hd), CH ** -0.5),
        'cmp_v_b2': nrm((N_ODD, hd), 0.01),
    }


def reference(x, p, rel_bias, norm_mix, norm_ffn, norm_ple, w_ffn_gate, w_ffn_up, w_ffn_down,
              w_ple_proj, w_ple_gate, w_in_ab, w_out_ab, qn_moba, kn_moba, qn_dil, kn_dil,
              w_in_nsa, w_out_nsa, qn_nsa, kn_cmp, kn_slc, kn_win,
              cmp_k_pos, cmp_k_w1, cmp_k_b1, cmp_k_w2, cmp_k_b2,
              cmp_v_pos, cmp_v_w1, cmp_v_b1, cmp_v_w2, cmp_v_b2):
    h = x
    for i in range(DEPTH):
        e = i // 2
        xn = _rms(h, norm_mix[i])
        if i % 2 == 0:
            mix = _mixer_ab(xn, w_in_ab[e], w_out_ab[e], qn_moba[e], kn_moba[e], qn_dil[e], kn_dil[e], rel_bias)
        else:
            cmp_k = (cmp_k_pos[e], cmp_k_w1[e], cmp_k_b1[e], cmp_k_w2[e], cmp_k_b2[e])
            cmp_v = (cmp_v_pos[e], cmp_v_w1[e], cmp_v_b1[e], cmp_v_w2[e], cmp_v_b2[e])
            mix = _mixer_nsa(xn, w_in_nsa[e], w_out_nsa[e], qn_nsa[e], kn_cmp[e], kn_slc[e], kn_win[e], cmp_k, cmp_v, rel_bias)
        h = h + mix.astype(h.dtype)
        h = h + _swiglu(_rms(h, norm_ffn[i]), w_ffn_gate[i], w_ffn_up[i], w_ffn_down[i])
        gate = jax.nn.sigmoid(_rms(h, norm_ple[i]) @ w_ple_gate[i])
        h = h + gate * (p[i] @ w_ple_proj[i])
    return h
```

```python
import functools
import math

import numpy as np
import jax
import jax.numpy as jnp
from jax import lax
from jax.experimental import pallas as pl
from jax.experimental.pallas import tpu as pltpu

F32 = jnp.float32
BF16 = jnp.bfloat16

D_MODEL = 1024
HEAD_DIM = 64
N_HEADS = 16
N_PAIRS = N_HEADS // 2
MOBA_HEADS = 8
MOBA_BLOCK = 256
MOBA_TOPK = 3
MOBA_NBLK = 8
NSA_GROUPS = 4
NSA_CMP_LEN = 32
NSA_CMP_STRIDE = 16
NSA_SEL_BLOCK = 64
NSA_SEL_TOPN = 16
NSA_WINDOW = 512
NSA_FORCE = 1.0e6
REL_BUCKETS = 32
REL_MAX_DIST = 2048
RMS_EPS = 1e-6
ATTN_SCALE = HEAD_DIM ** -0.5

LANES = 128
TQ = 256
NEG = -1.0e30
ROW_TILE = 512
VMEM_LIMIT = 48 * 1024 * 1024

NT_DIMS = (((1,), (1,)), ((), ()))


def _cparams(n_axes):
    return pltpu.CompilerParams(dimension_semantics=("arbitrary",) * n_axes,
                                vmem_limit_bytes=VMEM_LIMIT)


def _rms_rows(x, g):
    ms = jnp.mean(x * x, axis=-1, keepdims=True)
    return x * lax.rsqrt(ms + RMS_EPS) * g


def _lane_lo(shape):
    return lax.broadcasted_iota(jnp.int32, shape, len(shape) - 1) < HEAD_DIM


def _proj_kernel(x_ref, g_ref, w_ref, cg_ref, cm_ref, bd_ref, o_ref, xn_ref, *, head_norm):
    @pl.when(pl.program_id(1) == 0)
    def _():
        xn_ref[...] = _rms_rows(x_ref[...], g_ref[...]).astype(BF16)

    y = jnp.dot(xn_ref[...], w_ref[...], preferred_element_type=F32)
    if head_norm:
        y2 = y * y
        hi = y2.astype(BF16)
        lo = (y2 - hi.astype(F32)).astype(BF16)
        bd = bd_ref[...]
        ssq = jnp.dot(hi, bd, preferred_element_type=F32) + jnp.dot(lo, bd, preferred_element_type=F32)
        yn = y * lax.rsqrt(ssq * (1.0 / HEAD_DIM) + RMS_EPS) * cg_ref[...]
        y = jnp.where(cm_ref[...] > 0.0, yn, y)
    o_ref[...] = y.astype(o_ref.dtype)


def _project(x, g, w, col_gain, col_mode, out_dtype, tn, head_norm):
    T, D = x.shape
    N = w.shape[1]
    blk = np.kron(np.eye(tn // HEAD_DIM), np.ones((HEAD_DIM, HEAD_DIM))).astype(np.float32)
    return pl.pallas_call(
        functools.partial(_proj_kernel, head_norm=head_norm),
        grid=(T // ROW_TILE, N // tn),
        in_specs=[
            pl.BlockSpec((ROW_TILE, D), lambda i, j: (i, 0)),
            pl.BlockSpec((1, D), lambda i, j: (0, 0)),
            pl.BlockSpec((D, tn), lambda i, j: (0, j)),
            pl.BlockSpec((1, tn), lambda i, j: (0, j)),
            pl.BlockSpec((1, tn), lambda i, j: (0, j)),
            pl.BlockSpec((tn, tn), lambda i, j: (0, 0)),
        ],
        out_specs=pl.BlockSpec((ROW_TILE, tn), lambda i, j: (i, j)),
        out_shape=jax.ShapeDtypeStruct((T, N), out_dtype),
        scratch_shapes=[pltpu.VMEM((ROW_TILE, D), BF16)],
        compiler_params=_cparams(2),
        name="proj",
    )(x, g.reshape(1, D), w, col_gain.reshape(1, N), col_mode.reshape(1, N), jnp.asarray(blk, BF16))


def _outproj_kernel(a_ref, w_ref, r_ref, o_ref):
    o_ref[...] = r_ref[...] + jnp.dot(a_ref[...], w_ref[...], preferred_element_type=F32)


def _outproj(a, w, res, tn=512):
    T, K = a.shape
    N = w.shape[1]
    return pl.pallas_call(
        _outproj_kernel,
        grid=(T // ROW_TILE, N // tn),
        in_specs=[
            pl.BlockSpec((ROW_TILE, K), lambda i, j: (i, 0)),
            pl.BlockSpec((K, tn), lambda i, j: (0, j)),
            pl.BlockSpec((ROW_TILE, tn), lambda i, j: (i, j)),
        ],
        out_specs=pl.BlockSpec((ROW_TILE, tn), lambda i, j: (i, j)),
        out_shape=jax.ShapeDtypeStruct((T, N), F32),
        compiler_params=_cparams(2),
        name="outproj",
    )(a, w, res)


def _ffn_kernel(h_ref, g_ref, wg_ref, wu_ref, wd_ref, o_ref, xn_ref, acc_ref):
    j = pl.program_id(1)

    @pl.when(j == 0)
    def _():
        xn_ref[...] = _rms_rows(h_ref[...], g_ref[...]).astype(BF16)
        acc_ref[...] = jnp.zeros_like(acc_ref)

    xn = xn_ref[...]
    a = jnp.dot(xn, wg_ref[...], preferred_element_type=F32)
    u = jnp.dot(xn, wu_ref[...], preferred_element_type=F32)
    act = (a * (1.0 / (1.0 + jnp.exp(-a))) * u).astype(BF16)
    acc_ref[...] += jnp.dot(act, wd_ref[...], preferred_element_type=F32)

    @pl.when(j == pl.num_programs(1) - 1)
    def _():
        o_ref[...] = h_ref[...] + acc_ref[...]


def _ffn(h, g, wg, wu, wd, tf=256):
    T, D = h.shape
    Fh = wg.shape[1]
    return pl.pallas_call(
        _ffn_kernel,
        grid=(T // ROW_TILE, Fh // tf),
        in_specs=[
            pl.BlockSpec((ROW_TILE, D), lambda i, j: (i, 0)),
            pl.BlockSpec((1, D), lambda i, j: (0, 0)),
            pl.BlockSpec((D, tf), lambda i, j: (0, j)),
            pl.BlockSpec((D, tf), lambda i, j: (0, j)),
            pl.BlockSpec((tf, D), lambda i, j: (j, 0)),
        ],
        out_specs=pl.BlockSpec((ROW_TILE, D), lambda i, j: (i, 0)),
        out_shape=jax.ShapeDtypeStruct((T, D), F32),
        scratch_shapes=[pltpu.VMEM((ROW_TILE, D), BF16), pltpu.VMEM((ROW_TILE, D), F32)],
        compiler_params=_cparams(2),
        name="ffn",
    )(h, g.reshape(1, D), wg, wu, wd)


def _ple_kernel(h_ref, hcol_ref, g_ref, wgate_ref, p_ref, wproj_ref, o_ref, hn_ref):
    @pl.when(pl.program_id(1) == 0)
    def _():
        hn_ref[...] = _rms_rows(h_ref[...], g_ref[...]).astype(BF16)

    z = jnp.dot(hn_ref[...], wgate_ref[...], preferred_element_type=F32)
    gate = 1.0 / (1.0 + jnp.exp(-z))
    proj = jnp.dot(p_ref[...].astype(BF16), wproj_ref[...], preferred_element_type=F32)
    o_ref[...] = hcol_ref[...] + gate * proj


def _ple(h, g, wgate, p, wproj, tn=512):
    T, D = h.shape
    Pd = p.shape[1]
    return pl.pallas_call(
        _ple_kernel,
        grid=(T // ROW_TILE, D // tn),
        in_specs=[
            pl.BlockSpec((ROW_TILE, D), lambda i, j: (i, 0)),
            pl.BlockSpec((ROW_TILE, tn), lambda i, j: (i, j)),
            pl.BlockSpec((1, D), lambda i, j: (0, 0)),
            pl.BlockSpec((D, tn), lambda i, j: (0, j)),
            pl.BlockSpec((ROW_TILE, Pd), lambda i, j: (i, 0)),
            pl.BlockSpec((Pd, tn), lambda i, j: (0, j)),
        ],
        out_specs=pl.BlockSpec((ROW_TILE, tn), lambda i, j: (i, j)),
        out_shape=jax.ShapeDtypeStruct((T, D), F32),
        scratch_shapes=[pltpu.VMEM((ROW_TILE, D), BF16)],
        compiler_params=_cparams(2),
        name="ple",
    )(h, h, g.reshape(1, D), wgate, p, wproj)


def _branch(q_aug, k_ref, v_ref, bias_ref, e_ref, i, j_lo, m_ref, l_ref, acc_ref):
    def tile(j, off, first):
        rows = pl.ds(pl.multiple_of(j * TQ, TQ), TQ)
        kj = k_ref[0, rows, :]
        vj = v_ref[0, rows, :]
        for h in range(2):
            ka = kj if e_ref is None else jnp.concatenate([kj, e_ref[h, rows, :]], axis=1)
            s = lax.dot_general(q_aug[h], ka, NT_DIMS, preferred_element_type=F32) + bias_ref[0, h, off]
            smax = jnp.max(s, axis=1, keepdims=True)
            if first:
                m_new = smax
                p = jnp.exp(s - m_new)
                l_ref[h] = jnp.sum(p, axis=1, keepdims=True)
                acc_ref[h] = jnp.dot(p.astype(BF16), vj, preferred_element_type=F32)
            else:
                m_old = m_ref[h]
                m_new = jnp.maximum(m_old, smax)
                alpha = jnp.exp(m_old - m_new)
                p = jnp.exp(s - m_new)
                l_ref[h] = alpha * l_ref[h] + jnp.sum(p, axis=1, keepdims=True)
                acc_ref[h] = alpha * acc_ref[h] + jnp.dot(p.astype(BF16), vj, preferred_element_type=F32)
            m_ref[h] = m_new

    tile(i, 0, True)

    def body(j, carry):
        tile(j, i - j, False)
        return carry

    lax.fori_loop(j_lo, i, body, 0)
    lo = _lane_lo((TQ, LANES))
    return jnp.where(lo, acc_ref[0] / l_ref[0], acc_ref[1] / l_ref[1])


def _split_heads(q):
    lo = _lane_lo(q.shape)
    zero = jnp.zeros_like(q)
    return jnp.where(lo, q, zero), jnp.where(lo, zero, q)


def _moba_penalty(q, kmt_ref, i):
    nb = MOBA_NBLK
    g = lax.dot_general(kmt_ref[0:2 * nb, :], q.astype(F32), NT_DIMS,
                        precision=lax.Precision.HIGHEST, preferred_element_type=F32)
    row = lax.broadcasted_iota(jnp.int32, g.shape, 0)
    n = row & (nb - 1)
    rank = jnp.zeros(g.shape, F32)
    for m in range(nb):
        gm = jnp.where(row < nb, g[m:m + 1, :], g[nb + m:nb + m + 1, :])
        tie = jnp.where(n > m, 1.0, 0.0)
        beats = jnp.where(gm > g, 1.0, jnp.where(gm == g, tie, 0.0))
        rank = rank + jnp.where(m < i, beats, 0.0)
    keep = jnp.where(n < i, jnp.where(rank < MOBA_TOPK, 1.0, 0.0), jnp.where(n == i, 1.0, 0.0))
    keep = jnp.concatenate([keep, jnp.zeros((LANES - 2 * nb, TQ), F32)], axis=0).T
    lane = lax.broadcasted_iota(jnp.int32, keep.shape, 1)
    return jnp.where(lane < 2 * nb, (keep - 1.0) * (-NEG), 0.0)


def _attn0_kernel(q_ref, k_ref, v_ref, bias_ref, e_ref, o_ref, kmt_ref, m_ref, l_ref, acc_ref):
    p = pl.program_id(0)
    i = pl.program_id(2)
    is_moba = p < MOBA_HEADS // 2

    @pl.when(jnp.logical_and(is_moba, i == 0))
    def _():
        lo = _lane_lo((LANES, LANES))
        row = lax.broadcasted_iota(jnp.int32, (LANES, LANES), 0)
        kmt = jnp.zeros((LANES, LANES), F32)
        for n in range(MOBA_NBLK):
            mean = jnp.mean(k_ref[0, n * MOBA_BLOCK:(n + 1) * MOBA_BLOCK, :].astype(F32), axis=0, keepdims=True)
            kmt = jnp.where(row == n, jnp.where(lo, mean, 0.0), kmt)
            kmt = jnp.where(row == MOBA_NBLK + n, jnp.where(lo, 0.0, mean), kmt)
        kmt_ref[...] = kmt

    q = q_ref[0]
    pen = lax.cond(is_moba, lambda: _moba_penalty(q, kmt_ref, i), lambda: jnp.zeros((TQ, LANES), F32))
    pen = pen.astype(BF16)
    q_aug = [jnp.concatenate([qh, pen], axis=1) for qh in _split_heads(q)]
    out = _branch(q_aug, k_ref, v_ref, bias_ref, e_ref, i, 0, m_ref, l_ref, acc_ref)
    o_ref[0] = out.astype(o_ref.dtype)


def _attn0(qkv, bias, e_onehot):
    B, S, _ = qkv.shape
    nq = S // TQ
    noff = bias.shape[2]
    return pl.pallas_call(
        _attn0_kernel,
        grid=(N_PAIRS, B, nq),
        in_specs=[
            pl.BlockSpec((1, TQ, LANES), lambda p, b, i: (b, i, p)),
            pl.BlockSpec((1, S, LANES), lambda p, b, i: (b, 0, N_PAIRS + p)),
            pl.BlockSpec((1, S, LANES), lambda p, b, i: (b, 0, 2 * N_PAIRS + p)),
            pl.BlockSpec((1, 2, noff, TQ, TQ), lambda p, b, i: (p, 0, 0, 0, 0)),
            pl.BlockSpec((2, S, LANES), lambda p, b, i: (0, 0, 0)),
        ],
        out_specs=pl.BlockSpec((1, TQ, LANES), lambda p, b, i: (b, i, p)),
        out_shape=jax.ShapeDtypeStruct((B, S, D_MODEL), BF16),
        scratch_shapes=[pltpu.VMEM((LANES, LANES), F32), pltpu.VMEM((2, TQ, 1), F32),
                        pltpu.VMEM((2, TQ, 1), F32), pltpu.VMEM((2, TQ, LANES), F32)],
        compiler_params=_cparams(3),
        name="attn_moba_dilated",
    )(qkv, qkv, qkv, bias, e_onehot)


def _compress_kernel(a_ref, pos_ref, w1_ref, b1_ref, w2_ref, b2_ref, gain_ref, o_ref, *, normed):
    y = None
    for g in range(2):
        a = a_ref[0, g].astype(F32)
        first = jnp.dot((a + pos_ref[0:1, :]).astype(BF16), w1_ref[0], preferred_element_type=F32)
        second = jnp.dot((a + pos_ref[1:2, :]).astype(BF16), w1_ref[1], preferred_element_type=F32)
        hid = first + pltpu.roll(second, LANES - 1, 0) + b1_ref[...]
        cdf = 0.5 * (1.0 + jnp.tanh(math.sqrt(2.0 / math.pi) * (hid + 0.044715 * (hid * hid * hid))))
        yg = jnp.dot((hid * cdf).astype(BF16), w2_ref[g], preferred_element_type=F32)
        y = yg if y is None else y + yg
    y = y + b2_ref[...]
    if normed:
        lo = _lane_lo(y.shape)
        y2 = y * y
        s_lo = jnp.sum(jnp.where(lo, y2, 0.0), axis=1, keepdims=True)
        s_hi = jnp.sum(jnp.where(lo, 0.0, y2), axis=1, keepdims=True)
        ms = jnp.where(lo, s_lo, s_hi) * (1.0 / HEAD_DIM)
        y = y * lax.rsqrt(ms + RMS_EPS) * gain_ref[...]
    o_ref[0, 0] = y.astype(o_ref.dtype)


def _compress(a, pos, w1, b1, w2, b2, gain, normed):
    B = a.shape[0]
    hid = w1.shape[1]
    half = NSA_CMP_STRIDE * HEAD_DIM
    zeros = jnp.zeros_like(w2)
    w2p = jnp.stack([jnp.concatenate([w2, zeros], axis=1), jnp.concatenate([zeros, w2], axis=1)]).astype(BF16)
    return pl.pallas_call(
        functools.partial(_compress_kernel, normed=normed),
        grid=(B, 2),
        in_specs=[
            pl.BlockSpec((1, 2, LANES, half), lambda b, m: (b, m, 0, 0)),
            pl.BlockSpec((2, half), lambda b, m: (0, 0)),
            pl.BlockSpec((2, half, hid), lambda b, m: (0, 0, 0)),
            pl.BlockSpec((1, hid), lambda b, m: (0, 0)),
            pl.BlockSpec((2, hid, LANES), lambda b, m: (0, 0, 0)),
            pl.BlockSpec((1, LANES), lambda b, m: (0, 0)),
            pl.BlockSpec((1, LANES), lambda b, m: (0, 0)),
        ],
        out_specs=pl.BlockSpec((1, 1, LANES, LANES), lambda b, m: (b, m, 0, 0)),
        out_shape=jax.ShapeDtypeStruct((B, 2, LANES, LANES), BF16),
        compiler_params=_cparams(2),
        name="nsa_compress",
    )(a, pos.reshape(2, half), w1.reshape(2, half, hid).astype(BF16), b1.reshape(1, hid), w2p,
      jnp.tile(b2, 2).reshape(1, LANES), jnp.tile(gain, 2).reshape(1, LANES))


def _cmp_kernel(q_ref, kc_ref, vc_ref, bias_ref, ovt_ref, o_ref, pen_ref):
    i = pl.program_id(2)
    kc = kc_ref[0, 0]
    vc = vc_ref[0, 0]
    lo = _lane_lo((TQ, LANES))
    psum = [jnp.zeros((TQ, LANES), F32), jnp.zeros((TQ, LANES), F32)]
    for r in range(4):
        heads = _split_heads(q_ref[0, :, r * LANES:(r + 1) * LANES])
        outs = []
        for h in range(2):
            s = lax.dot_general(heads[h], kc, NT_DIMS, preferred_element_type=F32) + bias_ref[0, 2 * r + h]
            m = jnp.max(s, axis=1, keepdims=True)
            e = jnp.where(s > 0.5 * NEG, jnp.exp(s - m), 0.0)
            l = jnp.sum(e, axis=1, keepdims=True)
            pr = e / jnp.where(l > 0.0, l, 1.0)
            psum[h] = psum[h] + pr
            outs.append(jnp.dot(pr.astype(BF16), vc, preferred_element_type=F32))
        o_ref[0, :, r * LANES:(r + 1) * LANES] = jnp.where(lo, outs[0], outs[1])

    n_sel = ovt_ref.shape[0]
    blk = lax.broadcasted_iota(jnp.int32, (n_sel, TQ), 0)
    t = i * TQ + lax.broadcasted_iota(jnp.int32, (n_sel, TQ), 1)
    cur = lax.shift_right_logical(t, int(math.log2(NSA_SEL_BLOCK)))
    keeps = []
    for h in range(2):
        imp = lax.dot_general(ovt_ref[...], psum[h], NT_DIMS,
                              precision=lax.Precision.HIGHEST, preferred_element_type=F32)
        forced = jnp.where(blk == 0, 1.0, jnp.where(blk == cur, 1.0, jnp.where(blk == cur - 1, 1.0, 0.0)))
        imp = jnp.where(blk <= cur, imp + forced * NSA_FORCE, -jnp.inf)
        rank = jnp.zeros(imp.shape, F32)
        for m in range(n_sel):
            im = imp[m:m + 1, :]
            tie = jnp.where(blk > m, 1.0, 0.0)
            rank = rank + jnp.where(im > imp, 1.0, jnp.where(im == imp, tie, 0.0))
        keeps.append(jnp.where(rank < NSA_SEL_TOPN, 1.0, 0.0))
    keep = jnp.concatenate(keeps + [jnp.zeros((LANES - 2 * n_sel, TQ), F32)], axis=0).T
    lane = lax.broadcasted_iota(jnp.int32, keep.shape, 1)
    pen_ref[0, 0] = jnp.where(lane < 2 * n_sel, (keep - 1.0) * (-NEG), 0.0).astype(pen_ref.dtype)


def _cmp_attention(qkv, kc, vc, bias, ovt):
    B, S, _ = qkv.shape
    nq = S // TQ
    n_sel = ovt.shape[0]
    return pl.pallas_call(
        _cmp_kernel,
        grid=(B, 2, nq),
        in_specs=[
            pl.BlockSpec((1, TQ, 4 * LANES), lambda b, m, i: (b, i, m)),
            pl.BlockSpec((1, 1, LANES, LANES), lambda b, m, i: (b, m, 0, 0)),
            pl.BlockSpec((1, 1, LANES, LANES), lambda b, m, i: (b, m, 0, 0)),
            pl.BlockSpec((1, 8, TQ, LANES), lambda b, m, i: (m, 0, i, 0)),
            pl.BlockSpec((n_sel, LANES), lambda b, m, i: (0, 0)),
        ],
        out_specs=[
            pl.BlockSpec((1, TQ, 4 * LANES), lambda b, m, i: (b, i, m)),
            pl.BlockSpec((1, 1, TQ, LANES), lambda b, m, i: (b, m, i, 0)),
        ],
        out_shape=[jax.ShapeDtypeStruct((B, S, D_MODEL), F32),
                   jax.ShapeDtypeStruct((B, 2, S, LANES), BF16)],
        compiler_params=_cparams(3),
        name="nsa_compressed_select",
    )(qkv, kc, vc, bias, ovt)


def _attn1_kernel(q_ref, ks_ref, vs_ref, kw_ref, vw_ref, bias_s_ref, bias_w_ref, e_ref, pen_ref,
                  ocmp_ref, graw_ref, gsel_ref, o_ref, m_ref, l_ref, acc_ref):
    i = pl.program_id(2)
    heads = _split_heads(q_ref[0])
    pen = pen_ref[0, 0]
    q_aug = [jnp.concatenate([qh, pen], axis=1) for qh in heads]
    o_slc = _branch(q_aug, ks_ref, vs_ref, bias_s_ref, e_ref, i, 0, m_ref, l_ref, acc_ref)
    n_win = bias_w_ref.shape[2]
    o_win = _branch(list(heads), kw_ref, vw_ref, bias_w_ref, None, i, jnp.maximum(i - (n_win - 1), 0),
                    m_ref, l_ref, acc_ref)
    z = graw_ref[0]
    sig = 1.0 / (1.0 + jnp.exp(-z))
    gates = [jnp.dot(sig, gsel_ref[0, br], precision=lax.Precision.HIGHEST, preferred_element_type=F32)
             for br in range(3)]
    out = gates[0] * ocmp_ref[0] + gates[1] * o_slc + gates[2] * o_win
    o_ref[0] = out.astype(o_ref.dtype)


def _attn1(qkv, bias_s, bias_w, e_onehot, pen, ocmp, graw, gsel):
    B, S, _ = qkv.shape
    nq = S // TQ
    kv0 = D_MODEL // LANES + 4
    return pl.pallas_call(
        _attn1_kernel,
        grid=(N_PAIRS, B, nq),
        in_specs=[
            pl.BlockSpec((1, TQ, LANES), lambda p, b, i: (b, i, p)),
            pl.BlockSpec((1, S, LANES), lambda p, b, i: (b, 0, kv0 + p // 4)),
            pl.BlockSpec((1, S, LANES), lambda p, b, i: (b, 0, kv0 + 2 + p // 4)),
            pl.BlockSpec((1, S, LANES), lambda p, b, i: (b, 0, kv0 + 4 + p // 4)),
            pl.BlockSpec((1, S, LANES), lambda p, b, i: (b, 0, kv0 + 6 + p // 4)),
            pl.BlockSpec((1, 2, bias_s.shape[2], TQ, TQ), lambda p, b, i: (p, 0, 0, 0, 0)),
            pl.BlockSpec((1, 2, bias_w.shape[2], TQ, TQ), lambda p, b, i: (p, 0, 0, 0, 0)),
            pl.BlockSpec((2, S, LANES), lambda p, b, i: (0, 0, 0)),
            pl.BlockSpec((1, 1, TQ, LANES), lambda p, b, i: (b, p // 4, i, 0)),
            pl.BlockSpec((1, TQ, LANES), lambda p, b, i: (b, i, p)),
            pl.BlockSpec((1, TQ, LANES), lambda p, b, i: (b, i, 0)),
            pl.BlockSpec((1, 3, LANES, LANES), lambda p, b, i: (p, 0, 0, 0)),
        ],
        out_specs=pl.BlockSpec((1, TQ, LANES), lambda p, b, i: (b, i, p)),
        out_shape=jax.ShapeDtypeStruct((B, S, D_MODEL), BF16),
        scratch_shapes=[pltpu.VMEM((2, TQ, 1), F32), pltpu.VMEM((2, TQ, 1), F32),
                        pltpu.VMEM((2, TQ, LANES), F32)],
        compiler_params=_cparams(3),
        name="attn_nsa",
    )(qkv, qkv, qkv, qkv, qkv, bias_s, bias_w, e_onehot, pen, ocmp, graw, gsel)


def _bucket(dist):
    n = jnp.maximum(dist, 0)
    exact = REL_BUCKETS // 2
    nf = jnp.maximum(n, 1).astype(F32)
    large = exact + (jnp.log(nf / exact) / math.log(REL_MAX_DIST / exact) * (REL_BUCKETS - exact)).astype(jnp.int32)
    return jnp.where(n < exact, n, jnp.minimum(large, REL_BUCKETS - 1))


def _tile_dist(n_off):
    o = np.arange(n_off)[:, None, None]
    q = np.arange(TQ)[None, :, None]
    k = np.arange(TQ)[None, None, :]
    return o * TQ + q - k


def _toeplitz_tiles(tab, dist, extra):
    vals = tab[:, _bucket(jnp.asarray(dist))].astype(F32) + extra
    return vals.reshape(tab.shape[0] // 2, 2, *dist.shape)


def _dilation_log_count(dist):
    c = ((dist >= 0) & (dist <= 128)).astype(np.float64)
    c += ((dist >= 0) & (dist % 4 == 0) & (dist <= 512))
    c += ((dist >= 0) & (dist % 16 == 0) & (dist <= 2048))
    return np.where(c > 0, np.log(np.maximum(c, 1.0)), NEG).astype(np.float32)


def _one_hot_blocks(S, block, per_head):
    e = np.zeros((2, S, LANES), np.float32)
    key = np.arange(S)
    for h in range(2):
        e[h, key, h * per_head + key // block] = 1.0
    return jnp.asarray(e, BF16)


_NSA_HEAD_ORDER = np.array([8 * (p // 4) + (p % 4) + 4 * h for p in range(N_PAIRS) for h in range(2)])


def _head_cols(heads):
    return (np.asarray(heads)[:, None] * HEAD_DIM + np.arange(HEAD_DIM)[None, :]).reshape(-1)


def _layer_ab(h, g_mix, w_in, w_out, qn_a, kn_a, qn_b, kn_b, rel_bias, B, S):
    wa = MOBA_HEADS * HEAD_DIM
    sec = [w_in[:, k * wa:(k + 1) * wa] for k in range(6)]
    w = jnp.concatenate([sec[0], sec[3], sec[1], sec[4], sec[2], sec[5]], axis=1).astype(BF16)
    ones = jnp.ones((wa,), F32)
    gain = jnp.concatenate([jnp.tile(qn_a, 8) * ATTN_SCALE, jnp.tile(qn_b, 8) * ATTN_SCALE,
                            jnp.tile(kn_a, 8), jnp.tile(kn_b, 8), ones, ones])
    mode = jnp.concatenate([jnp.ones((4 * wa,), F32), jnp.zeros((2 * wa,), F32)])
    qkv = _project(h, g_mix, w, gain, mode, BF16, 512, True).reshape(B, S, 3 * D_MODEL)

    dist = _tile_dist(S // TQ)
    causal = np.where(dist >= 0, 0.0, NEG).astype(np.float32)
    tab = rel_bias.T
    bias = jnp.concatenate([_toeplitz_tiles(tab[:MOBA_HEADS], dist, causal),
                            _toeplitz_tiles(tab[MOBA_HEADS:], dist, _dilation_log_count(dist))])
    o = _attn0(qkv, bias, _one_hot_blocks(S, MOBA_BLOCK, S // MOBA_BLOCK))
    return _outproj(o.reshape(B * S, D_MODEL), w_out.astype(BF16), h)


def _layer_nsa(h, g_mix, w_in, w_out, qn, kn_c, kn_s, kn_w, cmp_k, cmp_v, rel_bias, B, S):
    qw = N_HEADS * HEAD_DIM
    kvw = NSA_GROUPS * HEAD_DIM
    order = _NSA_HEAD_ORDER
    w_main = jnp.concatenate([w_in[:, _head_cols(order)], w_in[:, qw:qw + 6 * kvw]], axis=1).astype(BF16)
    ones = jnp.ones((kvw,), F32)
    gain = jnp.concatenate([jnp.tile(qn, N_HEADS) * ATTN_SCALE, ones, ones,
                            jnp.tile(kn_s, NSA_GROUPS), ones, jnp.tile(kn_w, NSA_GROUPS), ones])
    flag = jnp.ones((kvw,), F32)
    mode = jnp.concatenate([jnp.ones((qw,), F32), 0 * flag, 0 * flag, flag, 0 * flag, flag, 0 * flag])
    qkv = _project(h, g_mix, w_main, gain, mode, BF16, 512, True).reshape(B, S, qw + 6 * kvw)

    gcols = np.zeros((LANES,), np.int64)
    gused = np.zeros((LANES,), np.float32)
    gsel = np.zeros((N_PAIRS, 3, LANES, LANES), np.float32)
    for p in range(N_PAIRS):
        for br in range(3):
            for hh in range(2):
                c = 8 * p + 2 * br + hh
                gcols[c] = qw + 6 * kvw + 3 * order[2 * p + hh] + br
                gused[c] = 1.0
                gsel[p, br, c, hh * HEAD_DIM:(hh + 1) * HEAD_DIM] = 1.0
    w_gate = (w_in[:, gcols] * gused).astype(BF16)
    zeros = jnp.zeros((LANES,), F32)
    graw = _project(h, g_mix, w_gate, zeros, zeros, F32, LANES, False).reshape(B, S, LANES)

    def chunked(col0):
        t = qkv[:, :, col0:col0 + kvw].reshape(B, S // NSA_CMP_STRIDE, NSA_CMP_STRIDE, NSA_GROUPS, HEAD_DIM)
        return t.transpose(0, 3, 1, 2, 4).reshape(B, NSA_GROUPS, S // NSA_CMP_STRIDE, NSA_CMP_STRIDE * HEAD_DIM)

    kc = _compress(chunked(qw), *cmp_k, kn_c, True)
    vc = _compress(chunked(qw + kvw), *cmp_v, kn_c, False)

    tab = rel_bias.T[order]
    n_cmp = (S - NSA_CMP_LEN) // NSA_CMP_STRIDE + 1
    t_pos = np.arange(S)[:, None]
    c_idx = np.arange(LANES)[None, :]
    dc = t_pos - (c_idx * NSA_CMP_STRIDE + NSA_CMP_LEN - 1)
    cmp_mask = np.where((dc >= 0) & (c_idx < n_cmp), 0.0, NEG).astype(np.float32)
    bias_c = (tab[:, _bucket(jnp.asarray(dc))].astype(F32) + cmp_mask).reshape(2, 8, S, LANES)
    n_sel = S // NSA_SEL_BLOCK
    cstart = np.arange(LANES) * NSA_CMP_STRIDE
    sstart = np.arange(n_sel) * NSA_SEL_BLOCK
    ovt = np.maximum(np.minimum(cstart[None, :] + NSA_CMP_LEN, sstart[:, None] + NSA_SEL_BLOCK)
                     - np.maximum(cstart[None, :], sstart[:, None]), 0).astype(np.float32)
    ovt[:, n_cmp:] = 0.0
    ocmp, pen = _cmp_attention(qkv, kc, vc, bias_c, jnp.asarray(ovt))

    dist = _tile_dist(S // TQ)
    causal = np.where(dist >= 0, 0.0, NEG).astype(np.float32)
    bias_s = _toeplitz_tiles(tab, dist, causal)
    dist_w = _tile_dist(NSA_WINDOW // TQ + 1)
    window = np.where((dist_w >= 0) & (dist_w < NSA_WINDOW), 0.0, NEG).astype(np.float32)
    bias_w = _toeplitz_tiles(tab, dist_w, window)
    o = _attn1(qkv, bias_s, bias_w, _one_hot_blocks(S, NSA_SEL_BLOCK, n_sel), pen, ocmp, graw, jnp.asarray(gsel))
    w_out_p = w_out[_head_cols(order), :].astype(BF16)
    return _outproj(o.reshape(B * S, D_MODEL), w_out_p, h)


def kernel(x, p, rel_bias, norm_mix, norm_ffn, norm_ple, w_ffn_gate, w_ffn_up, w_ffn_down, w_ple_proj, w_ple_gate, w_in_ab, w_out_ab, qn_moba, kn_moba, qn_dil, kn_dil, w_in_nsa, w_out_nsa, qn_nsa, kn_cmp, kn_slc, kn_win, cmp_k_pos, cmp_k_w1, cmp_k_b1, cmp_k_w2, cmp_k_b2, cmp_v_pos, cmp_v_w1, cmp_v_b1, cmp_v_w2, cmp_v_b2):
    B, S, D = x.shape
    depth = p.shape[0]
    h = x.reshape(B * S, D)
    for i in range(depth):
        e = i // 2
        if i % 2 == 0:
            h = _layer_ab(h, norm_mix[i], w_in_ab[e], w_out_ab[e], qn_moba[e], kn_moba[e], qn_dil[e], kn_dil[e],
                          rel_bias, B, S)
        else:
            cmp_k = (cmp_k_pos[e], cmp_k_w1[e], cmp_k_b1[e], cmp_k_w2[e], cmp_k_b2[e])
            cmp_v = (cmp_v_pos[e], cmp_v_w1[e], cmp_v_b1[e], cmp_v_w2[e], cmp_v_b2[e])
            h = _layer_nsa(h, norm_mix[i], w_in_nsa[e], w_out_nsa[e], qn_nsa[e], kn_cmp[e], kn_slc[e], kn_win[e],
                           cmp_k, cmp_v, rel_bias, B, S)
        h = _ffn(h, norm_ffn[i], w_ffn_gate[i].astype(BF16), w_ffn_up[i].astype(BF16), w_ffn_down[i].astype(BF16))
        h = _ple(h, norm_ple[i], w_ple_gate[i].astype(BF16), p[i].reshape(B * S, -1), w_ple_proj[i].astype(BF16))
    return h.reshape(B, S, D)
```

```python
import functools
import math

import numpy as np
import jax
import jax.numpy as jnp
from jax import lax
from jax.experimental import pallas as pl
from jax.experimental.pallas import tpu as pltpu

F32 = jnp.float32
BF16 = jnp.bfloat16

D_MODEL = 1024
HEAD_DIM = 64
N_HEADS = 16
N_PAIRS = N_HEADS // 2
MOBA_HEADS = 8
MOBA_BLOCK = 256
MOBA_TOPK = 3
MOBA_NBLK = 8
NSA_GROUPS = 4
NSA_CMP_LEN = 32
NSA_CMP_STRIDE = 16
NSA_SEL_BLOCK = 64
NSA_SEL_TOPN = 16
NSA_WINDOW = 512
NSA_FORCE = 1.0e6
REL_BUCKETS = 32
REL_MAX_DIST = 2048
RMS_EPS = 1e-6
ATTN_SCALE = HEAD_DIM ** -0.5

LANES = 128
TQ = 256
NEG = -1.0e30
ROW_TILE = 512
VMEM_LIMIT = 48 * 1024 * 1024

NT_DIMS = (((1,), (1,)), ((), ()))


def _cparams(n_axes):
    return pltpu.CompilerParams(dimension_semantics=("arbitrary",) * n_axes,
                                vmem_limit_bytes=VMEM_LIMIT)


def _rms_rows(x, g):
    ms = jnp.mean(x * x, axis=-1, keepdims=True)
    return x * lax.rsqrt(ms + RMS_EPS) * g


def _lane_lo(shape):
    return lax.broadcasted_iota(jnp.int32, shape, len(shape) - 1) < HEAD_DIM


def _proj_kernel(x_ref, g_ref, w_ref, cg_ref, cm_ref, bd_ref, o_ref, xn_ref, *, head_norm):
    @pl.when(pl.program_id(1) == 0)
    def _():
        xn_ref[...] = _rms_rows(x_ref[...], g_ref[...]).astype(BF16)

    y = jnp.dot(xn_ref[...], w_ref[...], preferred_element_type=F32)
    if head_norm:
        y2 = y * y
        hi = y2.astype(BF16)
        lo = (y2 - hi.astype(F32)).astype(BF16)
        bd = bd_ref[...]
        ssq = jnp.dot(hi, bd, preferred_element_type=F32) + jnp.dot(lo, bd, preferred_element_type=F32)
        yn = y * lax.rsqrt(ssq * (1.0 / HEAD_DIM) + RMS_EPS) * cg_ref[...]
        y = jnp.where(cm_ref[...] > 0.0, yn, y)
    o_ref[...] = y.astype(o_ref.dtype)


def _project(x, g, w, col_gain, col_mode, out_dtype, tn, head_norm):
    T, D = x.shape
    N = w.shape[1]
    blk = np.kron(np.eye(tn // HEAD_DIM), np.ones((HEAD_DIM, HEAD_DIM))).astype(np.float32)
    return pl.pallas_call(
        functools.partial(_proj_kernel, head_norm=head_norm),
        grid=(T // ROW_TILE, N // tn),
        in_specs=[
            pl.BlockSpec((ROW_TILE, D), lambda i, j: (i, 0)),
            pl.BlockSpec((1, D), lambda i, j: (0, 0)),
            pl.BlockSpec((D, tn), lambda i, j: (0, j)),
            pl.BlockSpec((1, tn), lambda i, j: (0, j)),
            pl.BlockSpec((1, tn), lambda i, j: (0, j)),
            pl.BlockSpec((tn, tn), lambda i, j: (0, 0)),
        ],
        out_specs=pl.BlockSpec((ROW_TILE, tn), lambda i, j: (i, j)),
        out_shape=jax.ShapeDtypeStruct((T, N), out_dtype),
        scratch_shapes=[pltpu.VMEM((ROW_TILE, D), BF16)],
        compiler_params=_cparams(2),
        name="proj",
    )(x, g.reshape(1, D), w, col_gain.reshape(1, N), col_mode.reshape(1, N), jnp.asarray(blk, BF16))


def _outproj_kernel(a_ref, w_ref, r_ref, o_ref):
    o_ref[...] = r_ref[...] + jnp.dot(a_ref[...], w_ref[...], preferred_element_type=F32)


def _outproj(a, w, res, tn=512):
    T, K = a.shape
    N = w.shape[1]
    return pl.pallas_call(
        _outproj_kernel,
        grid=(T // ROW_TILE, N // tn),
        in_specs=[
            pl.BlockSpec((ROW_TILE, K), lambda i, j: (i, 0)),
            pl.BlockSpec((K, tn), lambda i, j: (0, j)),
            pl.BlockSpec((ROW_TILE, tn), lambda i, j: (i, j)),
        ],
        out_specs=pl.BlockSpec((ROW_TILE, tn), lambda i, j: (i, j)),
        out_shape=jax.ShapeDtypeStruct((T, N), F32),
        compiler_params=_cparams(2),
        name="outproj",
    )(a, w, res)


def _ffn_kernel(h_ref, g_ref, wg_ref, wu_ref, wd_ref, o_ref, xn_ref, acc_ref):
    j = pl.program_id(1)

    @pl.when(j == 0)
    def _():
        xn_ref[...] = _rms_rows(h_ref[...], g_ref[...]).astype(BF16)
        acc_ref[...] = jnp.zeros_like(acc_ref)

    xn = xn_ref[...]
    a = jnp.dot(xn, wg_ref[...], preferred_element_type=F32)
    u = jnp.dot(xn, wu_ref[...], preferred_element_type=F32)
    act = (a * (1.0 / (1.0 + jnp.exp(-a))) * u).astype(BF16)
    acc_ref[...] += jnp.dot(act, wd_ref[...], preferred_element_type=F32)

    @pl.when(j == pl.num_programs(1) - 1)
    def _():
        o_ref[...] = h_ref[...] + acc_ref[...]


def _ffn(h, g, wg, wu, wd, tf=256):
    T, D = h.shape
    Fh = wg.shape[1]
    return pl.pallas_call(
        _ffn_kernel,
        grid=(T // ROW_TILE, Fh // tf),
        in_specs=[
            pl.BlockSpec((ROW_TILE, D), lambda i, j: (i, 0)),
            pl.BlockSpec((1, D), lambda i, j: (0, 0)),
            pl.BlockSpec((D, tf), lambda i, j: (0, j)),
            pl.BlockSpec((D, tf), lambda i, j: (0, j)),
            pl.BlockSpec((tf, D), lambda i, j: (j, 0)),
        ],
        out_specs=pl.BlockSpec((ROW_TILE, D), lambda i, j: (i, 0)),
        out_shape=jax.ShapeDtypeStruct((T, D), F32),
        scratch_shapes=[pltpu.VMEM((ROW_TILE, D), BF16), pltpu.VMEM((ROW_TILE, D), F32)],
        compiler_params=_cparams(2),
        name="ffn",
    )(h, g.reshape(1, D), wg, wu, wd)


def _ple_kernel(h_ref, hcol_ref, g_ref, wgate_ref, p_ref, wproj_ref, o_ref, hn_ref):
    @pl.when(pl.program_id(1) == 0)
    def _():
        hn_ref[...] = _rms_rows(h_ref[...], g_ref[...]).astype(BF16)

    z = jnp.dot(hn_ref[...], wgate_ref[...], preferred_element_type=F32)
    gate = 1.0 / (1.0 + jnp.exp(-z))
    proj = jnp.dot(p_ref[...].astype(BF16), wproj_ref[...], preferred_element_type=F32)
    o_ref[...] = hcol_ref[...] + gate * proj


def _ple(h, g, wgate, p, wproj, tn=512):
    T, D = h.shape
    Pd = p.shape[1]
    return pl.pallas_call(
        _ple_kernel,
        grid=(T // ROW_TILE, D // tn),
        in_specs=[
            pl.BlockSpec((ROW_TILE, D), lambda i, j: (i, 0)),
            pl.BlockSpec((ROW_TILE, tn), lambda i, j: (i, j)),
            pl.BlockSpec((1, D), lambda i, j: (0, 0)),
            pl.BlockSpec((D, tn), lambda i, j: (0, j)),
            pl.BlockSpec((ROW_TILE, Pd), lambda i, j: (i, 0)),
            pl.BlockSpec((Pd, tn), lambda i, j: (0, j)),
        ],
        out_specs=pl.BlockSpec((ROW_TILE, tn), lambda i, j: (i, j)),
        out_shape=jax.ShapeDtypeStruct((T, D), F32),
        scratch_shapes=[pltpu.VMEM((ROW_TILE, D), BF16)],
        compiler_params=_cparams(2),
        name="ple",
    )(h, h, g.reshape(1, D), wgate, p, wproj)


def _strip(r_row, width):
    x = jnp.broadcast_to(r_row, (TQ, r_row.shape[-1]))
    return pltpu.roll(x, 0, 1, stride=1, stride_axis=0)[:, :width]


def _attend(q_op, k_rows, strip_cols, v_rows):
    s = lax.dot_general(q_op, k_rows, NT_DIMS, preferred_element_type=F32) + strip_cols
    m = jnp.max(s, axis=1, keepdims=True)
    p = jnp.exp(s - m).astype(BF16)
    oa = jnp.dot(p, v_rows, preferred_element_type=F32)
    return oa[:, :LANES] / oa[:, LANES:]


def _split_heads(q):
    lo = _lane_lo(q.shape)
    zero = jnp.zeros_like(q)
    return jnp.where(lo, q, zero), jnp.where(lo, zero, q)


def _fill_value_ones(vaug_ref):
    vaug_ref[:, LANES:] = jnp.ones((vaug_ref.shape[0], LANES), vaug_ref.dtype)


def _moba_block_means(k_ref):
    shape = (2 * MOBA_NBLK, LANES)
    lo = _lane_lo(shape)
    row = lax.broadcasted_iota(jnp.int32, shape, 0)
    kmt = jnp.zeros(shape, F32)
    for n in range(MOBA_NBLK):
        mean = jnp.mean(k_ref[0, n * MOBA_BLOCK:(n + 1) * MOBA_BLOCK, :].astype(F32), axis=0, keepdims=True)
        kmt = jnp.where(row == n, jnp.where(lo, mean, 0.0), kmt)
        kmt = jnp.where(row == MOBA_NBLK + n, jnp.where(lo, 0.0, mean), kmt)
    return kmt


def _moba_penalty(q, kmt, i):
    nb = MOBA_NBLK
    g = lax.dot_general(kmt, q.astype(F32), NT_DIMS,
                        precision=lax.Precision.HIGHEST, preferred_element_type=F32)
    row = lax.broadcasted_iota(jnp.int32, g.shape, 0)
    n = row & (nb - 1)
    rank = jnp.zeros(g.shape, F32)
    for m in range(i):
        gm = jnp.where(row < nb, g[m:m + 1, :], g[nb + m:nb + m + 1, :])
        tie = jnp.where(n > m, 1.0, 0.0)
        rank = rank + jnp.where(gm > g, 1.0, jnp.where(gm == g, tie, 0.0))
    keep = jnp.where(n < i, jnp.where(rank < MOBA_TOPK, 1.0, 0.0), jnp.where(n == i, 1.0, 0.0))
    keep = jnp.concatenate([keep, jnp.zeros((LANES - 2 * nb, TQ), F32)], axis=0).T
    lane = lax.broadcasted_iota(jnp.int32, keep.shape, 1)
    return jnp.where(lane < 2 * nb, (keep - 1.0) * (-NEG), 0.0)


def _attn0_kernel(q_ref, k_ref, v_ref, r_ref, e_ref, o_ref, strip_ref, kaug_ref, vaug_ref):
    p = pl.program_id(0)
    b = pl.program_id(1)
    S = q_ref.shape[1]

    @pl.when(b == 0)
    def _():
        for h in range(2):
            strip_ref[h] = _strip(r_ref[0, h], S)

    @pl.when(jnp.logical_and(p == 0, b == 0))
    def _():
        for h in range(2):
            kaug_ref[h, :, LANES:] = e_ref[h]
        _fill_value_ones(vaug_ref)

    k = k_ref[0]
    for h in range(2):
        kaug_ref[h, :, :LANES] = k
    vaug_ref[:, :LANES] = v_ref[0]
    kmt = _moba_block_means(k_ref)
    is_moba = p < MOBA_HEADS // 2
    lo = _lane_lo((TQ, LANES))
    for i in range(S // TQ):
        w = (i + 1) * TQ
        q = q_ref[0, i * TQ:w, :]
        pen = jnp.where(is_moba, _moba_penalty(q, kmt, i), 0.0).astype(BF16)
        outs = [_attend(jnp.concatenate([qh, pen], axis=1), kaug_ref[h, 0:w, :], strip_ref[h, :, S - w:S],
                        vaug_ref[0:w, :]) for h, qh in enumerate(_split_heads(q))]
        o_ref[0, i * TQ:w, :] = jnp.where(lo, outs[0], outs[1]).astype(o_ref.dtype)


def _attn0(qkv, r_tab, e_onehot):
    B, S, _ = qkv.shape
    L = r_tab.shape[-1]
    return pl.pallas_call(
        _attn0_kernel,
        grid=(N_PAIRS, B),
        in_specs=[
            pl.BlockSpec((1, S, LANES), lambda p, b: (b, 0, p)),
            pl.BlockSpec((1, S, LANES), lambda p, b: (b, 0, N_PAIRS + p)),
            pl.BlockSpec((1, S, LANES), lambda p, b: (b, 0, 2 * N_PAIRS + p)),
            pl.BlockSpec((1, 2, 1, L), lambda p, b: (p, 0, 0, 0)),
            pl.BlockSpec((2, S, LANES), lambda p, b: (0, 0, 0)),
        ],
        out_specs=pl.BlockSpec((1, S, LANES), lambda p, b: (b, 0, p)),
        out_shape=jax.ShapeDtypeStruct((B, S, D_MODEL), BF16),
        scratch_shapes=[pltpu.VMEM((2, TQ, S), F32), pltpu.VMEM((2, S, 2 * LANES), BF16),
                        pltpu.VMEM((S, 2 * LANES), BF16)],
        compiler_params=_cparams(2),
        name="attn_moba_dilated",
    )(qkv, qkv, qkv, r_tab, e_onehot)


def _compress_kernel(a_ref, pos_ref, w1_ref, b1_ref, w2_ref, b2_ref, gain_ref, o_ref, *, normed):
    y = None
    for g in range(2):
        a = a_ref[0, g].astype(F32)
        first = jnp.dot((a + pos_ref[0:1, :]).astype(BF16), w1_ref[0], preferred_element_type=F32)
        second = jnp.dot((a + pos_ref[1:2, :]).astype(BF16), w1_ref[1], preferred_element_type=F32)
        hid = first + pltpu.roll(second, LANES - 1, 0) + b1_ref[...]
        cdf = 0.5 * (1.0 + jnp.tanh(math.sqrt(2.0 / math.pi) * (hid + 0.044715 * (hid * hid * hid))))
        yg = jnp.dot((hid * cdf).astype(BF16), w2_ref[g], preferred_element_type=F32)
        y = yg if y is None else y + yg
    y = y + b2_ref[...]
    if normed:
        lo = _lane_lo(y.shape)
        y2 = y * y
        s_lo = jnp.sum(jnp.where(lo, y2, 0.0), axis=1, keepdims=True)
        s_hi = jnp.sum(jnp.where(lo, 0.0, y2), axis=1, keepdims=True)
        ms = jnp.where(lo, s_lo, s_hi) * (1.0 / HEAD_DIM)
        y = y * lax.rsqrt(ms + RMS_EPS) * gain_ref[...]
    o_ref[0, 0] = y.astype(o_ref.dtype)


def _compress(a, pos, w1, b1, w2, b2, gain, normed):
    B = a.shape[0]
    hid = w1.shape[1]
    half = NSA_CMP_STRIDE * HEAD_DIM
    zeros = jnp.zeros_like(w2)
    w2p = jnp.stack([jnp.concatenate([w2, zeros], axis=1), jnp.concatenate([zeros, w2], axis=1)]).astype(BF16)
    return pl.pallas_call(
        functools.partial(_compress_kernel, normed=normed),
        grid=(B, 2),
        in_specs=[
            pl.BlockSpec((1, 2, LANES, half), lambda b, m: (b, m, 0, 0)),
            pl.BlockSpec((2, half), lambda b, m: (0, 0)),
            pl.BlockSpec((2, half, hid), lambda b, m: (0, 0, 0)),
            pl.BlockSpec((1, hid), lambda b, m: (0, 0)),
            pl.BlockSpec((2, hid, LANES), lambda b, m: (0, 0, 0)),
            pl.BlockSpec((1, LANES), lambda b, m: (0, 0)),
            pl.BlockSpec((1, LANES), lambda b, m: (0, 0)),
        ],
        out_specs=pl.BlockSpec((1, 1, LANES, LANES), lambda b, m: (b, m, 0, 0)),
        out_shape=jax.ShapeDtypeStruct((B, 2, LANES, LANES), BF16),
        compiler_params=_cparams(2),
        name="nsa_compress",
    )(a, pos.reshape(2, half), w1.reshape(2, half, hid).astype(BF16), b1.reshape(1, hid), w2p,
      jnp.tile(b2, 2).reshape(1, LANES), jnp.tile(gain, 2).reshape(1, LANES))


def _cmp_kernel(q_ref, kc_ref, vc_ref, bias_ref, ovt_ref, o_ref, pen_ref):
    i = pl.program_id(2)
    kc = kc_ref[0, 0]
    vc = vc_ref[0, 0]
    lo = _lane_lo((TQ, LANES))
    psum = [jnp.zeros((TQ, LANES), F32), jnp.zeros((TQ, LANES), F32)]
    for r in range(4):
        heads = _split_heads(q_ref[0, :, r * LANES:(r + 1) * LANES])
        outs = []
        for h in range(2):
            s = lax.dot_general(heads[h], kc, NT_DIMS, preferred_element_type=F32) + bias_ref[0, 2 * r + h]
            m = jnp.max(s, axis=1, keepdims=True)
            e = jnp.where(s > 0.5 * NEG, jnp.exp(s - m), 0.0)
            l = jnp.sum(e, axis=1, keepdims=True)
            pr = e / jnp.where(l > 0.0, l, 1.0)
            psum[h] = psum[h] + pr
            outs.append(jnp.dot(pr.astype(BF16), vc, preferred_element_type=F32))
        o_ref[0, :, r * LANES:(r + 1) * LANES] = jnp.where(lo, outs[0], outs[1])

    n_sel = ovt_ref.shape[0]
    blk = lax.broadcasted_iota(jnp.int32, (n_sel, TQ), 0)
    t = i * TQ + lax.broadcasted_iota(jnp.int32, (n_sel, TQ), 1)
    cur = lax.shift_right_logical(t, int(math.log2(NSA_SEL_BLOCK)))
    keeps = []
    for h in range(2):
        imp = lax.dot_general(ovt_ref[...], psum[h], NT_DIMS,
                              precision=lax.Precision.HIGHEST, preferred_element_type=F32)
        forced = jnp.where(blk == 0, 1.0, jnp.where(blk == cur, 1.0, jnp.where(blk == cur - 1, 1.0, 0.0)))
        imp = jnp.where(blk <= cur, imp + forced * NSA_FORCE, -jnp.inf)
        rank = jnp.zeros(imp.shape, F32)
        for m in range(n_sel):
            im = imp[m:m + 1, :]
            tie = jnp.where(blk > m, 1.0, 0.0)
            rank = rank + jnp.where(im > imp, 1.0, jnp.where(im == imp, tie, 0.0))
        keeps.append(jnp.where(rank < NSA_SEL_TOPN, 1.0, 0.0))
    keep = jnp.concatenate(keeps + [jnp.zeros((LANES - 2 * n_sel, TQ), F32)], axis=0).T
    lane = lax.broadcasted_iota(jnp.int32, keep.shape, 1)
    pen_ref[0, 0] = jnp.where(lane < 2 * n_sel, (keep - 1.0) * (-NEG), 0.0).astype(pen_ref.dtype)


def _cmp_attention(qkv, kc, vc, bias, ovt):
    B, S, _ = qkv.shape
    nq = S // TQ
    n_sel = ovt.shape[0]
    return pl.pallas_call(
        _cmp_kernel,
        grid=(B, 2, nq),
        in_specs=[
            pl.BlockSpec((1, TQ, 4 * LANES), lambda b, m, i: (b, i, m)),
            pl.BlockSpec((1, 1, LANES, LANES), lambda b, m, i: (b, m, 0, 0)),
            pl.BlockSpec((1, 1, LANES, LANES), lambda b, m, i: (b, m, 0, 0)),
            pl.BlockSpec((1, 8, TQ, LANES), lambda b, m, i: (m, 0, i, 0)),
            pl.BlockSpec((n_sel, LANES), lambda b, m, i: (0, 0)),
        ],
        out_specs=[
            pl.BlockSpec((1, TQ, 4 * LANES), lambda b, m, i: (b, i, m)),
            pl.BlockSpec((1, 1, TQ, LANES), lambda b, m, i: (b, m, i, 0)),
        ],
        out_shape=[jax.ShapeDtypeStruct((B, S, D_MODEL), F32),
                   jax.ShapeDtypeStruct((B, 2, S, LANES), BF16)],
        compiler_params=_cparams(3),
        name="nsa_compressed_select",
    )(qkv, kc, vc, bias, ovt)


def _attn1_kernel(q_ref, ks_ref, vs_ref, kw_ref, vw_ref, rs_ref, rw_ref, e_ref, pen_ref, ocmp_ref, graw_ref,
                  gsel_ref, o_ref, strip_s_ref, strip_w_ref, kaug_ref, vsaug_ref, vwaug_ref):
    p = pl.program_id(0)
    b = pl.program_id(1)
    S = q_ref.shape[1]
    w_win = strip_w_ref.shape[2]

    @pl.when(b == 0)
    def _():
        for h in range(2):
            strip_s_ref[h] = _strip(rs_ref[0, h], S)
            strip_w_ref[h] = _strip(rw_ref[0, h], w_win)

    @pl.when(jnp.logical_and(p == 0, b == 0))
    def _():
        for h in range(2):
            kaug_ref[h, :, LANES:] = e_ref[h]
        _fill_value_ones(vsaug_ref)
        _fill_value_ones(vwaug_ref)

    ks = ks_ref[0]
    for h in range(2):
        kaug_ref[h, :, :LANES] = ks
    vsaug_ref[:, :LANES] = vs_ref[0]
    vwaug_ref[:, :LANES] = vw_ref[0]
    lo = _lane_lo((TQ, LANES))
    for i in range(S // TQ):
        w = (i + 1) * TQ
        rows = slice(i * TQ, w)
        heads = _split_heads(q_ref[0, rows, :])
        pen = pen_ref[0, 0, rows, :]
        slc = [_attend(jnp.concatenate([qh, pen], axis=1), kaug_ref[h, 0:w, :], strip_s_ref[h, :, S - w:S],
                       vsaug_ref[0:w, :]) for h, qh in enumerate(heads)]
        ww = min(w, w_win)
        win = [_attend(qh, kw_ref[0, w - ww:w, :], strip_w_ref[h, :, w_win - ww:w_win], vwaug_ref[w - ww:w, :])
               for h, qh in enumerate(heads)]
        z = graw_ref[0, rows, :]
        sig = 1.0 / (1.0 + jnp.exp(-z))
        gates = [jnp.dot(sig, gsel_ref[0, br], precision=lax.Precision.HIGHEST, preferred_element_type=F32)
                 for br in range(3)]
        out = (gates[0] * ocmp_ref[0, rows, :] + gates[1] * jnp.where(lo, slc[0], slc[1])
               + gates[2] * jnp.where(lo, win[0], win[1]))
        o_ref[0, rows, :] = out.astype(o_ref.dtype)


def _attn1(qkv, r_slc, r_win, e_onehot, pen, ocmp, graw, gsel):
    B, S, _ = qkv.shape
    kv0 = D_MODEL // LANES + 4
    w_win = r_win.shape[-1] - TQ
    return pl.pallas_call(
        _attn1_kernel,
        grid=(N_PAIRS, B),
        in_specs=[
            pl.BlockSpec((1, S, LANES), lambda p, b: (b, 0, p)),
            pl.BlockSpec((1, S, LANES), lambda p, b: (b, 0, kv0 + p // 4)),
            pl.BlockSpec((1, S, LANES), lambda p, b: (b, 0, kv0 + 2 + p // 4)),
            pl.BlockSpec((1, S, LANES), lambda p, b: (b, 0, kv0 + 4 + p // 4)),
            pl.BlockSpec((1, S, LANES), lambda p, b: (b, 0, kv0 + 6 + p // 4)),
            pl.BlockSpec((1, 2, 1, r_slc.shape[-1]), lambda p, b: (p, 0, 0, 0)),
            pl.BlockSpec((1, 2, 1, r_win.shape[-1]), lambda p, b: (p, 0, 0, 0)),
            pl.BlockSpec((2, S, LANES), lambda p, b: (0, 0, 0)),
            pl.BlockSpec((1, 1, S, LANES), lambda p, b: (b, p // 4, 0, 0)),
            pl.BlockSpec((1, S, LANES), lambda p, b: (b, 0, p)),
            pl.BlockSpec((1, S, LANES), lambda p, b: (b, 0, 0)),
            pl.BlockSpec((1, 3, LANES, LANES), lambda p, b: (p, 0, 0, 0)),
        ],
        out_specs=pl.BlockSpec((1, S, LANES), lambda p, b: (b, 0, p)),
        out_shape=jax.ShapeDtypeStruct((B, S, D_MODEL), BF16),
        scratch_shapes=[pltpu.VMEM((2, TQ, S), F32), pltpu.VMEM((2, TQ, w_win), F32),
                        pltpu.VMEM((2, S, 2 * LANES), BF16), pltpu.VMEM((S, 2 * LANES), BF16),
                        pltpu.VMEM((S, 2 * LANES), BF16)],
        compiler_params=_cparams(2),
        name="attn_nsa",
    )(qkv, qkv, qkv, qkv, qkv, r_slc, r_win, e_onehot, pen, ocmp, graw, gsel)


def _bucket(dist):
    n = np.maximum(dist, 0)
    exact = REL_BUCKETS // 2
    nf = np.maximum(n, 1).astype(np.float64)
    large = exact + (np.log(nf / exact) / math.log(REL_MAX_DIST / exact) * (REL_BUCKETS - exact)).astype(np.int64)
    return np.where(n < exact, n, np.minimum(large, REL_BUCKETS - 1))


def _strip_dist(width):
    u = np.arange(width + TQ)
    return (width - TQ) - np.where(u < width, u, u - (width + TQ))


def _strip_rows(tab, width, extra_of_dist):
    d = _strip_dist(width)
    onehot = (_bucket(d)[None, :] == np.arange(REL_BUCKETS)[:, None]).astype(np.float32)
    vals = jnp.dot(tab, jnp.asarray(onehot), precision=lax.Precision.HIGHEST) + extra_of_dist(d)
    return vals.reshape(tab.shape[0] // 2, 2, 1, width + TQ)


def _causal_mask(d):
    return np.where(d >= 0, 0.0, NEG).astype(np.float32)


def _window_mask(d):
    return np.where((d >= 0) & (d < NSA_WINDOW), 0.0, NEG).astype(np.float32)


def _dilation_log_count(d):
    c = ((d >= 0) & (d <= 128)).astype(np.float64)
    c += ((d >= 0) & (d % 4 == 0) & (d <= 512))
    c += ((d >= 0) & (d % 16 == 0) & (d <= 2048))
    return np.where(c > 0, np.log(np.maximum(c, 1.0)), NEG).astype(np.float32)


def _one_hot_blocks(S, block, per_head):
    e = np.zeros((2, S, LANES), np.float32)
    key = np.arange(S)
    for h in range(2):
        e[h, key, h * per_head + key // block] = 1.0
    return jnp.asarray(e, BF16)


_NSA_HEAD_ORDER = np.array([8 * (p // 4) + (p % 4) + 4 * h for p in range(N_PAIRS) for h in range(2)])


def _head_cols(heads):
    return (np.asarray(heads)[:, None] * HEAD_DIM + np.arange(HEAD_DIM)[None, :]).reshape(-1)


def _layer_ab(h, g_mix, w_in, w_out, qn_a, kn_a, qn_b, kn_b, rel_bias, B, S):
    wa = MOBA_HEADS * HEAD_DIM
    sec = [w_in[:, k * wa:(k + 1) * wa] for k in range(6)]
    w = jnp.concatenate([sec[0], sec[3], sec[1], sec[4], sec[2], sec[5]], axis=1).astype(BF16)
    ones = jnp.ones((wa,), F32)
    gain = jnp.concatenate([jnp.tile(qn_a, 8) * ATTN_SCALE, jnp.tile(qn_b, 8) * ATTN_SCALE,
                            jnp.tile(kn_a, 8), jnp.tile(kn_b, 8), ones, ones])
    mode = jnp.concatenate([jnp.ones((4 * wa,), F32), jnp.zeros((2 * wa,), F32)])
    qkv = _project(h, g_mix, w, gain, mode, BF16, 512, True).reshape(B, S, 3 * D_MODEL)

    tab = rel_bias.T
    r_tab = jnp.concatenate([_strip_rows(tab[:MOBA_HEADS], S, _causal_mask),
                             _strip_rows(tab[MOBA_HEADS:], S, _dilation_log_count)])
    o = _attn0(qkv, r_tab, _one_hot_blocks(S, MOBA_BLOCK, S // MOBA_BLOCK))
    return _outproj(o.reshape(B * S, D_MODEL), w_out.astype(BF16), h)


def _layer_nsa(h, g_mix, w_in, w_out, qn, kn_c, kn_s, kn_w, cmp_k, cmp_v, rel_bias, B, S):
    qw = N_HEADS * HEAD_DIM
    kvw = NSA_GROUPS * HEAD_DIM
    order = _NSA_HEAD_ORDER
    w_main = jnp.concatenate([w_in[:, _head_cols(order)], w_in[:, qw:qw + 6 * kvw]], axis=1).astype(BF16)
    ones = jnp.ones((kvw,), F32)
    gain = jnp.concatenate([jnp.tile(qn, N_HEADS) * ATTN_SCALE, ones, ones,
                            jnp.tile(kn_s, NSA_GROUPS), ones, jnp.tile(kn_w, NSA_GROUPS), ones])
    flag = jnp.ones((kvw,), F32)
    mode = jnp.concatenate([jnp.ones((qw,), F32), 0 * flag, 0 * flag, flag, 0 * flag, flag, 0 * flag])
    qkv = _project(h, g_mix, w_main, gain, mode, BF16, 512, True).reshape(B, S, qw + 6 * kvw)

    gcols = np.zeros((LANES,), np.int64)
    gused = np.zeros((LANES,), np.float32)
    gsel = np.zeros((N_PAIRS, 3, LANES, LANES), np.float32)
    for p in range(N_PAIRS):
        for br in range(3):
            for hh in range(2):
                c = 8 * p + 2 * br + hh
                gcols[c] = qw + 6 * kvw + 3 * order[2 * p + hh] + br
                gused[c] = 1.0
                gsel[p, br, c, hh * HEAD_DIM:(hh + 1) * HEAD_DIM] = 1.0
    w_gate = (w_in[:, gcols] * gused).astype(BF16)
    zeros = jnp.zeros((LANES,), F32)
    graw = _project(h, g_mix, w_gate, zeros, zeros, F32, LANES, False).reshape(B, S, LANES)

    def chunked(col0):
        t = qkv[:, :, col0:col0 + kvw].reshape(B, S // NSA_CMP_STRIDE, NSA_CMP_STRIDE, NSA_GROUPS, HEAD_DIM)
        return t.transpose(0, 3, 1, 2, 4).reshape(B, NSA_GROUPS, S // NSA_CMP_STRIDE, NSA_CMP_STRIDE * HEAD_DIM)

    kc = _compress(chunked(qw), *cmp_k, kn_c, True)
    vc = _compress(chunked(qw + kvw), *cmp_v, kn_c, False)

    tab = rel_bias.T[order]
    n_cmp = (S - NSA_CMP_LEN) // NSA_CMP_STRIDE + 1
    t_pos = np.arange(S)[:, None]
    c_idx = np.arange(LANES)[None, :]
    dc = t_pos - (c_idx * NSA_CMP_STRIDE + NSA_CMP_LEN - 1)
    cmp_mask = np.where((dc >= 0) & (c_idx < n_cmp), 0.0, NEG).astype(np.float32)
    bkt = jnp.asarray(_bucket(dc).astype(np.int32))[None]
    bias_c = jnp.broadcast_to(tab[:, 0][:, None, None], (N_HEADS, S, LANES))
    for bb in range(1, REL_BUCKETS):
        bias_c = jnp.where(bkt == bb, tab[:, bb][:, None, None], bias_c)
    bias_c = (bias_c + cmp_mask).reshape(2, 8, S, LANES)
    n_sel = S // NSA_SEL_BLOCK
    cstart = np.arange(LANES) * NSA_CMP_STRIDE
    sstart = np.arange(n_sel) * NSA_SEL_BLOCK
    ovt = np.maximum(np.minimum(cstart[None, :] + NSA_CMP_LEN, sstart[:, None] + NSA_SEL_BLOCK)
                     - np.maximum(cstart[None, :], sstart[:, None]), 0).astype(np.float32)
    ovt[:, n_cmp:] = 0.0
    ocmp, pen = _cmp_attention(qkv, kc, vc, bias_c, jnp.asarray(ovt))

    r_slc = _strip_rows(tab, S, _causal_mask)
    r_win = _strip_rows(tab, NSA_WINDOW + TQ, _window_mask)
    o = _attn1(qkv, r_slc, r_win, _one_hot_blocks(S, NSA_SEL_BLOCK, n_sel), pen, ocmp, graw, jnp.asarray(gsel))
    w_out_p = w_out[_head_cols(order), :].astype(BF16)
    return _outproj(o.reshape(B * S, D_MODEL), w_out_p, h)


def kernel(x, p, rel_bias, norm_mix, norm_ffn, norm_ple, w_ffn_gate, w_ffn_up, w_ffn_down, w_ple_proj, w_ple_gate, w_in_ab, w_out_ab, qn_moba, kn_moba, qn_dil, kn_dil, w_in_nsa, w_out_nsa, qn_nsa, kn_cmp, kn_slc, kn_win, cmp_k_pos, cmp_k_w1, cmp_k_b1, cmp_k_w2, cmp_k_b2, cmp_v_pos, cmp_v_w1, cmp_v_b1, cmp_v_w2, cmp_v_b2):
    B, S, D = x.shape
    depth = p.shape[0]
    h = x.reshape(B * S, D)
    for i in range(depth):
        e = i // 2
        if i % 2 == 0:
            h = _layer_ab(h, norm_mix[i], w_in_ab[e], w_out_ab[e], qn_moba[e], kn_moba[e], qn_dil[e], kn_dil[e],
                          rel_bias, B, S)
        else:
            cmp_k = (cmp_k_pos[e], cmp_k_w1[e], cmp_k_b1[e], cmp_k_w2[e], cmp_k_b2[e])
            cmp_v = (cmp_v_pos[e], cmp_v_w1[e], cmp_v_b1[e], cmp_v_w2[e], cmp_v_b2[e])
            h = _layer_nsa(h, norm_mix[i], w_in_nsa[e], w_out_nsa[e], qn_nsa[e], kn_cmp[e], kn_slc[e], kn_win[e],
                           cmp_k, cmp_v, rel_bias, B, S)
        h = _ffn(h, norm_ffn[i], w_ffn_gate[i].astype(BF16), w_ffn_up[i].astype(BF16), w_ffn_down[i].astype(BF16))
        h = _ple(h, norm_ple[i], w_ple_gate[i].astype(BF16), p[i].reshape(B * S, -1), w_ple_proj[i].astype(BF16))
    return h.reshape(B, S, D)
```

```python
import functools
import math

import numpy as np
import jax
import jax.numpy as jnp
from jax import lax
from jax.experimental import pallas as pl
from jax.experimental.pallas import tpu as pltpu

F32 = jnp.float32
BF16 = jnp.bfloat16

D_MODEL = 1024
HEAD_DIM = 64
N_HEADS = 16
N_PAIRS = N_HEADS // 2
MOBA_HEADS = 8
MOBA_BLOCK = 256
MOBA_TOPK = 3
MOBA_NBLK = 8
NSA_GROUPS = 4
NSA_CMP_LEN = 32
NSA_CMP_STRIDE = 16
NSA_SEL_BLOCK = 64
NSA_SEL_TOPN = 16
NSA_WINDOW = 512
NSA_FORCE = 1.0e6
REL_BUCKETS = 32
REL_MAX_DIST = 2048
RMS_EPS = 1e-6
ATTN_SCALE = HEAD_DIM ** -0.5

LANES = 128
TQ = 256
NEG = -1.0e30
ROW_TILE = 512
PROJ_CHUNK = 256
FFN_SPLIT = 2
VMEM_LIMIT = 48 * 1024 * 1024
POST_VMEM_LIMIT = 56 * 1024 * 1024

NT_DIMS = (((1,), (1,)), ((), ()))


def _cparams(n_axes):
    return pltpu.CompilerParams(dimension_semantics=("arbitrary",) * n_axes,
                                vmem_limit_bytes=VMEM_LIMIT)


def _rms_rows(x, g):
    ms = jnp.mean(x * x, axis=-1, keepdims=True)
    return x * lax.rsqrt(ms + RMS_EPS) * g


def _lane_lo(shape):
    return lax.broadcasted_iota(jnp.int32, shape, len(shape) - 1) < HEAD_DIM


def _resident(shape):
    return pl.BlockSpec(shape, lambda i: (0,) * len(shape), pipeline_mode=pl.Buffered(1))


def _proj_kernel(x_ref, g_ref, w_ref, cg_ref, bd_ref, *rest, norm_chunks, with_extra):
    if with_extra:
        wx_ref, o_ref, ox_ref = rest
    else:
        (o_ref,) = rest
    xn = _rms_rows(x_ref[...], g_ref[...]).astype(BF16)
    bd = bd_ref[...]

    def finish(c, y):
        cols = slice(c * PROJ_CHUNK, (c + 1) * PROJ_CHUNK)
        if norm_chunks[c]:
            y2 = y * y
            hi = y2.astype(BF16)
            lo = (y2 - hi.astype(F32)).astype(BF16)
            ssq = jnp.dot(hi, bd, preferred_element_type=F32) + jnp.dot(lo, bd, preferred_element_type=F32)
            y = y * lax.rsqrt(ssq * (1.0 / HEAD_DIM) + RMS_EPS) * cg_ref[:, cols]
        o_ref[:, cols] = y.astype(o_ref.dtype)

    prev = None
    for c in range(len(norm_chunks)):
        y = jnp.dot(xn, w_ref[:, c * PROJ_CHUNK:(c + 1) * PROJ_CHUNK], preferred_element_type=F32)
        if prev is not None:
            finish(c - 1, prev)
        prev = y
    finish(len(norm_chunks) - 1, prev)
    if with_extra:
        ox_ref[...] = jnp.dot(xn, wx_ref[...], preferred_element_type=F32)


def _project(x, g, w, col_gain, norm_chunks, w_extra=None):
    T, D = x.shape
    N = w.shape[1]
    blk = np.kron(np.eye(PROJ_CHUNK // HEAD_DIM), np.ones((HEAD_DIM, HEAD_DIM))).astype(np.float32)
    row_in = lambda n: pl.BlockSpec((ROW_TILE, n), lambda i: (i, 0))
    in_specs = [row_in(D), _resident((1, D)), _resident((D, N)), _resident((1, N)),
                _resident((PROJ_CHUNK, PROJ_CHUNK))]
    args = [x, g.reshape(1, D), w, col_gain.reshape(1, N), jnp.asarray(blk, BF16)]
    out_specs = [row_in(N)]
    out_shape = [jax.ShapeDtypeStruct((T, N), BF16)]
    if w_extra is not None:
        nx = w_extra.shape[1]
        in_specs.append(_resident((D, nx)))
        args.append(w_extra)
        out_specs.append(row_in(nx))
        out_shape.append(jax.ShapeDtypeStruct((T, nx), F32))
    return pl.pallas_call(
        functools.partial(_proj_kernel, norm_chunks=tuple(norm_chunks), with_extra=w_extra is not None),
        grid=(T // ROW_TILE,),
        in_specs=in_specs,
        out_specs=out_specs,
        out_shape=out_shape,
        compiler_params=_cparams(1),
        name="proj",
    )(*args)


def _sigmoid(z):
    return 1.0 / (1.0 + jnp.exp(-z))


def _post_kernel(o_ref, h_ref, wout_ref, gf_ref, wg_ref, wu_ref, wd_ref, gp_ref, wpg_ref, p_ref, wpp_ref, out_ref):
    h1 = h_ref[...] + jnp.dot(o_ref[...], wout_ref[...], preferred_element_type=F32)
    xn = _rms_rows(h1, gf_ref[...]).astype(BF16)
    h2 = h1
    fh = wg_ref.shape[1]
    for c in range(FFN_SPLIT):
        cols = slice(c * fh // FFN_SPLIT, (c + 1) * fh // FFN_SPLIT)
        a = jnp.dot(xn, wg_ref[:, cols], preferred_element_type=F32)
        u = jnp.dot(xn, wu_ref[:, cols], preferred_element_type=F32)
        act = (a * _sigmoid(a) * u).astype(BF16)
        h2 = h2 + jnp.dot(act, wd_ref[cols, :], preferred_element_type=F32)
    hn = _rms_rows(h2, gp_ref[...]).astype(BF16)
    gate = _sigmoid(jnp.dot(hn, wpg_ref[...], preferred_element_type=F32))
    out_ref[...] = h2 + gate * jnp.dot(p_ref[...].astype(BF16), wpp_ref[...], preferred_element_type=F32)


def _post_attention(o, h, w_out, g_ffn, wg, wu, wd, g_ple, w_pgate, p, w_pproj):
    T, D = h.shape
    Fh = wg.shape[1]
    Pd = p.shape[1]
    row_in = lambda n: pl.BlockSpec((ROW_TILE, n), lambda i: (i, 0))
    return pl.pallas_call(
        _post_kernel,
        grid=(T // ROW_TILE,),
        in_specs=[row_in(D), row_in(D), _resident((D, D)), _resident((1, D)), _resident((D, Fh)),
                  _resident((D, Fh)), _resident((Fh, D)), _resident((1, D)), _resident((D, D)),
                  row_in(Pd), _resident((Pd, D))],
        out_specs=row_in(D),
        out_shape=jax.ShapeDtypeStruct((T, D), F32),
        compiler_params=pltpu.CompilerParams(dimension_semantics=("arbitrary",), vmem_limit_bytes=POST_VMEM_LIMIT),
        name="post_attention",
    )(o, h, w_out, g_ffn.reshape(1, D), wg, wu, wd, g_ple.reshape(1, D), w_pgate, p, w_pproj)


def _strip(r_row, width):
    x = jnp.broadcast_to(r_row, (TQ, r_row.shape[-1]))
    return pltpu.roll(x, 0, 1, stride=1, stride_axis=0)[:, :width]


def _attend(q_op, k_rows, strip_cols, v_rows):
    s = lax.dot_general(q_op, k_rows, NT_DIMS, preferred_element_type=F32) + strip_cols
    m = jnp.max(s, axis=1, keepdims=True)
    p = jnp.exp(s - m).astype(BF16)
    oa = jnp.dot(p, v_rows, preferred_element_type=F32)
    return oa[:, :LANES] / oa[:, LANES:]


def _split_heads(q):
    lo = _lane_lo(q.shape)
    zero = jnp.zeros_like(q)
    return jnp.where(lo, q, zero), jnp.where(lo, zero, q)


def _fill_value_ones(vaug_ref):
    vaug_ref[:, LANES:] = jnp.ones((vaug_ref.shape[0], LANES), vaug_ref.dtype)


def _moba_block_means(k_ref):
    shape = (2 * MOBA_NBLK, LANES)
    lo = _lane_lo(shape)
    row = lax.broadcasted_iota(jnp.int32, shape, 0)
    kmt = jnp.zeros(shape, F32)
    for n in range(MOBA_NBLK):
        mean = jnp.mean(k_ref[0, n * MOBA_BLOCK:(n + 1) * MOBA_BLOCK, :].astype(F32), axis=0, keepdims=True)
        kmt = jnp.where(row == n, jnp.where(lo, mean, 0.0), kmt)
        kmt = jnp.where(row == MOBA_NBLK + n, jnp.where(lo, 0.0, mean), kmt)
    return kmt


def _moba_penalty(q, kmt, i):
    nb = MOBA_NBLK
    g = lax.dot_general(kmt, q.astype(F32), NT_DIMS,
                        precision=lax.Precision.HIGHEST, preferred_element_type=F32)
    row = lax.broadcasted_iota(jnp.int32, g.shape, 0)
    n = row & (nb - 1)
    rank = jnp.zeros(g.shape, F32)
    for m in range(i):
        gm = jnp.where(row < nb, g[m:m + 1, :], g[nb + m:nb + m + 1, :])
        tie = jnp.where(n > m, 1.0, 0.0)
        rank = rank + jnp.where(gm > g, 1.0, jnp.where(gm == g, tie, 0.0))
    keep = jnp.where(n < i, jnp.where(rank < MOBA_TOPK, 1.0, 0.0), jnp.where(n == i, 1.0, 0.0))
    keep = jnp.concatenate([keep, jnp.zeros((LANES - 2 * nb, TQ), F32)], axis=0).T
    lane = lax.broadcasted_iota(jnp.int32, keep.shape, 1)
    return jnp.where(lane < 2 * nb, (keep - 1.0) * (-NEG), 0.0)


def _attn0_kernel(q_ref, k_ref, v_ref, r_ref, e_ref, o_ref, strip_ref, kaug_ref, vaug_ref):
    p = pl.program_id(0)
    b = pl.program_id(1)
    S = q_ref.shape[1]

    @pl.when(b == 0)
    def _():
        for h in range(2):
            strip_ref[h] = _strip(r_ref[0, h], S)

    @pl.when(jnp.logical_and(p == 0, b == 0))
    def _():
        for h in range(2):
            kaug_ref[h, :, LANES:] = e_ref[h]
        _fill_value_ones(vaug_ref)

    k = k_ref[0]
    for h in range(2):
        kaug_ref[h, :, :LANES] = k
    vaug_ref[:, :LANES] = v_ref[0]
    kmt = _moba_block_means(k_ref)
    is_moba = p < MOBA_HEADS // 2
    lo = _lane_lo((TQ, LANES))
    for i in range(S // TQ):
        w = (i + 1) * TQ
        q = q_ref[0, i * TQ:w, :]
        pen = jnp.where(is_moba, _moba_penalty(q, kmt, i), 0.0).astype(BF16)
        outs = [_attend(jnp.concatenate([qh, pen], axis=1), kaug_ref[h, 0:w, :], strip_ref[h, :, S - w:S],
                        vaug_ref[0:w, :]) for h, qh in enumerate(_split_heads(q))]
        o_ref[0, i * TQ:w, :] = jnp.where(lo, outs[0], outs[1]).astype(o_ref.dtype)


def _attn0(qkv, r_tab, e_onehot):
    B, S, _ = qkv.shape
    L = r_tab.shape[-1]
    return pl.pallas_call(
        _attn0_kernel,
        grid=(N_PAIRS, B),
        in_specs=[
            pl.BlockSpec((1, S, LANES), lambda p, b: (b, 0, p)),
            pl.BlockSpec((1, S, LANES), lambda p, b: (b, 0, N_PAIRS + p)),
            pl.BlockSpec((1, S, LANES), lambda p, b: (b, 0, 2 * N_PAIRS + p)),
            pl.BlockSpec((1, 2, 1, L), lambda p, b: (p, 0, 0, 0)),
            pl.BlockSpec((2, S, LANES), lambda p, b: (0, 0, 0)),
        ],
        out_specs=pl.BlockSpec((1, S, LANES), lambda p, b: (b, 0, p)),
        out_shape=jax.ShapeDtypeStruct((B, S, D_MODEL), BF16),
        scratch_shapes=[pltpu.VMEM((2, TQ, S), F32), pltpu.VMEM((2, S, 2 * LANES), BF16),
                        pltpu.VMEM((S, 2 * LANES), BF16)],
        compiler_params=_cparams(2),
        name="attn_moba_dilated",
    )(qkv, qkv, qkv, r_tab, e_onehot)


def _compress_kernel(a_ref, pos_ref, w1_ref, b1_ref, w2_ref, b2_ref, gain_ref, o_ref, *, normed):
    y = None
    for g in range(2):
        a = a_ref[0, g].astype(F32)
        first = jnp.dot((a + pos_ref[0:1, :]).astype(BF16), w1_ref[0], preferred_element_type=F32)
        second = jnp.dot((a + pos_ref[1:2, :]).astype(BF16), w1_ref[1], preferred_element_type=F32)
        hid = first + pltpu.roll(second, LANES - 1, 0) + b1_ref[...]
        cdf = 0.5 * (1.0 + jnp.tanh(math.sqrt(2.0 / math.pi) * (hid + 0.044715 * (hid * hid * hid))))
        yg = jnp.dot((hid * cdf).astype(BF16), w2_ref[g], preferred_element_type=F32)
        y = yg if y is None else y + yg
    y = y + b2_ref[...]
    if normed:
        lo = _lane_lo(y.shape)
        y2 = y * y
        s_lo = jnp.sum(jnp.where(lo, y2, 0.0), axis=1, keepdims=True)
        s_hi = jnp.sum(jnp.where(lo, 0.0, y2), axis=1, keepdims=True)
        ms = jnp.where(lo, s_lo, s_hi) * (1.0 / HEAD_DIM)
        y = y * lax.rsqrt(ms + RMS_EPS) * gain_ref[...]
    o_ref[0, 0] = y.astype(o_ref.dtype)


def _compress(a, pos, w1, b1, w2, b2, gain, normed):
    B = a.shape[0]
    hid = w1.shape[1]
    half = NSA_CMP_STRIDE * HEAD_DIM
    zeros = jnp.zeros_like(w2)
    w2p = jnp.stack([jnp.concatenate([w2, zeros], axis=1), jnp.concatenate([zeros, w2], axis=1)]).astype(BF16)
    return pl.pallas_call(
        functools.partial(_compress_kernel, normed=normed),
        grid=(B, 2),
        in_specs=[
            pl.BlockSpec((1, 2, LANES, half), lambda b, m: (b, m, 0, 0)),
            pl.BlockSpec((2, half), lambda b, m: (0, 0)),
            pl.BlockSpec((2, half, hid), lambda b, m: (0, 0, 0)),
            pl.BlockSpec((1, hid), lambda b, m: (0, 0)),
            pl.BlockSpec((2, hid, LANES), lambda b, m: (0, 0, 0)),
            pl.BlockSpec((1, LANES), lambda b, m: (0, 0)),
            pl.BlockSpec((1, LANES), lambda b, m: (0, 0)),
        ],
        out_specs=pl.BlockSpec((1, 1, LANES, LANES), lambda b, m: (b, m, 0, 0)),
        out_shape=jax.ShapeDtypeStruct((B, 2, LANES, LANES), BF16),
        compiler_params=_cparams(2),
        name="nsa_compress",
    )(a, pos.reshape(2, half), w1.reshape(2, half, hid).astype(BF16), b1.reshape(1, hid), w2p,
      jnp.tile(b2, 2).reshape(1, LANES), jnp.tile(gain, 2).reshape(1, LANES))


def _cmp_kernel(q_ref, kc_ref, vc_ref, bias_ref, ovt_ref, o_ref, pen_ref):
    i = pl.program_id(2)
    kc = kc_ref[0, 0]
    vc = vc_ref[0, 0]
    lo = _lane_lo((TQ, LANES))
    psum = [jnp.zeros((TQ, LANES), F32), jnp.zeros((TQ, LANES), F32)]
    for r in range(4):
        heads = _split_heads(q_ref[0, :, r * LANES:(r + 1) * LANES])
        outs = []
        for h in range(2):
            s = lax.dot_general(heads[h], kc, NT_DIMS, preferred_element_type=F32) + bias_ref[0, 2 * r + h]
            m = jnp.max(s, axis=1, keepdims=True)
            e = jnp.where(s > 0.5 * NEG, jnp.exp(s - m), 0.0)
            l = jnp.sum(e, axis=1, keepdims=True)
            pr = e / jnp.where(l > 0.0, l, 1.0)
            psum[h] = psum[h] + pr
            outs.append(jnp.dot(pr.astype(BF16), vc, preferred_element_type=F32))
        o_ref[0, :, r * LANES:(r + 1) * LANES] = jnp.where(lo, outs[0], outs[1])

    n_sel = ovt_ref.shape[0]
    blk = lax.broadcasted_iota(jnp.int32, (n_sel, TQ), 0)
    t = i * TQ + lax.broadcasted_iota(jnp.int32, (n_sel, TQ), 1)
    cur = lax.shift_right_logical(t, int(math.log2(NSA_SEL_BLOCK)))
    keeps = []
    for h in range(2):
        imp = lax.dot_general(ovt_ref[...], psum[h], NT_DIMS,
                              precision=lax.Precision.HIGHEST, preferred_element_type=F32)
        forced = jnp.where(blk == 0, 1.0, jnp.where(blk == cur, 1.0, jnp.where(blk == cur - 1, 1.0, 0.0)))
        imp = jnp.where(blk <= cur, imp + forced * NSA_FORCE, -jnp.inf)
        rank = jnp.zeros(imp.shape, F32)
        for m in range(n_sel):
            im = imp[m:m + 1, :]
            tie = jnp.where(blk > m, 1.0, 0.0)
            rank = rank + jnp.where(im > imp, 1.0, jnp.where(im == imp, tie, 0.0))
        keeps.append(jnp.where(rank < NSA_SEL_TOPN, 1.0, 0.0))
    keep = jnp.concatenate(keeps + [jnp.zeros((LANES - 2 * n_sel, TQ), F32)], axis=0).T
    lane = lax.broadcasted_iota(jnp.int32, keep.shape, 1)
    pen_ref[0, 0] = jnp.where(lane < 2 * n_sel, (keep - 1.0) * (-NEG), 0.0).astype(pen_ref.dtype)


def _cmp_attention(qkv, kc, vc, bias, ovt):
    B, S, _ = qkv.shape
    nq = S // TQ
    n_sel = ovt.shape[0]
    return pl.pallas_call(
        _cmp_kernel,
        grid=(B, 2, nq),
        in_specs=[
            pl.BlockSpec((1, TQ, 4 * LANES), lambda b, m, i: (b, i, m)),
            pl.BlockSpec((1, 1, LANES, LANES), lambda b, m, i: (b, m, 0, 0)),
            pl.BlockSpec((1, 1, LANES, LANES), lambda b, m, i: (b, m, 0, 0)),
            pl.BlockSpec((1, 8, TQ, LANES), lambda b, m, i: (m, 0, i, 0)),
            pl.BlockSpec((n_sel, LANES), lambda b, m, i: (0, 0)),
        ],
        out_specs=[
            pl.BlockSpec((1, TQ, 4 * LANES), lambda b, m, i: (b, i, m)),
            pl.BlockSpec((1, 1, TQ, LANES), lambda b, m, i: (b, m, i, 0)),
        ],
        out_shape=[jax.ShapeDtypeStruct((B, S, D_MODEL), F32),
                   jax.ShapeDtypeStruct((B, 2, S, LANES), BF16)],
        compiler_params=_cparams(3),
        name="nsa_compressed_select",
    )(qkv, kc, vc, bias, ovt)


def _attn1_kernel(q_ref, ks_ref, vs_ref, kw_ref, vw_ref, rs_ref, rw_ref, e_ref, pen_ref, ocmp_ref, graw_ref,
                  gsel_ref, o_ref, strip_s_ref, strip_w_ref, kaug_ref, vsaug_ref, vwaug_ref):
    p = pl.program_id(0)
    b = pl.program_id(1)
    S = q_ref.shape[1]
    w_win = strip_w_ref.shape[2]

    @pl.when(b == 0)
    def _():
        for h in range(2):
            strip_s_ref[h] = _strip(rs_ref[0, h], S)
            strip_w_ref[h] = _strip(rw_ref[0, h], w_win)

    @pl.when(jnp.logical_and(p == 0, b == 0))
    def _():
        for h in range(2):
            kaug_ref[h, :, LANES:] = e_ref[h]
        _fill_value_ones(vsaug_ref)
        _fill_value_ones(vwaug_ref)

    ks = ks_ref[0]
    for h in range(2):
        kaug_ref[h, :, :LANES] = ks
    vsaug_ref[:, :LANES] = vs_ref[0]
    vwaug_ref[:, :LANES] = vw_ref[0]
    lo = _lane_lo((TQ, LANES))
    for i in range(S // TQ):
        w = (i + 1) * TQ
        rows = slice(i * TQ, w)
        heads = _split_heads(q_ref[0, rows, :])
        pen = pen_ref[0, 0, rows, :]
        slc = [_attend(jnp.concatenate([qh, pen], axis=1), kaug_ref[h, 0:w, :], strip_s_ref[h, :, S - w:S],
                       vsaug_ref[0:w, :]) for h, qh in enumerate(heads)]
        ww = min(w, w_win)
        win = [_attend(qh, kw_ref[0, w - ww:w, :], strip_w_ref[h, :, w_win - ww:w_win], vwaug_ref[w - ww:w, :])
               for h, qh in enumerate(heads)]
        z = graw_ref[0, rows, :]
        sig = 1.0 / (1.0 + jnp.exp(-z))
        gates = [jnp.dot(sig, gsel_ref[0, br], precision=lax.Precision.HIGHEST, preferred_element_type=F32)
                 for br in range(3)]
        out = (gates[0] * ocmp_ref[0, rows, :] + gates[1] * jnp.where(lo, slc[0], slc[1])
               + gates[2] * jnp.where(lo, win[0], win[1]))
        o_ref[0, rows, :] = out.astype(o_ref.dtype)


def _attn1(qkv, r_slc, r_win, e_onehot, pen, ocmp, graw, gsel):
    B, S, _ = qkv.shape
    kv0 = D_MODEL // LANES + 4
    w_win = r_win.shape[-1] - TQ
    return pl.pallas_call(
        _attn1_kernel,
        grid=(N_PAIRS, B),
        in_specs=[
            pl.BlockSpec((1, S, LANES), lambda p, b: (b, 0, p)),
            pl.BlockSpec((1, S, LANES), lambda p, b: (b, 0, kv0 + p // 4)),
            pl.BlockSpec((1, S, LANES), lambda p, b: (b, 0, kv0 + 2 + p // 4)),
            pl.BlockSpec((1, S, LANES), lambda p, b: (b, 0, kv0 + 4 + p // 4)),
            pl.BlockSpec((1, S, LANES), lambda p, b: (b, 0, kv0 + 6 + p // 4)),
            pl.BlockSpec((1, 2, 1, r_slc.shape[-1]), lambda p, b: (p, 0, 0, 0)),
            pl.BlockSpec((1, 2, 1, r_win.shape[-1]), lambda p, b: (p, 0, 0, 0)),
            pl.BlockSpec((2, S, LANES), lambda p, b: (0, 0, 0)),
            pl.BlockSpec((1, 1, S, LANES), lambda p, b: (b, p // 4, 0, 0)),
            pl.BlockSpec((1, S, LANES), lambda p, b: (b, 0, p)),
            pl.BlockSpec((1, S, LANES), lambda p, b: (b, 0, 0)),
            pl.BlockSpec((1, 3, LANES, LANES), lambda p, b: (p, 0, 0, 0)),
        ],
        out_specs=pl.BlockSpec((1, S, LANES), lambda p, b: (b, 0, p)),
        out_shape=jax.ShapeDtypeStruct((B, S, D_MODEL), BF16),
        scratch_shapes=[pltpu.VMEM((2, TQ, S), F32), pltpu.VMEM((2, TQ, w_win), F32),
                        pltpu.VMEM((2, S, 2 * LANES), BF16), pltpu.VMEM((S, 2 * LANES), BF16),
                        pltpu.VMEM((S, 2 * LANES), BF16)],
        compiler_params=_cparams(2),
        name="attn_nsa",
    )(qkv, qkv, qkv, qkv, qkv, r_slc, r_win, e_onehot, pen, ocmp, graw, gsel)


def _bucket(dist):
    n = np.maximum(dist, 0)
    exact = REL_BUCKETS // 2
    nf = np.maximum(n, 1).astype(np.float64)
    large = exact + (np.log(nf / exact) / math.log(REL_MAX_DIST / exact) * (REL_BUCKETS - exact)).astype(np.int64)
    return np.where(n < exact, n, np.minimum(large, REL_BUCKETS - 1))


def _strip_dist(width):
    u = np.arange(width + TQ)
    return (width - TQ) - np.where(u < width, u, u - (width + TQ))


def _strip_rows(tab, width, extra_of_dist):
    d = _strip_dist(width)
    onehot = (_bucket(d)[None, :] == np.arange(REL_BUCKETS)[:, None]).astype(np.float32)
    vals = jnp.dot(tab, jnp.asarray(onehot), precision=lax.Precision.HIGHEST) + extra_of_dist(d)
    return vals.reshape(tab.shape[0] // 2, 2, 1, width + TQ)


def _causal_mask(d):
    return np.where(d >= 0, 0.0, NEG).astype(np.float32)


def _window_mask(d):
    return np.where((d >= 0) & (d < NSA_WINDOW), 0.0, NEG).astype(np.float32)


def _dilation_log_count(d):
    c = ((d >= 0) & (d <= 128)).astype(np.float64)
    c += ((d >= 0) & (d % 4 == 0) & (d <= 512))
    c += ((d >= 0) & (d % 16 == 0) & (d <= 2048))
    return np.where(c > 0, np.log(np.maximum(c, 1.0)), NEG).astype(np.float32)


def _one_hot_blocks(S, block, per_head):
    e = np.zeros((2, S, LANES), np.float32)
    key = np.arange(S)
    for h in range(2):
        e[h, key, h * per_head + key // block] = 1.0
    return jnp.asarray(e, BF16)


_NSA_HEAD_ORDER = np.array([8 * (p // 4) + (p % 4) + 4 * h for p in range(N_PAIRS) for h in range(2)])


def _head_cols(heads):
    return (np.asarray(heads)[:, None] * HEAD_DIM + np.arange(HEAD_DIM)[None, :]).reshape(-1)


def _mixer_ab(h, g_mix, w_in, qn_a, kn_a, qn_b, kn_b, rel_bias, B, S):
    wa = MOBA_HEADS * HEAD_DIM
    sec = [w_in[:, k * wa:(k + 1) * wa] for k in range(6)]
    w = jnp.concatenate([sec[0], sec[3], sec[1], sec[4], sec[2], sec[5]], axis=1).astype(BF16)
    ones = jnp.ones((wa,), F32)
    gain = jnp.concatenate([jnp.tile(qn_a, 8) * ATTN_SCALE, jnp.tile(qn_b, 8) * ATTN_SCALE,
                            jnp.tile(kn_a, 8), jnp.tile(kn_b, 8), ones, ones])
    per_sec = wa // PROJ_CHUNK
    (qkv,) = _project(h, g_mix, w, gain, [True] * (4 * per_sec) + [False] * (2 * per_sec))
    qkv = qkv.reshape(B, S, 3 * D_MODEL)

    tab = rel_bias.T
    r_tab = jnp.concatenate([_strip_rows(tab[:MOBA_HEADS], S, _causal_mask),
                             _strip_rows(tab[MOBA_HEADS:], S, _dilation_log_count)])
    o = _attn0(qkv, r_tab, _one_hot_blocks(S, MOBA_BLOCK, S // MOBA_BLOCK))
    return o.reshape(B * S, D_MODEL)


def _mixer_nsa(h, g_mix, w_in, qn, kn_c, kn_s, kn_w, cmp_k, cmp_v, rel_bias, B, S):
    qw = N_HEADS * HEAD_DIM
    kvw = NSA_GROUPS * HEAD_DIM
    order = _NSA_HEAD_ORDER
    w_main = jnp.concatenate([w_in[:, _head_cols(order)], w_in[:, qw:qw + 6 * kvw]], axis=1).astype(BF16)
    ones = jnp.ones((kvw,), F32)
    gain = jnp.concatenate([jnp.tile(qn, N_HEADS) * ATTN_SCALE, ones, ones,
                            jnp.tile(kn_s, NSA_GROUPS), ones, jnp.tile(kn_w, NSA_GROUPS), ones])
    assert kvw == PROJ_CHUNK
    norm_chunks = [True] * (qw // PROJ_CHUNK) + [False, False, True, False, True, False]

    gcols = np.zeros((LANES,), np.int64)
    gused = np.zeros((LANES,), np.float32)
    gsel = np.zeros((N_PAIRS, 3, LANES, LANES), np.float32)
    for p in range(N_PAIRS):
        for br in range(3):
            for hh in range(2):
                c = 8 * p + 2 * br + hh
                gcols[c] = qw + 6 * kvw + 3 * order[2 * p + hh] + br
                gused[c] = 1.0
                gsel[p, br, c, hh * HEAD_DIM:(hh + 1) * HEAD_DIM] = 1.0
    w_gate = (w_in[:, gcols] * gused).astype(BF16)
    qkv, graw = _project(h, g_mix, w_main, gain, norm_chunks, w_gate)
    qkv = qkv.reshape(B, S, qw + 6 * kvw)
    graw = graw.reshape(B, S, LANES)

    def chunked(col0):
        t = qkv[:, :, col0:col0 + kvw].reshape(B, S // NSA_CMP_STRIDE, NSA_CMP_STRIDE, NSA_GROUPS, HEAD_DIM)
        return t.transpose(0, 3, 1, 2, 4).reshape(B, NSA_GROUPS, S // NSA_CMP_STRIDE, NSA_CMP_STRIDE * HEAD_DIM)

    kc = _compress(chunked(qw), *cmp_k, kn_c, True)
    vc = _compress(chunked(qw + kvw), *cmp_v, kn_c, False)

    tab = rel_bias.T[order]
    n_cmp = (S - NSA_CMP_LEN) // NSA_CMP_STRIDE + 1
    t_pos = np.arange(S)[:, None]
    c_idx = np.arange(LANES)[None, :]
    dc = t_pos - (c_idx * NSA_CMP_STRIDE + NSA_CMP_LEN - 1)
    cmp_mask = np.where((dc >= 0) & (c_idx < n_cmp), 0.0, NEG).astype(np.float32)
    bkt = jnp.asarray(_bucket(dc).astype(np.int32))[None]
    bias_c = jnp.broadcast_to(tab[:, 0][:, None, None], (N_HEADS, S, LANES))
    for bb in range(1, REL_BUCKETS):
        bias_c = jnp.where(bkt == bb, tab[:, bb][:, None, None], bias_c)
    bias_c = (bias_c + cmp_mask).reshape(2, 8, S, LANES)
    n_sel = S // NSA_SEL_BLOCK
    cstart = np.arange(LANES) * NSA_CMP_STRIDE
    sstart = np.arange(n_sel) * NSA_SEL_BLOCK
    ovt = np.maximum(np.minimum(cstart[None, :] + NSA_CMP_LEN, sstart[:, None] + NSA_SEL_BLOCK)
                     - np.maximum(cstart[None, :], sstart[:, None]), 0).astype(np.float32)
    ovt[:, n_cmp:] = 0.0
    ocmp, pen = _cmp_attention(qkv, kc, vc, bias_c, jnp.asarray(ovt))

    r_slc = _strip_rows(tab, S, _causal_mask)
    r_win = _strip_rows(tab, NSA_WINDOW + TQ, _window_mask)
    o = _attn1(qkv, r_slc, r_win, _one_hot_blocks(S, NSA_SEL_BLOCK, n_sel), pen, ocmp, graw, jnp.asarray(gsel))
    return o.reshape(B * S, D_MODEL)


def kernel(x, p, rel_bias, norm_mix, norm_ffn, norm_ple, w_ffn_gate, w_ffn_up, w_ffn_down, w_ple_proj, w_ple_gate, w_in_ab, w_out_ab, qn_moba, kn_moba, qn_dil, kn_dil, w_in_nsa, w_out_nsa, qn_nsa, kn_cmp, kn_slc, kn_win, cmp_k_pos, cmp_k_w1, cmp_k_b1, cmp_k_w2, cmp_k_b2, cmp_v_pos, cmp_v_w1, cmp_v_b1, cmp_v_w2, cmp_v_b2):
    B, S, D = x.shape
    depth = p.shape[0]
    h = x.reshape(B * S, D)
    for i in range(depth):
        e = i // 2
        if i % 2 == 0:
            o = _mixer_ab(h, norm_mix[i], w_in_ab[e], qn_moba[e], kn_moba[e], qn_dil[e], kn_dil[e], rel_bias, B, S)
            w_out = w_out_ab[e]
        else:
            cmp_k = (cmp_k_pos[e], cmp_k_w1[e], cmp_k_b1[e], cmp_k_w2[e], cmp_k_b2[e])
            cmp_v = (cmp_v_pos[e], cmp_v_w1[e], cmp_v_b1[e], cmp_v_w2[e], cmp_v_b2[e])
            o = _mixer_nsa(h, norm_mix[i], w_in_nsa[e], qn_nsa[e], kn_cmp[e], kn_slc[e], kn_win[e],
                           cmp_k, cmp_v, rel_bias, B, S)
            w_out = w_out_nsa[e][_head_cols(_NSA_HEAD_ORDER), :]
        h = _post_attention(o, h, w_out.astype(BF16), norm_ffn[i], w_ffn_gate[i].astype(BF16),
                            w_ffn_up[i].astype(BF16), w_ffn_down[i].astype(BF16), norm_ple[i],
                            w_ple_gate[i].astype(BF16), p[i].reshape(B * S, -1), w_ple_proj[i].astype(BF16))
    return h.reshape(B, S, D)
```

```python
import functools
import math

import numpy as np
import jax
import jax.numpy as jnp
from jax import lax
from jax.experimental import pallas as pl
from jax.experimental.pallas import tpu as pltpu

F32 = jnp.float32
BF16 = jnp.bfloat16

D_MODEL = 1024
HEAD_DIM = 64
N_HEADS = 16
N_PAIRS = N_HEADS // 2
MOBA_HEADS = 8
MOBA_BLOCK = 256
MOBA_TOPK = 3
MOBA_NBLK = 8
NSA_GROUPS = 4
NSA_CMP_LEN = 32
NSA_CMP_STRIDE = 16
NSA_SEL_BLOCK = 64
NSA_SEL_TOPN = 16
NSA_WINDOW = 512
NSA_FORCE = 1.0e6
REL_BUCKETS = 32
REL_MAX_DIST = 2048
RMS_EPS = 1e-6
ATTN_SCALE = HEAD_DIM ** -0.5

LANES = 128
TQ = 256
NEG = -1.0e30
ROW_TILE = 512
PROJ_CHUNK = 256
FFN_SPLIT = 2
VMEM_LIMIT = 48 * 1024 * 1024
POST_VMEM_LIMIT = 56 * 1024 * 1024
ATTN_VMEM_LIMIT = 56 * 1024 * 1024

NT_DIMS = (((1,), (1,)), ((), ()))


def _cparams(n_axes):
    return pltpu.CompilerParams(dimension_semantics=("arbitrary",) * n_axes,
                                vmem_limit_bytes=VMEM_LIMIT)


def _rms_rows(x, g):
    ms = jnp.mean(x * x, axis=-1, keepdims=True)
    return x * lax.rsqrt(ms + RMS_EPS) * g


def _lane_lo(shape):
    return lax.broadcasted_iota(jnp.int32, shape, len(shape) - 1) < HEAD_DIM


def _resident(shape):
    return pl.BlockSpec(shape, lambda i: (0,) * len(shape), pipeline_mode=pl.Buffered(1))


def _proj_kernel(x_ref, g_ref, w_ref, cg_ref, bd_ref, *rest, norm_chunks, with_extra):
    if with_extra:
        wx_ref, o_ref, ox_ref = rest
    else:
        (o_ref,) = rest
    xn = _rms_rows(x_ref[...], g_ref[...]).astype(BF16)
    bd = bd_ref[...]

    def finish(c, y):
        cols = slice(c * PROJ_CHUNK, (c + 1) * PROJ_CHUNK)
        if norm_chunks[c]:
            y2 = y * y
            hi = y2.astype(BF16)
            lo = (y2 - hi.astype(F32)).astype(BF16)
            ssq = jnp.dot(hi, bd, preferred_element_type=F32) + jnp.dot(lo, bd, preferred_element_type=F32)
            y = y * lax.rsqrt(ssq * (1.0 / HEAD_DIM) + RMS_EPS) * cg_ref[:, cols]
        o_ref[:, cols] = y.astype(o_ref.dtype)

    prev = None
    for c in range(len(norm_chunks)):
        y = jnp.dot(xn, w_ref[:, c * PROJ_CHUNK:(c + 1) * PROJ_CHUNK], preferred_element_type=F32)
        if prev is not None:
            finish(c - 1, prev)
        prev = y
    finish(len(norm_chunks) - 1, prev)
    if with_extra:
        ox_ref[...] = jnp.dot(xn, wx_ref[...], preferred_element_type=F32)


def _project(x, g, w, col_gain, norm_chunks, w_extra=None):
    T, D = x.shape
    N = w.shape[1]
    blk = np.kron(np.eye(PROJ_CHUNK // HEAD_DIM), np.ones((HEAD_DIM, HEAD_DIM))).astype(np.float32)
    row_in = lambda n: pl.BlockSpec((ROW_TILE, n), lambda i: (i, 0))
    in_specs = [row_in(D), _resident((1, D)), _resident((D, N)), _resident((1, N)),
                _resident((PROJ_CHUNK, PROJ_CHUNK))]
    args = [x, g.reshape(1, D), w, col_gain.reshape(1, N), jnp.asarray(blk, BF16)]
    out_specs = [row_in(N)]
    out_shape = [jax.ShapeDtypeStruct((T, N), BF16)]
    if w_extra is not None:
        nx = w_extra.shape[1]
        in_specs.append(_resident((D, nx)))
        args.append(w_extra)
        out_specs.append(row_in(nx))
        out_shape.append(jax.ShapeDtypeStruct((T, nx), F32))
    return pl.pallas_call(
        functools.partial(_proj_kernel, norm_chunks=tuple(norm_chunks), with_extra=w_extra is not None),
        grid=(T // ROW_TILE,),
        in_specs=in_specs,
        out_specs=out_specs,
        out_shape=out_shape,
        compiler_params=_cparams(1),
        name="proj",
    )(*args)


def _sigmoid(z):
    return 1.0 / (1.0 + jnp.exp(-z))


def _post_kernel(o_ref, h_ref, wout_ref, gf_ref, wg_ref, wu_ref, wd_ref, gp_ref, wpg_ref, p_ref, wpp_ref, out_ref):
    h1 = h_ref[...] + jnp.dot(o_ref[...], wout_ref[...], preferred_element_type=F32)
    xn = _rms_rows(h1, gf_ref[...]).astype(BF16)
    h2 = h1
    fh = wg_ref.shape[1]
    for c in range(FFN_SPLIT):
        cols = slice(c * fh // FFN_SPLIT, (c + 1) * fh // FFN_SPLIT)
        a = jnp.dot(xn, wg_ref[:, cols], preferred_element_type=F32)
        u = jnp.dot(xn, wu_ref[:, cols], preferred_element_type=F32)
        act = (a * _sigmoid(a) * u).astype(BF16)
        h2 = h2 + jnp.dot(act, wd_ref[cols, :], preferred_element_type=F32)
    hn = _rms_rows(h2, gp_ref[...]).astype(BF16)
    gate = _sigmoid(jnp.dot(hn, wpg_ref[...], preferred_element_type=F32))
    out_ref[...] = h2 + gate * jnp.dot(p_ref[...].astype(BF16), wpp_ref[...], preferred_element_type=F32)


def _post_attention(o, h, w_out, g_ffn, wg, wu, wd, g_ple, w_pgate, p, w_pproj):
    T, D = h.shape
    Fh = wg.shape[1]
    Pd = p.shape[1]
    row_in = lambda n: pl.BlockSpec((ROW_TILE, n), lambda i: (i, 0))
    return pl.pallas_call(
        _post_kernel,
        grid=(T // ROW_TILE,),
        in_specs=[row_in(D), row_in(D), _resident((D, D)), _resident((1, D)), _resident((D, Fh)),
                  _resident((D, Fh)), _resident((Fh, D)), _resident((1, D)), _resident((D, D)),
                  row_in(Pd), _resident((Pd, D))],
        out_specs=row_in(D),
        out_shape=jax.ShapeDtypeStruct((T, D), F32),
        compiler_params=pltpu.CompilerParams(dimension_semantics=("arbitrary",), vmem_limit_bytes=POST_VMEM_LIMIT),
        name="post_attention",
    )(o, h, w_out, g_ffn.reshape(1, D), wg, wu, wd, g_ple.reshape(1, D), w_pgate, p, w_pproj)


def _fill_strips(strip_ref, r_ref):
    for h in range(2):
        for o in range(r_ref.shape[2]):
            x = jnp.broadcast_to(r_ref[0, h, o:o + 1, :], (TQ, 2 * TQ))
            strip_ref[h, o * TQ:(o + 1) * TQ, :] = pltpu.roll(x, 0, 1, stride=1, stride_axis=0)[:, :TQ]


def _tile_groups(nblk, span):
    cnt = [min(span, nblk - j) for j in range(nblk)]
    off = [0]
    for c in cnt:
        off.append(off[-1] + c * TQ)
    return cnt, off


def _tile_rows(nblk, span):
    return _tile_groups(nblk, span)[1][-1]


def _key_major_attention(q_rows, k_rows, v_rows, strip_ref, s_ref, p_ref, acc_ref, nblk, span):
    cnt, off = _tile_groups(nblk, span)
    tot = off[-1]

    def scores(h, j):
        n = cnt[j] * TQ
        s = lax.dot_general(q_rows(h, j * TQ, j * TQ + n), k_rows(h, j), NT_DIMS, preferred_element_type=F32)
        s_ref[off[j]:off[j] + n, :] = s + strip_ref[h, 0:n, :]

    def probs(h, i):
        rows = [off[j] + (i - j) * TQ for j in range(max(0, i - span + 1), i + 1)]
        mx = s_ref[rows[0]:rows[0] + TQ, :]
        for r in rows[1:]:
            mx = jnp.maximum(mx, s_ref[r:r + TQ, :])
        m = jnp.max(mx, axis=1, keepdims=True)
        for r in rows:
            p_ref[h * tot + r:h * tot + r + TQ, :] = jnp.exp(s_ref[r:r + TQ, :] - m).astype(BF16)

    def values(h, j):
        n = cnt[j] * TQ
        out = jnp.dot(p_ref[h * tot + off[j]:h * tot + off[j] + n, :], v_rows(j), preferred_element_type=F32)
        for blk in range(cnt[j]):
            i = j + blk
            part = out[blk * TQ:(blk + 1) * TQ, :]
            if j == max(0, i - span + 1):
                acc_ref[h, i * TQ:(i + 1) * TQ, :] = part
            else:
                acc_ref[h, i * TQ:(i + 1) * TQ, :] += part

    for h in range(2):
        for j in range(nblk):
            scores(h, j)
            if h == 1:
                values(0, j)
            if j >= 1:
                probs(h, j - 1)
        probs(h, nblk - 1)
    for j in range(nblk):
        values(1, j)


def _normalised(acc_ref):
    a0 = acc_ref[0]
    a1 = acc_ref[1]
    lo = _lane_lo((a0.shape[0], LANES))
    return jnp.where(lo, a0[:, :LANES] / a0[:, LANES:], a1[:, :LANES] / a1[:, LANES:])


def _split_heads(q):
    lo = _lane_lo(q.shape)
    zero = jnp.zeros_like(q)
    return jnp.where(lo, q, zero), jnp.where(lo, zero, q)


def _fill_value_ones(vaug_ref):
    vaug_ref[:, LANES:] = jnp.ones((vaug_ref.shape[0], LANES), vaug_ref.dtype)


def _moba_block_means(k_ref):
    shape = (2 * MOBA_NBLK, LANES)
    lo = _lane_lo(shape)
    row = lax.broadcasted_iota(jnp.int32, shape, 0)
    kmt = jnp.zeros(shape, F32)
    for n in range(MOBA_NBLK):
        mean = jnp.mean(k_ref[0, n * MOBA_BLOCK:(n + 1) * MOBA_BLOCK, :].astype(F32), axis=0, keepdims=True)
        kmt = jnp.where(row == n, jnp.where(lo, mean, 0.0), kmt)
        kmt = jnp.where(row == MOBA_NBLK + n, jnp.where(lo, 0.0, mean), kmt)
    return kmt


def _moba_penalty(q, kmt, i):
    nb = MOBA_NBLK
    g = lax.dot_general(kmt, q.astype(F32), NT_DIMS,
                        precision=lax.Precision.HIGHEST, preferred_element_type=F32)
    row = lax.broadcasted_iota(jnp.int32, g.shape, 0)
    n = row & (nb - 1)
    rank = jnp.zeros(g.shape, F32)
    for m in range(i):
        gm = jnp.where(row < nb, g[m:m + 1, :], g[nb + m:nb + m + 1, :])
        tie = jnp.where(n > m, 1.0, 0.0)
        rank = rank + jnp.where(gm > g, 1.0, jnp.where(gm == g, tie, 0.0))
    keep = jnp.where(n < i, jnp.where(rank < MOBA_TOPK, 1.0, 0.0), jnp.where(n == i, 1.0, 0.0))
    keep = jnp.concatenate([keep, jnp.zeros((LANES - 2 * nb, TQ), F32)], axis=0).T
    lane = lax.broadcasted_iota(jnp.int32, keep.shape, 1)
    return jnp.where(lane < 2 * nb, (keep - 1.0) * (-NEG), 0.0)


def _attn0_kernel(q_ref, k_ref, v_ref, r_ref, e_ref, o_ref, strip_ref, qaug_ref, kaug_ref, vaug_ref,
                  s_ref, p_ref, acc_ref):
    p = pl.program_id(0)
    b = pl.program_id(1)
    S = q_ref.shape[1]
    nblk = S // TQ

    @pl.when(b == 0)
    def _():
        _fill_strips(strip_ref, r_ref)

    @pl.when(jnp.logical_and(p == 0, b == 0))
    def _():
        for h in range(2):
            kaug_ref[h, :, LANES:] = e_ref[h]
        _fill_value_ones(vaug_ref)

    k = k_ref[0]
    for h, qh in enumerate(_split_heads(q_ref[0])):
        kaug_ref[h, :, :LANES] = k
        qaug_ref[h, :, :LANES] = qh
    vaug_ref[:, :LANES] = v_ref[0]
    kmt = _moba_block_means(k_ref)
    is_moba = p < MOBA_HEADS // 2
    for i in range(nblk):
        rows = slice(i * TQ, (i + 1) * TQ)
        pen = jnp.where(is_moba, _moba_penalty(q_ref[0, rows, :], kmt, i), 0.0).astype(BF16)
        for h in range(2):
            qaug_ref[h, rows, LANES:] = pen
    _key_major_attention(lambda h, r0, r1: qaug_ref[h, r0:r1, :],
                         lambda h, j: kaug_ref[h, j * TQ:(j + 1) * TQ, :],
                         lambda j: vaug_ref[j * TQ:(j + 1) * TQ, :],
                         strip_ref, s_ref, p_ref, acc_ref, nblk, nblk)
    o_ref[0] = _normalised(acc_ref).astype(o_ref.dtype)


def _attn_scratch(S, span):
    rows = _tile_rows(S // TQ, span)
    return [pltpu.VMEM((2, S, TQ), F32), pltpu.VMEM((2, S, 2 * LANES), BF16), pltpu.VMEM((2, S, 2 * LANES), BF16),
            pltpu.VMEM((S, 2 * LANES), BF16), pltpu.VMEM((rows, TQ), F32), pltpu.VMEM((2 * rows, TQ), BF16),
            pltpu.VMEM((2, S, 2 * LANES), F32)]


def _attn0(qkv, r_tab, e_onehot):
    B, S, _ = qkv.shape
    nblk = S // TQ
    return pl.pallas_call(
        _attn0_kernel,
        grid=(N_PAIRS, B),
        in_specs=[
            pl.BlockSpec((1, S, LANES), lambda p, b: (b, 0, p)),
            pl.BlockSpec((1, S, LANES), lambda p, b: (b, 0, N_PAIRS + p)),
            pl.BlockSpec((1, S, LANES), lambda p, b: (b, 0, 2 * N_PAIRS + p)),
            pl.BlockSpec((1, 2, nblk, 2 * TQ), lambda p, b: (p, 0, 0, 0)),
            pl.BlockSpec((2, S, LANES), lambda p, b: (0, 0, 0), pipeline_mode=pl.Buffered(1)),
        ],
        out_specs=pl.BlockSpec((1, S, LANES), lambda p, b: (b, 0, p)),
        out_shape=jax.ShapeDtypeStruct((B, S, D_MODEL), BF16),
        scratch_shapes=_attn_scratch(S, nblk),
        compiler_params=pltpu.CompilerParams(dimension_semantics=("arbitrary",) * 2, vmem_limit_bytes=ATTN_VMEM_LIMIT),
        name="attn_moba_dilated",
    )(qkv, qkv, qkv, r_tab, e_onehot)


def _compress_kernel(a_ref, pos_ref, w1_ref, b1_ref, w2_ref, b2_ref, gain_ref, o_ref, *, normed):
    y = None
    for g in range(2):
        a = a_ref[0, g].astype(F32)
        first = jnp.dot((a + pos_ref[0:1, :]).astype(BF16), w1_ref[0], preferred_element_type=F32)
        second = jnp.dot((a + pos_ref[1:2, :]).astype(BF16), w1_ref[1], preferred_element_type=F32)
        hid = first + pltpu.roll(second, LANES - 1, 0) + b1_ref[...]
        cdf = 0.5 * (1.0 + jnp.tanh(math.sqrt(2.0 / math.pi) * (hid + 0.044715 * (hid * hid * hid))))
        yg = jnp.dot((hid * cdf).astype(BF16), w2_ref[g], preferred_element_type=F32)
        y = yg if y is None else y + yg
    y = y + b2_ref[...]
    if normed:
        lo = _lane_lo(y.shape)
        y2 = y * y
        s_lo = jnp.sum(jnp.where(lo, y2, 0.0), axis=1, keepdims=True)
        s_hi = jnp.sum(jnp.where(lo, 0.0, y2), axis=1, keepdims=True)
        ms = jnp.where(lo, s_lo, s_hi) * (1.0 / HEAD_DIM)
        y = y * lax.rsqrt(ms + RMS_EPS) * gain_ref[...]
    o_ref[0, 0] = y.astype(o_ref.dtype)


def _compress(a, pos, w1, b1, w2, b2, gain, normed):
    B = a.shape[0]
    hid = w1.shape[1]
    half = NSA_CMP_STRIDE * HEAD_DIM
    zeros = jnp.zeros_like(w2)
    w2p = jnp.stack([jnp.concatenate([w2, zeros], axis=1), jnp.concatenate([zeros, w2], axis=1)]).astype(BF16)
    return pl.pallas_call(
        functools.partial(_compress_kernel, normed=normed),
        grid=(B, 2),
        in_specs=[
            pl.BlockSpec((1, 2, LANES, half), lambda b, m: (b, m, 0, 0)),
            pl.BlockSpec((2, half), lambda b, m: (0, 0)),
            pl.BlockSpec((2, half, hid), lambda b, m: (0, 0, 0)),
            pl.BlockSpec((1, hid), lambda b, m: (0, 0)),
            pl.BlockSpec((2, hid, LANES), lambda b, m: (0, 0, 0)),
            pl.BlockSpec((1, LANES), lambda b, m: (0, 0)),
            pl.BlockSpec((1, LANES), lambda b, m: (0, 0)),
        ],
        out_specs=pl.BlockSpec((1, 1, LANES, LANES), lambda b, m: (b, m, 0, 0)),
        out_shape=jax.ShapeDtypeStruct((B, 2, LANES, LANES), BF16),
        compiler_params=_cparams(2),
        name="nsa_compress",
    )(a, pos.reshape(2, half), w1.reshape(2, half, hid).astype(BF16), b1.reshape(1, hid), w2p,
      jnp.tile(b2, 2).reshape(1, LANES), jnp.tile(gain, 2).reshape(1, LANES))


def _cmp_kernel(q_ref, kc_ref, vc_ref, bias_ref, ovt_ref, o_ref, pen_ref):
    i = pl.program_id(2)
    kc = kc_ref[0, 0]
    vc = vc_ref[0, 0]
    lo = _lane_lo((TQ, LANES))
    psum = [jnp.zeros((TQ, LANES), F32), jnp.zeros((TQ, LANES), F32)]
    for r in range(4):
        heads = _split_heads(q_ref[0, :, r * LANES:(r + 1) * LANES])
        outs = []
        for h in range(2):
            s = lax.dot_general(heads[h], kc, NT_DIMS, preferred_element_type=F32) + bias_ref[0, 2 * r + h]
            m = jnp.max(s, axis=1, keepdims=True)
            e = jnp.where(s > 0.5 * NEG, jnp.exp(s - m), 0.0)
            l = jnp.sum(e, axis=1, keepdims=True)
            pr = e / jnp.where(l > 0.0, l, 1.0)
            psum[h] = psum[h] + pr
            outs.append(jnp.dot(pr.astype(BF16), vc, preferred_element_type=F32))
        o_ref[0, :, r * LANES:(r + 1) * LANES] = jnp.where(lo, outs[0], outs[1])

    n_sel = ovt_ref.shape[0]
    blk = lax.broadcasted_iota(jnp.int32, (n_sel, TQ), 0)
    t = i * TQ + lax.broadcasted_iota(jnp.int32, (n_sel, TQ), 1)
    cur = lax.shift_right_logical(t, int(math.log2(NSA_SEL_BLOCK)))
    keeps = []
    for h in range(2):
        imp = lax.dot_general(ovt_ref[...], psum[h], NT_DIMS,
                              precision=lax.Precision.HIGHEST, preferred_element_type=F32)
        forced = jnp.where(blk == 0, 1.0, jnp.where(blk == cur, 1.0, jnp.where(blk == cur - 1, 1.0, 0.0)))
        imp = jnp.where(blk <= cur, imp + forced * NSA_FORCE, -jnp.inf)
        rank = jnp.zeros(imp.shape, F32)
        for m in range(n_sel):
            im = imp[m:m + 1, :]
            tie = jnp.where(blk > m, 1.0, 0.0)
            rank = rank + jnp.where(im > imp, 1.0, jnp.where(im == imp, tie, 0.0))
        keeps.append(jnp.where(rank < NSA_SEL_TOPN, 1.0, 0.0))
    keep = jnp.concatenate(keeps + [jnp.zeros((LANES - 2 * n_sel, TQ), F32)], axis=0).T
    lane = lax.broadcasted_iota(jnp.int32, keep.shape, 1)
    pen_ref[0, 0] = jnp.where(lane < 2 * n_sel, (keep - 1.0) * (-NEG), 0.0).astype(pen_ref.dtype)


def _cmp_attention(qkv, kc, vc, bias, ovt):
    B, S, _ = qkv.shape
    nq = S // TQ
    n_sel = ovt.shape[0]
    return pl.pallas_call(
        _cmp_kernel,
        grid=(B, 2, nq),
        in_specs=[
            pl.BlockSpec((1, TQ, 4 * LANES), lambda b, m, i: (b, i, m)),
            pl.BlockSpec((1, 1, LANES, LANES), lambda b, m, i: (b, m, 0, 0)),
            pl.BlockSpec((1, 1, LANES, LANES), lambda b, m, i: (b, m, 0, 0)),
            pl.BlockSpec((1, 8, TQ, LANES), lambda b, m, i: (m, 0, i, 0)),
            pl.BlockSpec((n_sel, LANES), lambda b, m, i: (0, 0)),
        ],
        out_specs=[
            pl.BlockSpec((1, TQ, 4 * LANES), lambda b, m, i: (b, i, m)),
            pl.BlockSpec((1, 1, TQ, LANES), lambda b, m, i: (b, m, i, 0)),
        ],
        out_shape=[jax.ShapeDtypeStruct((B, S, D_MODEL), F32),
                   jax.ShapeDtypeStruct((B, 2, S, LANES), BF16)],
        compiler_params=_cparams(3),
        name="nsa_compressed_select",
    )(qkv, kc, vc, bias, ovt)


def _attn1_kernel(q_ref, ks_ref, vs_ref, kw_ref, vw_ref, rs_ref, rw_ref, e_ref, pen_ref, ocmp_ref, graw_ref,
                  gsel_ref, o_ref, strip_s_ref, qaug_ref, kaug_ref, vsaug_ref, s_ref, p_ref, acc_ref,
                  strip_w_ref, vwaug_ref, oslc_ref):
    p = pl.program_id(0)
    b = pl.program_id(1)
    S = q_ref.shape[1]
    nblk = S // TQ

    @pl.when(b == 0)
    def _():
        _fill_strips(strip_s_ref, rs_ref)
        _fill_strips(strip_w_ref, rw_ref)

    @pl.when(jnp.logical_and(p == 0, b == 0))
    def _():
        for h in range(2):
            kaug_ref[h, :, LANES:] = e_ref[h]
        _fill_value_ones(vsaug_ref)
        _fill_value_ones(vwaug_ref)

    ks = ks_ref[0]
    pen = pen_ref[0, 0]
    for h, qh in enumerate(_split_heads(q_ref[0])):
        kaug_ref[h, :, :LANES] = ks
        qaug_ref[h, :, :LANES] = qh
        qaug_ref[h, :, LANES:] = pen
    vsaug_ref[:, :LANES] = vs_ref[0]
    vwaug_ref[:, :LANES] = vw_ref[0]
    _key_major_attention(lambda h, r0, r1: qaug_ref[h, r0:r1, :],
                         lambda h, j: kaug_ref[h, j * TQ:(j + 1) * TQ, :],
                         lambda j: vsaug_ref[j * TQ:(j + 1) * TQ, :],
                         strip_s_ref, s_ref, p_ref, acc_ref, nblk, nblk)
    oslc_ref[...] = _normalised(acc_ref)
    _key_major_attention(lambda h, r0, r1: qaug_ref[h, r0:r1, :LANES],
                         lambda h, j: kw_ref[0, j * TQ:(j + 1) * TQ, :],
                         lambda j: vwaug_ref[j * TQ:(j + 1) * TQ, :],
                         strip_w_ref, s_ref, p_ref, acc_ref, nblk, strip_w_ref.shape[1] // TQ)
    sig = _sigmoid(graw_ref[0])
    gates = [jnp.dot(sig, gsel_ref[0, br], precision=lax.Precision.HIGHEST, preferred_element_type=F32)
             for br in range(3)]
    out = gates[0] * ocmp_ref[0] + gates[1] * oslc_ref[...] + gates[2] * _normalised(acc_ref)
    o_ref[0] = out.astype(o_ref.dtype)


def _attn1(qkv, r_slc, r_win, e_onehot, pen, ocmp, graw, gsel):
    B, S, _ = qkv.shape
    kv0 = D_MODEL // LANES + 4
    n_win = r_win.shape[2]
    return pl.pallas_call(
        _attn1_kernel,
        grid=(N_PAIRS, B),
        in_specs=[
            pl.BlockSpec((1, S, LANES), lambda p, b: (b, 0, p)),
            pl.BlockSpec((1, S, LANES), lambda p, b: (b, 0, kv0 + p // 4)),
            pl.BlockSpec((1, S, LANES), lambda p, b: (b, 0, kv0 + 2 + p // 4)),
            pl.BlockSpec((1, S, LANES), lambda p, b: (b, 0, kv0 + 4 + p // 4)),
            pl.BlockSpec((1, S, LANES), lambda p, b: (b, 0, kv0 + 6 + p // 4)),
            pl.BlockSpec((1, 2, r_slc.shape[2], 2 * TQ), lambda p, b: (p, 0, 0, 0)),
            pl.BlockSpec((1, 2, n_win, 2 * TQ), lambda p, b: (p, 0, 0, 0)),
            pl.BlockSpec((2, S, LANES), lambda p, b: (0, 0, 0), pipeline_mode=pl.Buffered(1)),
            pl.BlockSpec((1, 1, S, LANES), lambda p, b: (b, p // 4, 0, 0)),
            pl.BlockSpec((1, S, LANES), lambda p, b: (b, 0, p)),
            pl.BlockSpec((1, S, LANES), lambda p, b: (b, 0, 0)),
            pl.BlockSpec((1, 3, LANES, LANES), lambda p, b: (p, 0, 0, 0)),
        ],
        out_specs=pl.BlockSpec((1, S, LANES), lambda p, b: (b, 0, p)),
        out_shape=jax.ShapeDtypeStruct((B, S, D_MODEL), BF16),
        scratch_shapes=_attn_scratch(S, S // TQ) + [pltpu.VMEM((2, n_win * TQ, TQ), F32),
                                                    pltpu.VMEM((S, 2 * LANES), BF16), pltpu.VMEM((S, LANES), F32)],
        compiler_params=pltpu.CompilerParams(dimension_semantics=("arbitrary",) * 2, vmem_limit_bytes=ATTN_VMEM_LIMIT),
        name="attn_nsa",
    )(qkv, qkv, qkv, qkv, qkv, r_slc, r_win, e_onehot, pen, ocmp, graw, gsel)


def _bucket(dist):
    n = np.maximum(dist, 0)
    exact = REL_BUCKETS // 2
    nf = np.maximum(n, 1).astype(np.float64)
    large = exact + (np.log(nf / exact) / math.log(REL_MAX_DIST / exact) * (REL_BUCKETS - exact)).astype(np.int64)
    return np.where(n < exact, n, np.minimum(large, REL_BUCKETS - 1))


def _strip_rows(tab, n_off, extra_of_dist):
    u = np.arange(2 * TQ)
    d = np.arange(n_off)[:, None] * TQ - np.where(u < TQ, u, u - 2 * TQ)[None, :]
    onehot = (_bucket(d)[None] == np.arange(REL_BUCKETS)[:, None, None]).astype(np.float32)
    vals = jnp.einsum("hb,bou->hou", tab, jnp.asarray(onehot), precision=lax.Precision.HIGHEST) + extra_of_dist(d)
    return vals.reshape(tab.shape[0] // 2, 2, n_off, 2 * TQ)


def _causal_mask(d):
    return np.where(d >= 0, 0.0, NEG).astype(np.float32)


def _window_mask(d):
    return np.where((d >= 0) & (d < NSA_WINDOW), 0.0, NEG).astype(np.float32)


def _dilation_log_count(d):
    c = ((d >= 0) & (d <= 128)).astype(np.float64)
    c += ((d >= 0) & (d % 4 == 0) & (d <= 512))
    c += ((d >= 0) & (d % 16 == 0) & (d <= 2048))
    return np.where(c > 0, np.log(np.maximum(c, 1.0)), NEG).astype(np.float32)


def _one_hot_blocks(S, block, per_head):
    e = np.zeros((2, S, LANES), np.float32)
    key = np.arange(S)
    for h in range(2):
        e[h, key, h * per_head + key // block] = 1.0
    return jnp.asarray(e, BF16)


_NSA_HEAD_ORDER = np.array([8 * (p // 4) + (p % 4) + 4 * h for p in range(N_PAIRS) for h in range(2)])


def _head_cols(heads):
    return (np.asarray(heads)[:, None] * HEAD_DIM + np.arange(HEAD_DIM)[None, :]).reshape(-1)


def _mixer_ab(h, g_mix, w_in, qn_a, kn_a, qn_b, kn_b, rel_bias, B, S):
    wa = MOBA_HEADS * HEAD_DIM
    sec = [w_in[:, k * wa:(k + 1) * wa] for k in range(6)]
    w = jnp.concatenate([sec[0], sec[3], sec[1], sec[4], sec[2], sec[5]], axis=1).astype(BF16)
    ones = jnp.ones((wa,), F32)
    gain = jnp.concatenate([jnp.tile(qn_a, 8) * ATTN_SCALE, jnp.tile(qn_b, 8) * ATTN_SCALE,
                            jnp.tile(kn_a, 8), jnp.tile(kn_b, 8), ones, ones])
    per_sec = wa // PROJ_CHUNK
    (qkv,) = _project(h, g_mix, w, gain, [True] * (4 * per_sec) + [False] * (2 * per_sec))
    qkv = qkv.reshape(B, S, 3 * D_MODEL)

    tab = rel_bias.T
    r_tab = jnp.concatenate([_strip_rows(tab[:MOBA_HEADS], S // TQ, _causal_mask),
                             _strip_rows(tab[MOBA_HEADS:], S // TQ, _dilation_log_count)])
    o = _attn0(qkv, r_tab, _one_hot_blocks(S, MOBA_BLOCK, S // MOBA_BLOCK))
    return o.reshape(B * S, D_MODEL)


def _mixer_nsa(h, g_mix, w_in, qn, kn_c, kn_s, kn_w, cmp_k, cmp_v, rel_bias, B, S):
    qw = N_HEADS * HEAD_DIM
    kvw = NSA_GROUPS * HEAD_DIM
    order = _NSA_HEAD_ORDER
    w_main = jnp.concatenate([w_in[:, _head_cols(order)], w_in[:, qw:qw + 6 * kvw]], axis=1).astype(BF16)
    ones = jnp.ones((kvw,), F32)
    gain = jnp.concatenate([jnp.tile(qn, N_HEADS) * ATTN_SCALE, ones, ones,
                            jnp.tile(kn_s, NSA_GROUPS), ones, jnp.tile(kn_w, NSA_GROUPS), ones])
    assert kvw == PROJ_CHUNK
    norm_chunks = [True] * (qw // PROJ_CHUNK) + [False, False, True, False, True, False]

    gcols = np.zeros((LANES,), np.int64)
    gused = np.zeros((LANES,), np.float32)
    gsel = np.zeros((N_PAIRS, 3, LANES, LANES), np.float32)
    for p in range(N_PAIRS):
        for br in range(3):
            for hh in range(2):
                c = 8 * p + 2 * br + hh
                gcols[c] = qw + 6 * kvw + 3 * order[2 * p + hh] + br
                gused[c] = 1.0
                gsel[p, br, c, hh * HEAD_DIM:(hh + 1) * HEAD_DIM] = 1.0
    w_gate = (w_in[:, gcols] * gused).astype(BF16)
    qkv, graw = _project(h, g_mix, w_main, gain, norm_chunks, w_gate)
    qkv = qkv.reshape(B, S, qw + 6 * kvw)
    graw = graw.reshape(B, S, LANES)

    def chunked(col0):
        t = qkv[:, :, col0:col0 + kvw].reshape(B, S // NSA_CMP_STRIDE, NSA_CMP_STRIDE, NSA_GROUPS, HEAD_DIM)
        return t.transpose(0, 3, 1, 2, 4).reshape(B, NSA_GROUPS, S // NSA_CMP_STRIDE, NSA_CMP_STRIDE * HEAD_DIM)

    kc = _compress(chunked(qw), *cmp_k, kn_c, True)
    vc = _compress(chunked(qw + kvw), *cmp_v, kn_c, False)

    tab = rel_bias.T[order]
    n_cmp = (S - NSA_CMP_LEN) // NSA_CMP_STRIDE + 1
    t_pos = np.arange(S)[:, None]
    c_idx = np.arange(LANES)[None, :]
    dc = t_pos - (c_idx * NSA_CMP_STRIDE + NSA_CMP_LEN - 1)
    cmp_mask = np.where((dc >= 0) & (c_idx < n_cmp), 0.0, NEG).astype(np.float32)
    bkt = jnp.asarray(_bucket(dc).astype(np.int32))[None]
    bias_c = jnp.broadcast_to(tab[:, 0][:, None, None], (N_HEADS, S, LANES))
    for bb in range(1, REL_BUCKETS):
        bias_c = jnp.where(bkt == bb, tab[:, bb][:, None, None], bias_c)
    bias_c = (bias_c + cmp_mask).reshape(2, 8, S, LANES)
    n_sel = S // NSA_SEL_BLOCK
    cstart = np.arange(LANES) * NSA_CMP_STRIDE
    sstart = np.arange(n_sel) * NSA_SEL_BLOCK
    ovt = np.maximum(np.minimum(cstart[None, :] + NSA_CMP_LEN, sstart[:, None] + NSA_SEL_BLOCK)
                     - np.maximum(cstart[None, :], sstart[:, None]), 0).astype(np.float32)
    ovt[:, n_cmp:] = 0.0
    ocmp, pen = _cmp_attention(qkv, kc, vc, bias_c, jnp.asarray(ovt))

    r_slc = _strip_rows(tab, S // TQ, _causal_mask)
    r_win = _strip_rows(tab, NSA_WINDOW // TQ + 1, _window_mask)
    o = _attn1(qkv, r_slc, r_win, _one_hot_blocks(S, NSA_SEL_BLOCK, n_sel), pen, ocmp, graw, jnp.asarray(gsel))
    return o.reshape(B * S, D_MODEL)


def kernel(x, p, rel_bias, norm_mix, norm_ffn, norm_ple, w_ffn_gate, w_ffn_up, w_ffn_down, w_ple_proj, w_ple_gate, w_in_ab, w_out_ab, qn_moba, kn_moba, qn_dil, kn_dil, w_in_nsa, w_out_nsa, qn_nsa, kn_cmp, kn_slc, kn_win, cmp_k_pos, cmp_k_w1, cmp_k_b1, cmp_k_w2, cmp_k_b2, cmp_v_pos, cmp_v_w1, cmp_v_b1, cmp_v_w2, cmp_v_b2):
    B, S, D = x.shape
    depth = p.shape[0]
    h = x.reshape(B * S, D)
    for i in range(depth):
        e = i // 2
        if i % 2 == 0:
            o = _mixer_ab(h, norm_mix[i], w_in_ab[e], qn_moba[e], kn_moba[e], qn_dil[e], kn_dil[e], rel_bias, B, S)
            w_out = w_out_ab[e]
        else:
            cmp_k = (cmp_k_pos[e], cmp_k_w1[e], cmp_k_b1[e], cmp_k_w2[e], cmp_k_b2[e])
            cmp_v = (cmp_v_pos[e], cmp_v_w1[e], cmp_v_b1[e], cmp_v_w2[e], cmp_v_b2[e])
            o = _mixer_nsa(h, norm_mix[i], w_in_nsa[e], qn_nsa[e], kn_cmp[e], kn_slc[e], kn_win[e],
                           cmp_k, cmp_v, rel_bias, B, S)
            w_out = w_out_nsa[e][_head_cols(_NSA_HEAD_ORDER), :]
        h = _post_attention(o, h, w_out.astype(BF16), norm_ffn[i], w_ffn_gate[i].astype(BF16),
                            w_ffn_up[i].astype(BF16), w_ffn_down[i].astype(BF16), norm_ple[i],
                            w_ple_gate[i].astype(BF16), p[i].reshape(B * S, -1), w_ple_proj[i].astype(BF16))
    return h.reshape(B, S, D)
```

```python
import functools
import math

import numpy as np
import jax
import jax.numpy as jnp
from jax import lax
from jax.experimental import pallas as pl
from jax.experimental.pallas import tpu as pltpu

F32 = jnp.float32
BF16 = jnp.bfloat16

D_MODEL = 1024
HEAD_DIM = 64
N_HEADS = 16
N_PAIRS = N_HEADS // 2
MOBA_HEADS = 8
MOBA_BLOCK = 256
MOBA_TOPK = 3
MOBA_NBLK = 8
NSA_GROUPS = 4
NSA_CMP_LEN = 32
NSA_CMP_STRIDE = 16
NSA_SEL_BLOCK = 64
NSA_SEL_TOPN = 16
NSA_WINDOW = 512
NSA_FORCE = 1.0e6
REL_BUCKETS = 32
REL_MAX_DIST = 2048
RMS_EPS = 1e-6
ATTN_SCALE = HEAD_DIM ** -0.5

LANES = 128
TQ = 256
NEG = -1.0e30
ROW_TILE = 512
PROJ_CHUNK = 256
FFN_SPLIT = 2
CMP_ROWS = 512
VMEM_LIMIT = 48 * 1024 * 1024
POST_VMEM_LIMIT = 56 * 1024 * 1024
ATTN_VMEM_LIMIT = 56 * 1024 * 1024

NT_DIMS = (((1,), (1,)), ((), ()))


def _cparams(n_axes):
    return pltpu.CompilerParams(dimension_semantics=("arbitrary",) * n_axes,
                                vmem_limit_bytes=VMEM_LIMIT)


def _rms_rows(x, g):
    ms = jnp.mean(x * x, axis=-1, keepdims=True)
    return x * lax.rsqrt(ms + RMS_EPS) * g


def _lane_lo(shape):
    return lax.broadcasted_iota(jnp.int32, shape, len(shape) - 1) < HEAD_DIM


def _resident(shape):
    return pl.BlockSpec(shape, lambda i: (0,) * len(shape), pipeline_mode=pl.Buffered(1))


def _proj_kernel(x_ref, g_ref, w_ref, cg_ref, bd_ref, *rest, norm_chunks, with_extra):
    if with_extra:
        wx_ref, o_ref, ox_ref = rest
    else:
        (o_ref,) = rest
    xn = _rms_rows(x_ref[...], g_ref[...]).astype(BF16)
    bd = bd_ref[...]

    def finish(c, y):
        cols = slice(c * PROJ_CHUNK, (c + 1) * PROJ_CHUNK)
        if norm_chunks[c]:
            y2 = y * y
            hi = y2.astype(BF16)
            lo = (y2 - hi.astype(F32)).astype(BF16)
            ssq = jnp.dot(hi, bd, preferred_element_type=F32) + jnp.dot(lo, bd, preferred_element_type=F32)
            y = y * lax.rsqrt(ssq * (1.0 / HEAD_DIM) + RMS_EPS) * cg_ref[:, cols]
        o_ref[:, cols] = y.astype(o_ref.dtype)

    prev = None
    for c in range(len(norm_chunks)):
        y = jnp.dot(xn, w_ref[:, c * PROJ_CHUNK:(c + 1) * PROJ_CHUNK], preferred_element_type=F32)
        if prev is not None:
            finish(c - 1, prev)
        prev = y
    finish(len(norm_chunks) - 1, prev)
    if with_extra:
        ox_ref[...] = jnp.dot(xn, wx_ref[...], preferred_element_type=F32)


def _project(x, g, w, col_gain, norm_chunks, w_extra=None):
    T, D = x.shape
    N = w.shape[1]
    blk = np.kron(np.eye(PROJ_CHUNK // HEAD_DIM), np.ones((HEAD_DIM, HEAD_DIM))).astype(np.float32)
    row_in = lambda n: pl.BlockSpec((ROW_TILE, n), lambda i: (i, 0))
    in_specs = [row_in(D), _resident((1, D)), _resident((D, N)), _resident((1, N)),
                _resident((PROJ_CHUNK, PROJ_CHUNK))]
    args = [x, g.reshape(1, D), w, col_gain.reshape(1, N), jnp.asarray(blk, BF16)]
    out_specs = [row_in(N)]
    out_shape = [jax.ShapeDtypeStruct((T, N), BF16)]
    if w_extra is not None:
        nx = w_extra.shape[1]
        in_specs.append(_resident((D, nx)))
        args.append(w_extra)
        out_specs.append(row_in(nx))
        out_shape.append(jax.ShapeDtypeStruct((T, nx), F32))
    return pl.pallas_call(
        functools.partial(_proj_kernel, norm_chunks=tuple(norm_chunks), with_extra=w_extra is not None),
        grid=(T // ROW_TILE,),
        in_specs=in_specs,
        out_specs=out_specs,
        out_shape=out_shape,
        compiler_params=_cparams(1),
        name="proj",
    )(*args)


def _sigmoid(z):
    return 1.0 / (1.0 + jnp.exp(-z))


def _post_kernel(o_ref, h_ref, wout_ref, gf_ref, wg_ref, wu_ref, wd_ref, gp_ref, wpg_ref, p_ref, wpp_ref, out_ref):
    h1 = h_ref[...] + jnp.dot(o_ref[...], wout_ref[...], preferred_element_type=F32)
    xn = _rms_rows(h1, gf_ref[...]).astype(BF16)
    h2 = h1
    fh = wg_ref.shape[1]
    for c in range(FFN_SPLIT):
        cols = slice(c * fh // FFN_SPLIT, (c + 1) * fh // FFN_SPLIT)
        a = jnp.dot(xn, wg_ref[:, cols], preferred_element_type=F32)
        u = jnp.dot(xn, wu_ref[:, cols], preferred_element_type=F32)
        act = (a * _sigmoid(a) * u).astype(BF16)
        h2 = h2 + jnp.dot(act, wd_ref[cols, :], preferred_element_type=F32)
    hn = _rms_rows(h2, gp_ref[...]).astype(BF16)
    gate = _sigmoid(jnp.dot(hn, wpg_ref[...], preferred_element_type=F32))
    out_ref[...] = h2 + gate * jnp.dot(p_ref[...].astype(BF16), wpp_ref[...], preferred_element_type=F32)


def _post_attention(o, h, w_out, g_ffn, wg, wu, wd, g_ple, w_pgate, p, w_pproj):
    T, D = h.shape
    Fh = wg.shape[1]
    Pd = p.shape[1]
    row_in = lambda n: pl.BlockSpec((ROW_TILE, n), lambda i: (i, 0))
    return pl.pallas_call(
        _post_kernel,
        grid=(T // ROW_TILE,),
        in_specs=[row_in(D), row_in(D), _resident((D, D)), _resident((1, D)), _resident((D, Fh)),
                  _resident((D, Fh)), _resident((Fh, D)), _resident((1, D)), _resident((D, D)),
                  row_in(Pd), _resident((Pd, D))],
        out_specs=row_in(D),
        out_shape=jax.ShapeDtypeStruct((T, D), F32),
        compiler_params=pltpu.CompilerParams(dimension_semantics=("arbitrary",), vmem_limit_bytes=POST_VMEM_LIMIT),
        name="post_attention",
    )(o, h, w_out, g_ffn.reshape(1, D), wg, wu, wd, g_ple.reshape(1, D), w_pgate, p, w_pproj)


def _fill_strips(strip_ref, r_ref):
    for h in range(2):
        for o in range(r_ref.shape[2]):
            x = jnp.broadcast_to(r_ref[0, h, o:o + 1, :], (TQ, 2 * TQ))
            strip_ref[h, o * TQ:(o + 1) * TQ, :] = pltpu.roll(x, 0, 1, stride=1, stride_axis=0)[:, :TQ]


def _tile_groups(nblk, span):
    cnt = [min(span, nblk - j) for j in range(nblk)]
    off = [0]
    for c in cnt:
        off.append(off[-1] + c * TQ)
    return cnt, off


def _tile_rows(nblk, span):
    return _tile_groups(nblk, span)[1][-1]


def _pair_attention(q_rows, k_rows, v_rows, strip_ref, s_ref, p_ref, o_ref, nblk, span):
    cnt, off = _tile_groups(nblk, span)

    def scores(h, j):
        n = cnt[j] * TQ
        s = lax.dot_general(q_rows(h, j * TQ, j * TQ + n), k_rows(h, j), NT_DIMS, preferred_element_type=F32)
        s_ref[off[j]:off[j] + n, :] = s + strip_ref[h, 0:n, :]

    def probs(i):
        j0 = max(0, i - span + 1)
        rows = [off[j] + (i - j) * TQ for j in range(j0, i + 1)]
        mx = s_ref[rows[0]:rows[0] + TQ, :]
        for r in rows[1:]:
            mx = jnp.maximum(mx, s_ref[r:r + TQ, :])
        m = jnp.max(mx, axis=1, keepdims=True)
        for t, r in enumerate(rows):
            p_ref[i, :, t * TQ:(t + 1) * TQ] = jnp.exp(s_ref[r:r + TQ, :] - m).astype(BF16)

    def values(h, i):
        j0 = max(0, i - span + 1)
        out = jnp.dot(p_ref[i, :, 0:(i + 1 - j0) * TQ], v_rows(j0 * TQ, (i + 1) * TQ), preferred_element_type=F32)
        o_ref[h, i * TQ:(i + 1) * TQ, :] = out[:, :LANES] / out[:, LANES:]

    for h in range(2):
        for j in range(nblk):
            scores(h, j)
            if j >= 1:
                values(h, j - 1)
            probs(j)
        values(h, nblk - 1)


def _merge_heads(o_ref):
    return jnp.where(_lane_lo((o_ref.shape[1], LANES)), o_ref[0], o_ref[1])


def _split_heads(q):
    lo = _lane_lo(q.shape)
    zero = jnp.zeros_like(q)
    return jnp.where(lo, q, zero), jnp.where(lo, zero, q)


def _fill_value_ones(vaug_ref):
    vaug_ref[:, LANES:] = jnp.ones((vaug_ref.shape[0], LANES), vaug_ref.dtype)


def _moba_block_means(k_ref):
    shape = (2 * MOBA_NBLK, LANES)
    lo = _lane_lo(shape)
    row = lax.broadcasted_iota(jnp.int32, shape, 0)
    kmt = jnp.zeros(shape, F32)
    for n in range(MOBA_NBLK):
        mean = jnp.mean(k_ref[0, n * MOBA_BLOCK:(n + 1) * MOBA_BLOCK, :].astype(F32), axis=0, keepdims=True)
        kmt = jnp.where(row == n, jnp.where(lo, mean, 0.0), kmt)
        kmt = jnp.where(row == MOBA_NBLK + n, jnp.where(lo, 0.0, mean), kmt)
    return kmt


def _moba_penalty(q, kmt, i):
    nb = MOBA_NBLK
    g = lax.dot_general(kmt, q.astype(F32), NT_DIMS,
                        precision=lax.Precision.HIGHEST, preferred_element_type=F32)
    row = lax.broadcasted_iota(jnp.int32, g.shape, 0)
    n = row & (nb - 1)
    rank = jnp.zeros(g.shape, F32)
    for m in range(i):
        gm = jnp.where(row < nb, g[m:m + 1, :], g[nb + m:nb + m + 1, :])
        tie = jnp.where(n > m, 1.0, 0.0)
        rank = rank + jnp.where(gm > g, 1.0, jnp.where(gm == g, tie, 0.0))
    keep = jnp.where(n < i, jnp.where(rank < MOBA_TOPK, 1.0, 0.0), jnp.where(n == i, 1.0, 0.0))
    keep = jnp.concatenate([keep, jnp.zeros((LANES - 2 * nb, TQ), F32)], axis=0).T
    lane = lax.broadcasted_iota(jnp.int32, keep.shape, 1)
    return jnp.where(lane < 2 * nb, (keep - 1.0) * (-NEG), 0.0)


def _attn0_kernel(q_ref, k_ref, v_ref, r_ref, e_ref, o_ref, strip_ref, qaug_ref, kaug_ref, vaug_ref,
                  s_ref, p_ref, oh_ref):
    p = pl.program_id(0)
    b = pl.program_id(1)
    S = q_ref.shape[1]
    nblk = S // TQ

    @pl.when(b == 0)
    def _():
        _fill_strips(strip_ref, r_ref)

    @pl.when(jnp.logical_and(p == 0, b == 0))
    def _():
        for h in range(2):
            kaug_ref[h, :, LANES:] = e_ref[h]
        _fill_value_ones(vaug_ref)

    k = k_ref[0]
    for h, qh in enumerate(_split_heads(q_ref[0])):
        kaug_ref[h, :, :LANES] = k
        qaug_ref[h, :, :LANES] = qh
    vaug_ref[:, :LANES] = v_ref[0]
    kmt = _moba_block_means(k_ref)
    is_moba = p < MOBA_HEADS // 2
    for i in range(nblk):
        rows = slice(i * TQ, (i + 1) * TQ)
        pen = jnp.where(is_moba, _moba_penalty(q_ref[0, rows, :], kmt, i), 0.0).astype(BF16)
        for h in range(2):
            qaug_ref[h, rows, LANES:] = pen
    _pair_attention(lambda h, r0, r1: qaug_ref[h, r0:r1, :],
                    lambda h, j: kaug_ref[h, j * TQ:(j + 1) * TQ, :],
                    lambda r0, r1: vaug_ref[r0:r1, :],
                    strip_ref, s_ref, p_ref, oh_ref, nblk, nblk)
    o_ref[0] = _merge_heads(oh_ref).astype(o_ref.dtype)


def _attn_scratch(S, span):
    nblk = S // TQ
    return [pltpu.VMEM((2, S, TQ), F32), pltpu.VMEM((2, S, 2 * LANES), BF16), pltpu.VMEM((2, S, 2 * LANES), BF16),
            pltpu.VMEM((S, 2 * LANES), BF16), pltpu.VMEM((_tile_rows(nblk, span), TQ), F32),
            pltpu.VMEM((nblk, TQ, span * TQ), BF16), pltpu.VMEM((2, S, LANES), F32)]


def _attn0(qkv, r_tab, e_onehot):
    B, S, _ = qkv.shape
    nblk = S // TQ
    return pl.pallas_call(
        _attn0_kernel,
        grid=(N_PAIRS, B),
        in_specs=[
            pl.BlockSpec((1, S, LANES), lambda p, b: (b, 0, p)),
            pl.BlockSpec((1, S, LANES), lambda p, b: (b, 0, N_PAIRS + p)),
            pl.BlockSpec((1, S, LANES), lambda p, b: (b, 0, 2 * N_PAIRS + p)),
            pl.BlockSpec((1, 2, nblk, 2 * TQ), lambda p, b: (p, 0, 0, 0)),
            pl.BlockSpec((2, S, LANES), lambda p, b: (0, 0, 0), pipeline_mode=pl.Buffered(1)),
        ],
        out_specs=pl.BlockSpec((1, S, LANES), lambda p, b: (b, 0, p)),
        out_shape=jax.ShapeDtypeStruct((B, S, D_MODEL), BF16),
        scratch_shapes=_attn_scratch(S, nblk),
        compiler_params=pltpu.CompilerParams(dimension_semantics=("arbitrary",) * 2, vmem_limit_bytes=ATTN_VMEM_LIMIT),
        name="attn_moba_dilated",
    )(qkv, qkv, qkv, r_tab, e_onehot)


def _compress_kernel(a_ref, pos_ref, w1_ref, b1_ref, w2_ref, b2_ref, gain_ref, o_ref, *, normed):
    y = None
    for g in range(2):
        a = a_ref[0, g].astype(F32)
        first = jnp.dot((a + pos_ref[0:1, :]).astype(BF16), w1_ref[0], preferred_element_type=F32)
        second = jnp.dot((a + pos_ref[1:2, :]).astype(BF16), w1_ref[1], preferred_element_type=F32)
        hid = first + pltpu.roll(second, LANES - 1, 0) + b1_ref[...]
        cdf = 0.5 * (1.0 + jnp.tanh(math.sqrt(2.0 / math.pi) * (hid + 0.044715 * (hid * hid * hid))))
        yg = jnp.dot((hid * cdf).astype(BF16), w2_ref[g], preferred_element_type=F32)
        y = yg if y is None else y + yg
    y = y + b2_ref[...]
    if normed:
        lo = _lane_lo(y.shape)
        y2 = y * y
        s_lo = jnp.sum(jnp.where(lo, y2, 0.0), axis=1, keepdims=True)
        s_hi = jnp.sum(jnp.where(lo, 0.0, y2), axis=1, keepdims=True)
        ms = jnp.where(lo, s_lo, s_hi) * (1.0 / HEAD_DIM)
        y = y * lax.rsqrt(ms + RMS_EPS) * gain_ref[...]
    o_ref[0, 0] = y.astype(o_ref.dtype)


def _compress(a, pos, w1, b1, w2, b2, gain, normed):
    B = a.shape[0]
    hid = w1.shape[1]
    half = NSA_CMP_STRIDE * HEAD_DIM
    zeros = jnp.zeros_like(w2)
    w2p = jnp.stack([jnp.concatenate([w2, zeros], axis=1), jnp.concatenate([zeros, w2], axis=1)]).astype(BF16)
    return pl.pallas_call(
        functools.partial(_compress_kernel, normed=normed),
        grid=(B, 2),
        in_specs=[
            pl.BlockSpec((1, 2, LANES, half), lambda b, m: (b, m, 0, 0)),
            pl.BlockSpec((2, half), lambda b, m: (0, 0)),
            pl.BlockSpec((2, half, hid), lambda b, m: (0, 0, 0)),
            pl.BlockSpec((1, hid), lambda b, m: (0, 0)),
            pl.BlockSpec((2, hid, LANES), lambda b, m: (0, 0, 0)),
            pl.BlockSpec((1, LANES), lambda b, m: (0, 0)),
            pl.BlockSpec((1, LANES), lambda b, m: (0, 0)),
        ],
        out_specs=pl.BlockSpec((1, 1, LANES, LANES), lambda b, m: (b, m, 0, 0)),
        out_shape=jax.ShapeDtypeStruct((B, 2, LANES, LANES), BF16),
        compiler_params=_cparams(2),
        name="nsa_compress",
    )(a, pos.reshape(2, half), w1.reshape(2, half, hid).astype(BF16), b1.reshape(1, hid), w2p,
      jnp.tile(b2, 2).reshape(1, LANES), jnp.tile(gain, 2).reshape(1, LANES))


def _cmp_kernel(q_ref, kc_ref, vc_ref, bias_ref, ovt_ref, o_ref, pen_ref):
    kc = kc_ref[0, 0]
    vc = vc_ref[0, 0]
    S = q_ref.shape[1]
    n_sel = ovt_ref.shape[0]
    lo = _lane_lo((CMP_ROWS, LANES))
    blk = lax.broadcasted_iota(jnp.int32, (n_sel, CMP_ROWS), 0)
    for c in range(S // CMP_ROWS):
        rows = slice(c * CMP_ROWS, (c + 1) * CMP_ROWS)
        psum = [jnp.zeros((CMP_ROWS, LANES), F32), jnp.zeros((CMP_ROWS, LANES), F32)]
        for r in range(4):
            heads = _split_heads(q_ref[0, rows, r * LANES:(r + 1) * LANES])
            outs = []
            for h in range(2):
                s = lax.dot_general(heads[h], kc, NT_DIMS, preferred_element_type=F32) + bias_ref[0, 2 * r + h, rows, :]
                m = jnp.max(s, axis=1, keepdims=True)
                e = jnp.where(s > 0.5 * NEG, jnp.exp(s - m), 0.0)
                l = jnp.sum(e, axis=1, keepdims=True)
                pr = e / jnp.where(l > 0.0, l, 1.0)
                psum[h] = psum[h] + pr
                outs.append(jnp.dot(pr.astype(BF16), vc, preferred_element_type=F32))
            o_ref[0, rows, r * LANES:(r + 1) * LANES] = jnp.where(lo, outs[0], outs[1])

        t = c * CMP_ROWS + lax.broadcasted_iota(jnp.int32, (n_sel, CMP_ROWS), 1)
        cur = lax.shift_right_logical(t, int(math.log2(NSA_SEL_BLOCK)))
        keeps = []
        for h in range(2):
            imp = lax.dot_general(ovt_ref[...], psum[h], NT_DIMS,
                                  precision=lax.Precision.HIGHEST, preferred_element_type=F32)
            forced = jnp.where(blk == 0, 1.0, jnp.where(blk == cur, 1.0, jnp.where(blk == cur - 1, 1.0, 0.0)))
            imp = jnp.where(blk <= cur, imp + forced * NSA_FORCE, -jnp.inf)
            rank = jnp.zeros(imp.shape, F32)
            for m in range(n_sel):
                im = imp[m:m + 1, :]
                tie = jnp.where(blk > m, 1.0, 0.0)
                rank = rank + jnp.where(im > imp, 1.0, jnp.where(im == imp, tie, 0.0))
            keeps.append(jnp.where(rank < NSA_SEL_TOPN, 1.0, 0.0))
        keep = jnp.concatenate(keeps + [jnp.zeros((LANES - 2 * n_sel, CMP_ROWS), F32)], axis=0).T
        lane = lax.broadcasted_iota(jnp.int32, keep.shape, 1)
        pen_ref[0, 0, rows, :] = jnp.where(lane < 2 * n_sel, (keep - 1.0) * (-NEG), 0.0).astype(pen_ref.dtype)


def _cmp_attention(qkv, kc, vc, bias, ovt):
    B, S, _ = qkv.shape
    n_sel = ovt.shape[0]
    return pl.pallas_call(
        _cmp_kernel,
        grid=(2, B),
        in_specs=[
            pl.BlockSpec((1, S, 4 * LANES), lambda m, b: (b, 0, m)),
            pl.BlockSpec((1, 1, LANES, LANES), lambda m, b: (b, m, 0, 0)),
            pl.BlockSpec((1, 1, LANES, LANES), lambda m, b: (b, m, 0, 0)),
            pl.BlockSpec((1, 8, S, LANES), lambda m, b: (m, 0, 0, 0), pipeline_mode=pl.Buffered(1)),
            pl.BlockSpec((n_sel, LANES), lambda m, b: (0, 0)),
        ],
        out_specs=[
            pl.BlockSpec((1, S, 4 * LANES), lambda m, b: (b, 0, m)),
            pl.BlockSpec((1, 1, S, LANES), lambda m, b: (b, m, 0, 0)),
        ],
        out_shape=[jax.ShapeDtypeStruct((B, S, D_MODEL), F32),
                   jax.ShapeDtypeStruct((B, 2, S, LANES), BF16)],
        compiler_params=_cparams(2),
        name="nsa_compressed_select",
    )(qkv, kc, vc, bias, ovt)


def _attn1_kernel(q_ref, ks_ref, vs_ref, kw_ref, vw_ref, rs_ref, rw_ref, e_ref, pen_ref, ocmp_ref, graw_ref,
                  gsel_ref, o_ref, strip_s_ref, qaug_ref, kaug_ref, vsaug_ref, s_ref, p_ref, oslc_ref,
                  strip_w_ref, vwaug_ref, owin_ref):
    p = pl.program_id(0)
    b = pl.program_id(1)
    S = q_ref.shape[1]
    nblk = S // TQ

    @pl.when(b == 0)
    def _():
        _fill_strips(strip_s_ref, rs_ref)
        _fill_strips(strip_w_ref, rw_ref)

    @pl.when(jnp.logical_and(p == 0, b == 0))
    def _():
        for h in range(2):
            kaug_ref[h, :, LANES:] = e_ref[h]
        _fill_value_ones(vsaug_ref)
        _fill_value_ones(vwaug_ref)

    ks = ks_ref[0]
    pen = pen_ref[0, 0]
    for h, qh in enumerate(_split_heads(q_ref[0])):
        kaug_ref[h, :, :LANES] = ks
        qaug_ref[h, :, :LANES] = qh
        qaug_ref[h, :, LANES:] = pen
    vsaug_ref[:, :LANES] = vs_ref[0]
    vwaug_ref[:, :LANES] = vw_ref[0]
    _pair_attention(lambda h, r0, r1: qaug_ref[h, r0:r1, :],
                    lambda h, j: kaug_ref[h, j * TQ:(j + 1) * TQ, :],
                    lambda r0, r1: vsaug_ref[r0:r1, :],
                    strip_s_ref, s_ref, p_ref, oslc_ref, nblk, nblk)
    _pair_attention(lambda h, r0, r1: qaug_ref[h, r0:r1, :LANES],
                    lambda h, j: kw_ref[0, j * TQ:(j + 1) * TQ, :],
                    lambda r0, r1: vwaug_ref[r0:r1, :],
                    strip_w_ref, s_ref, p_ref, owin_ref, nblk, strip_w_ref.shape[1] // TQ)
    sig = _sigmoid(graw_ref[0])
    gates = [jnp.dot(sig, gsel_ref[0, br], precision=lax.Precision.HIGHEST, preferred_element_type=F32)
             for br in range(3)]
    out = gates[0] * ocmp_ref[0] + gates[1] * _merge_heads(oslc_ref) + gates[2] * _merge_heads(owin_ref)
    o_ref[0] = out.astype(o_ref.dtype)


def _attn1(qkv, r_slc, r_win, e_onehot, pen, ocmp, graw, gsel):
    B, S, _ = qkv.shape
    kv0 = D_MODEL // LANES + 4
    n_win = r_win.shape[2]
    return pl.pallas_call(
        _attn1_kernel,
        grid=(N_PAIRS, B),
        in_specs=[
            pl.BlockSpec((1, S, LANES), lambda p, b: (b, 0, p)),
            pl.BlockSpec((1, S, LANES), lambda p, b: (b, 0, kv0 + p // 4)),
            pl.BlockSpec((1, S, LANES), lambda p, b: (b, 0, kv0 + 2 + p // 4)),
            pl.BlockSpec((1, S, LANES), lambda p, b: (b, 0, kv0 + 4 + p // 4)),
            pl.BlockSpec((1, S, LANES), lambda p, b: (b, 0, kv0 + 6 + p // 4)),
            pl.BlockSpec((1, 2, r_slc.shape[2], 2 * TQ), lambda p, b: (p, 0, 0, 0)),
            pl.BlockSpec((1, 2, n_win, 2 * TQ), lambda p, b: (p, 0, 0, 0)),
            pl.BlockSpec((2, S, LANES), lambda p, b: (0, 0, 0), pipeline_mode=pl.Buffered(1)),
            pl.BlockSpec((1, 1, S, LANES), lambda p, b: (b, p // 4, 0, 0)),
            pl.BlockSpec((1, S, LANES), lambda p, b: (b, 0, p)),
            pl.BlockSpec((1, S, LANES), lambda p, b: (b, 0, 0)),
            pl.BlockSpec((1, 3, LANES, LANES), lambda p, b: (p, 0, 0, 0)),
        ],
        out_specs=pl.BlockSpec((1, S, LANES), lambda p, b: (b, 0, p)),
        out_shape=jax.ShapeDtypeStruct((B, S, D_MODEL), BF16),
        scratch_shapes=_attn_scratch(S, S // TQ) + [pltpu.VMEM((2, n_win * TQ, TQ), F32),
                                                    pltpu.VMEM((S, 2 * LANES), BF16), pltpu.VMEM((2, S, LANES), F32)],
        compiler_params=pltpu.CompilerParams(dimension_semantics=("arbitrary",) * 2, vmem_limit_bytes=ATTN_VMEM_LIMIT),
        name="attn_nsa",
    )(qkv, qkv, qkv, qkv, qkv, r_slc, r_win, e_onehot, pen, ocmp, graw, gsel)


def _bucket(dist):
    n = np.maximum(dist, 0)
    exact = REL_BUCKETS // 2
    nf = np.maximum(n, 1).astype(np.float64)
    large = exact + (np.log(nf / exact) / math.log(REL_MAX_DIST / exact) * (REL_BUCKETS - exact)).astype(np.int64)
    return np.where(n < exact, n, np.minimum(large, REL_BUCKETS - 1))


def _strip_rows(tab, n_off, extra_of_dist):
    u = np.arange(2 * TQ)
    d = np.arange(n_off)[:, None] * TQ - np.where(u < TQ, u, u - 2 * TQ)[None, :]
    onehot = (_bucket(d)[None] == np.arange(REL_BUCKETS)[:, None, None]).astype(np.float32)
    vals = jnp.einsum("hb,bou->hou", tab, jnp.asarray(onehot), precision=lax.Precision.HIGHEST) + extra_of_dist(d)
    return vals.reshape(tab.shape[0] // 2, 2, n_off, 2 * TQ)


def _causal_mask(d):
    return np.where(d >= 0, 0.0, NEG).astype(np.float32)


def _window_mask(d):
    return np.where((d >= 0) & (d < NSA_WINDOW), 0.0, NEG).astype(np.float32)


def _dilation_log_count(d):
    c = ((d >= 0) & (d <= 128)).astype(np.float64)
    c += ((d >= 0) & (d % 4 == 0) & (d <= 512))
    c += ((d >= 0) & (d % 16 == 0) & (d <= 2048))
    return np.where(c > 0, np.log(np.maximum(c, 1.0)), NEG).astype(np.float32)


def _one_hot_blocks(S, block, per_head):
    e = np.zeros((2, S, LANES), np.float32)
    key = np.arange(S)
    for h in range(2):
        e[h, key, h * per_head + key // block] = 1.0
    return jnp.asarray(e, BF16)


_NSA_HEAD_ORDER = np.array([8 * (p // 4) + (p % 4) + 4 * h for p in range(N_PAIRS) for h in range(2)])


def _head_cols(heads):
    return (np.asarray(heads)[:, None] * HEAD_DIM + np.arange(HEAD_DIM)[None, :]).reshape(-1)


def _mixer_ab(h, g_mix, w_in, qn_a, kn_a, qn_b, kn_b, rel_bias, B, S):
    wa = MOBA_HEADS * HEAD_DIM
    sec = [w_in[:, k * wa:(k + 1) * wa] for k in range(6)]
    w = jnp.concatenate([sec[0], sec[3], sec[1], sec[4], sec[2], sec[5]], axis=1).astype(BF16)
    ones = jnp.ones((wa,), F32)
    gain = jnp.concatenate([jnp.tile(qn_a, 8) * ATTN_SCALE, jnp.tile(qn_b, 8) * ATTN_SCALE,
                            jnp.tile(kn_a, 8), jnp.tile(kn_b, 8), ones, ones])
    per_sec = wa // PROJ_CHUNK
    (qkv,) = _project(h, g_mix, w, gain, [True] * (4 * per_sec) + [False] * (2 * per_sec))
    qkv = qkv.reshape(B, S, 3 * D_MODEL)

    tab = rel_bias.T
    r_tab = jnp.concatenate([_strip_rows(tab[:MOBA_HEADS], S // TQ, _causal_mask),
                             _strip_rows(tab[MOBA_HEADS:], S // TQ, _dilation_log_count)])
    o = _attn0(qkv, r_tab, _one_hot_blocks(S, MOBA_BLOCK, S // MOBA_BLOCK))
    return o.reshape(B * S, D_MODEL)


def _mixer_nsa(h, g_mix, w_in, qn, kn_c, kn_s, kn_w, cmp_k, cmp_v, rel_bias, B, S):
    qw = N_HEADS * HEAD_DIM
    kvw = NSA_GROUPS * HEAD_DIM
    order = _NSA_HEAD_ORDER
    w_main = jnp.concatenate([w_in[:, _head_cols(order)], w_in[:, qw:qw + 6 * kvw]], axis=1).astype(BF16)
    ones = jnp.ones((kvw,), F32)
    gain = jnp.concatenate([jnp.tile(qn, N_HEADS) * ATTN_SCALE, ones, ones,
                            jnp.tile(kn_s, NSA_GROUPS), ones, jnp.tile(kn_w, NSA_GROUPS), ones])
    assert kvw == PROJ_CHUNK
    norm_chunks = [True] * (qw // PROJ_CHUNK) + [False, False, True, False, True, False]

    gcols = np.zeros((LANES,), np.int64)
    gused = np.zeros((LANES,), np.float32)
    gsel = np.zeros((N_PAIRS, 3, LANES, LANES), np.float32)
    for p in range(N_PAIRS):
        for br in range(3):
            for hh in range(2):
                c = 8 * p + 2 * br + hh
                gcols[c] = qw + 6 * kvw + 3 * order[2 * p + hh] + br
                gused[c] = 1.0
                gsel[p, br, c, hh * HEAD_DIM:(hh + 1) * HEAD_DIM] = 1.0
    w_gate = (w_in[:, gcols] * gused).astype(BF16)
    qkv, graw = _project(h, g_mix, w_main, gain, norm_chunks, w_gate)
    qkv = qkv.reshape(B, S, qw + 6 * kvw)
    graw = graw.reshape(B, S, LANES)

    def chunked(col0):
        t = qkv[:, :, col0:col0 + kvw].reshape(B, S // NSA_CMP_STRIDE, NSA_CMP_STRIDE, NSA_GROUPS, HEAD_DIM)
        return t.transpose(0, 3, 1, 2, 4).reshape(B, NSA_GROUPS, S // NSA_CMP_STRIDE, NSA_CMP_STRIDE * HEAD_DIM)

    kc = _compress(chunked(qw), *cmp_k, kn_c, True)
    vc = _compress(chunked(qw + kvw), *cmp_v, kn_c, False)

    tab = rel_bias.T[order]
    n_cmp = (S - NSA_CMP_LEN) // NSA_CMP_STRIDE + 1
    t_pos = np.arange(S)[:, None]
    c_idx = np.arange(LANES)[None, :]
    dc = t_pos - (c_idx * NSA_CMP_STRIDE + NSA_CMP_LEN - 1)
    cmp_mask = np.where((dc >= 0) & (c_idx < n_cmp), 0.0, NEG).astype(np.float32)
    onehot = jnp.asarray(_bucket(dc).astype(np.int8))[None] == jnp.arange(REL_BUCKETS, dtype=jnp.int8)[:, None, None]
    bias_c = jnp.einsum("hb,bsc->hsc", tab, onehot.astype(F32), precision=lax.Precision.HIGHEST)
    bias_c = (bias_c + cmp_mask).reshape(2, 8, S, LANES)
    n_sel = S // NSA_SEL_BLOCK
    cstart = np.arange(LANES) * NSA_CMP_STRIDE
    sstart = np.arange(n_sel) * NSA_SEL_BLOCK
    ovt = np.maximum(np.minimum(cstart[None, :] + NSA_CMP_LEN, sstart[:, None] + NSA_SEL_BLOCK)
                     - np.maximum(cstart[None, :], sstart[:, None]), 0).astype(np.float32)
    ovt[:, n_cmp:] = 0.0
    ocmp, pen = _cmp_attention(qkv, kc, vc, bias_c, jnp.asarray(ovt))

    r_slc = _strip_rows(tab, S // TQ, _causal_mask)
    r_win = _strip_rows(tab, NSA_WINDOW // TQ + 1, _window_mask)
    o = _attn1(qkv, r_slc, r_win, _one_hot_blocks(S, NSA_SEL_BLOCK, n_sel), pen, ocmp, graw, jnp.asarray(gsel))
    return o.reshape(B * S, D_MODEL)


def kernel(x, p, rel_bias, norm_mix, norm_ffn, norm_ple, w_ffn_gate, w_ffn_up, w_ffn_down, w_ple_proj, w_ple_gate, w_in_ab, w_out_ab, qn_moba, kn_moba, qn_dil, kn_dil, w_in_nsa, w_out_nsa, qn_nsa, kn_cmp, kn_slc, kn_win, cmp_k_pos, cmp_k_w1, cmp_k_b1, cmp_k_w2, cmp_k_b2, cmp_v_pos, cmp_v_w1, cmp_v_b1, cmp_v_w2, cmp_v_b2):
    B, S, D = x.shape
    depth = p.shape[0]
    h = x.reshape(B * S, D)
    for i in range(depth):
        e = i // 2
        if i % 2 == 0:
            o = _mixer_ab(h, norm_mix[i], w_in_ab[e], qn_moba[e], kn_moba[e], qn_dil[e], kn_dil[e], rel_bias, B, S)
            w_out = w_out_ab[e]
        else:
            cmp_k = (cmp_k_pos[e], cmp_k_w1[e], cmp_k_b1[e], cmp_k_w2[e], cmp_k_b2[e])
            cmp_v = (cmp_v_pos[e], cmp_v_w1[e], cmp_v_b1[e], cmp_v_w2[e], cmp_v_b2[e])
            o = _mixer_nsa(h, norm_mix[i], w_in_nsa[e], qn_nsa[e], kn_cmp[e], kn_slc[e], kn_win[e],
                           cmp_k, cmp_v, rel_bias, B, S)
            w_out = w_out_nsa[e][_head_cols(_NSA_HEAD_ORDER), :]
        h = _post_attention(o, h, w_out.astype(BF16), norm_ffn[i], w_ffn_gate[i].astype(BF16),
                            w_ffn_up[i].astype(BF16), w_ffn_down[i].astype(BF16), norm_ple[i],
                            w_ple_gate[i].astype(BF16), p[i].reshape(B * S, -1), w_ple_proj[i].astype(BF16))
    return h.reshape(B, S, D)
```

```python
import functools
import math

import numpy as np
import jax
import jax.numpy as jnp
from jax import lax
from jax.experimental import pallas as pl
from jax.experimental.pallas import tpu as pltpu

F32 = jnp.float32
BF16 = jnp.bfloat16

D_MODEL = 1024
HEAD_DIM = 64
N_HEADS = 16
N_PAIRS = N_HEADS // 2
MOBA_HEADS = 8
MOBA_BLOCK = 256
MOBA_TOPK = 3
MOBA_NBLK = 8
NSA_GROUPS = 4
NSA_CMP_LEN = 32
NSA_CMP_STRIDE = 16
NSA_SEL_BLOCK = 64
NSA_SEL_TOPN = 16
NSA_WINDOW = 512
NSA_FORCE = 1.0e6
REL_BUCKETS = 32
REL_MAX_DIST = 2048
RMS_EPS = 1e-6
ATTN_SCALE = HEAD_DIM ** -0.5

LANES = 128
TQ = 256
NEG = -1.0e30
ROW_TILE = 512
PROJ_CHUNK = 256
FFN_SPLIT = 2
CMP_ROWS = 512
VMEM_LIMIT = 48 * 1024 * 1024
POST_VMEM_LIMIT = 56 * 1024 * 1024
ATTN_VMEM_LIMIT = 56 * 1024 * 1024

NT_DIMS = (((1,), (1,)), ((), ()))


def _cparams(n_axes):
    return pltpu.CompilerParams(dimension_semantics=("arbitrary",) * n_axes,
                                vmem_limit_bytes=VMEM_LIMIT)


def _rms_rows(x, g):
    ms = jnp.mean(x * x, axis=-1, keepdims=True)
    return x * lax.rsqrt(ms + RMS_EPS) * g


def _lane_lo(shape):
    return lax.broadcasted_iota(jnp.int32, shape, len(shape) - 1) < HEAD_DIM


def _resident(shape):
    return pl.BlockSpec(shape, lambda i: (0,) * len(shape), pipeline_mode=pl.Buffered(1))


def _proj_kernel(x_ref, g_ref, w_ref, cg_ref, bd_ref, *rest, norm_chunks, with_extra):
    if with_extra:
        wx_ref, o_ref, ox_ref = rest
    else:
        (o_ref,) = rest
    xn = _rms_rows(x_ref[...], g_ref[...]).astype(BF16)
    bd = bd_ref[...]

    def finish(c, y):
        cols = slice(c * PROJ_CHUNK, (c + 1) * PROJ_CHUNK)
        if norm_chunks[c]:
            y2 = y * y
            hi = y2.astype(BF16)
            lo = (y2 - hi.astype(F32)).astype(BF16)
            ssq = jnp.dot(hi, bd, preferred_element_type=F32) + jnp.dot(lo, bd, preferred_element_type=F32)
            y = y * lax.rsqrt(ssq * (1.0 / HEAD_DIM) + RMS_EPS) * cg_ref[:, cols]
        o_ref[:, cols] = y.astype(o_ref.dtype)

    prev = None
    for c in range(len(norm_chunks)):
        y = jnp.dot(xn, w_ref[:, c * PROJ_CHUNK:(c + 1) * PROJ_CHUNK], preferred_element_type=F32)
        if prev is not None:
            finish(c - 1, prev)
        prev = y
    finish(len(norm_chunks) - 1, prev)
    if with_extra:
        ox_ref[...] = jnp.dot(xn, wx_ref[...], preferred_element_type=F32)


def _project(x, g, w, col_gain, norm_chunks, w_extra=None):
    T, D = x.shape
    N = w.shape[1]
    blk = np.kron(np.eye(PROJ_CHUNK // HEAD_DIM), np.ones((HEAD_DIM, HEAD_DIM))).astype(np.float32)
    row_in = lambda n: pl.BlockSpec((ROW_TILE, n), lambda i: (i, 0))
    in_specs = [row_in(D), _resident((1, D)), _resident((D, N)), _resident((1, N)),
                _resident((PROJ_CHUNK, PROJ_CHUNK))]
    args = [x, g.reshape(1, D), w, col_gain.reshape(1, N), jnp.asarray(blk, BF16)]
    out_specs = [row_in(N)]
    out_shape = [jax.ShapeDtypeStruct((T, N), BF16)]
    if w_extra is not None:
        nx = w_extra.shape[1]
        in_specs.append(_resident((D, nx)))
        args.append(w_extra)
        out_specs.append(row_in(nx))
        out_shape.append(jax.ShapeDtypeStruct((T, nx), F32))
    return pl.pallas_call(
        functools.partial(_proj_kernel, norm_chunks=tuple(norm_chunks), with_extra=w_extra is not None),
        grid=(T // ROW_TILE,),
        in_specs=in_specs,
        out_specs=out_specs,
        out_shape=out_shape,
        compiler_params=_cparams(1),
        name="proj",
    )(*args)


def _sigmoid(z):
    return 1.0 / (1.0 + jnp.exp(-z))


def _post_kernel(o_ref, h_ref, wout_ref, gf_ref, wg_ref, wu_ref, wd_ref, gp_ref, wpg_ref, p_ref, wpp_ref, out_ref):
    h1 = h_ref[...] + jnp.dot(o_ref[...], wout_ref[...], preferred_element_type=F32)
    xn = _rms_rows(h1, gf_ref[...]).astype(BF16)
    h2 = h1
    fh = wg_ref.shape[1]
    for c in range(FFN_SPLIT):
        cols = slice(c * fh // FFN_SPLIT, (c + 1) * fh // FFN_SPLIT)
        a = jnp.dot(xn, wg_ref[:, cols], preferred_element_type=F32)
        u = jnp.dot(xn, wu_ref[:, cols], preferred_element_type=F32)
        act = (a * _sigmoid(a) * u).astype(BF16)
        h2 = h2 + jnp.dot(act, wd_ref[cols, :], preferred_element_type=F32)
    hn = _rms_rows(h2, gp_ref[...]).astype(BF16)
    gate = _sigmoid(jnp.dot(hn, wpg_ref[...], preferred_element_type=F32))
    out_ref[...] = h2 + gate * jnp.dot(p_ref[...].astype(BF16), wpp_ref[...], preferred_element_type=F32)


def _post_attention(o, h, w_out, g_ffn, wg, wu, wd, g_ple, w_pgate, p, w_pproj):
    T, D = h.shape
    Fh = wg.shape[1]
    Pd = p.shape[1]
    row_in = lambda n: pl.BlockSpec((ROW_TILE, n), lambda i: (i, 0))
    return pl.pallas_call(
        _post_kernel,
        grid=(T // ROW_TILE,),
        in_specs=[row_in(D), row_in(D), _resident((D, D)), _resident((1, D)), _resident((D, Fh)),
                  _resident((D, Fh)), _resident((Fh, D)), _resident((1, D)), _resident((D, D)),
                  row_in(Pd), _resident((Pd, D))],
        out_specs=row_in(D),
        out_shape=jax.ShapeDtypeStruct((T, D), F32),
        compiler_params=pltpu.CompilerParams(dimension_semantics=("arbitrary",), vmem_limit_bytes=POST_VMEM_LIMIT),
        name="post_attention",
    )(o, h, w_out, g_ffn.reshape(1, D), wg, wu, wd, g_ple.reshape(1, D), w_pgate, p, w_pproj)


def _fill_strips(strip_ref, r_ref):
    for h in range(2):
        for o in range(r_ref.shape[2]):
            x = jnp.broadcast_to(r_ref[0, h, o:o + 1, :], (TQ, 2 * TQ))
            strip_ref[h, o * TQ:(o + 1) * TQ, :] = pltpu.roll(x, 0, 1, stride=1, stride_axis=0)[:, :TQ]


def _tile_groups(nblk, span):
    cnt = [min(span, nblk - j) for j in range(nblk)]
    off = [0]
    for c in cnt:
        off.append(off[-1] + c * TQ)
    return cnt, off


def _tile_rows(nblk, span):
    return _tile_groups(nblk, span)[1][-1]


def _pair_attention(q_rows, k_rows, v_rows, strip_ref, s_ref, p_ref, o_ref, nblk, span, skew=1):
    cnt, off = _tile_groups(nblk, span)
    together = s_ref.shape[0] >= 2 * off[-1] and p_ref.shape[2] >= 2 * span * TQ
    s0 = [0, off[-1] if together else 0]
    p0 = [0, span * TQ if together else 0]

    def scores(h, j):
        n = cnt[j] * TQ
        s = lax.dot_general(q_rows(h, j * TQ, j * TQ + n), k_rows(h, j), NT_DIMS, preferred_element_type=F32)
        s_ref[s0[h] + off[j]:s0[h] + off[j] + n, :] = s + strip_ref[h, 0:n, :]

    def probs(h, i):
        j0 = max(0, i - span + 1)
        rows = [s0[h] + off[j] + (i - j) * TQ for j in range(j0, i + 1)]
        mx = s_ref[rows[0]:rows[0] + TQ, :]
        for r in rows[1:]:
            mx = jnp.maximum(mx, s_ref[r:r + TQ, :])
        m = jnp.max(mx, axis=1, keepdims=True)
        for t, r in enumerate(rows):
            p_ref[i, :, p0[h] + t * TQ:p0[h] + (t + 1) * TQ] = jnp.exp(s_ref[r:r + TQ, :] - m).astype(BF16)

    def values(h, i):
        j0 = max(0, i - span + 1)
        out = jnp.dot(p_ref[i, :, p0[h]:p0[h] + (i + 1 - j0) * TQ], v_rows(j0 * TQ, (i + 1) * TQ),
                      preferred_element_type=F32)
        o_ref[h, i * TQ:(i + 1) * TQ, :] = out[:, :LANES] / out[:, LANES:]

    for heads in ([(0, 1)] if together else [(0,), (1,)]):
        for j in range(nblk + skew):
            for h in heads:
                if j < nblk:
                    scores(h, j)
            for h in heads:
                if j >= skew:
                    values(h, j - skew)
            for h in heads:
                if j < nblk:
                    probs(h, j)


def _merge_heads(o_ref):
    return jnp.where(_lane_lo((o_ref.shape[1], LANES)), o_ref[0], o_ref[1])


def _split_heads(q):
    lo = _lane_lo(q.shape)
    zero = jnp.zeros_like(q)
    return jnp.where(lo, q, zero), jnp.where(lo, zero, q)


def _fill_value_ones(vaug_ref):
    vaug_ref[:, LANES:] = jnp.ones((vaug_ref.shape[0], LANES), vaug_ref.dtype)


def _moba_block_means(k_ref):
    shape = (2 * MOBA_NBLK, LANES)
    lo = _lane_lo(shape)
    row = lax.broadcasted_iota(jnp.int32, shape, 0)
    kmt = jnp.zeros(shape, F32)
    for n in range(MOBA_NBLK):
        mean = jnp.mean(k_ref[0, n * MOBA_BLOCK:(n + 1) * MOBA_BLOCK, :].astype(F32), axis=0, keepdims=True)
        kmt = jnp.where(row == n, jnp.where(lo, mean, 0.0), kmt)
        kmt = jnp.where(row == MOBA_NBLK + n, jnp.where(lo, 0.0, mean), kmt)
    return kmt


def _moba_penalty(q, kmt, i):
    nb = MOBA_NBLK
    g = lax.dot_general(kmt, q.astype(F32), NT_DIMS,
                        precision=lax.Precision.HIGHEST, preferred_element_type=F32)
    row = lax.broadcasted_iota(jnp.int32, g.shape, 0)
    n = row & (nb - 1)
    rank = jnp.zeros(g.shape, F32)
    for m in range(i):
        gm = jnp.where(row < nb, g[m:m + 1, :], g[nb + m:nb + m + 1, :])
        tie = jnp.where(n > m, 1.0, 0.0)
        rank = rank + jnp.where(gm > g, 1.0, jnp.where(gm == g, tie, 0.0))
    keep = jnp.where(n < i, jnp.where(rank < MOBA_TOPK, 1.0, 0.0), jnp.where(n == i, 1.0, 0.0))
    keep = jnp.concatenate([keep, jnp.zeros((LANES - 2 * nb, TQ), F32)], axis=0).T
    lane = lax.broadcasted_iota(jnp.int32, keep.shape, 1)
    return jnp.where(lane < 2 * nb, (keep - 1.0) * (-NEG), 0.0)


def _attn0_kernel(q_ref, k_ref, v_ref, r_ref, e_ref, o_ref, strip_ref, qaug_ref, kaug_ref, vaug_ref,
                  s_ref, p_ref, oh_ref):
    p = pl.program_id(0)
    b = pl.program_id(1)
    S = q_ref.shape[1]
    nblk = S // TQ

    @pl.when(b == 0)
    def _():
        _fill_strips(strip_ref, r_ref)

    @pl.when(jnp.logical_and(p == 0, b == 0))
    def _():
        for h in range(2):
            kaug_ref[h, :, LANES:] = e_ref[h]
        _fill_value_ones(vaug_ref)

    k = k_ref[0]
    for h, qh in enumerate(_split_heads(q_ref[0])):
        kaug_ref[h, :, :LANES] = k
        qaug_ref[h, :, :LANES] = qh
    vaug_ref[:, :LANES] = v_ref[0]
    kmt = _moba_block_means(k_ref)
    is_moba = p < MOBA_HEADS // 2
    for i in range(nblk):
        rows = slice(i * TQ, (i + 1) * TQ)
        pen = jnp.where(is_moba, _moba_penalty(q_ref[0, rows, :], kmt, i), 0.0).astype(BF16)
        for h in range(2):
            qaug_ref[h, rows, LANES:] = pen
    _pair_attention(lambda h, r0, r1: qaug_ref[h, r0:r1, :],
                    lambda h, j: kaug_ref[h, j * TQ:(j + 1) * TQ, :],
                    lambda r0, r1: vaug_ref[r0:r1, :],
                    strip_ref, s_ref, p_ref, oh_ref, nblk, nblk)
    o_ref[0] = _merge_heads(oh_ref).astype(o_ref.dtype)


def _attn_scratch(S, span, narrow_span=0):
    nblk = S // TQ
    s_rows = max(_tile_rows(nblk, span), 2 * _tile_rows(nblk, narrow_span))
    return [pltpu.VMEM((2, S, TQ), F32), pltpu.VMEM((2, S, 2 * LANES), BF16), pltpu.VMEM((2, S, 2 * LANES), BF16),
            pltpu.VMEM((S, 2 * LANES), BF16), pltpu.VMEM((s_rows, TQ), F32),
            pltpu.VMEM((nblk, TQ, span * TQ), BF16), pltpu.VMEM((2, S, LANES), F32)]


def _attn0(qkv, r_tab, e_onehot):
    B, S, _ = qkv.shape
    nblk = S // TQ
    return pl.pallas_call(
        _attn0_kernel,
        grid=(N_PAIRS, B),
        in_specs=[
            pl.BlockSpec((1, S, LANES), lambda p, b: (b, 0, p)),
            pl.BlockSpec((1, S, LANES), lambda p, b: (b, 0, N_PAIRS + p)),
            pl.BlockSpec((1, S, LANES), lambda p, b: (b, 0, 2 * N_PAIRS + p)),
            pl.BlockSpec((1, 2, nblk, 2 * TQ), lambda p, b: (p, 0, 0, 0)),
            pl.BlockSpec((2, S, LANES), lambda p, b: (0, 0, 0), pipeline_mode=pl.Buffered(1)),
        ],
        out_specs=pl.BlockSpec((1, S, LANES), lambda p, b: (b, 0, p)),
        out_shape=jax.ShapeDtypeStruct((B, S, D_MODEL), BF16),
        scratch_shapes=_attn_scratch(S, nblk),
        compiler_params=pltpu.CompilerParams(dimension_semantics=("arbitrary",) * 2, vmem_limit_bytes=ATTN_VMEM_LIMIT),
        name="attn_moba_dilated",
    )(qkv, qkv, qkv, r_tab, e_onehot)


def _compress_kernel(a_ref, pos_ref, w1_ref, b1_ref, w2_ref, b2_ref, gain_ref, o_ref, *, normed):
    y = None
    for g in range(2):
        a = a_ref[0, g].astype(F32)
        first = jnp.dot((a + pos_ref[0:1, :]).astype(BF16), w1_ref[0], preferred_element_type=F32)
        second = jnp.dot((a + pos_ref[1:2, :]).astype(BF16), w1_ref[1], preferred_element_type=F32)
        hid = first + pltpu.roll(second, LANES - 1, 0) + b1_ref[...]
        cdf = 0.5 * (1.0 + jnp.tanh(math.sqrt(2.0 / math.pi) * (hid + 0.044715 * (hid * hid * hid))))
        yg = jnp.dot((hid * cdf).astype(BF16), w2_ref[g], preferred_element_type=F32)
        y = yg if y is None else y + yg
    y = y + b2_ref[...]
    if normed:
        lo = _lane_lo(y.shape)
        y2 = y * y
        s_lo = jnp.sum(jnp.where(lo, y2, 0.0), axis=1, keepdims=True)
        s_hi = jnp.sum(jnp.where(lo, 0.0, y2), axis=1, keepdims=True)
        ms = jnp.where(lo, s_lo, s_hi) * (1.0 / HEAD_DIM)
        y = y * lax.rsqrt(ms + RMS_EPS) * gain_ref[...]
    o_ref[0, 0] = y.astype(o_ref.dtype)


def _compress(a, pos, w1, b1, w2, b2, gain, normed):
    B = a.shape[0]
    hid = w1.shape[1]
    half = NSA_CMP_STRIDE * HEAD_DIM
    zeros = jnp.zeros_like(w2)
    w2p = jnp.stack([jnp.concatenate([w2, zeros], axis=1), jnp.concatenate([zeros, w2], axis=1)]).astype(BF16)
    return pl.pallas_call(
        functools.partial(_compress_kernel, normed=normed),
        grid=(B, 2),
        in_specs=[
            pl.BlockSpec((1, 2, LANES, half), lambda b, m: (b, m, 0, 0)),
            pl.BlockSpec((2, half), lambda b, m: (0, 0)),
            pl.BlockSpec((2, half, hid), lambda b, m: (0, 0, 0)),
            pl.BlockSpec((1, hid), lambda b, m: (0, 0)),
            pl.BlockSpec((2, hid, LANES), lambda b, m: (0, 0, 0)),
            pl.BlockSpec((1, LANES), lambda b, m: (0, 0)),
            pl.BlockSpec((1, LANES), lambda b, m: (0, 0)),
        ],
        out_specs=pl.BlockSpec((1, 1, LANES, LANES), lambda b, m: (b, m, 0, 0)),
        out_shape=jax.ShapeDtypeStruct((B, 2, LANES, LANES), BF16),
        compiler_params=_cparams(2),
        name="nsa_compress",
    )(a, pos.reshape(2, half), w1.reshape(2, half, hid).astype(BF16), b1.reshape(1, hid), w2p,
      jnp.tile(b2, 2).reshape(1, LANES), jnp.tile(gain, 2).reshape(1, LANES))


def _cmp_kernel(q_ref, kc_ref, vc_ref, bias_ref, ovt_ref, o_ref, pen_ref):
    kc = kc_ref[0, 0]
    vc = vc_ref[0, 0]
    S = q_ref.shape[1]
    n_sel = ovt_ref.shape[0]
    lo = _lane_lo((CMP_ROWS, LANES))
    blk = lax.broadcasted_iota(jnp.int32, (n_sel, CMP_ROWS), 0)
    for c in range(S // CMP_ROWS):
        rows = slice(c * CMP_ROWS, (c + 1) * CMP_ROWS)
        psum = [jnp.zeros((CMP_ROWS, LANES), F32), jnp.zeros((CMP_ROWS, LANES), F32)]
        for r in range(4):
            heads = _split_heads(q_ref[0, rows, r * LANES:(r + 1) * LANES])
            outs = []
            for h in range(2):
                s = lax.dot_general(heads[h], kc, NT_DIMS, preferred_element_type=F32) + bias_ref[0, 2 * r + h, rows, :]
                m = jnp.max(s, axis=1, keepdims=True)
                e = jnp.where(s > 0.5 * NEG, jnp.exp(s - m), 0.0)
                l = jnp.sum(e, axis=1, keepdims=True)
                pr = e / jnp.where(l > 0.0, l, 1.0)
                psum[h] = psum[h] + pr
                outs.append(jnp.dot(pr.astype(BF16), vc, preferred_element_type=F32))
            o_ref[0, rows, r * LANES:(r + 1) * LANES] = jnp.where(lo, outs[0], outs[1])

        t = c * CMP_ROWS + lax.broadcasted_iota(jnp.int32, (n_sel, CMP_ROWS), 1)
        cur = lax.shift_right_logical(t, int(math.log2(NSA_SEL_BLOCK)))
        keeps = []
        for h in range(2):
            imp = lax.dot_general(ovt_ref[...], psum[h], NT_DIMS,
                                  precision=lax.Precision.HIGHEST, preferred_element_type=F32)
            forced = jnp.where(blk == 0, 1.0, jnp.where(blk == cur, 1.0, jnp.where(blk == cur - 1, 1.0, 0.0)))
            imp = jnp.where(blk <= cur, imp + forced * NSA_FORCE, -jnp.inf)
            rank = jnp.zeros(imp.shape, F32)
            for m in range(n_sel):
                im = imp[m:m + 1, :]
                tie = jnp.where(blk > m, 1.0, 0.0)
                rank = rank + jnp.where(im > imp, 1.0, jnp.where(im == imp, tie, 0.0))
            keeps.append(jnp.where(rank < NSA_SEL_TOPN, 1.0, 0.0))
        keep = jnp.concatenate(keeps + [jnp.zeros((LANES - 2 * n_sel, CMP_ROWS), F32)], axis=0).T
        lane = lax.broadcasted_iota(jnp.int32, keep.shape, 1)
        pen_ref[0, 0, rows, :] = jnp.where(lane < 2 * n_sel, (keep - 1.0) * (-NEG), 0.0).astype(pen_ref.dtype)


def _cmp_attention(qkv, kc, vc, bias, ovt):
    B, S, _ = qkv.shape
    n_sel = ovt.shape[0]
    return pl.pallas_call(
        _cmp_kernel,
        grid=(2, B),
        in_specs=[
            pl.BlockSpec((1, S, 4 * LANES), lambda m, b: (b, 0, m)),
            pl.BlockSpec((1, 1, LANES, LANES), lambda m, b: (b, m, 0, 0)),
            pl.BlockSpec((1, 1, LANES, LANES), lambda m, b: (b, m, 0, 0)),
            pl.BlockSpec((1, 8, S, LANES), lambda m, b: (m, 0, 0, 0), pipeline_mode=pl.Buffered(1)),
            pl.BlockSpec((n_sel, LANES), lambda m, b: (0, 0)),
        ],
        out_specs=[
            pl.BlockSpec((1, S, 4 * LANES), lambda m, b: (b, 0, m)),
            pl.BlockSpec((1, 1, S, LANES), lambda m, b: (b, m, 0, 0)),
        ],
        out_shape=[jax.ShapeDtypeStruct((B, S, D_MODEL), F32),
                   jax.ShapeDtypeStruct((B, 2, S, LANES), BF16)],
        compiler_params=_cparams(2),
        name="nsa_compressed_select",
    )(qkv, kc, vc, bias, ovt)


def _attn1_kernel(q_ref, ks_ref, vs_ref, kw_ref, vw_ref, rs_ref, rw_ref, e_ref, pen_ref, ocmp_ref, graw_ref,
                  gsel_ref, o_ref, strip_s_ref, qaug_ref, kaug_ref, vsaug_ref, s_ref, p_ref, oslc_ref,
                  strip_w_ref, vwaug_ref, owin_ref):
    p = pl.program_id(0)
    b = pl.program_id(1)
    S = q_ref.shape[1]
    nblk = S // TQ

    @pl.when(b == 0)
    def _():
        _fill_strips(strip_s_ref, rs_ref)
        _fill_strips(strip_w_ref, rw_ref)

    @pl.when(jnp.logical_and(p == 0, b == 0))
    def _():
        for h in range(2):
            kaug_ref[h, :, LANES:] = e_ref[h]
        _fill_value_ones(vsaug_ref)
        _fill_value_ones(vwaug_ref)

    ks = ks_ref[0]
    pen = pen_ref[0, 0]
    for h, qh in enumerate(_split_heads(q_ref[0])):
        kaug_ref[h, :, :LANES] = ks
        qaug_ref[h, :, :LANES] = qh
        qaug_ref[h, :, LANES:] = pen
    vsaug_ref[:, :LANES] = vs_ref[0]
    vwaug_ref[:, :LANES] = vw_ref[0]
    _pair_attention(lambda h, r0, r1: qaug_ref[h, r0:r1, :],
                    lambda h, j: kaug_ref[h, j * TQ:(j + 1) * TQ, :],
                    lambda r0, r1: vsaug_ref[r0:r1, :],
                    strip_s_ref, s_ref, p_ref, oslc_ref, nblk, nblk)
    _pair_attention(lambda h, r0, r1: qaug_ref[h, r0:r1, :LANES],
                    lambda h, j: kw_ref[0, j * TQ:(j + 1) * TQ, :],
                    lambda r0, r1: vwaug_ref[r0:r1, :],
                    strip_w_ref, s_ref, p_ref, owin_ref, nblk, strip_w_ref.shape[1] // TQ, skew=2)
    sig = _sigmoid(graw_ref[0])
    hi = sig.astype(BF16)
    lo = (sig - hi.astype(F32)).astype(BF16)
    gates = [jnp.dot(hi, gsel_ref[0, br], preferred_element_type=F32)
             + jnp.dot(lo, gsel_ref[0, br], preferred_element_type=F32) for br in range(3)]
    out = gates[0] * ocmp_ref[0] + gates[1] * _merge_heads(oslc_ref) + gates[2] * _merge_heads(owin_ref)
    o_ref[0] = out.astype(o_ref.dtype)


def _attn1(qkv, r_slc, r_win, e_onehot, pen, ocmp, graw, gsel):
    B, S, _ = qkv.shape
    kv0 = D_MODEL // LANES + 4
    n_win = r_win.shape[2]
    return pl.pallas_call(
        _attn1_kernel,
        grid=(N_PAIRS, B),
        in_specs=[
            pl.BlockSpec((1, S, LANES), lambda p, b: (b, 0, p)),
            pl.BlockSpec((1, S, LANES), lambda p, b: (b, 0, kv0 + p // 4)),
            pl.BlockSpec((1, S, LANES), lambda p, b: (b, 0, kv0 + 2 + p // 4)),
            pl.BlockSpec((1, S, LANES), lambda p, b: (b, 0, kv0 + 4 + p // 4)),
            pl.BlockSpec((1, S, LANES), lambda p, b: (b, 0, kv0 + 6 + p // 4)),
            pl.BlockSpec((1, 2, r_slc.shape[2], 2 * TQ), lambda p, b: (p, 0, 0, 0)),
            pl.BlockSpec((1, 2, n_win, 2 * TQ), lambda p, b: (p, 0, 0, 0)),
            pl.BlockSpec((2, S, LANES), lambda p, b: (0, 0, 0), pipeline_mode=pl.Buffered(1)),
            pl.BlockSpec((1, 1, S, LANES), lambda p, b: (b, p // 4, 0, 0)),
            pl.BlockSpec((1, S, LANES), lambda p, b: (b, 0, p)),
            pl.BlockSpec((1, S, LANES), lambda p, b: (b, 0, 0)),
            pl.BlockSpec((1, 3, LANES, LANES), lambda p, b: (p, 0, 0, 0)),
        ],
        out_specs=pl.BlockSpec((1, S, LANES), lambda p, b: (b, 0, p)),
        out_shape=jax.ShapeDtypeStruct((B, S, D_MODEL), BF16),
        scratch_shapes=_attn_scratch(S, S // TQ, n_win) + [pltpu.VMEM((2, n_win * TQ, TQ), F32),
                                                    pltpu.VMEM((S, 2 * LANES), BF16), pltpu.VMEM((2, S, LANES), F32)],
        compiler_params=pltpu.CompilerParams(dimension_semantics=("arbitrary",) * 2, vmem_limit_bytes=ATTN_VMEM_LIMIT),
        name="attn_nsa",
    )(qkv, qkv, qkv, qkv, qkv, r_slc, r_win, e_onehot, pen, ocmp, graw, gsel)


def _bucket(dist):
    n = np.maximum(dist, 0)
    exact = REL_BUCKETS // 2
    nf = np.maximum(n, 1).astype(np.float64)
    large = exact + (np.log(nf / exact) / math.log(REL_MAX_DIST / exact) * (REL_BUCKETS - exact)).astype(np.int64)
    return np.where(n < exact, n, np.minimum(large, REL_BUCKETS - 1))


def _strip_rows(tab, n_off, extra_of_dist):
    u = np.arange(2 * TQ)
    d = np.arange(n_off)[:, None] * TQ - np.where(u < TQ, u, u - 2 * TQ)[None, :]
    onehot = (_bucket(d)[None] == np.arange(REL_BUCKETS)[:, None, None]).astype(np.float32)
    vals = jnp.einsum("hb,bou->hou", tab, jnp.asarray(onehot), precision=lax.Precision.HIGHEST) + extra_of_dist(d)
    return vals.reshape(tab.shape[0] // 2, 2, n_off, 2 * TQ)


def _causal_mask(d):
    return np.where(d >= 0, 0.0, NEG).astype(np.float32)


def _window_mask(d):
    return np.where((d >= 0) & (d < NSA_WINDOW), 0.0, NEG).astype(np.float32)


def _dilation_log_count(d):
    c = ((d >= 0) & (d <= 128)).astype(np.float64)
    c += ((d >= 0) & (d % 4 == 0) & (d <= 512))
    c += ((d >= 0) & (d % 16 == 0) & (d <= 2048))
    return np.where(c > 0, np.log(np.maximum(c, 1.0)), NEG).astype(np.float32)


def _one_hot_blocks(S, block, per_head):
    e = np.zeros((2, S, LANES), np.float32)
    key = np.arange(S)
    for h in range(2):
        e[h, key, h * per_head + key // block] = 1.0
    return jnp.asarray(e, BF16)


_NSA_HEAD_ORDER = np.array([8 * (p // 4) + (p % 4) + 4 * h for p in range(N_PAIRS) for h in range(2)])


def _head_cols(heads):
    return (np.asarray(heads)[:, None] * HEAD_DIM + np.arange(HEAD_DIM)[None, :]).reshape(-1)


def _mixer_ab(h, g_mix, w_in, qn_a, kn_a, qn_b, kn_b, rel_bias, B, S):
    wa = MOBA_HEADS * HEAD_DIM
    sec = [w_in[:, k * wa:(k + 1) * wa] for k in range(6)]
    w = jnp.concatenate([sec[0], sec[3], sec[1], sec[4], sec[2], sec[5]], axis=1).astype(BF16)
    ones = jnp.ones((wa,), F32)
    gain = jnp.concatenate([jnp.tile(qn_a, 8) * ATTN_SCALE, jnp.tile(qn_b, 8) * ATTN_SCALE,
                            jnp.tile(kn_a, 8), jnp.tile(kn_b, 8), ones, ones])
    per_sec = wa // PROJ_CHUNK
    (qkv,) = _project(h, g_mix, w, gain, [True] * (4 * per_sec) + [False] * (2 * per_sec))
    qkv = qkv.reshape(B, S, 3 * D_MODEL)

    tab = rel_bias.T
    r_tab = jnp.concatenate([_strip_rows(tab[:MOBA_HEADS], S // TQ, _causal_mask),
                             _strip_rows(tab[MOBA_HEADS:], S // TQ, _dilation_log_count)])
    o = _attn0(qkv, r_tab, _one_hot_blocks(S, MOBA_BLOCK, S // MOBA_BLOCK))
    return o.reshape(B * S, D_MODEL)


def _mixer_nsa(h, g_mix, w_in, qn, kn_c, kn_s, kn_w, cmp_k, cmp_v, rel_bias, B, S):
    qw = N_HEADS * HEAD_DIM
    kvw = NSA_GROUPS * HEAD_DIM
    order = _NSA_HEAD_ORDER
    w_main = jnp.concatenate([w_in[:, _head_cols(order)], w_in[:, qw:qw + 6 * kvw]], axis=1).astype(BF16)
    ones = jnp.ones((kvw,), F32)
    gain = jnp.concatenate([jnp.tile(qn, N_HEADS) * ATTN_SCALE, ones, ones,
                            jnp.tile(kn_s, NSA_GROUPS), ones, jnp.tile(kn_w, NSA_GROUPS), ones])
    assert kvw == PROJ_CHUNK
    norm_chunks = [True] * (qw // PROJ_CHUNK) + [False, False, True, False, True, False]

    gcols = np.zeros((LANES,), np.int64)
    gused = np.zeros((LANES,), np.float32)
    gsel = np.zeros((N_PAIRS, 3, LANES, LANES), np.float32)
    for p in range(N_PAIRS):
        for br in range(3):
            for hh in range(2):
                c = 8 * p + 2 * br + hh
                gcols[c] = qw + 6 * kvw + 3 * order[2 * p + hh] + br
                gused[c] = 1.0
                gsel[p, br, c, hh * HEAD_DIM:(hh + 1) * HEAD_DIM] = 1.0
    w_gate = (w_in[:, gcols] * gused).astype(BF16)
    qkv, graw = _project(h, g_mix, w_main, gain, norm_chunks, w_gate)
    qkv = qkv.reshape(B, S, qw + 6 * kvw)
    graw = graw.reshape(B, S, LANES)

    def chunked(col0):
        t = qkv[:, :, col0:col0 + kvw].reshape(B, S // NSA_CMP_STRIDE, NSA_CMP_STRIDE, NSA_GROUPS, HEAD_DIM)
        return t.transpose(0, 3, 1, 2, 4).reshape(B, NSA_GROUPS, S // NSA_CMP_STRIDE, NSA_CMP_STRIDE * HEAD_DIM)

    kc = _compress(chunked(qw), *cmp_k, kn_c, True)
    vc = _compress(chunked(qw + kvw), *cmp_v, kn_c, False)

    tab = rel_bias.T[order]
    n_cmp = (S - NSA_CMP_LEN) // NSA_CMP_STRIDE + 1
    t_pos = np.arange(S)[:, None]
    c_idx = np.arange(LANES)[None, :]
    dc = t_pos - (c_idx * NSA_CMP_STRIDE + NSA_CMP_LEN - 1)
    cmp_mask = np.where((dc >= 0) & (c_idx < n_cmp), 0.0, NEG).astype(np.float32)
    onehot = jnp.asarray(_bucket(dc).astype(np.int8))[None] == jnp.arange(REL_BUCKETS, dtype=jnp.int8)[:, None, None]
    bias_c = jnp.einsum("hb,bsc->hsc", tab, onehot.astype(F32), precision=lax.Precision.HIGHEST)
    bias_c = (bias_c + cmp_mask).reshape(2, 8, S, LANES)
    n_sel = S // NSA_SEL_BLOCK
    cstart = np.arange(LANES) * NSA_CMP_STRIDE
    sstart = np.arange(n_sel) * NSA_SEL_BLOCK
    ovt = np.maximum(np.minimum(cstart[None, :] + NSA_CMP_LEN, sstart[:, None] + NSA_SEL_BLOCK)
                     - np.maximum(cstart[None, :], sstart[:, None]), 0).astype(np.float32)
    ovt[:, n_cmp:] = 0.0
    ocmp, pen = _cmp_attention(qkv, kc, vc, bias_c, jnp.asarray(ovt))

    r_slc = _strip_rows(tab, S // TQ, _causal_mask)
    r_win = _strip_rows(tab, NSA_WINDOW // TQ + 1, _window_mask)
    o = _attn1(qkv, r_slc, r_win, _one_hot_blocks(S, NSA_SEL_BLOCK, n_sel), pen, ocmp, graw, jnp.asarray(gsel, BF16))
    return o.reshape(B * S, D_MODEL)


def kernel(x, p, rel_bias, norm_mix, norm_ffn, norm_ple, w_ffn_gate, w_ffn_up, w_ffn_down, w_ple_proj, w_ple_gate, w_in_ab, w_out_ab, qn_moba, kn_moba, qn_dil, kn_dil, w_in_nsa, w_out_nsa, qn_nsa, kn_cmp, kn_slc, kn_win, cmp_k_pos, cmp_k_w1, cmp_k_b1, cmp_k_w2, cmp_k_b2, cmp_v_pos, cmp_v_w1, cmp_v_b1, cmp_v_w2, cmp_v_b2):
    B, S, D = x.shape
    depth = p.shape[0]
    h = x.reshape(B * S, D)
    for i in range(depth):
        e = i // 2
        if i % 2 == 0:
            o = _mixer_ab(h, norm_mix[i], w_in_ab[e], qn_moba[e], kn_moba[e], qn_dil[e], kn_dil[e], rel_bias, B, S)
            w_out = w_out_ab[e]
        else:
            cmp_k = (cmp_k_pos[e], cmp_k_w1[e], cmp_k_b1[e], cmp_k_w2[e], cmp_k_b2[e])
            cmp_v = (cmp_v_pos[e], cmp_v_w1[e], cmp_v_b1[e], cmp_v_w2[e], cmp_v_b2[e])
            o = _mixer_nsa(h, norm_mix[i], w_in_nsa[e], qn_nsa[e], kn_cmp[e], kn_slc[e], kn_win[e],
                           cmp_k, cmp_v, rel_bias, B, S)
            w_out = w_out_nsa[e][_head_cols(_NSA_HEAD_ORDER), :]
        h = _post_attention(o, h, w_out.astype(BF16), norm_ffn[i], w_ffn_gate[i].astype(BF16),
                            w_ffn_up[i].astype(BF16), w_ffn_down[i].astype(BF16), norm_ple[i],
                            w_ple_gate[i].astype(BF16), p[i].reshape(B * S, -1), w_ple_proj[i].astype(BF16))
    return h.reshape(B, S, D)
```

```python
import functools
import math

import numpy as np
import jax
import jax.numpy as jnp
from jax import lax
from jax.experimental import pallas as pl
from jax.experimental.pallas import tpu as pltpu

F32 = jnp.float32
BF16 = jnp.bfloat16

D_MODEL = 1024
HEAD_DIM = 64
N_HEADS = 16
N_PAIRS = N_HEADS // 2
MOBA_HEADS = 8
MOBA_BLOCK = 256
MOBA_TOPK = 3
MOBA_NBLK = 8
NSA_GROUPS = 4
NSA_CMP_LEN = 32
NSA_CMP_STRIDE = 16
NSA_SEL_BLOCK = 64
NSA_SEL_TOPN = 16
NSA_WINDOW = 512
NSA_FORCE = 1.0e6
REL_BUCKETS = 32
REL_MAX_DIST = 2048
RMS_EPS = 1e-6
ATTN_SCALE = HEAD_DIM ** -0.5

LANES = 128
SUBLANES = 8
TQ = 256
NEG = -1.0e30
ROW_TILE = 512
PROJ_CHUNK = 256
FFN_SPLIT = 2
CMP_ROWS = 512
VMEM_LIMIT = 48 * 1024 * 1024
POST_VMEM_LIMIT = 56 * 1024 * 1024
ATTN_VMEM_LIMIT = 56 * 1024 * 1024

NT_DIMS = (((1,), (1,)), ((), ()))


def _cparams(n_axes):
    return pltpu.CompilerParams(dimension_semantics=("arbitrary",) * n_axes,
                                vmem_limit_bytes=VMEM_LIMIT)


def _rms_rows(x, g):
    ms = jnp.mean(x * x, axis=-1, keepdims=True)
    return x * lax.rsqrt(ms + RMS_EPS) * g


def _lane_lo(shape):
    return lax.broadcasted_iota(jnp.int32, shape, len(shape) - 1) < HEAD_DIM


def _resident(shape):
    return pl.BlockSpec(shape, lambda i: (0,) * len(shape), pipeline_mode=pl.Buffered(1))


def _proj_kernel(x_ref, g_ref, w_ref, cg_ref, bd_ref, *rest, norm_chunks, with_extra):
    if with_extra:
        wx_ref, o_ref, ox_ref = rest
    else:
        (o_ref,) = rest
    xn = _rms_rows(x_ref[...], g_ref[...]).astype(BF16)
    bd = bd_ref[...]

    def finish(c, y):
        cols = slice(c * PROJ_CHUNK, (c + 1) * PROJ_CHUNK)
        if norm_chunks[c]:
            y2 = y * y
            hi = y2.astype(BF16)
            lo = (y2 - hi.astype(F32)).astype(BF16)
            ssq = jnp.dot(hi, bd, preferred_element_type=F32) + jnp.dot(lo, bd, preferred_element_type=F32)
            y = y * lax.rsqrt(ssq * (1.0 / HEAD_DIM) + RMS_EPS) * cg_ref[:, cols]
        o_ref[:, cols] = y.astype(o_ref.dtype)

    prev = None
    for c in range(len(norm_chunks)):
        y = jnp.dot(xn, w_ref[:, c * PROJ_CHUNK:(c + 1) * PROJ_CHUNK], preferred_element_type=F32)
        if prev is not None:
            finish(c - 1, prev)
        prev = y
    finish(len(norm_chunks) - 1, prev)
    if with_extra:
        ox_ref[...] = jnp.dot(xn, wx_ref[...], preferred_element_type=F32)


def _project(x, g, w, col_gain, norm_chunks, w_extra=None):
    T, D = x.shape
    N = w.shape[1]
    blk = np.kron(np.eye(PROJ_CHUNK // HEAD_DIM), np.ones((HEAD_DIM, HEAD_DIM))).astype(np.float32)
    row_in = lambda n: pl.BlockSpec((ROW_TILE, n), lambda i: (i, 0))
    in_specs = [row_in(D), _resident((1, D)), _resident((D, N)), _resident((1, N)),
                _resident((PROJ_CHUNK, PROJ_CHUNK))]
    args = [x, g.reshape(1, D), w, col_gain.reshape(1, N), jnp.asarray(blk, BF16)]
    out_specs = [row_in(N)]
    out_shape = [jax.ShapeDtypeStruct((T, N), BF16)]
    if w_extra is not None:
        nx = w_extra.shape[1]
        in_specs.append(_resident((D, nx)))
        args.append(w_extra)
        out_specs.append(row_in(nx))
        out_shape.append(jax.ShapeDtypeStruct((T, nx), F32))
    return pl.pallas_call(
        functools.partial(_proj_kernel, norm_chunks=tuple(norm_chunks), with_extra=w_extra is not None),
        grid=(T // ROW_TILE,),
        in_specs=in_specs,
        out_specs=out_specs,
        out_shape=out_shape,
        compiler_params=_cparams(1),
        name="proj",
    )(*args)


def _sigmoid(z):
    return 1.0 / (1.0 + jnp.exp(-z))


def _post_kernel(o_ref, h_ref, wout_ref, gf_ref, wg_ref, wu_ref, wd_ref, gp_ref, wpg_ref, p_ref, wpp_ref, out_ref):
    h1 = h_ref[...] + jnp.dot(o_ref[...], wout_ref[...], preferred_element_type=F32)
    xn = _rms_rows(h1, gf_ref[...]).astype(BF16)
    h2 = h1
    fh = wg_ref.shape[1]
    for c in range(FFN_SPLIT):
        cols = slice(c * fh // FFN_SPLIT, (c + 1) * fh // FFN_SPLIT)
        a = jnp.dot(xn, wg_ref[:, cols], preferred_element_type=F32)
        u = jnp.dot(xn, wu_ref[:, cols], preferred_element_type=F32)
        act = (a * _sigmoid(a) * u).astype(BF16)
        h2 = h2 + jnp.dot(act, wd_ref[cols, :], preferred_element_type=F32)
    hn = _rms_rows(h2, gp_ref[...]).astype(BF16)
    gate = _sigmoid(jnp.dot(hn, wpg_ref[...], preferred_element_type=F32))
    out_ref[...] = h2 + gate * jnp.dot(p_ref[...].astype(BF16), wpp_ref[...], preferred_element_type=F32)


def _post_attention(o, h, w_out, g_ffn, wg, wu, wd, g_ple, w_pgate, p, w_pproj):
    T, D = h.shape
    Fh = wg.shape[1]
    Pd = p.shape[1]
    row_in = lambda n: pl.BlockSpec((ROW_TILE, n), lambda i: (i, 0))
    return pl.pallas_call(
        _post_kernel,
        grid=(T // ROW_TILE,),
        in_specs=[row_in(D), row_in(D), _resident((D, D)), _resident((1, D)), _resident((D, Fh)),
                  _resident((D, Fh)), _resident((Fh, D)), _resident((1, D)), _resident((D, D)),
                  row_in(Pd), _resident((Pd, D))],
        out_specs=row_in(D),
        out_shape=jax.ShapeDtypeStruct((T, D), F32),
        compiler_params=pltpu.CompilerParams(dimension_semantics=("arbitrary",), vmem_limit_bytes=POST_VMEM_LIMIT),
        name="post_attention",
    )(o, h, w_out, g_ffn.reshape(1, D), wg, wu, wd, g_ple.reshape(1, D), w_pgate, p, w_pproj)


def _fill_strips(strip_ref, r_ref):
    for h in range(2):
        for o in range(r_ref.shape[2]):
            x = jnp.broadcast_to(r_ref[0, h, o:o + 1, :], (TQ, 2 * TQ))
            strip_ref[h, o * TQ:(o + 1) * TQ, :] = pltpu.roll(x, 0, 1, stride=1, stride_axis=0)[:, :TQ]


def _tile_groups(nblk, span):
    cnt = [min(span, nblk - j) for j in range(nblk)]
    off = [0]
    for c in cnt:
        off.append(off[-1] + c * TQ)
    return cnt, off


def _tile_rows(nblk, span):
    return _tile_groups(nblk, span)[1][-1]


def _pair_attention(q_rows, k_rows, v_rows, strip_ref, s_ref, p_ref, o_ref, nblk, span, skew=1):
    cnt, off = _tile_groups(nblk, span)
    together = s_ref.shape[0] >= 2 * off[-1] and p_ref.shape[2] >= 2 * span * TQ
    s0 = [0, off[-1] if together else 0]
    p0 = [0, span * TQ if together else 0]

    def scores(h, j):
        n = cnt[j] * TQ
        s = lax.dot_general(q_rows(h, j * TQ, j * TQ + n), k_rows(h, j), NT_DIMS, preferred_element_type=F32)
        s_ref[s0[h] + off[j]:s0[h] + off[j] + n, :] = s + strip_ref[h, 0:n, :]

    def probs(h, i):
        j0 = max(0, i - span + 1)
        rows = [s0[h] + off[j] + (i - j) * TQ for j in range(j0, i + 1)]
        mx = s_ref[rows[0]:rows[0] + TQ, :]
        for r in rows[1:]:
            mx = jnp.maximum(mx, s_ref[r:r + TQ, :])
        m = jnp.max(mx, axis=1, keepdims=True)
        for t, r in enumerate(rows):
            p_ref[i, :, p0[h] + t * TQ:p0[h] + (t + 1) * TQ] = jnp.exp(s_ref[r:r + TQ, :] - m).astype(BF16)

    def values(h, i):
        j0 = max(0, i - span + 1)
        out = jnp.dot(p_ref[i, :, p0[h]:p0[h] + (i + 1 - j0) * TQ], v_rows(j0 * TQ, (i + 1) * TQ),
                      preferred_element_type=F32)
        o_ref[h, i * TQ:(i + 1) * TQ, :] = out[:, :LANES] / out[:, LANES:]

    for heads in ([(0, 1)] if together else [(0,), (1,)]):
        for j in range(nblk + skew):
            for h in heads:
                if j < nblk:
                    scores(h, j)
            for h in heads:
                if j >= skew:
                    values(h, j - skew)
            for h in heads:
                if j < nblk:
                    probs(h, j)


def _merge_heads(o_ref):
    return jnp.where(_lane_lo((o_ref.shape[1], LANES)), o_ref[0], o_ref[1])


def _split_heads(q):
    lo = _lane_lo(q.shape)
    zero = jnp.zeros_like(q)
    return jnp.where(lo, q, zero), jnp.where(lo, zero, q)


def _fill_value_ones(vaug_ref):
    vaug_ref[:, LANES:] = jnp.ones((vaug_ref.shape[0], LANES), vaug_ref.dtype)


def _moba_block_means(k_ref):
    shape = (2 * MOBA_NBLK, LANES)
    lo = _lane_lo(shape)
    row = lax.broadcasted_iota(jnp.int32, shape, 0)
    kmt = jnp.zeros(shape, F32)
    for n in range(MOBA_NBLK):
        mean = jnp.mean(k_ref[0, n * MOBA_BLOCK:(n + 1) * MOBA_BLOCK, :].astype(F32), axis=0, keepdims=True)
        kmt = jnp.where(row == n, jnp.where(lo, mean, 0.0), kmt)
        kmt = jnp.where(row == MOBA_NBLK + n, jnp.where(lo, 0.0, mean), kmt)
    return kmt


def _moba_penalty(q, kmt):
    nb = MOBA_NBLK
    S = q.shape[0]
    hi = kmt.astype(BF16)
    rest = kmt - hi.astype(F32)
    mid = rest.astype(BF16)
    lo = (rest - mid.astype(F32)).astype(BF16)
    g3 = lax.dot_general(jnp.concatenate([hi, mid, lo], axis=0), q, NT_DIMS, preferred_element_type=F32)
    g = g3[0:2 * nb] + g3[2 * nb:4 * nb] + g3[4 * nb:6 * nb]
    row = lax.broadcasted_iota(jnp.int32, g.shape, 0)
    n = row & (nb - 1)
    own = lax.shift_right_logical(lax.broadcasted_iota(jnp.int32, g.shape, 1), int(math.log2(MOBA_BLOCK)))
    rank = jnp.zeros(g.shape, F32)
    for m in range(nb - 1):
        gm = jnp.where(row < nb, g[m:m + 1, :], g[nb + m:nb + m + 1, :])
        tie = jnp.where(n > m, 1.0, 0.0)
        beats = jnp.where(gm > g, 1.0, jnp.where(gm == g, tie, 0.0))
        rank = rank + jnp.where(own > m, beats, 0.0)
    keep = jnp.where(n < own, jnp.where(rank < MOBA_TOPK, 1.0, 0.0), jnp.where(n == own, 1.0, 0.0))
    keep = jnp.concatenate([keep, jnp.zeros((LANES - 2 * nb, S), F32)], axis=0).T
    lane = lax.broadcasted_iota(jnp.int32, keep.shape, 1)
    return jnp.where(lane < 2 * nb, (keep - 1.0) * (-NEG), 0.0)


def _attn0_kernel(q_ref, k_ref, v_ref, r_ref, e_ref, o_ref, strip_ref, qaug_ref, kaug_ref, vaug_ref,
                  s_ref, p_ref, oh_ref):
    p = pl.program_id(0)
    b = pl.program_id(1)
    S = q_ref.shape[1]
    nblk = S // TQ

    @pl.when(b == 0)
    def _():
        _fill_strips(strip_ref, r_ref)

    @pl.when(jnp.logical_and(p == 0, b == 0))
    def _():
        for h in range(2):
            kaug_ref[h, :, LANES:] = e_ref[h]
        _fill_value_ones(vaug_ref)

    k = k_ref[0]
    for h, qh in enumerate(_split_heads(q_ref[0])):
        kaug_ref[h, :, :LANES] = k
        qaug_ref[h, :, :LANES] = qh
    vaug_ref[:, :LANES] = v_ref[0]
    is_moba = p < MOBA_HEADS // 2

    @pl.when(is_moba)
    def _():
        pen = _moba_penalty(q_ref[0], _moba_block_means(k_ref)).astype(BF16)
        for h in range(2):
            qaug_ref[h, :, LANES:] = pen

    @pl.when(jnp.logical_not(is_moba))
    def _():
        for h in range(2):
            qaug_ref[h, :, LANES:] = jnp.zeros((S, LANES), BF16)
    _pair_attention(lambda h, r0, r1: qaug_ref[h, r0:r1, :],
                    lambda h, j: kaug_ref[h, j * TQ:(j + 1) * TQ, :],
                    lambda r0, r1: vaug_ref[r0:r1, :],
                    strip_ref, s_ref, p_ref, oh_ref, nblk, nblk)
    o_ref[0] = _merge_heads(oh_ref).astype(o_ref.dtype)


def _attn_scratch(S, span, narrow_span=0):
    nblk = S // TQ
    s_rows = max(_tile_rows(nblk, span), 2 * _tile_rows(nblk, narrow_span))
    return [pltpu.VMEM((2, S, TQ), F32), pltpu.VMEM((2, S, 2 * LANES), BF16), pltpu.VMEM((2, S, 2 * LANES), BF16),
            pltpu.VMEM((S, 2 * LANES), BF16), pltpu.VMEM((s_rows, TQ), F32),
            pltpu.VMEM((nblk, TQ, span * TQ), BF16), pltpu.VMEM((2, S, LANES), F32)]


def _attn0(qkv, r_tab, e_onehot):
    B, S, _ = qkv.shape
    nblk = S // TQ
    return pl.pallas_call(
        _attn0_kernel,
        grid=(N_PAIRS, B),
        in_specs=[
            pl.BlockSpec((1, S, LANES), lambda p, b: (b, 0, p)),
            pl.BlockSpec((1, S, LANES), lambda p, b: (b, 0, N_PAIRS + p)),
            pl.BlockSpec((1, S, LANES), lambda p, b: (b, 0, 2 * N_PAIRS + p)),
            pl.BlockSpec((1, 2, nblk, 2 * TQ), lambda p, b: (p, 0, 0, 0)),
            pl.BlockSpec((2, S, LANES), lambda p, b: (0, 0, 0), pipeline_mode=pl.Buffered(1)),
        ],
        out_specs=pl.BlockSpec((1, S, LANES), lambda p, b: (b, 0, p)),
        out_shape=jax.ShapeDtypeStruct((B, S, D_MODEL), BF16),
        scratch_shapes=_attn_scratch(S, nblk),
        compiler_params=pltpu.CompilerParams(dimension_semantics=("arbitrary",) * 2, vmem_limit_bytes=ATTN_VMEM_LIMIT),
        name="attn_moba_dilated",
    )(qkv, qkv, qkv, r_tab, e_onehot)


def _compress_kernel(a_ref, pos_ref, w1_ref, b1_ref, w2_ref, b2_ref, gain_ref, o_ref, *, normed):
    y = None
    for g in range(2):
        a = a_ref[0, g].astype(F32)
        first = jnp.dot((a + pos_ref[0:1, :]).astype(BF16), w1_ref[0], preferred_element_type=F32)
        second = jnp.dot((a + pos_ref[1:2, :]).astype(BF16), w1_ref[1], preferred_element_type=F32)
        hid = first + pltpu.roll(second, LANES - 1, 0) + b1_ref[...]
        cdf = 0.5 * (1.0 + jnp.tanh(math.sqrt(2.0 / math.pi) * (hid + 0.044715 * (hid * hid * hid))))
        yg = jnp.dot((hid * cdf).astype(BF16), w2_ref[g], preferred_element_type=F32)
        y = yg if y is None else y + yg
    y = y + b2_ref[...]
    if normed:
        lo = _lane_lo(y.shape)
        y2 = y * y
        s_lo = jnp.sum(jnp.where(lo, y2, 0.0), axis=1, keepdims=True)
        s_hi = jnp.sum(jnp.where(lo, 0.0, y2), axis=1, keepdims=True)
        ms = jnp.where(lo, s_lo, s_hi) * (1.0 / HEAD_DIM)
        y = y * lax.rsqrt(ms + RMS_EPS) * gain_ref[...]
    o_ref[0, 0] = y.astype(o_ref.dtype)


def _compress(a, pos, w1, b1, w2, b2, gain, normed):
    B = a.shape[0]
    hid = w1.shape[1]
    half = NSA_CMP_STRIDE * HEAD_DIM
    zeros = jnp.zeros_like(w2)
    w2p = jnp.stack([jnp.concatenate([w2, zeros], axis=1), jnp.concatenate([zeros, w2], axis=1)]).astype(BF16)
    return pl.pallas_call(
        functools.partial(_compress_kernel, normed=normed),
        grid=(B, 2),
        in_specs=[
            pl.BlockSpec((1, 2, LANES, half), lambda b, m: (b, m, 0, 0)),
            pl.BlockSpec((2, half), lambda b, m: (0, 0)),
            pl.BlockSpec((2, half, hid), lambda b, m: (0, 0, 0)),
            pl.BlockSpec((1, hid), lambda b, m: (0, 0)),
            pl.BlockSpec((2, hid, LANES), lambda b, m: (0, 0, 0)),
            pl.BlockSpec((1, LANES), lambda b, m: (0, 0)),
            pl.BlockSpec((1, LANES), lambda b, m: (0, 0)),
        ],
        out_specs=pl.BlockSpec((1, 1, LANES, LANES), lambda b, m: (b, m, 0, 0)),
        out_shape=jax.ShapeDtypeStruct((B, 2, LANES, LANES), BF16),
        compiler_params=_cparams(2),
        name="nsa_compress",
    )(a, pos.reshape(2, half), w1.reshape(2, half, hid).astype(BF16), b1.reshape(1, hid), w2p,
      jnp.tile(b2, 2).reshape(1, LANES), jnp.tile(gain, 2).reshape(1, LANES))


def _cmp_kernel(q_ref, kc_ref, vc_ref, bias_ref, ovt_ref, o_ref, pen_ref):
    kc = kc_ref[0, 0]
    vc = vc_ref[0, 0]
    S = q_ref.shape[1]
    n_sel = ovt_ref.shape[0]
    lo = _lane_lo((CMP_ROWS, LANES))
    blk = lax.broadcasted_iota(jnp.int32, (n_sel, CMP_ROWS), 0)
    for c in range(S // CMP_ROWS):
        rows = slice(c * CMP_ROWS, (c + 1) * CMP_ROWS)
        psum = [jnp.zeros((CMP_ROWS, LANES), F32), jnp.zeros((CMP_ROWS, LANES), F32)]
        for r in range(4):
            heads = _split_heads(q_ref[0, rows, r * LANES:(r + 1) * LANES])
            outs = []
            for h in range(2):
                s = lax.dot_general(heads[h], kc, NT_DIMS, preferred_element_type=F32) + bias_ref[0, 2 * r + h, rows, :]
                m = jnp.max(s, axis=1, keepdims=True)
                e = jnp.exp(s - m)
                l = jnp.sum(e, axis=1, keepdims=True)
                pr = e * jnp.where(m > 0.5 * NEG, 1.0 / l, 0.0)
                psum[h] = psum[h] + pr
                outs.append(jnp.dot(pr.astype(BF16), vc, preferred_element_type=F32))
            o_ref[0, rows, r * LANES:(r + 1) * LANES] = jnp.where(lo, outs[0], outs[1])

        t = c * CMP_ROWS + lax.broadcasted_iota(jnp.int32, (n_sel, CMP_ROWS), 1)
        cur = lax.shift_right_logical(t, int(math.log2(NSA_SEL_BLOCK)))
        keeps = []
        for h in range(2):
            imp = lax.dot_general(ovt_ref[...], psum[h], NT_DIMS,
                                  precision=lax.Precision.HIGHEST, preferred_element_type=F32)
            forced = jnp.where(blk == 0, 1.0, jnp.where(blk == cur, 1.0, jnp.where(blk == cur - 1, 1.0, 0.0)))
            imp = jnp.where(blk <= cur, imp + forced * NSA_FORCE, -jnp.inf)
            groups = [imp[g * SUBLANES:(g + 1) * SUBLANES, :] for g in range(n_sel // SUBLANES)]
            ranks = [jnp.zeros(g.shape, F32) for g in groups]
            for m in range(n_sel):
                im = imp[m:m + 1, :]
                for g, sub in enumerate(groups):
                    ge = jnp.where(im >= sub, 1.0, 0.0)
                    gt = jnp.where(im > sub, 1.0, 0.0)
                    if g * SUBLANES > m:
                        inc = ge
                    elif (g + 1) * SUBLANES - 1 <= m:
                        inc = gt
                    else:
                        inc = jnp.where(lax.broadcasted_iota(jnp.int32, sub.shape, 0) + g * SUBLANES > m, ge, gt)
                    ranks[g] = ranks[g] + inc
            keeps.append(jnp.where(jnp.concatenate(ranks, axis=0) < NSA_SEL_TOPN, 1.0, 0.0))
        keep = jnp.concatenate(keeps + [jnp.zeros((LANES - 2 * n_sel, CMP_ROWS), F32)], axis=0).T
        lane = lax.broadcasted_iota(jnp.int32, keep.shape, 1)
        pen_ref[0, 0, rows, :] = jnp.where(lane < 2 * n_sel, (keep - 1.0) * (-NEG), 0.0).astype(pen_ref.dtype)


def _cmp_attention(qkv, kc, vc, bias, ovt):
    B, S, _ = qkv.shape
    n_sel = ovt.shape[0]
    return pl.pallas_call(
        _cmp_kernel,
        grid=(2, B),
        in_specs=[
            pl.BlockSpec((1, S, 4 * LANES), lambda m, b: (b, 0, m)),
            pl.BlockSpec((1, 1, LANES, LANES), lambda m, b: (b, m, 0, 0)),
            pl.BlockSpec((1, 1, LANES, LANES), lambda m, b: (b, m, 0, 0)),
            pl.BlockSpec((1, 8, S, LANES), lambda m, b: (m, 0, 0, 0), pipeline_mode=pl.Buffered(1)),
            pl.BlockSpec((n_sel, LANES), lambda m, b: (0, 0)),
        ],
        out_specs=[
            pl.BlockSpec((1, S, 4 * LANES), lambda m, b: (b, 0, m)),
            pl.BlockSpec((1, 1, S, LANES), lambda m, b: (b, m, 0, 0)),
        ],
        out_shape=[jax.ShapeDtypeStruct((B, S, D_MODEL), F32),
                   jax.ShapeDtypeStruct((B, 2, S, LANES), BF16)],
        compiler_params=_cparams(2),
        name="nsa_compressed_select",
    )(qkv, kc, vc, bias, ovt)


def _attn1_kernel(q_ref, ks_ref, vs_ref, kw_ref, vw_ref, rs_ref, rw_ref, e_ref, pen_ref, ocmp_ref, graw_ref,
                  gsel_ref, o_ref, strip_s_ref, qaug_ref, kaug_ref, vsaug_ref, s_ref, p_ref, oslc_ref,
                  strip_w_ref, vwaug_ref, owin_ref):
    p = pl.program_id(0)
    b = pl.program_id(1)
    S = q_ref.shape[1]
    nblk = S // TQ

    @pl.when(b == 0)
    def _():
        _fill_strips(strip_s_ref, rs_ref)
        _fill_strips(strip_w_ref, rw_ref)

    @pl.when(jnp.logical_and(p == 0, b == 0))
    def _():
        for h in range(2):
            kaug_ref[h, :, LANES:] = e_ref[h]
        _fill_value_ones(vsaug_ref)
        _fill_value_ones(vwaug_ref)

    ks = ks_ref[0]
    pen = pen_ref[0, 0]
    for h, qh in enumerate(_split_heads(q_ref[0])):
        kaug_ref[h, :, :LANES] = ks
        qaug_ref[h, :, :LANES] = qh
        qaug_ref[h, :, LANES:] = pen
    vsaug_ref[:, :LANES] = vs_ref[0]
    vwaug_ref[:, :LANES] = vw_ref[0]
    _pair_attention(lambda h, r0, r1: qaug_ref[h, r0:r1, :],
                    lambda h, j: kaug_ref[h, j * TQ:(j + 1) * TQ, :],
                    lambda r0, r1: vsaug_ref[r0:r1, :],
                    strip_s_ref, s_ref, p_ref, oslc_ref, nblk, nblk)
    _pair_attention(lambda h, r0, r1: qaug_ref[h, r0:r1, :LANES],
                    lambda h, j: kw_ref[0, j * TQ:(j + 1) * TQ, :],
                    lambda r0, r1: vwaug_ref[r0:r1, :],
                    strip_w_ref, s_ref, p_ref, owin_ref, nblk, strip_w_ref.shape[1] // TQ, skew=2)
    sig = _sigmoid(graw_ref[0])
    hi = sig.astype(BF16)
    lo = (sig - hi.astype(F32)).astype(BF16)
    gates = [jnp.dot(hi, gsel_ref[0, br], preferred_element_type=F32)
             + jnp.dot(lo, gsel_ref[0, br], preferred_element_type=F32) for br in range(3)]
    out = gates[0] * ocmp_ref[0] + gates[1] * _merge_heads(oslc_ref) + gates[2] * _merge_heads(owin_ref)
    o_ref[0] = out.astype(o_ref.dtype)


def _attn1(qkv, r_slc, r_win, e_onehot, pen, ocmp, graw, gsel):
    B, S, _ = qkv.shape
    kv0 = D_MODEL // LANES + 4
    n_win = r_win.shape[2]
    return pl.pallas_call(
        _attn1_kernel,
        grid=(N_PAIRS, B),
        in_specs=[
            pl.BlockSpec((1, S, LANES), lambda p, b: (b, 0, p)),
            pl.BlockSpec((1, S, LANES), lambda p, b: (b, 0, kv0 + p // 4)),
            pl.BlockSpec((1, S, LANES), lambda p, b: (b, 0, kv0 + 2 + p // 4)),
            pl.BlockSpec((1, S, LANES), lambda p, b: (b, 0, kv0 + 4 + p // 4)),
            pl.BlockSpec((1, S, LANES), lambda p, b: (b, 0, kv0 + 6 + p // 4)),
            pl.BlockSpec((1, 2, r_slc.shape[2], 2 * TQ), lambda p, b: (p, 0, 0, 0)),
            pl.BlockSpec((1, 2, n_win, 2 * TQ), lambda p, b: (p, 0, 0, 0)),
            pl.BlockSpec((2, S, LANES), lambda p, b: (0, 0, 0), pipeline_mode=pl.Buffered(1)),
            pl.BlockSpec((1, 1, S, LANES), lambda p, b: (b, p // 4, 0, 0)),
            pl.BlockSpec((1, S, LANES), lambda p, b: (b, 0, p)),
            pl.BlockSpec((1, S, LANES), lambda p, b: (b, 0, 0)),
            pl.BlockSpec((1, 3, LANES, LANES), lambda p, b: (p, 0, 0, 0)),
        ],
        out_specs=pl.BlockSpec((1, S, LANES), lambda p, b: (b, 0, p)),
        out_shape=jax.ShapeDtypeStruct((B, S, D_MODEL), BF16),
        scratch_shapes=_attn_scratch(S, S // TQ, n_win) + [pltpu.VMEM((2, n_win * TQ, TQ), F32),
                                                    pltpu.VMEM((S, 2 * LANES), BF16), pltpu.VMEM((2, S, LANES), F32)],
        compiler_params=pltpu.CompilerParams(dimension_semantics=("arbitrary",) * 2, vmem_limit_bytes=ATTN_VMEM_LIMIT),
        name="attn_nsa",
    )(qkv, qkv, qkv, qkv, qkv, r_slc, r_win, e_onehot, pen, ocmp, graw, gsel)


def _bucket(dist):
    n = np.maximum(dist, 0)
    exact = REL_BUCKETS // 2
    nf = np.maximum(n, 1).astype(np.float64)
    large = exact + (np.log(nf / exact) / math.log(REL_MAX_DIST / exact) * (REL_BUCKETS - exact)).astype(np.int64)
    return np.where(n < exact, n, np.minimum(large, REL_BUCKETS - 1))


def _strip_rows(tab, n_off, extra_of_dist):
    u = np.arange(2 * TQ)
    d = np.arange(n_off)[:, None] * TQ - np.where(u < TQ, u, u - 2 * TQ)[None, :]
    onehot = (_bucket(d)[None] == np.arange(REL_BUCKETS)[:, None, None]).astype(np.float32)
    vals = jnp.einsum("hb,bou->hou", tab, jnp.asarray(onehot), precision=lax.Precision.HIGHEST) + extra_of_dist(d)
    return vals.reshape(tab.shape[0] // 2, 2, n_off, 2 * TQ)


def _causal_mask(d):
    return np.where(d >= 0, 0.0, NEG).astype(np.float32)


def _window_mask(d):
    return np.where((d >= 0) & (d < NSA_WINDOW), 0.0, NEG).astype(np.float32)


def _dilation_log_count(d):
    c = ((d >= 0) & (d <= 128)).astype(np.float64)
    c += ((d >= 0) & (d % 4 == 0) & (d <= 512))
    c += ((d >= 0) & (d % 16 == 0) & (d <= 2048))
    return np.where(c > 0, np.log(np.maximum(c, 1.0)), NEG).astype(np.float32)


def _one_hot_blocks(S, block, per_head):
    e = np.zeros((2, S, LANES), np.float32)
    key = np.arange(S)
    for h in range(2):
        e[h, key, h * per_head + key // block] = 1.0
    return jnp.asarray(e, BF16)


_NSA_HEAD_ORDER = np.array([8 * (p // 4) + (p % 4) + 4 * h for p in range(N_PAIRS) for h in range(2)])


def _head_cols(heads):
    return (np.asarray(heads)[:, None] * HEAD_DIM + np.arange(HEAD_DIM)[None, :]).reshape(-1)


def _mixer_ab(h, g_mix, w_in, qn_a, kn_a, qn_b, kn_b, rel_bias, B, S):
    wa = MOBA_HEADS * HEAD_DIM
    sec = [w_in[:, k * wa:(k + 1) * wa] for k in range(6)]
    w = jnp.concatenate([sec[0], sec[3], sec[1], sec[4], sec[2], sec[5]], axis=1).astype(BF16)
    ones = jnp.ones((wa,), F32)
    gain = jnp.concatenate([jnp.tile(qn_a, 8) * ATTN_SCALE, jnp.tile(qn_b, 8) * ATTN_SCALE,
                            jnp.tile(kn_a, 8), jnp.tile(kn_b, 8), ones, ones])
    per_sec = wa // PROJ_CHUNK
    (qkv,) = _project(h, g_mix, w, gain, [True] * (4 * per_sec) + [False] * (2 * per_sec))
    qkv = qkv.reshape(B, S, 3 * D_MODEL)

    tab = rel_bias.T
    r_tab = jnp.concatenate([_strip_rows(tab[:MOBA_HEADS], S // TQ, _causal_mask),
                             _strip_rows(tab[MOBA_HEADS:], S // TQ, _dilation_log_count)])
    o = _attn0(qkv, r_tab, _one_hot_blocks(S, MOBA_BLOCK, S // MOBA_BLOCK))
    return o.reshape(B * S, D_MODEL)


def _mixer_nsa(h, g_mix, w_in, qn, kn_c, kn_s, kn_w, cmp_k, cmp_v, rel_bias, B, S):
    qw = N_HEADS * HEAD_DIM
    kvw = NSA_GROUPS * HEAD_DIM
    order = _NSA_HEAD_ORDER
    w_main = jnp.concatenate([w_in[:, _head_cols(order)], w_in[:, qw:qw + 6 * kvw]], axis=1).astype(BF16)
    ones = jnp.ones((kvw,), F32)
    gain = jnp.concatenate([jnp.tile(qn, N_HEADS) * ATTN_SCALE, ones, ones,
                            jnp.tile(kn_s, NSA_GROUPS), ones, jnp.tile(kn_w, NSA_GROUPS), ones])
    assert kvw == PROJ_CHUNK
    norm_chunks = [True] * (qw // PROJ_CHUNK) + [False, False, True, False, True, False]

    gcols = np.zeros((LANES,), np.int64)
    gused = np.zeros((LANES,), np.float32)
    gsel = np.zeros((N_PAIRS, 3, LANES, LANES), np.float32)
    for p in range(N_PAIRS):
        for br in range(3):
            for hh in range(2):
                c = 8 * p + 2 * br + hh
                gcols[c] = qw + 6 * kvw + 3 * order[2 * p + hh] + br
                gused[c] = 1.0
                gsel[p, br, c, hh * HEAD_DIM:(hh + 1) * HEAD_DIM] = 1.0
    w_gate = (w_in[:, gcols] * gused).astype(BF16)
    qkv, graw = _project(h, g_mix, w_main, gain, norm_chunks, w_gate)
    qkv = qkv.reshape(B, S, qw + 6 * kvw)
    graw = graw.reshape(B, S, LANES)

    def chunked(col0):
        t = qkv[:, :, col0:col0 + kvw].reshape(B, S // NSA_CMP_STRIDE, NSA_CMP_STRIDE, NSA_GROUPS, HEAD_DIM)
        return t.transpose(0, 3, 1, 2, 4).reshape(B, NSA_GROUPS, S // NSA_CMP_STRIDE, NSA_CMP_STRIDE * HEAD_DIM)

    kc = _compress(chunked(qw), *cmp_k, kn_c, True)
    vc = _compress(chunked(qw + kvw), *cmp_v, kn_c, False)

    tab = rel_bias.T[order]
    n_cmp = (S - NSA_CMP_LEN) // NSA_CMP_STRIDE + 1
    t_pos = np.arange(S)[:, None]
    c_idx = np.arange(LANES)[None, :]
    dc = t_pos - (c_idx * NSA_CMP_STRIDE + NSA_CMP_LEN - 1)
    cmp_mask = np.where((dc >= 0) & (c_idx < n_cmp), 0.0, NEG).astype(np.float32)
    onehot = jnp.asarray(_bucket(dc).astype(np.int8))[None] == jnp.arange(REL_BUCKETS, dtype=jnp.int8)[:, None, None]
    bias_c = jnp.einsum("hb,bsc->hsc", tab, onehot.astype(F32), precision=lax.Precision.HIGHEST)
    bias_c = (bias_c + cmp_mask).reshape(2, 8, S, LANES)
    n_sel = S // NSA_SEL_BLOCK
    cstart = np.arange(LANES) * NSA_CMP_STRIDE
    sstart = np.arange(n_sel) * NSA_SEL_BLOCK
    ovt = np.maximum(np.minimum(cstart[None, :] + NSA_CMP_LEN, sstart[:, None] + NSA_SEL_BLOCK)
                     - np.maximum(cstart[None, :], sstart[:, None]), 0).astype(np.float32)
    ovt[:, n_cmp:] = 0.0
    ocmp, pen = _cmp_attention(qkv, kc, vc, bias_c, jnp.asarray(ovt))

    r_slc = _strip_rows(tab, S // TQ, _causal_mask)
    r_win = _strip_rows(tab, NSA_WINDOW // TQ + 1, _window_mask)
    o = _attn1(qkv, r_slc, r_win, _one_hot_blocks(S, NSA_SEL_BLOCK, n_sel), pen, ocmp, graw, jnp.asarray(gsel, BF16))
    return o.reshape(B * S, D_MODEL)


def kernel(x, p, rel_bias, norm_mix, norm_ffn, norm_ple, w_ffn_gate, w_ffn_up, w_ffn_down, w_ple_proj, w_ple_gate, w_in_ab, w_out_ab, qn_moba, kn_moba, qn_dil, kn_dil, w_in_nsa, w_out_nsa, qn_nsa, kn_cmp, kn_slc, kn_win, cmp_k_pos, cmp_k_w1, cmp_k_b1, cmp_k_w2, cmp_k_b2, cmp_v_pos, cmp_v_w1, cmp_v_b1, cmp_v_w2, cmp_v_b2):
    B, S, D = x.shape
    depth = p.shape[0]
    h = x.reshape(B * S, D)
    for i in range(depth):
        e = i // 2
        if i % 2 == 0:
            o = _mixer_ab(h, norm_mix[i], w_in_ab[e], qn_moba[e], kn_moba[e], qn_dil[e], kn_dil[e], rel_bias, B, S)
            w_out = w_out_ab[e]
        else:
            cmp_k = (cmp_k_pos[e], cmp_k_w1[e], cmp_k_b1[e], cmp_k_w2[e], cmp_k_b2[e])
            cmp_v = (cmp_v_pos[e], cmp_v_w1[e], cmp_v_b1[e], cmp_v_w2[e], cmp_v_b2[e])
            o = _mixer_nsa(h, norm_mix[i], w_in_nsa[e], qn_nsa[e], kn_cmp[e], kn_slc[e], kn_win[e],
                           cmp_k, cmp_v, rel_bias, B, S)
            w_out = w_out_nsa[e][_head_cols(_NSA_HEAD_ORDER), :]
        h = _post_attention(o, h, w_out.astype(BF16), norm_ffn[i], w_ffn_gate[i].astype(BF16),
                            w_ffn_up[i].astype(BF16), w_ffn_down[i].astype(BF16), norm_ple[i],
                            w_ple_gate[i].astype(BF16), p[i].reshape(B * S, -1), w_ple_proj[i].astype(BF16))
    return h.reshape(B, S, D)
```

```python
import functools
import math

import numpy as np
import jax
import jax.numpy as jnp
from jax import lax
from jax.experimental import pallas as pl
from jax.experimental.pallas import tpu as pltpu

F32 = jnp.float32
BF16 = jnp.bfloat16

D_MODEL = 1024
HEAD_DIM = 64
N_HEADS = 16
N_PAIRS = N_HEADS // 2
MOBA_HEADS = 8
MOBA_BLOCK = 256
MOBA_TOPK = 3
MOBA_NBLK = 8
NSA_GROUPS = 4
NSA_CMP_LEN = 32
NSA_CMP_STRIDE = 16
NSA_SEL_BLOCK = 64
NSA_SEL_TOPN = 16
NSA_WINDOW = 512
NSA_FORCE = 1.0e6
REL_BUCKETS = 32
REL_MAX_DIST = 2048
RMS_EPS = 1e-6
ATTN_SCALE = HEAD_DIM ** -0.5

LANES = 128
SUBLANES = 8
TQ = 256
NEG = -1.0e30
ROW_TILE = 512
PROJ_CHUNK = 256
POST_SPLIT = 2
CMP_ROWS = 512
VMEM_LIMIT = 48 * 1024 * 1024
POST_VMEM_LIMIT = 56 * 1024 * 1024
ATTN_VMEM_LIMIT = 56 * 1024 * 1024

NT_DIMS = (((1,), (1,)), ((), ()))


def _cparams(n_axes):
    return pltpu.CompilerParams(dimension_semantics=("arbitrary",) * n_axes,
                                vmem_limit_bytes=VMEM_LIMIT)


def _rms_rows(x, g):
    ms = jnp.mean(x * x, axis=-1, keepdims=True)
    return x * lax.rsqrt(ms + RMS_EPS) * g


def _lane_lo(shape):
    return lax.broadcasted_iota(jnp.int32, shape, len(shape) - 1) < HEAD_DIM


def _resident(shape):
    return pl.BlockSpec(shape, lambda i: (0,) * len(shape), pipeline_mode=pl.Buffered(1))


def _proj_kernel(x_ref, g_ref, w_ref, cg_ref, bd_ref, *rest, norm_chunks, with_extra):
    if with_extra:
        wx_ref, o_ref, ox_ref = rest
    else:
        (o_ref,) = rest
    xn = _rms_rows(x_ref[...], g_ref[...]).astype(BF16)
    bd = bd_ref[...]

    def finish(c, y):
        cols = slice(c * PROJ_CHUNK, (c + 1) * PROJ_CHUNK)
        if norm_chunks[c]:
            y2 = y * y
            hi = y2.astype(BF16)
            lo = (y2 - hi.astype(F32)).astype(BF16)
            ssq = jnp.dot(hi, bd, preferred_element_type=F32) + jnp.dot(lo, bd, preferred_element_type=F32)
            y = y * lax.rsqrt(ssq * (1.0 / HEAD_DIM) + RMS_EPS) * cg_ref[:, cols]
        o_ref[:, cols] = y.astype(o_ref.dtype)

    prev = None
    for c in range(len(norm_chunks)):
        y = jnp.dot(xn, w_ref[:, c * PROJ_CHUNK:(c + 1) * PROJ_CHUNK], preferred_element_type=F32)
        if prev is not None:
            finish(c - 1, prev)
        prev = y
    finish(len(norm_chunks) - 1, prev)
    if with_extra:
        ox_ref[...] = jnp.dot(xn, wx_ref[...], preferred_element_type=F32)


def _project(x, g, w, col_gain, norm_chunks, w_extra=None):
    T, D = x.shape
    N = w.shape[1]
    blk = np.kron(np.eye(PROJ_CHUNK // HEAD_DIM), np.ones((HEAD_DIM, HEAD_DIM))).astype(np.float32)
    row_in = lambda n: pl.BlockSpec((ROW_TILE, n), lambda i: (i, 0))
    in_specs = [row_in(D), _resident((1, D)), _resident((D, N)), _resident((1, N)),
                _resident((PROJ_CHUNK, PROJ_CHUNK))]
    args = [x, g.reshape(1, D), w, col_gain.reshape(1, N), jnp.asarray(blk, BF16)]
    out_specs = [row_in(N)]
    out_shape = [jax.ShapeDtypeStruct((T, N), BF16)]
    if w_extra is not None:
        nx = w_extra.shape[1]
        in_specs.append(_resident((D, nx)))
        args.append(w_extra)
        out_specs.append(row_in(nx))
        out_shape.append(jax.ShapeDtypeStruct((T, nx), F32))
    return pl.pallas_call(
        functools.partial(_proj_kernel, norm_chunks=tuple(norm_chunks), with_extra=w_extra is not None),
        grid=(T // ROW_TILE,),
        in_specs=in_specs,
        out_specs=out_specs,
        out_shape=out_shape,
        compiler_params=_cparams(1),
        name="proj",
    )(*args)


def _sigmoid(z):
    return 1.0 / (1.0 + jnp.exp(-z))


def _post_kernel(o_ref, h_ref, wout_ref, gf_ref, wg_ref, wu_ref, wd_ref, gp_ref, wpg_ref, p_ref, wpp_ref, out_ref):
    halves = [slice(r * ROW_TILE // POST_SPLIT, (r + 1) * ROW_TILE // POST_SPLIT) for r in range(POST_SPLIT)]
    h1 = [h_ref[r, :] + jnp.dot(o_ref[r, :], wout_ref[...], preferred_element_type=F32) for r in halves]
    xn = [_rms_rows(x, gf_ref[...]).astype(BF16) for x in h1]
    a = [jnp.dot(x, wg_ref[...], preferred_element_type=F32) for x in xn]
    u = [jnp.dot(x, wu_ref[...], preferred_element_type=F32) for x in xn]
    act = [(ai * _sigmoid(ai) * ui).astype(BF16) for ai, ui in zip(a, u)]
    h2 = [x + jnp.dot(t, wd_ref[...], preferred_element_type=F32) for x, t in zip(h1, act)]
    hn = [_rms_rows(x, gp_ref[...]).astype(BF16) for x in h2]
    gate = [_sigmoid(jnp.dot(x, wpg_ref[...], preferred_element_type=F32)) for x in hn]
    for r, x, g in zip(halves, h2, gate):
        out_ref[r, :] = x + g * jnp.dot(p_ref[r, :].astype(BF16), wpp_ref[...], preferred_element_type=F32)


def _post_attention(o, h, w_out, g_ffn, wg, wu, wd, g_ple, w_pgate, p, w_pproj):
    T, D = h.shape
    Fh = wg.shape[1]
    Pd = p.shape[1]
    row_in = lambda n: pl.BlockSpec((ROW_TILE, n), lambda i: (i, 0))
    return pl.pallas_call(
        _post_kernel,
        grid=(T // ROW_TILE,),
        in_specs=[row_in(D), row_in(D), _resident((D, D)), _resident((1, D)), _resident((D, Fh)),
                  _resident((D, Fh)), _resident((Fh, D)), _resident((1, D)), _resident((D, D)),
                  row_in(Pd), _resident((Pd, D))],
        out_specs=row_in(D),
        out_shape=jax.ShapeDtypeStruct((T, D), F32),
        compiler_params=pltpu.CompilerParams(dimension_semantics=("arbitrary",), vmem_limit_bytes=POST_VMEM_LIMIT),
        name="post_attention",
    )(o, h, w_out, g_ffn.reshape(1, D), wg, wu, wd, g_ple.reshape(1, D), w_pgate, p, w_pproj)


def _fill_strips(strip_ref, r_ref):
    for h in range(2):
        for o in range(r_ref.shape[2]):
            x = jnp.broadcast_to(r_ref[0, h, o:o + 1, :], (TQ, 2 * TQ))
            strip_ref[h, o * TQ:(o + 1) * TQ, :] = pltpu.roll(x, 0, 1, stride=1, stride_axis=0)[:, :TQ]


def _tile_groups(nblk, span):
    cnt = [min(span, nblk - j) for j in range(nblk)]
    off = [0]
    for c in cnt:
        off.append(off[-1] + c * TQ)
    return cnt, off


def _tile_rows(nblk, span):
    return _tile_groups(nblk, span)[1][-1]


def _pair_attention(q_rows, k_rows, v_rows, strip_ref, s_ref, p_ref, o_ref, nblk, span, skew=1):
    cnt, off = _tile_groups(nblk, span)
    together = s_ref.shape[0] >= 2 * off[-1] and p_ref.shape[2] >= 2 * span * TQ
    s0 = [0, off[-1] if together else 0]
    p0 = [0, span * TQ if together else 0]

    def scores(h, j):
        n = cnt[j] * TQ
        s = lax.dot_general(q_rows(h, j * TQ, j * TQ + n), k_rows(h, j), NT_DIMS, preferred_element_type=F32)
        s_ref[s0[h] + off[j]:s0[h] + off[j] + n, :] = s + strip_ref[h, 0:n, :]

    def probs(h, i):
        j0 = max(0, i - span + 1)
        rows = [s0[h] + off[j] + (i - j) * TQ for j in range(j0, i + 1)]
        mx = s_ref[rows[0]:rows[0] + TQ, :]
        for r in rows[1:]:
            mx = jnp.maximum(mx, s_ref[r:r + TQ, :])
        m = jnp.max(mx, axis=1, keepdims=True)
        for t, r in enumerate(rows):
            p_ref[i, :, p0[h] + t * TQ:p0[h] + (t + 1) * TQ] = jnp.exp(s_ref[r:r + TQ, :] - m).astype(BF16)

    def values(h, i):
        j0 = max(0, i - span + 1)
        out = jnp.dot(p_ref[i, :, p0[h]:p0[h] + (i + 1 - j0) * TQ], v_rows(j0 * TQ, (i + 1) * TQ),
                      preferred_element_type=F32)
        o_ref[h, i * TQ:(i + 1) * TQ, :] = out[:, :LANES] / out[:, LANES:]

    for heads in ([(0, 1)] if together else [(0,), (1,)]):
        for j in range(nblk + skew):
            for h in heads:
                if j < nblk:
                    scores(h, j)
            for h in heads:
                if j >= skew:
                    values(h, j - skew)
            for h in heads:
                if j < nblk:
                    probs(h, j)


def _merge_heads(o_ref):
    return jnp.where(_lane_lo((o_ref.shape[1], LANES)), o_ref[0], o_ref[1])


def _split_heads(q):
    lo = _lane_lo(q.shape)
    zero = jnp.zeros_like(q)
    return jnp.where(lo, q, zero), jnp.where(lo, zero, q)


def _fill_value_ones(vaug_ref):
    vaug_ref[:, LANES:] = jnp.ones((vaug_ref.shape[0], LANES), vaug_ref.dtype)


def _moba_block_means(k_ref):
    shape = (2 * MOBA_NBLK, LANES)
    lo = _lane_lo(shape)
    row = lax.broadcasted_iota(jnp.int32, shape, 0)
    kmt = jnp.zeros(shape, F32)
    for n in range(MOBA_NBLK):
        mean = jnp.mean(k_ref[0, n * MOBA_BLOCK:(n + 1) * MOBA_BLOCK, :].astype(F32), axis=0, keepdims=True)
        kmt = jnp.where(row == n, jnp.where(lo, mean, 0.0), kmt)
        kmt = jnp.where(row == MOBA_NBLK + n, jnp.where(lo, 0.0, mean), kmt)
    return kmt


def _moba_penalty(q, kmt):
    nb = MOBA_NBLK
    S = q.shape[0]
    hi = kmt.astype(BF16)
    rest = kmt - hi.astype(F32)
    mid = rest.astype(BF16)
    lo = (rest - mid.astype(F32)).astype(BF16)
    g3 = lax.dot_general(jnp.concatenate([hi, mid, lo], axis=0), q, NT_DIMS, preferred_element_type=F32)
    g = g3[0:2 * nb] + g3[2 * nb:4 * nb] + g3[4 * nb:6 * nb]
    row = lax.broadcasted_iota(jnp.int32, g.shape, 0)
    n = row & (nb - 1)
    own = lax.shift_right_logical(lax.broadcasted_iota(jnp.int32, g.shape, 1), int(math.log2(MOBA_BLOCK)))
    rank = jnp.zeros(g.shape, F32)
    for m in range(nb - 1):
        gm = jnp.where(row < nb, g[m:m + 1, :], g[nb + m:nb + m + 1, :])
        tie = jnp.where(n > m, 1.0, 0.0)
        beats = jnp.where(gm > g, 1.0, jnp.where(gm == g, tie, 0.0))
        rank = rank + jnp.where(own > m, beats, 0.0)
    keep = jnp.where(n < own, jnp.where(rank < MOBA_TOPK, 1.0, 0.0), jnp.where(n == own, 1.0, 0.0))
    keep = jnp.concatenate([keep, jnp.zeros((LANES - 2 * nb, S), F32)], axis=0).T
    lane = lax.broadcasted_iota(jnp.int32, keep.shape, 1)
    return jnp.where(lane < 2 * nb, (keep - 1.0) * (-NEG), 0.0)


def _attn0_kernel(q_ref, k_ref, v_ref, r_ref, e_ref, o_ref, strip_ref, qaug_ref, kaug_ref, vaug_ref,
                  s_ref, p_ref, oh_ref):
    p = pl.program_id(0)
    b = pl.program_id(1)
    S = q_ref.shape[1]
    nblk = S // TQ

    @pl.when(b == 0)
    def _():
        _fill_strips(strip_ref, r_ref)

    @pl.when(jnp.logical_and(p == 0, b == 0))
    def _():
        for h in range(2):
            kaug_ref[h, :, LANES:] = e_ref[h]
        _fill_value_ones(vaug_ref)

    k = k_ref[0]
    for h, qh in enumerate(_split_heads(q_ref[0])):
        kaug_ref[h, :, :LANES] = k
        qaug_ref[h, :, :LANES] = qh
    vaug_ref[:, :LANES] = v_ref[0]
    is_moba = p < MOBA_HEADS // 2

    @pl.when(is_moba)
    def _():
        pen = _moba_penalty(q_ref[0], _moba_block_means(k_ref)).astype(BF16)
        for h in range(2):
            qaug_ref[h, :, LANES:] = pen

    @pl.when(jnp.logical_not(is_moba))
    def _():
        for h in range(2):
            qaug_ref[h, :, LANES:] = jnp.zeros((S, LANES), BF16)
    _pair_attention(lambda h, r0, r1: qaug_ref[h, r0:r1, :],
                    lambda h, j: kaug_ref[h, j * TQ:(j + 1) * TQ, :],
                    lambda r0, r1: vaug_ref[r0:r1, :],
                    strip_ref, s_ref, p_ref, oh_ref, nblk, nblk)
    o_ref[0] = _merge_heads(oh_ref).astype(o_ref.dtype)


def _attn_scratch(S, span, narrow_span=0):
    nblk = S // TQ
    s_rows = max(_tile_rows(nblk, span), 2 * _tile_rows(nblk, narrow_span))
    return [pltpu.VMEM((2, S, TQ), F32), pltpu.VMEM((2, S, 2 * LANES), BF16), pltpu.VMEM((2, S, 2 * LANES), BF16),
            pltpu.VMEM((S, 2 * LANES), BF16), pltpu.VMEM((s_rows, TQ), F32),
            pltpu.VMEM((nblk, TQ, span * TQ), BF16), pltpu.VMEM((2, S, LANES), F32)]


def _attn0(qkv, r_tab, e_onehot):
    B, S, _ = qkv.shape
    nblk = S // TQ
    return pl.pallas_call(
        _attn0_kernel,
        grid=(N_PAIRS, B),
        in_specs=[
            pl.BlockSpec((1, S, LANES), lambda p, b: (b, 0, p)),
            pl.BlockSpec((1, S, LANES), lambda p, b: (b, 0, N_PAIRS + p)),
            pl.BlockSpec((1, S, LANES), lambda p, b: (b, 0, 2 * N_PAIRS + p)),
            pl.BlockSpec((1, 2, nblk, 2 * TQ), lambda p, b: (p, 0, 0, 0)),
            pl.BlockSpec((2, S, LANES), lambda p, b: (0, 0, 0), pipeline_mode=pl.Buffered(1)),
        ],
        out_specs=pl.BlockSpec((1, S, LANES), lambda p, b: (b, 0, p)),
        out_shape=jax.ShapeDtypeStruct((B, S, D_MODEL), BF16),
        scratch_shapes=_attn_scratch(S, nblk),
        compiler_params=pltpu.CompilerParams(dimension_semantics=("arbitrary",) * 2, vmem_limit_bytes=ATTN_VMEM_LIMIT),
        name="attn_moba_dilated",
    )(qkv, qkv, qkv, r_tab, e_onehot)


def _compress_kernel(a_ref, pos_ref, w1_ref, b1_ref, w2_ref, b2_ref, gain_ref, o_ref, *, normed):
    y = None
    for g in range(2):
        a = a_ref[0, g].astype(F32)
        first = jnp.dot((a + pos_ref[0:1, :]).astype(BF16), w1_ref[0], preferred_element_type=F32)
        second = jnp.dot((a + pos_ref[1:2, :]).astype(BF16), w1_ref[1], preferred_element_type=F32)
        hid = first + pltpu.roll(second, LANES - 1, 0) + b1_ref[...]
        cdf = 0.5 * (1.0 + jnp.tanh(math.sqrt(2.0 / math.pi) * (hid + 0.044715 * (hid * hid * hid))))
        yg = jnp.dot((hid * cdf).astype(BF16), w2_ref[g], preferred_element_type=F32)
        y = yg if y is None else y + yg
    y = y + b2_ref[...]
    if normed:
        lo = _lane_lo(y.shape)
        y2 = y * y
        s_lo = jnp.sum(jnp.where(lo, y2, 0.0), axis=1, keepdims=True)
        s_hi = jnp.sum(jnp.where(lo, 0.0, y2), axis=1, keepdims=True)
        ms = jnp.where(lo, s_lo, s_hi) * (1.0 / HEAD_DIM)
        y = y * lax.rsqrt(ms + RMS_EPS) * gain_ref[...]
    o_ref[0, 0] = y.astype(o_ref.dtype)


def _compress(a, pos, w1, b1, w2, b2, gain, normed):
    B = a.shape[0]
    hid = w1.shape[1]
    half = NSA_CMP_STRIDE * HEAD_DIM
    zeros = jnp.zeros_like(w2)
    w2p = jnp.stack([jnp.concatenate([w2, zeros], axis=1), jnp.concatenate([zeros, w2], axis=1)]).astype(BF16)
    return pl.pallas_call(
        functools.partial(_compress_kernel, normed=normed),
        grid=(B, 2),
        in_specs=[
            pl.BlockSpec((1, 2, LANES, half), lambda b, m: (b, m, 0, 0)),
            pl.BlockSpec((2, half), lambda b, m: (0, 0)),
            pl.BlockSpec((2, half, hid), lambda b, m: (0, 0, 0)),
            pl.BlockSpec((1, hid), lambda b, m: (0, 0)),
            pl.BlockSpec((2, hid, LANES), lambda b, m: (0, 0, 0)),
            pl.BlockSpec((1, LANES), lambda b, m: (0, 0)),
            pl.BlockSpec((1, LANES), lambda b, m: (0, 0)),
        ],
        out_specs=pl.BlockSpec((1, 1, LANES, LANES), lambda b, m: (b, m, 0, 0)),
        out_shape=jax.ShapeDtypeStruct((B, 2, LANES, LANES), BF16),
        compiler_params=_cparams(2),
        name="nsa_compress",
    )(a, pos.reshape(2, half), w1.reshape(2, half, hid).astype(BF16), b1.reshape(1, hid), w2p,
      jnp.tile(b2, 2).reshape(1, LANES), jnp.tile(gain, 2).reshape(1, LANES))


def _cmp_kernel(q_ref, kc_ref, vc_ref, bias_ref, ovt_ref, o_ref, pen_ref):
    kc = kc_ref[0, 0]
    vc = vc_ref[0, 0]
    S = q_ref.shape[1]
    n_sel = ovt_ref.shape[0]
    lo = _lane_lo((CMP_ROWS, LANES))
    blk = lax.broadcasted_iota(jnp.int32, (n_sel, CMP_ROWS), 0)
    for c in range(S // CMP_ROWS):
        rows = slice(c * CMP_ROWS, (c + 1) * CMP_ROWS)
        psum = [jnp.zeros((CMP_ROWS, LANES), F32), jnp.zeros((CMP_ROWS, LANES), F32)]
        for r in range(4):
            heads = _split_heads(q_ref[0, rows, r * LANES:(r + 1) * LANES])
            outs = []
            for h in range(2):
                s = lax.dot_general(heads[h], kc, NT_DIMS, preferred_element_type=F32) + bias_ref[0, 2 * r + h, rows, :]
                m = jnp.max(s, axis=1, keepdims=True)
                e = jnp.exp(s - m)
                l = jnp.sum(e, axis=1, keepdims=True)
                pr = e * jnp.where(m > 0.5 * NEG, 1.0 / l, 0.0)
                psum[h] = psum[h] + pr
                outs.append(jnp.dot(pr.astype(BF16), vc, preferred_element_type=F32))
            o_ref[0, rows, r * LANES:(r + 1) * LANES] = jnp.where(lo, outs[0], outs[1])

        t = c * CMP_ROWS + lax.broadcasted_iota(jnp.int32, (n_sel, CMP_ROWS), 1)
        cur = lax.shift_right_logical(t, int(math.log2(NSA_SEL_BLOCK)))
        keeps = []
        for h in range(2):
            imp = lax.dot_general(ovt_ref[...], psum[h], NT_DIMS,
                                  precision=lax.Precision.HIGHEST, preferred_element_type=F32)
            forced = jnp.where(blk == 0, 1.0, jnp.where(blk == cur, 1.0, jnp.where(blk == cur - 1, 1.0, 0.0)))
            imp = jnp.where(blk <= cur, imp + forced * NSA_FORCE, -jnp.inf)
            groups = [imp[g * SUBLANES:(g + 1) * SUBLANES, :] for g in range(n_sel // SUBLANES)]
            ranks = [jnp.zeros(g.shape, F32) for g in groups]
            for m in range(n_sel):
                im = imp[m:m + 1, :]
                for g, sub in enumerate(groups):
                    ge = jnp.where(im >= sub, 1.0, 0.0)
                    gt = jnp.where(im > sub, 1.0, 0.0)
                    if g * SUBLANES > m:
                        inc = ge
                    elif (g + 1) * SUBLANES - 1 <= m:
                        inc = gt
                    else:
                        inc = jnp.where(lax.broadcasted_iota(jnp.int32, sub.shape, 0) + g * SUBLANES > m, ge, gt)
                    ranks[g] = ranks[g] + inc
            keeps.append(jnp.where(jnp.concatenate(ranks, axis=0) < NSA_SEL_TOPN, 1.0, 0.0))
        keep = jnp.concatenate(keeps + [jnp.zeros((LANES - 2 * n_sel, CMP_ROWS), F32)], axis=0).T
        lane = lax.broadcasted_iota(jnp.int32, keep.shape, 1)
        pen_ref[0, 0, rows, :] = jnp.where(lane < 2 * n_sel, (keep - 1.0) * (-NEG), 0.0).astype(pen_ref.dtype)


def _cmp_attention(qkv, kc, vc, bias, ovt):
    B, S, _ = qkv.shape
    n_sel = ovt.shape[0]
    return pl.pallas_call(
        _cmp_kernel,
        grid=(2, B),
        in_specs=[
            pl.BlockSpec((1, S, 4 * LANES), lambda m, b: (b, 0, m)),
            pl.BlockSpec((1, 1, LANES, LANES), lambda m, b: (b, m, 0, 0)),
            pl.BlockSpec((1, 1, LANES, LANES), lambda m, b: (b, m, 0, 0)),
            pl.BlockSpec((1, 8, S, LANES), lambda m, b: (m, 0, 0, 0), pipeline_mode=pl.Buffered(1)),
            pl.BlockSpec((n_sel, LANES), lambda m, b: (0, 0)),
        ],
        out_specs=[
            pl.BlockSpec((1, S, 4 * LANES), lambda m, b: (b, 0, m)),
            pl.BlockSpec((1, 1, S, LANES), lambda m, b: (b, m, 0, 0)),
        ],
        out_shape=[jax.ShapeDtypeStruct((B, S, D_MODEL), F32),
                   jax.ShapeDtypeStruct((B, 2, S, LANES), BF16)],
        compiler_params=_cparams(2),
        name="nsa_compressed_select",
    )(qkv, kc, vc, bias, ovt)


def _attn1_kernel(q_ref, ks_ref, vs_ref, kw_ref, vw_ref, rs_ref, rw_ref, e_ref, pen_ref, ocmp_ref, graw_ref,
                  gsel_ref, o_ref, strip_s_ref, qaug_ref, kaug_ref, vsaug_ref, s_ref, p_ref, oslc_ref,
                  strip_w_ref, vwaug_ref, owin_ref):
    p = pl.program_id(0)
    b = pl.program_id(1)
    S = q_ref.shape[1]
    nblk = S // TQ

    @pl.when(b == 0)
    def _():
        _fill_strips(strip_s_ref, rs_ref)
        _fill_strips(strip_w_ref, rw_ref)

    @pl.when(jnp.logical_and(p == 0, b == 0))
    def _():
        for h in range(2):
            kaug_ref[h, :, LANES:] = e_ref[h]
        _fill_value_ones(vsaug_ref)
        _fill_value_ones(vwaug_ref)

    ks = ks_ref[0]
    pen = pen_ref[0, 0]
    for h, qh in enumerate(_split_heads(q_ref[0])):
        kaug_ref[h, :, :LANES] = ks
        qaug_ref[h, :, :LANES] = qh
        qaug_ref[h, :, LANES:] = pen
    vsaug_ref[:, :LANES] = vs_ref[0]
    vwaug_ref[:, :LANES] = vw_ref[0]
    _pair_attention(lambda h, r0, r1: qaug_ref[h, r0:r1, :],
                    lambda h, j: kaug_ref[h, j * TQ:(j + 1) * TQ, :],
                    lambda r0, r1: vsaug_ref[r0:r1, :],
                    strip_s_ref, s_ref, p_ref, oslc_ref, nblk, nblk)
    _pair_attention(lambda h, r0, r1: qaug_ref[h, r0:r1, :LANES],
                    lambda h, j: kw_ref[0, j * TQ:(j + 1) * TQ, :],
                    lambda r0, r1: vwaug_ref[r0:r1, :],
                    strip_w_ref, s_ref, p_ref, owin_ref, nblk, strip_w_ref.shape[1] // TQ, skew=2)
    sig = _sigmoid(graw_ref[0])
    hi = sig.astype(BF16)
    lo = (sig - hi.astype(F32)).astype(BF16)
    gates = [jnp.dot(hi, gsel_ref[0, br], preferred_element_type=F32)
             + jnp.dot(lo, gsel_ref[0, br], preferred_element_type=F32) for br in range(3)]
    out = gates[0] * ocmp_ref[0] + gates[1] * _merge_heads(oslc_ref) + gates[2] * _merge_heads(owin_ref)
    o_ref[0] = out.astype(o_ref.dtype)


def _attn1(qkv, r_slc, r_win, e_onehot, pen, ocmp, graw, gsel):
    B, S, _ = qkv.shape
    kv0 = D_MODEL // LANES + 4
    n_win = r_win.shape[2]
    return pl.pallas_call(
        _attn1_kernel,
        grid=(N_PAIRS, B),
        in_specs=[
            pl.BlockSpec((1, S, LANES), lambda p, b: (b, 0, p)),
            pl.BlockSpec((1, S, LANES), lambda p, b: (b, 0, kv0 + p // 4)),
            pl.BlockSpec((1, S, LANES), lambda p, b: (b, 0, kv0 + 2 + p // 4)),
            pl.BlockSpec((1, S, LANES), lambda p, b: (b, 0, kv0 + 4 + p // 4)),
            pl.BlockSpec((1, S, LANES), lambda p, b: (b, 0, kv0 + 6 + p // 4)),
            pl.BlockSpec((1, 2, r_slc.shape[2], 2 * TQ), lambda p, b: (p, 0, 0, 0)),
            pl.BlockSpec((1, 2, n_win, 2 * TQ), lambda p, b: (p, 0, 0, 0)),
            pl.BlockSpec((2, S, LANES), lambda p, b: (0, 0, 0), pipeline_mode=pl.Buffered(1)),
            pl.BlockSpec((1, 1, S, LANES), lambda p, b: (b, p // 4, 0, 0)),
            pl.BlockSpec((1, S, LANES), lambda p, b: (b, 0, p)),
            pl.BlockSpec((1, S, LANES), lambda p, b: (b, 0, 0)),
            pl.BlockSpec((1, 3, LANES, LANES), lambda p, b: (p, 0, 0, 0)),
        ],
        out_specs=pl.BlockSpec((1, S, LANES), lambda p, b: (b, 0, p)),
        out_shape=jax.ShapeDtypeStruct((B, S, D_MODEL), BF16),
        scratch_shapes=_attn_scratch(S, S // TQ, n_win) + [pltpu.VMEM((2, n_win * TQ, TQ), F32),
                                                    pltpu.VMEM((S, 2 * LANES), BF16), pltpu.VMEM((2, S, LANES), F32)],
        compiler_params=pltpu.CompilerParams(dimension_semantics=("arbitrary",) * 2, vmem_limit_bytes=ATTN_VMEM_LIMIT),
        name="attn_nsa",
    )(qkv, qkv, qkv, qkv, qkv, r_slc, r_win, e_onehot, pen, ocmp, graw, gsel)


def _bucket(dist):
    n = np.maximum(dist, 0)
    exact = REL_BUCKETS // 2
    nf = np.maximum(n, 1).astype(np.float64)
    large = exact + (np.log(nf / exact) / math.log(REL_MAX_DIST / exact) * (REL_BUCKETS - exact)).astype(np.int64)
    return np.where(n < exact, n, np.minimum(large, REL_BUCKETS - 1))


def _strip_rows(tab, n_off, extra_of_dist):
    u = np.arange(2 * TQ)
    d = np.arange(n_off)[:, None] * TQ - np.where(u < TQ, u, u - 2 * TQ)[None, :]
    onehot = (_bucket(d)[None] == np.arange(REL_BUCKETS)[:, None, None]).astype(np.float32)
    vals = jnp.einsum("hb,bou->hou", tab, jnp.asarray(onehot), precision=lax.Precision.HIGHEST) + extra_of_dist(d)
    return vals.reshape(tab.shape[0] // 2, 2, n_off, 2 * TQ)


def _causal_mask(d):
    return np.where(d >= 0, 0.0, NEG).astype(np.float32)


def _window_mask(d):
    return np.where((d >= 0) & (d < NSA_WINDOW), 0.0, NEG).astype(np.float32)


def _dilation_log_count(d):
    c = ((d >= 0) & (d <= 128)).astype(np.float64)
    c += ((d >= 0) & (d % 4 == 0) & (d <= 512))
    c += ((d >= 0) & (d % 16 == 0) & (d <= 2048))
    return np.where(c > 0, np.log(np.maximum(c, 1.0)), NEG).astype(np.float32)


def _one_hot_blocks(S, block, per_head):
    e = np.zeros((2, S, LANES), np.float32)
    key = np.arange(S)
    for h in range(2):
        e[h, key, h * per_head + key // block] = 1.0
    return jnp.asarray(e, BF16)


_NSA_HEAD_ORDER = np.array([8 * (p // 4) + (p % 4) + 4 * h for p in range(N_PAIRS) for h in range(2)])


def _head_cols(heads):
    return (np.asarray(heads)[:, None] * HEAD_DIM + np.arange(HEAD_DIM)[None, :]).reshape(-1)


def _mixer_ab(h, g_mix, w_in, qn_a, kn_a, qn_b, kn_b, rel_bias, B, S):
    wa = MOBA_HEADS * HEAD_DIM
    sec = [w_in[:, k * wa:(k + 1) * wa] for k in range(6)]
    w = jnp.concatenate([sec[0], sec[3], sec[1], sec[4], sec[2], sec[5]], axis=1).astype(BF16)
    ones = jnp.ones((wa,), F32)
    gain = jnp.concatenate([jnp.tile(qn_a, 8) * ATTN_SCALE, jnp.tile(qn_b, 8) * ATTN_SCALE,
                            jnp.tile(kn_a, 8), jnp.tile(kn_b, 8), ones, ones])
    per_sec = wa // PROJ_CHUNK
    (qkv,) = _project(h, g_mix, w, gain, [True] * (4 * per_sec) + [False] * (2 * per_sec))
    qkv = qkv.reshape(B, S, 3 * D_MODEL)

    tab = rel_bias.T
    r_tab = jnp.concatenate([_strip_rows(tab[:MOBA_HEADS], S // TQ, _causal_mask),
                             _strip_rows(tab[MOBA_HEADS:], S // TQ, _dilation_log_count)])
    o = _attn0(qkv, r_tab, _one_hot_blocks(S, MOBA_BLOCK, S // MOBA_BLOCK))
    return o.reshape(B * S, D_MODEL)


def _mixer_nsa(h, g_mix, w_in, qn, kn_c, kn_s, kn_w, cmp_k, cmp_v, rel_bias, B, S):
    qw = N_HEADS * HEAD_DIM
    kvw = NSA_GROUPS * HEAD_DIM
    order = _NSA_HEAD_ORDER
    w_main = jnp.concatenate([w_in[:, _head_cols(order)], w_in[:, qw:qw + 6 * kvw]], axis=1).astype(BF16)
    ones = jnp.ones((kvw,), F32)
    gain = jnp.concatenate([jnp.tile(qn, N_HEADS) * ATTN_SCALE, ones, ones,
                            jnp.tile(kn_s, NSA_GROUPS), ones, jnp.tile(kn_w, NSA_GROUPS), ones])
    assert kvw == PROJ_CHUNK
    norm_chunks = [True] * (qw // PROJ_CHUNK) + [False, False, True, False, True, False]

    gcols = np.zeros((LANES,), np.int64)
    gused = np.zeros((LANES,), np.float32)
    gsel = np.zeros((N_PAIRS, 3, LANES, LANES), np.float32)
    for p in range(N_PAIRS):
        for br in range(3):
            for hh in range(2):
                c = 8 * p + 2 * br + hh
                gcols[c] = qw + 6 * kvw + 3 * order[2 * p + hh] + br
                gused[c] = 1.0
                gsel[p, br, c, hh * HEAD_DIM:(hh + 1) * HEAD_DIM] = 1.0
    w_gate = (w_in[:, gcols] * gused).astype(BF16)
    qkv, graw = _project(h, g_mix, w_main, gain, norm_chunks, w_gate)
    qkv = qkv.reshape(B, S, qw + 6 * kvw)
    graw = graw.reshape(B, S, LANES)

    def chunked(col0):
        t = qkv[:, :, col0:col0 + kvw].reshape(B, S // NSA_CMP_STRIDE, NSA_CMP_STRIDE, NSA_GROUPS, HEAD_DIM)
        return t.transpose(0, 3, 1, 2, 4).reshape(B, NSA_GROUPS, S // NSA_CMP_STRIDE, NSA_CMP_STRIDE * HEAD_DIM)

    kc = _compress(chunked(qw), *cmp_k, kn_c, True)
    vc = _compress(chunked(qw + kvw), *cmp_v, kn_c, False)

    tab = rel_bias.T[order]
    n_cmp = (S - NSA_CMP_LEN) // NSA_CMP_STRIDE + 1
    t_pos = np.arange(S)[:, None]
    c_idx = np.arange(LANES)[None, :]
    dc = t_pos - (c_idx * NSA_CMP_STRIDE + NSA_CMP_LEN - 1)
    cmp_mask = np.where((dc >= 0) & (c_idx < n_cmp), 0.0, NEG).astype(np.float32)
    onehot = jnp.asarray(_bucket(dc).astype(np.int8))[None] == jnp.arange(REL_BUCKETS, dtype=jnp.int8)[:, None, None]
    bias_c = jnp.einsum("hb,bsc->hsc", tab, onehot.astype(F32), precision=lax.Precision.HIGHEST)
    bias_c = (bias_c + cmp_mask).reshape(2, 8, S, LANES)
    n_sel = S // NSA_SEL_BLOCK
    cstart = np.arange(LANES) * NSA_CMP_STRIDE
    sstart = np.arange(n_sel) * NSA_SEL_BLOCK
    ovt = np.maximum(np.minimum(cstart[None, :] + NSA_CMP_LEN, sstart[:, None] + NSA_SEL_BLOCK)
                     - np.maximum(cstart[None, :], sstart[:, None]), 0).astype(np.float32)
    ovt[:, n_cmp:] = 0.0
    ocmp, pen = _cmp_attention(qkv, kc, vc, bias_c, jnp.asarray(ovt))

    r_slc = _strip_rows(tab, S // TQ, _causal_mask)
    r_win = _strip_rows(tab, NSA_WINDOW // TQ + 1, _window_mask)
    o = _attn1(qkv, r_slc, r_win, _one_hot_blocks(S, NSA_SEL_BLOCK, n_sel), pen, ocmp, graw, jnp.asarray(gsel, BF16))
    return o.reshape(B * S, D_MODEL)


def kernel(x, p, rel_bias, norm_mix, norm_ffn, norm_ple, w_ffn_gate, w_ffn_up, w_ffn_down, w_ple_proj, w_ple_gate, w_in_ab, w_out_ab, qn_moba, kn_moba, qn_dil, kn_dil, w_in_nsa, w_out_nsa, qn_nsa, kn_cmp, kn_slc, kn_win, cmp_k_pos, cmp_k_w1, cmp_k_b1, cmp_k_w2, cmp_k_b2, cmp_v_pos, cmp_v_w1, cmp_v_b1, cmp_v_w2, cmp_v_b2):
    B, S, D = x.shape
    depth = p.shape[0]
    h = x.reshape(B * S, D)
    for i in range(depth):
        e = i // 2
        if i % 2 == 0:
            o = _mixer_ab(h, norm_mix[i], w_in_ab[e], qn_moba[e], kn_moba[e], qn_dil[e], kn_dil[e], rel_bias, B, S)
            w_out = w_out_ab[e]
        else:
            cmp_k = (cmp_k_pos[e], cmp_k_w1[e], cmp_k_b1[e], cmp_k_w2[e], cmp_k_b2[e])
            cmp_v = (cmp_v_pos[e], cmp_v_w1[e], cmp_v_b1[e], cmp_v_w2[e], cmp_v_b2[e])
            o = _mixer_nsa(h, norm_mix[i], w_in_nsa[e], qn_nsa[e], kn_cmp[e], kn_slc[e], kn_win[e],
                           cmp_k, cmp_v, rel_bias, B, S)
            w_out = w_out_nsa[e][_head_cols(_NSA_HEAD_ORDER), :]
        h = _post_attention(o, h, w_out.astype(BF16), norm_ffn[i], w_ffn_gate[i].astype(BF16),
                            w_ffn_up[i].astype(BF16), w_ffn_down[i].astype(BF16), norm_ple[i],
                            w_ple_gate[i].astype(BF16), p[i].reshape(B * S, -1), w_ple_proj[i].astype(BF16))
    return h.reshape(B, S, D)
```

```python
import functools
import math

import numpy as np
import jax
import jax.numpy as jnp
from jax import lax
from jax.experimental import pallas as pl
from jax.experimental.pallas import tpu as pltpu

F32 = jnp.float32
BF16 = jnp.bfloat16

D_MODEL = 1024
HEAD_DIM = 64
N_HEADS = 16
N_PAIRS = N_HEADS // 2
MOBA_HEADS = 8
MOBA_BLOCK = 256
MOBA_TOPK = 3
MOBA_NBLK = 8
NSA_GROUPS = 4
NSA_CMP_LEN = 32
NSA_CMP_STRIDE = 16
NSA_SEL_BLOCK = 64
NSA_SEL_TOPN = 16
NSA_WINDOW = 512
NSA_FORCE = 1.0e6
REL_BUCKETS = 32
REL_MAX_DIST = 2048
RMS_EPS = 1e-6
ATTN_SCALE = HEAD_DIM ** -0.5

LANES = 128
SUBLANES = 8
TQ = 256
NEG = -1.0e30
ROW_TILE = 512
PROJ_CHUNK = 256
POST_SPLIT = 2
CMP_ROWS = 512
VMEM_LIMIT = 48 * 1024 * 1024
POST_VMEM_LIMIT = 56 * 1024 * 1024
ATTN_VMEM_LIMIT = 56 * 1024 * 1024

NT_DIMS = (((1,), (1,)), ((), ()))


def _cparams(n_axes):
    return pltpu.CompilerParams(dimension_semantics=("arbitrary",) * n_axes,
                                vmem_limit_bytes=VMEM_LIMIT)


def _rms_rows(x, g):
    ms = jnp.mean(x * x, axis=-1, keepdims=True)
    return x * lax.rsqrt(ms + RMS_EPS) * g


def _lane_lo(shape):
    return lax.broadcasted_iota(jnp.int32, shape, len(shape) - 1) < HEAD_DIM


def _resident(shape):
    return pl.BlockSpec(shape, lambda i: (0,) * len(shape), pipeline_mode=pl.Buffered(1))


def _proj_kernel(x_ref, g_ref, w_ref, cg_ref, bd_ref, *rest, norm_chunks, src_chunks, raw_chunks, with_extra):
    if with_extra:
        wx_ref, o_ref, ox_ref, oraw_ref = rest
    else:
        (o_ref,) = rest
    xn = _rms_rows(x_ref[...], g_ref[...]).astype(BF16)
    bd = bd_ref[...]

    def finish(c, y):
        cols = slice(c * PROJ_CHUNK, (c + 1) * PROJ_CHUNK)
        if c in raw_chunks:
            k = raw_chunks.index(c)
            oraw_ref[:, k * PROJ_CHUNK:(k + 1) * PROJ_CHUNK] = y
        if norm_chunks[c]:
            y2 = y * y
            hi = y2.astype(BF16)
            lo = (y2 - hi.astype(F32)).astype(BF16)
            ssq = jnp.dot(hi, bd, preferred_element_type=F32) + jnp.dot(lo, bd, preferred_element_type=F32)
            y = y * lax.rsqrt(ssq * (1.0 / HEAD_DIM) + RMS_EPS) * cg_ref[:, cols]
        o_ref[:, cols] = y.astype(o_ref.dtype)

    prev = None
    for c in range(len(norm_chunks)):
        src = src_chunks[c] * PROJ_CHUNK
        y = jnp.dot(xn, w_ref[:, src:src + PROJ_CHUNK], preferred_element_type=F32)
        if prev is not None:
            finish(c - 1, prev)
        prev = y
    finish(len(norm_chunks) - 1, prev)
    if with_extra:
        ox_ref[...] = jnp.dot(xn, wx_ref[...], preferred_element_type=F32)


def _project(x, g, w, col_gain, norm_chunks, w_extra=None, raw_chunks=(), src_chunks=None):
    if src_chunks is None:
        src_chunks = range(len(norm_chunks))
    T, D = x.shape
    N = w.shape[1]
    blk = np.kron(np.eye(PROJ_CHUNK // HEAD_DIM), np.ones((HEAD_DIM, HEAD_DIM))).astype(np.float32)
    row_in = lambda n: pl.BlockSpec((ROW_TILE, n), lambda i: (i, 0))
    in_specs = [row_in(D), _resident((1, D)), _resident((D, N)), _resident((1, N)),
                _resident((PROJ_CHUNK, PROJ_CHUNK))]
    args = [x, g.reshape(1, D), w, col_gain.reshape(1, N), jnp.asarray(blk, BF16)]
    out_specs = [row_in(N)]
    out_shape = [jax.ShapeDtypeStruct((T, N), BF16)]
    if w_extra is not None:
        nx = w_extra.shape[1]
        in_specs.append(_resident((D, nx)))
        args.append(w_extra)
        out_specs += [row_in(nx), row_in(len(raw_chunks) * PROJ_CHUNK)]
        out_shape += [jax.ShapeDtypeStruct((T, nx), F32), jax.ShapeDtypeStruct((T, len(raw_chunks) * PROJ_CHUNK), F32)]
    return pl.pallas_call(
        functools.partial(_proj_kernel, norm_chunks=tuple(norm_chunks), src_chunks=tuple(src_chunks),
                          raw_chunks=tuple(raw_chunks),
                          with_extra=w_extra is not None),
        grid=(T // ROW_TILE,),
        in_specs=in_specs,
        out_specs=out_specs,
        out_shape=out_shape,
        compiler_params=_cparams(1),
        name="proj",
    )(*args)


def _sigmoid(z):
    return 1.0 / (1.0 + jnp.exp(-z))


def _post_kernel(o_ref, h_ref, wout_ref, gf_ref, wg_ref, wu_ref, wd_ref, gp_ref, wpg_ref, p_ref, wpp_ref, out_ref):
    halves = [slice(r * ROW_TILE // POST_SPLIT, (r + 1) * ROW_TILE // POST_SPLIT) for r in range(POST_SPLIT)]
    h1 = [h_ref[r, :] + jnp.dot(o_ref[r, :], wout_ref[...], preferred_element_type=F32) for r in halves]
    xn = [_rms_rows(x, gf_ref[...]).astype(BF16) for x in h1]
    a = [jnp.dot(x, wg_ref[...], preferred_element_type=F32) for x in xn]
    u = [jnp.dot(x, wu_ref[...], preferred_element_type=F32) for x in xn]
    act = [(ai * _sigmoid(ai) * ui).astype(BF16) for ai, ui in zip(a, u)]
    h2 = [x + jnp.dot(t, wd_ref[...], preferred_element_type=F32) for x, t in zip(h1, act)]
    hn = [_rms_rows(x, gp_ref[...]).astype(BF16) for x in h2]
    gate = [_sigmoid(jnp.dot(x, wpg_ref[...], preferred_element_type=F32)) for x in hn]
    for r, x, g in zip(halves, h2, gate):
        out_ref[r, :] = x + g * jnp.dot(p_ref[r, :].astype(BF16), wpp_ref[...], preferred_element_type=F32)


def _post_attention(o, h, w_out, g_ffn, wg, wu, wd, g_ple, w_pgate, p, w_pproj):
    T, D = h.shape
    Fh = wg.shape[1]
    Pd = p.shape[1]
    row_in = lambda n: pl.BlockSpec((ROW_TILE, n), lambda i: (i, 0))
    return pl.pallas_call(
        _post_kernel,
        grid=(T // ROW_TILE,),
        in_specs=[row_in(D), row_in(D), _resident((D, D)), _resident((1, D)), _resident((D, Fh)),
                  _resident((D, Fh)), _resident((Fh, D)), _resident((1, D)), _resident((D, D)),
                  row_in(Pd), _resident((Pd, D))],
        out_specs=row_in(D),
        out_shape=jax.ShapeDtypeStruct((T, D), F32),
        compiler_params=pltpu.CompilerParams(dimension_semantics=("arbitrary",), vmem_limit_bytes=POST_VMEM_LIMIT),
        name="post_attention",
    )(o, h, w_out, g_ffn.reshape(1, D), wg, wu, wd, g_ple.reshape(1, D), w_pgate, p, w_pproj)


def _fill_strips(strip_ref, r_ref):
    for h in range(2):
        for o in range(r_ref.shape[2]):
            x = jnp.broadcast_to(r_ref[0, h, o:o + 1, :], (TQ, 2 * TQ))
            strip_ref[h, o * TQ:(o + 1) * TQ, :] = pltpu.roll(x, 0, 1, stride=1, stride_axis=0)[:, :TQ]


def _tile_groups(nblk, span):
    cnt = [min(span, nblk - j) for j in range(nblk)]
    off = [0]
    for c in cnt:
        off.append(off[-1] + c * TQ)
    return cnt, off


def _tile_rows(nblk, span):
    return _tile_groups(nblk, span)[1][-1]


def _pair_attention(q_rows, k_rows, v_rows, strip_ref, s_ref, p_ref, o_ref, nblk, span, skew=1):
    cnt, off = _tile_groups(nblk, span)
    together = s_ref.shape[0] >= 2 * off[-1] and p_ref.shape[2] >= 2 * span * TQ
    s0 = [0, off[-1] if together else 0]
    p0 = [0, span * TQ if together else 0]

    def scores(h, j):
        n = cnt[j] * TQ
        s = lax.dot_general(q_rows(h, j * TQ, j * TQ + n), k_rows(h, j), NT_DIMS, preferred_element_type=F32)
        s_ref[s0[h] + off[j]:s0[h] + off[j] + n, :] = s + strip_ref[h, 0:n, :]

    def probs(h, i):
        j0 = max(0, i - span + 1)
        rows = [s0[h] + off[j] + (i - j) * TQ for j in range(j0, i + 1)]
        mx = s_ref[rows[0]:rows[0] + TQ, :]
        for r in rows[1:]:
            mx = jnp.maximum(mx, s_ref[r:r + TQ, :])
        m = jnp.max(mx, axis=1, keepdims=True)
        for t, r in enumerate(rows):
            p_ref[i, :, p0[h] + t * TQ:p0[h] + (t + 1) * TQ] = jnp.exp(s_ref[r:r + TQ, :] - m).astype(BF16)

    def values(h, i):
        j0 = max(0, i - span + 1)
        out = jnp.dot(p_ref[i, :, p0[h]:p0[h] + (i + 1 - j0) * TQ], v_rows(j0 * TQ, (i + 1) * TQ),
                      preferred_element_type=F32)
        o_ref[h, i * TQ:(i + 1) * TQ, :] = out[:, :LANES] / out[:, LANES:]

    for heads in ([(0, 1)] if together else [(0,), (1,)]):
        for j in range(nblk + skew):
            for h in heads:
                if j < nblk:
                    scores(h, j)
            for h in heads:
                if j >= skew:
                    values(h, j - skew)
            for h in heads:
                if j < nblk:
                    probs(h, j)


def _merge_heads(o_ref):
    return jnp.where(_lane_lo((o_ref.shape[1], LANES)), o_ref[0], o_ref[1])


def _split_heads(q):
    lo = _lane_lo(q.shape)
    zero = jnp.zeros_like(q)
    return jnp.where(lo, q, zero), jnp.where(lo, zero, q)


def _fill_value_ones(vaug_ref):
    vaug_ref[:, LANES:] = jnp.ones((vaug_ref.shape[0], LANES), vaug_ref.dtype)


def _moba_block_means(k_ref):
    shape = (2 * MOBA_NBLK, LANES)
    lo = _lane_lo(shape)
    row = lax.broadcasted_iota(jnp.int32, shape, 0)
    kmt = jnp.zeros(shape, F32)
    for n in range(MOBA_NBLK):
        mean = jnp.mean(k_ref[0, n * MOBA_BLOCK:(n + 1) * MOBA_BLOCK, :].astype(F32), axis=0, keepdims=True)
        kmt = jnp.where(row == n, jnp.where(lo, mean, 0.0), kmt)
        kmt = jnp.where(row == MOBA_NBLK + n, jnp.where(lo, 0.0, mean), kmt)
    return kmt


def _moba_penalty(q, kmt):
    nb = MOBA_NBLK
    S = q.shape[0]
    hi = kmt.astype(BF16)
    rest = kmt - hi.astype(F32)
    mid = rest.astype(BF16)
    lo = (rest - mid.astype(F32)).astype(BF16)
    g3 = lax.dot_general(jnp.concatenate([hi, mid, lo], axis=0), q, NT_DIMS, preferred_element_type=F32)
    g = g3[0:2 * nb] + g3[2 * nb:4 * nb] + g3[4 * nb:6 * nb]
    row = lax.broadcasted_iota(jnp.int32, g.shape, 0)
    n = row & (nb - 1)
    own = lax.shift_right_logical(lax.broadcasted_iota(jnp.int32, g.shape, 1), int(math.log2(MOBA_BLOCK)))
    rank = jnp.zeros(g.shape, F32)
    for m in range(nb - 1):
        gm = jnp.where(row < nb, g[m:m + 1, :], g[nb + m:nb + m + 1, :])
        tie = jnp.where(n > m, 1.0, 0.0)
        beats = jnp.where(gm > g, 1.0, jnp.where(gm == g, tie, 0.0))
        rank = rank + jnp.where(own > m, beats, 0.0)
    keep = jnp.where(n < own, jnp.where(rank < MOBA_TOPK, 1.0, 0.0), jnp.where(n == own, 1.0, 0.0))
    keep = jnp.concatenate([keep, jnp.zeros((LANES - 2 * nb, S), F32)], axis=0).T
    lane = lax.broadcasted_iota(jnp.int32, keep.shape, 1)
    return jnp.where(lane < 2 * nb, (keep - 1.0) * (-NEG), 0.0)


def _attn0_kernel(q_ref, k_ref, v_ref, r_ref, e_ref, o_ref, strip_ref, qaug_ref, kaug_ref, vaug_ref,
                  s_ref, p_ref, oh_ref):
    p = pl.program_id(0)
    b = pl.program_id(1)
    S = q_ref.shape[1]
    nblk = S // TQ

    @pl.when(b == 0)
    def _():
        _fill_strips(strip_ref, r_ref)

    @pl.when(jnp.logical_and(p == 0, b == 0))
    def _():
        for h in range(2):
            kaug_ref[h, :, LANES:] = e_ref[h]
        _fill_value_ones(vaug_ref)

    k = k_ref[0]
    for h, qh in enumerate(_split_heads(q_ref[0])):
        kaug_ref[h, :, :LANES] = k
        qaug_ref[h, :, :LANES] = qh
    vaug_ref[:, :LANES] = v_ref[0]
    is_moba = p < MOBA_HEADS // 2

    @pl.when(is_moba)
    def _():
        pen = _moba_penalty(q_ref[0], _moba_block_means(k_ref)).astype(BF16)
        for h in range(2):
            qaug_ref[h, :, LANES:] = pen

    @pl.when(jnp.logical_not(is_moba))
    def _():
        for h in range(2):
            qaug_ref[h, :, LANES:] = jnp.zeros((S, LANES), BF16)
    _pair_attention(lambda h, r0, r1: qaug_ref[h, r0:r1, :],
                    lambda h, j: kaug_ref[h, j * TQ:(j + 1) * TQ, :],
                    lambda r0, r1: vaug_ref[r0:r1, :],
                    strip_ref, s_ref, p_ref, oh_ref, nblk, nblk)
    o_ref[0] = _merge_heads(oh_ref).astype(o_ref.dtype)


def _attn_scratch(S, span, narrow_span=0):
    nblk = S // TQ
    s_rows = max(_tile_rows(nblk, span), 2 * _tile_rows(nblk, narrow_span))
    return [pltpu.VMEM((2, S, TQ), F32), pltpu.VMEM((2, S, 2 * LANES), BF16), pltpu.VMEM((2, S, 2 * LANES), BF16),
            pltpu.VMEM((S, 2 * LANES), BF16), pltpu.VMEM((s_rows, TQ), F32),
            pltpu.VMEM((nblk, TQ, span * TQ), BF16), pltpu.VMEM((2, S, LANES), F32)]


def _attn0(qkv, r_tab, e_onehot):
    B, S, _ = qkv.shape
    nblk = S // TQ
    return pl.pallas_call(
        _attn0_kernel,
        grid=(N_PAIRS, B),
        in_specs=[
            pl.BlockSpec((1, S, LANES), lambda p, b: (b, 0, p)),
            pl.BlockSpec((1, S, LANES), lambda p, b: (b, 0, N_PAIRS + p)),
            pl.BlockSpec((1, S, LANES), lambda p, b: (b, 0, 2 * N_PAIRS + p)),
            pl.BlockSpec((1, 2, nblk, 2 * TQ), lambda p, b: (p, 0, 0, 0)),
            pl.BlockSpec((2, S, LANES), lambda p, b: (0, 0, 0), pipeline_mode=pl.Buffered(1)),
        ],
        out_specs=pl.BlockSpec((1, S, LANES), lambda p, b: (b, 0, p)),
        out_shape=jax.ShapeDtypeStruct((B, S, D_MODEL), BF16),
        scratch_shapes=_attn_scratch(S, nblk),
        compiler_params=pltpu.CompilerParams(dimension_semantics=("arbitrary",) * 2, vmem_limit_bytes=ATTN_VMEM_LIMIT),
        name="attn_moba_dilated",
    )(qkv, qkv, qkv, r_tab, e_onehot)


def _compress_kernel(x_ref, pos_ref, w1_ref, b1_ref, w2_ref, b2_ref, gain_ref, o_ref, *, normed):
    n_chunk = x_ref.shape[1] // NSA_CMP_STRIDE
    first = second = None
    for a in range(NSA_CMP_STRIDE):
        t = x_ref[0, pl.ds(a, n_chunk, stride=NSA_CMP_STRIDE), :]
        fa = jnp.dot((t + pos_ref[a:a + 1, :]).astype(BF16), w1_ref[a], preferred_element_type=F32)
        sa = jnp.dot((t + pos_ref[NSA_CMP_STRIDE + a:NSA_CMP_STRIDE + a + 1, :]).astype(BF16),
                     w1_ref[NSA_CMP_STRIDE + a], preferred_element_type=F32)
        first = fa if first is None else first + fa
        second = sa if second is None else second + sa
    hid = first + pltpu.roll(second, n_chunk - 1, 0) + b1_ref[...]
    cdf = 0.5 * (1.0 + jnp.tanh(math.sqrt(2.0 / math.pi) * (hid + 0.044715 * (hid * hid * hid))))
    y = jnp.dot((hid * cdf).astype(BF16), w2_ref[...], preferred_element_type=F32) + b2_ref[...]
    if normed:
        lo = _lane_lo(y.shape)
        y2 = y * y
        s_lo = jnp.sum(jnp.where(lo, y2, 0.0), axis=1, keepdims=True)
        s_hi = jnp.sum(jnp.where(lo, 0.0, y2), axis=1, keepdims=True)
        ms = jnp.where(lo, s_lo, s_hi) * (1.0 / HEAD_DIM)
        y = y * lax.rsqrt(ms + RMS_EPS) * gain_ref[...]
    o_ref[0, 0] = y.astype(o_ref.dtype)


def _pair_block_diag(w):
    z = jnp.zeros_like(w)
    return jnp.concatenate([jnp.concatenate([w, z], axis=-1), jnp.concatenate([z, w], axis=-1)], axis=-2)


def _compress(raw, col_block0, pos, w1, b1, w2, b2, gain, normed):
    B, S, _ = raw.shape
    hid = w1.shape[1]
    n_pos = pos.shape[0]
    w1bd = _pair_block_diag(w1.reshape(n_pos, HEAD_DIM, hid)).astype(BF16)
    return pl.pallas_call(
        functools.partial(_compress_kernel, normed=normed),
        grid=(B, 2),
        in_specs=[
            pl.BlockSpec((1, S, LANES), lambda b, m: (b, 0, col_block0 + m)),
            pl.BlockSpec((n_pos, LANES), lambda b, m: (0, 0)),
            pl.BlockSpec((n_pos, LANES, 2 * hid), lambda b, m: (0, 0, 0)),
            pl.BlockSpec((1, 2 * hid), lambda b, m: (0, 0)),
            pl.BlockSpec((2 * hid, LANES), lambda b, m: (0, 0)),
            pl.BlockSpec((1, LANES), lambda b, m: (0, 0)),
            pl.BlockSpec((1, LANES), lambda b, m: (0, 0)),
        ],
        out_specs=pl.BlockSpec((1, 1, S // NSA_CMP_STRIDE, LANES), lambda b, m: (b, m, 0, 0)),
        out_shape=jax.ShapeDtypeStruct((B, 2, S // NSA_CMP_STRIDE, LANES), BF16),
        compiler_params=_cparams(2),
        name="nsa_compress",
    )(raw, jnp.tile(pos, (1, 2)), w1bd, jnp.tile(b1, 2).reshape(1, 2 * hid), _pair_block_diag(w2).astype(BF16),
      jnp.tile(b2, 2).reshape(1, LANES), jnp.tile(gain, 2).reshape(1, LANES))


def _cmp_kernel(q_ref, kc_ref, vc_ref, bias_ref, ovt_ref, o_ref, pen_ref):
    kc = kc_ref[0, 0]
    vc = vc_ref[0, 0]
    S = q_ref.shape[1]
    n_sel = ovt_ref.shape[0]
    lo = _lane_lo((CMP_ROWS, LANES))
    blk = lax.broadcasted_iota(jnp.int32, (n_sel, CMP_ROWS), 0)
    for c in range(S // CMP_ROWS):
        rows = slice(c * CMP_ROWS, (c + 1) * CMP_ROWS)
        psum = [jnp.zeros((CMP_ROWS, LANES), F32), jnp.zeros((CMP_ROWS, LANES), F32)]
        for r in range(4):
            heads = _split_heads(q_ref[0, rows, r * LANES:(r + 1) * LANES])
            outs = []
            for h in range(2):
                s = lax.dot_general(heads[h], kc, NT_DIMS, preferred_element_type=F32) + bias_ref[0, 2 * r + h, rows, :]
                m = jnp.max(s, axis=1, keepdims=True)
                e = jnp.exp(s - m)
                l = jnp.sum(e, axis=1, keepdims=True)
                pr = e * jnp.where(m > 0.5 * NEG, 1.0 / l, 0.0)
                psum[h] = psum[h] + pr
                outs.append(jnp.dot(pr.astype(BF16), vc, preferred_element_type=F32))
            o_ref[0, rows, r * LANES:(r + 1) * LANES] = jnp.where(lo, outs[0], outs[1])

        t = c * CMP_ROWS + lax.broadcasted_iota(jnp.int32, (n_sel, CMP_ROWS), 1)
        cur = lax.shift_right_logical(t, int(math.log2(NSA_SEL_BLOCK)))
        keeps = []
        for h in range(2):
            imp = lax.dot_general(ovt_ref[...], psum[h], NT_DIMS,
                                  precision=lax.Precision.HIGHEST, preferred_element_type=F32)
            forced = jnp.where(blk == 0, 1.0, jnp.where(blk == cur, 1.0, jnp.where(blk == cur - 1, 1.0, 0.0)))
            imp = jnp.where(blk <= cur, imp + forced * NSA_FORCE, -jnp.inf)
            groups = [imp[g * SUBLANES:(g + 1) * SUBLANES, :] for g in range(n_sel // SUBLANES)]
            ranks = [jnp.zeros(g.shape, F32) for g in groups]
            for m in range(n_sel):
                im = imp[m:m + 1, :]
                for g, sub in enumerate(groups):
                    ge = jnp.where(im >= sub, 1.0, 0.0)
                    gt = jnp.where(im > sub, 1.0, 0.0)
                    if g * SUBLANES > m:
                        inc = ge
                    elif (g + 1) * SUBLANES - 1 <= m:
                        inc = gt
                    else:
                        inc = jnp.where(lax.broadcasted_iota(jnp.int32, sub.shape, 0) + g * SUBLANES > m, ge, gt)
                    ranks[g] = ranks[g] + inc
            keeps.append(jnp.where(jnp.concatenate(ranks, axis=0) < NSA_SEL_TOPN, 1.0, 0.0))
        keep = jnp.concatenate(keeps + [jnp.zeros((LANES - 2 * n_sel, CMP_ROWS), F32)], axis=0).T
        lane = lax.broadcasted_iota(jnp.int32, keep.shape, 1)
        pen_ref[0, 0, rows, :] = jnp.where(lane < 2 * n_sel, (keep - 1.0) * (-NEG), 0.0).astype(pen_ref.dtype)


def _cmp_attention(qkv, kc, vc, bias, ovt):
    B, S, _ = qkv.shape
    n_sel = ovt.shape[0]
    return pl.pallas_call(
        _cmp_kernel,
        grid=(2, B),
        in_specs=[
            pl.BlockSpec((1, S, 4 * LANES), lambda m, b: (b, 0, m)),
            pl.BlockSpec((1, 1, LANES, LANES), lambda m, b: (b, m, 0, 0)),
            pl.BlockSpec((1, 1, LANES, LANES), lambda m, b: (b, m, 0, 0)),
            pl.BlockSpec((1, 8, S, LANES), lambda m, b: (m, 0, 0, 0), pipeline_mode=pl.Buffered(1)),
            pl.BlockSpec((n_sel, LANES), lambda m, b: (0, 0)),
        ],
        out_specs=[
            pl.BlockSpec((1, S, 4 * LANES), lambda m, b: (b, 0, m)),
            pl.BlockSpec((1, 1, S, LANES), lambda m, b: (b, m, 0, 0)),
        ],
        out_shape=[jax.ShapeDtypeStruct((B, S, D_MODEL), F32),
                   jax.ShapeDtypeStruct((B, 2, S, LANES), BF16)],
        compiler_params=_cparams(2),
        name="nsa_compressed_select",
    )(qkv, kc, vc, bias, ovt)


def _attn1_kernel(q_ref, ks_ref, vs_ref, kw_ref, vw_ref, rs_ref, rw_ref, e_ref, pen_ref, ocmp_ref, graw_ref,
                  gsel_ref, o_ref, strip_s_ref, qaug_ref, kaug_ref, vsaug_ref, s_ref, p_ref, oslc_ref,
                  strip_w_ref, vwaug_ref, owin_ref):
    p = pl.program_id(0)
    b = pl.program_id(1)
    S = q_ref.shape[1]
    nblk = S // TQ

    @pl.when(b == 0)
    def _():
        _fill_strips(strip_s_ref, rs_ref)
        _fill_strips(strip_w_ref, rw_ref)

    @pl.when(jnp.logical_and(p == 0, b == 0))
    def _():
        for h in range(2):
            kaug_ref[h, :, LANES:] = e_ref[h]
        _fill_value_ones(vsaug_ref)
        _fill_value_ones(vwaug_ref)

    ks = ks_ref[0]
    pen = pen_ref[0, 0]
    for h, qh in enumerate(_split_heads(q_ref[0])):
        kaug_ref[h, :, :LANES] = ks
        qaug_ref[h, :, :LANES] = qh
        qaug_ref[h, :, LANES:] = pen
    vsaug_ref[:, :LANES] = vs_ref[0]
    vwaug_ref[:, :LANES] = vw_ref[0]
    _pair_attention(lambda h, r0, r1: qaug_ref[h, r0:r1, :],
                    lambda h, j: kaug_ref[h, j * TQ:(j + 1) * TQ, :],
                    lambda r0, r1: vsaug_ref[r0:r1, :],
                    strip_s_ref, s_ref, p_ref, oslc_ref, nblk, nblk)
    _pair_attention(lambda h, r0, r1: qaug_ref[h, r0:r1, :LANES],
                    lambda h, j: kw_ref[0, j * TQ:(j + 1) * TQ, :],
                    lambda r0, r1: vwaug_ref[r0:r1, :],
                    strip_w_ref, s_ref, p_ref, owin_ref, nblk, strip_w_ref.shape[1] // TQ, skew=2)
    sig = _sigmoid(graw_ref[0])
    hi = sig.astype(BF16)
    lo = (sig - hi.astype(F32)).astype(BF16)
    gates = [jnp.dot(hi, gsel_ref[0, br], preferred_element_type=F32)
             + jnp.dot(lo, gsel_ref[0, br], preferred_element_type=F32) for br in range(3)]
    out = gates[0] * ocmp_ref[0] + gates[1] * _merge_heads(oslc_ref) + gates[2] * _merge_heads(owin_ref)
    o_ref[0] = out.astype(o_ref.dtype)


def _attn1(qkv, r_slc, r_win, e_onehot, pen, ocmp, graw, gsel):
    B, S, _ = qkv.shape
    kv0 = D_MODEL // LANES + 4
    n_win = r_win.shape[2]
    return pl.pallas_call(
        _attn1_kernel,
        grid=(N_PAIRS, B),
        in_specs=[
            pl.BlockSpec((1, S, LANES), lambda p, b: (b, 0, p)),
            pl.BlockSpec((1, S, LANES), lambda p, b: (b, 0, kv0 + p // 4)),
            pl.BlockSpec((1, S, LANES), lambda p, b: (b, 0, kv0 + 2 + p // 4)),
            pl.BlockSpec((1, S, LANES), lambda p, b: (b, 0, kv0 + 4 + p // 4)),
            pl.BlockSpec((1, S, LANES), lambda p, b: (b, 0, kv0 + 6 + p // 4)),
            pl.BlockSpec((1, 2, r_slc.shape[2], 2 * TQ), lambda p, b: (p, 0, 0, 0)),
            pl.BlockSpec((1, 2, n_win, 2 * TQ), lambda p, b: (p, 0, 0, 0)),
            pl.BlockSpec((2, S, LANES), lambda p, b: (0, 0, 0), pipeline_mode=pl.Buffered(1)),
            pl.BlockSpec((1, 1, S, LANES), lambda p, b: (b, p // 4, 0, 0)),
            pl.BlockSpec((1, S, LANES), lambda p, b: (b, 0, p)),
            pl.BlockSpec((1, S, LANES), lambda p, b: (b, 0, 0)),
            pl.BlockSpec((1, 3, LANES, LANES), lambda p, b: (p, 0, 0, 0)),
        ],
        out_specs=pl.BlockSpec((1, S, LANES), lambda p, b: (b, 0, p)),
        out_shape=jax.ShapeDtypeStruct((B, S, D_MODEL), BF16),
        scratch_shapes=_attn_scratch(S, S // TQ, n_win) + [pltpu.VMEM((2, n_win * TQ, TQ), F32),
                                                    pltpu.VMEM((S, 2 * LANES), BF16), pltpu.VMEM((2, S, LANES), F32)],
        compiler_params=pltpu.CompilerParams(dimension_semantics=("arbitrary",) * 2, vmem_limit_bytes=ATTN_VMEM_LIMIT),
        name="attn_nsa",
    )(qkv, qkv, qkv, qkv, qkv, r_slc, r_win, e_onehot, pen, ocmp, graw, gsel)


def _bucket(dist):
    n = np.maximum(dist, 0)
    exact = REL_BUCKETS // 2
    nf = np.maximum(n, 1).astype(np.float64)
    large = exact + (np.log(nf / exact) / math.log(REL_MAX_DIST / exact) * (REL_BUCKETS - exact)).astype(np.int64)
    return np.where(n < exact, n, np.minimum(large, REL_BUCKETS - 1))


def _strip_rows(tab, n_off, extra_of_dist):
    u = np.arange(2 * TQ)
    d = np.arange(n_off)[:, None] * TQ - np.where(u < TQ, u, u - 2 * TQ)[None, :]
    onehot = (_bucket(d)[None] == np.arange(REL_BUCKETS)[:, None, None]).astype(np.float32)
    vals = jnp.einsum("hb,bou->hou", tab, jnp.asarray(onehot), precision=lax.Precision.HIGHEST) + extra_of_dist(d)
    return vals.reshape(tab.shape[0] // 2, 2, n_off, 2 * TQ)


def _causal_mask(d):
    return np.where(d >= 0, 0.0, NEG).astype(np.float32)


def _window_mask(d):
    return np.where((d >= 0) & (d < NSA_WINDOW), 0.0, NEG).astype(np.float32)


def _dilation_log_count(d):
    c = ((d >= 0) & (d <= 128)).astype(np.float64)
    c += ((d >= 0) & (d % 4 == 0) & (d <= 512))
    c += ((d >= 0) & (d % 16 == 0) & (d <= 2048))
    return np.where(c > 0, np.log(np.maximum(c, 1.0)), NEG).astype(np.float32)


def _one_hot_blocks(S, block, per_head):
    e = np.zeros((2, S, LANES), np.float32)
    key = np.arange(S)
    for h in range(2):
        e[h, key, h * per_head + key // block] = 1.0
    return jnp.asarray(e, BF16)


_NSA_HEAD_ORDER = np.array([8 * (p // 4) + (p % 4) + 4 * h for p in range(N_PAIRS) for h in range(2)])


def _head_cols(heads):
    return (np.asarray(heads)[:, None] * HEAD_DIM + np.arange(HEAD_DIM)[None, :]).reshape(-1)


def _mixer_ab(h, g_mix, w_in, qn_a, kn_a, qn_b, kn_b, rel_bias, B, S):
    wa = MOBA_HEADS * HEAD_DIM
    ones = jnp.ones((wa,), F32)
    gain = jnp.concatenate([jnp.tile(qn_a, 8) * ATTN_SCALE, jnp.tile(qn_b, 8) * ATTN_SCALE,
                            jnp.tile(kn_a, 8), jnp.tile(kn_b, 8), ones, ones])
    per_sec = wa // PROJ_CHUNK
    src = [sec * per_sec + c for sec in (0, 3, 1, 4, 2, 5) for c in range(per_sec)]
    (qkv,) = _project(h, g_mix, w_in.astype(BF16), gain, [True] * (4 * per_sec) + [False] * (2 * per_sec),
                      src_chunks=src)
    qkv = qkv.reshape(B, S, 3 * D_MODEL)

    tab = rel_bias.T
    r_tab = jnp.concatenate([_strip_rows(tab[:MOBA_HEADS], S // TQ, _causal_mask),
                             _strip_rows(tab[MOBA_HEADS:], S // TQ, _dilation_log_count)])
    o = _attn0(qkv, r_tab, _one_hot_blocks(S, MOBA_BLOCK, S // MOBA_BLOCK))
    return o.reshape(B * S, D_MODEL)


def _mixer_nsa(h, g_mix, w_in, qn, kn_c, kn_s, kn_w, cmp_k, cmp_v, rel_bias, B, S):
    qw = N_HEADS * HEAD_DIM
    kvw = NSA_GROUPS * HEAD_DIM
    order = _NSA_HEAD_ORDER
    w_main = jnp.concatenate([w_in[:, _head_cols(order)], w_in[:, qw:qw + 6 * kvw]], axis=1).astype(BF16)
    ones = jnp.ones((kvw,), F32)
    gain = jnp.concatenate([jnp.tile(qn, N_HEADS) * ATTN_SCALE, ones, ones,
                            jnp.tile(kn_s, NSA_GROUPS), ones, jnp.tile(kn_w, NSA_GROUPS), ones])
    assert kvw == PROJ_CHUNK
    norm_chunks = [True] * (qw // PROJ_CHUNK) + [False, False, True, False, True, False]

    gcols = np.zeros((LANES,), np.int64)
    gused = np.zeros((LANES,), np.float32)
    gsel = np.zeros((N_PAIRS, 3, LANES, LANES), np.float32)
    for p in range(N_PAIRS):
        for br in range(3):
            for hh in range(2):
                c = 8 * p + 2 * br + hh
                gcols[c] = qw + 6 * kvw + 3 * order[2 * p + hh] + br
                gused[c] = 1.0
                gsel[p, br, c, hh * HEAD_DIM:(hh + 1) * HEAD_DIM] = 1.0
    w_gate = (w_in[:, gcols] * gused).astype(BF16)
    kc_chunk = qw // PROJ_CHUNK
    qkv, graw, raw = _project(h, g_mix, w_main, gain, norm_chunks, w_gate, raw_chunks=(kc_chunk, kc_chunk + 1))
    qkv = qkv.reshape(B, S, qw + 6 * kvw)
    graw = graw.reshape(B, S, LANES)
    raw = raw.reshape(B, S, 2 * kvw)
    kc = _compress(raw, 0, *cmp_k, kn_c, True)
    vc = _compress(raw, kvw // LANES, *cmp_v, kn_c, False)

    tab = rel_bias.T[order]
    n_cmp = (S - NSA_CMP_LEN) // NSA_CMP_STRIDE + 1
    t_pos = np.arange(S)[:, None]
    c_idx = np.arange(LANES)[None, :]
    dc = t_pos - (c_idx * NSA_CMP_STRIDE + NSA_CMP_LEN - 1)
    cmp_mask = np.where((dc >= 0) & (c_idx < n_cmp), 0.0, NEG).astype(np.float32)
    onehot = jnp.asarray(_bucket(dc).astype(np.int8))[None] == jnp.arange(REL_BUCKETS, dtype=jnp.int8)[:, None, None]
    bias_c = jnp.einsum("hb,bsc->hsc", tab, onehot.astype(F32), precision=lax.Precision.HIGHEST)
    bias_c = (bias_c + cmp_mask).reshape(2, 8, S, LANES)
    n_sel = S // NSA_SEL_BLOCK
    cstart = np.arange(LANES) * NSA_CMP_STRIDE
    sstart = np.arange(n_sel) * NSA_SEL_BLOCK
    ovt = np.maximum(np.minimum(cstart[None, :] + NSA_CMP_LEN, sstart[:, None] + NSA_SEL_BLOCK)
                     - np.maximum(cstart[None, :], sstart[:, None]), 0).astype(np.float32)
    ovt[:, n_cmp:] = 0.0
    ocmp, pen = _cmp_attention(qkv, kc, vc, bias_c, jnp.asarray(ovt))

    r_slc = _strip_rows(tab, S // TQ, _causal_mask)
    r_win = _strip_rows(tab, NSA_WINDOW // TQ + 1, _window_mask)
    o = _attn1(qkv, r_slc, r_win, _one_hot_blocks(S, NSA_SEL_BLOCK, n_sel), pen, ocmp, graw, jnp.asarray(gsel, BF16))
    return o.reshape(B * S, D_MODEL)


def kernel(x, p, rel_bias, norm_mix, norm_ffn, norm_ple, w_ffn_gate, w_ffn_up, w_ffn_down, w_ple_proj, w_ple_gate, w_in_ab, w_out_ab, qn_moba, kn_moba, qn_dil, kn_dil, w_in_nsa, w_out_nsa, qn_nsa, kn_cmp, kn_slc, kn_win, cmp_k_pos, cmp_k_w1, cmp_k_b1, cmp_k_w2, cmp_k_b2, cmp_v_pos, cmp_v_w1, cmp_v_b1, cmp_v_w2, cmp_v_b2):
    B, S, D = x.shape
    depth = p.shape[0]
    h = x.reshape(B * S, D)
    for i in range(depth):
        e = i // 2
        if i % 2 == 0:
            o = _mixer_ab(h, norm_mix[i], w_in_ab[e], qn_moba[e], kn_moba[e], qn_dil[e], kn_dil[e], rel_bias, B, S)
            w_out = w_out_ab[e]
        else:
            cmp_k = (cmp_k_pos[e], cmp_k_w1[e], cmp_k_b1[e], cmp_k_w2[e], cmp_k_b2[e])
            cmp_v = (cmp_v_pos[e], cmp_v_w1[e], cmp_v_b1[e], cmp_v_w2[e], cmp_v_b2[e])
            o = _mixer_nsa(h, norm_mix[i], w_in_nsa[e], qn_nsa[e], kn_cmp[e], kn_slc[e], kn_win[e],
                           cmp_k, cmp_v, rel_bias, B, S)
            w_out = w_out_nsa[e][_head_cols(_NSA_HEAD_ORDER), :]
        h = _post_attention(o, h, w_out.astype(BF16), norm_ffn[i], w_ffn_gate[i].astype(BF16),
                            w_ffn_up[i].astype(BF16), w_ffn_down[i].astype(BF16), norm_ple[i],
                            w_ple_gate[i].astype(BF16), p[i].reshape(B * S, -1), w_ple_proj[i].astype(BF16))
    return h.reshape(B, S, D)
```

```python
import functools
import math

import numpy as np
import jax
import jax.numpy as jnp
from jax import lax
from jax.experimental import pallas as pl
from jax.experimental.pallas import tpu as pltpu

F32 = jnp.float32
BF16 = jnp.bfloat16

D_MODEL = 1024
HEAD_DIM = 64
N_HEADS = 16
N_PAIRS = N_HEADS // 2
MOBA_HEADS = 8
MOBA_BLOCK = 256
MOBA_TOPK = 3
MOBA_NBLK = 8
NSA_GROUPS = 4
NSA_CMP_LEN = 32
NSA_CMP_STRIDE = 16
NSA_SEL_BLOCK = 64
NSA_SEL_TOPN = 16
NSA_WINDOW = 512
NSA_FORCE = 1.0e6
REL_BUCKETS = 32
REL_MAX_DIST = 2048
RMS_EPS = 1e-6
ATTN_SCALE = HEAD_DIM ** -0.5

LANES = 128
SUBLANES = 8
TQ = 256
NEG = -1.0e30
ROW_TILE = 512
PROJ_CHUNK = 256
POST_SPLIT = 2
CMP_ROWS = 2048
VMEM_LIMIT = 48 * 1024 * 1024
POST_VMEM_LIMIT = 56 * 1024 * 1024
ATTN_VMEM_LIMIT = 56 * 1024 * 1024

NT_DIMS = (((1,), (1,)), ((), ()))


def _cparams(n_axes):
    return pltpu.CompilerParams(dimension_semantics=("arbitrary",) * n_axes,
                                vmem_limit_bytes=VMEM_LIMIT)


def _rms_rows(x, g):
    ms = jnp.mean(x * x, axis=-1, keepdims=True)
    return x * lax.rsqrt(ms + RMS_EPS) * g


def _lane_lo(shape):
    return lax.broadcasted_iota(jnp.int32, shape, len(shape) - 1) < HEAD_DIM


def _resident(shape):
    return pl.BlockSpec(shape, lambda i: (0,) * len(shape), pipeline_mode=pl.Buffered(1))


def _proj_kernel(x_ref, g_ref, w_ref, cg_ref, bd_ref, *rest, norm_chunks, src_chunks, raw_chunks, with_extra):
    if with_extra:
        wx_ref, o_ref, ox_ref, oraw_ref = rest
    else:
        (o_ref,) = rest
    xn = _rms_rows(x_ref[...], g_ref[...]).astype(BF16)
    bd = bd_ref[...]

    def finish(c, y):
        cols = slice(c * PROJ_CHUNK, (c + 1) * PROJ_CHUNK)
        if c in raw_chunks:
            k = raw_chunks.index(c)
            oraw_ref[:, k * PROJ_CHUNK:(k + 1) * PROJ_CHUNK] = y
        if norm_chunks[c]:
            y2 = y * y
            hi = y2.astype(BF16)
            lo = (y2 - hi.astype(F32)).astype(BF16)
            ssq = jnp.dot(hi, bd, preferred_element_type=F32) + jnp.dot(lo, bd, preferred_element_type=F32)
            y = y * lax.rsqrt(ssq * (1.0 / HEAD_DIM) + RMS_EPS) * cg_ref[:, cols]
        o_ref[:, cols] = y.astype(o_ref.dtype)

    prev = None
    for c in range(len(norm_chunks)):
        src = src_chunks[c] * PROJ_CHUNK
        y = jnp.dot(xn, w_ref[:, src:src + PROJ_CHUNK], preferred_element_type=F32)
        if prev is not None:
            finish(c - 1, prev)
        prev = y
    finish(len(norm_chunks) - 1, prev)
    if with_extra:
        ox_ref[...] = jnp.dot(xn, wx_ref[...], preferred_element_type=F32)


def _project(x, g, w, col_gain, norm_chunks, w_extra=None, raw_chunks=(), src_chunks=None):
    if src_chunks is None:
        src_chunks = range(len(norm_chunks))
    T, D = x.shape
    N = w.shape[1]
    blk = np.kron(np.eye(PROJ_CHUNK // HEAD_DIM), np.ones((HEAD_DIM, HEAD_DIM))).astype(np.float32)
    row_in = lambda n: pl.BlockSpec((ROW_TILE, n), lambda i: (i, 0))
    in_specs = [row_in(D), _resident((1, D)), _resident((D, N)), _resident((1, N)),
                _resident((PROJ_CHUNK, PROJ_CHUNK))]
    args = [x, g.reshape(1, D), w, col_gain.reshape(1, N), jnp.asarray(blk, BF16)]
    out_specs = [row_in(N)]
    out_shape = [jax.ShapeDtypeStruct((T, N), BF16)]
    if w_extra is not None:
        nx = w_extra.shape[1]
        in_specs.append(_resident((D, nx)))
        args.append(w_extra)
        out_specs += [row_in(nx), row_in(len(raw_chunks) * PROJ_CHUNK)]
        out_shape += [jax.ShapeDtypeStruct((T, nx), F32), jax.ShapeDtypeStruct((T, len(raw_chunks) * PROJ_CHUNK), F32)]
    return pl.pallas_call(
        functools.partial(_proj_kernel, norm_chunks=tuple(norm_chunks), src_chunks=tuple(src_chunks),
                          raw_chunks=tuple(raw_chunks),
                          with_extra=w_extra is not None),
        grid=(T // ROW_TILE,),
        in_specs=in_specs,
        out_specs=out_specs,
        out_shape=out_shape,
        compiler_params=_cparams(1),
        name="proj",
    )(*args)


def _sigmoid(z):
    return 1.0 / (1.0 + jnp.exp(-z))


def _post_kernel(o_ref, h_ref, wout_ref, gf_ref, wg_ref, wu_ref, wd_ref, gp_ref, wpg_ref, p_ref, wpp_ref, out_ref):
    halves = [slice(r * ROW_TILE // POST_SPLIT, (r + 1) * ROW_TILE // POST_SPLIT) for r in range(POST_SPLIT)]
    h1 = [h_ref[r, :] + jnp.dot(o_ref[r, :], wout_ref[...], preferred_element_type=F32) for r in halves]
    xn = [_rms_rows(x, gf_ref[...]).astype(BF16) for x in h1]
    a = [jnp.dot(x, wg_ref[...], preferred_element_type=F32) for x in xn]
    u = [jnp.dot(x, wu_ref[...], preferred_element_type=F32) for x in xn]
    act = [(ai * _sigmoid(ai) * ui).astype(BF16) for ai, ui in zip(a, u)]
    h2 = [x + jnp.dot(t, wd_ref[...], preferred_element_type=F32) for x, t in zip(h1, act)]
    hn = [_rms_rows(x, gp_ref[...]).astype(BF16) for x in h2]
    gate = [_sigmoid(jnp.dot(x, wpg_ref[...], preferred_element_type=F32)) for x in hn]
    for r, x, g in zip(halves, h2, gate):
        out_ref[r, :] = x + g * jnp.dot(p_ref[r, :].astype(BF16), wpp_ref[...], preferred_element_type=F32)


def _post_attention(o, h, w_out, g_ffn, wg, wu, wd, g_ple, w_pgate, p, w_pproj):
    T, D = h.shape
    Fh = wg.shape[1]
    Pd = p.shape[1]
    row_in = lambda n: pl.BlockSpec((ROW_TILE, n), lambda i: (i, 0))
    return pl.pallas_call(
        _post_kernel,
        grid=(T // ROW_TILE,),
        in_specs=[row_in(D), row_in(D), _resident((D, D)), _resident((1, D)), _resident((D, Fh)),
                  _resident((D, Fh)), _resident((Fh, D)), _resident((1, D)), _resident((D, D)),
                  row_in(Pd), _resident((Pd, D))],
        out_specs=row_in(D),
        out_shape=jax.ShapeDtypeStruct((T, D), F32),
        compiler_params=pltpu.CompilerParams(dimension_semantics=("arbitrary",), vmem_limit_bytes=POST_VMEM_LIMIT),
        name="post_attention",
    )(o, h, w_out, g_ffn.reshape(1, D), wg, wu, wd, g_ple.reshape(1, D), w_pgate, p, w_pproj)


def _fill_strips(strip_ref, r_ref):
    for h in range(2):
        for o in range(r_ref.shape[2]):
            x = jnp.broadcast_to(r_ref[0, h, o:o + 1, :], (TQ, 2 * TQ))
            strip_ref[h, o * TQ:(o + 1) * TQ, :] = pltpu.roll(x, 0, 1, stride=1, stride_axis=0)[:, :TQ]


def _tile_groups(nblk, span):
    cnt = [min(span, nblk - j) for j in range(nblk)]
    off = [0]
    for c in cnt:
        off.append(off[-1] + c * TQ)
    return cnt, off


def _tile_rows(nblk, span):
    return _tile_groups(nblk, span)[1][-1]


def _pair_attention(q_rows, k_rows, v_rows, strip_ref, s_ref, p_ref, o_ref, nblk, span, skew=1):
    cnt, off = _tile_groups(nblk, span)
    together = s_ref.shape[0] >= 2 * off[-1] and p_ref.shape[2] >= 2 * span * TQ
    s0 = [0, off[-1] if together else 0]
    p0 = [0, span * TQ if together else 0]

    def scores(h, j):
        n = cnt[j] * TQ
        s = lax.dot_general(q_rows(h, j * TQ, j * TQ + n), k_rows(h, j), NT_DIMS, preferred_element_type=F32)
        s_ref[s0[h] + off[j]:s0[h] + off[j] + n, :] = s + strip_ref[h, 0:n, :]

    def probs(h, i):
        j0 = max(0, i - span + 1)
        rows = [s0[h] + off[j] + (i - j) * TQ for j in range(j0, i + 1)]
        mx = s_ref[rows[0]:rows[0] + TQ, :]
        for r in rows[1:]:
            mx = jnp.maximum(mx, s_ref[r:r + TQ, :])
        m = jnp.max(mx, axis=1, keepdims=True)
        for t, r in enumerate(rows):
            p_ref[i, :, p0[h] + t * TQ:p0[h] + (t + 1) * TQ] = jnp.exp(s_ref[r:r + TQ, :] - m).astype(BF16)

    def values(h, i):
        j0 = max(0, i - span + 1)
        out = jnp.dot(p_ref[i, :, p0[h]:p0[h] + (i + 1 - j0) * TQ], v_rows(j0 * TQ, (i + 1) * TQ),
                      preferred_element_type=F32)
        o_ref[h, i * TQ:(i + 1) * TQ, :] = out[:, :LANES] / out[:, LANES:]

    for heads in ([(0, 1)] if together else [(0,), (1,)]):
        for j in range(nblk + skew):
            for h in heads:
                if j < nblk:
                    scores(h, j)
            for h in heads:
                if j >= skew:
                    values(h, j - skew)
            for h in heads:
                if j < nblk:
                    probs(h, j)


def _merge_heads(o_ref):
    return jnp.where(_lane_lo((o_ref.shape[1], LANES)), o_ref[0], o_ref[1])


def _split_heads(q):
    lo = _lane_lo(q.shape)
    zero = jnp.zeros_like(q)
    return jnp.where(lo, q, zero), jnp.where(lo, zero, q)


def _fill_value_ones(vaug_ref):
    vaug_ref[:, LANES:] = jnp.ones((vaug_ref.shape[0], LANES), vaug_ref.dtype)


def _moba_block_means(k_ref):
    shape = (2 * MOBA_NBLK, LANES)
    lo = _lane_lo(shape)
    row = lax.broadcasted_iota(jnp.int32, shape, 0)
    kmt = jnp.zeros(shape, F32)
    for n in range(MOBA_NBLK):
        mean = jnp.mean(k_ref[0, n * MOBA_BLOCK:(n + 1) * MOBA_BLOCK, :].astype(F32), axis=0, keepdims=True)
        kmt = jnp.where(row == n, jnp.where(lo, mean, 0.0), kmt)
        kmt = jnp.where(row == MOBA_NBLK + n, jnp.where(lo, 0.0, mean), kmt)
    return kmt


def _moba_penalty(q, kmt):
    nb = MOBA_NBLK
    S = q.shape[0]
    hi = kmt.astype(BF16)
    rest = kmt - hi.astype(F32)
    mid = rest.astype(BF16)
    lo = (rest - mid.astype(F32)).astype(BF16)
    g3 = lax.dot_general(jnp.concatenate([hi, mid, lo], axis=0), q, NT_DIMS, preferred_element_type=F32)
    g = g3[0:2 * nb] + g3[2 * nb:4 * nb] + g3[4 * nb:6 * nb]
    row = lax.broadcasted_iota(jnp.int32, g.shape, 0)
    n = row & (nb - 1)
    own = lax.shift_right_logical(lax.broadcasted_iota(jnp.int32, g.shape, 1), int(math.log2(MOBA_BLOCK)))
    rank = jnp.zeros(g.shape, F32)
    for m in range(nb - 1):
        gm = jnp.where(row < nb, g[m:m + 1, :], g[nb + m:nb + m + 1, :])
        tie = jnp.where(n > m, 1.0, 0.0)
        beats = jnp.where(gm > g, 1.0, jnp.where(gm == g, tie, 0.0))
        rank = rank + jnp.where(own > m, beats, 0.0)
    keep = jnp.where(n < own, jnp.where(rank < MOBA_TOPK, 1.0, 0.0), jnp.where(n == own, 1.0, 0.0))
    keep = jnp.concatenate([keep, jnp.zeros((LANES - 2 * nb, S), F32)], axis=0).T
    lane = lax.broadcasted_iota(jnp.int32, keep.shape, 1)
    return jnp.where(lane < 2 * nb, (keep - 1.0) * (-NEG), 0.0)


def _attn0_kernel(q_ref, k_ref, v_ref, r_ref, e_ref, o_ref, strip_ref, qaug_ref, kaug_ref, vaug_ref,
                  s_ref, p_ref, oh_ref):
    p = pl.program_id(0)
    b = pl.program_id(1)
    S = q_ref.shape[1]
    nblk = S // TQ

    @pl.when(b == 0)
    def _():
        _fill_strips(strip_ref, r_ref)

    @pl.when(jnp.logical_and(p == 0, b == 0))
    def _():
        for h in range(2):
            kaug_ref[h, :, LANES:] = e_ref[h]
        _fill_value_ones(vaug_ref)

    k = k_ref[0]
    for h, qh in enumerate(_split_heads(q_ref[0])):
        kaug_ref[h, :, :LANES] = k
        qaug_ref[h, :, :LANES] = qh
    vaug_ref[:, :LANES] = v_ref[0]
    is_moba = p < MOBA_HEADS // 2

    @pl.when(is_moba)
    def _():
        pen = _moba_penalty(q_ref[0], _moba_block_means(k_ref)).astype(BF16)
        for h in range(2):
            qaug_ref[h, :, LANES:] = pen

    @pl.when(jnp.logical_not(is_moba))
    def _():
        for h in range(2):
            qaug_ref[h, :, LANES:] = jnp.zeros((S, LANES), BF16)
    _pair_attention(lambda h, r0, r1: qaug_ref[h, r0:r1, :],
                    lambda h, j: kaug_ref[h, j * TQ:(j + 1) * TQ, :],
                    lambda r0, r1: vaug_ref[r0:r1, :],
                    strip_ref, s_ref, p_ref, oh_ref, nblk, nblk)
    o_ref[0] = _merge_heads(oh_ref).astype(o_ref.dtype)


def _attn_scratch(S, span, narrow_span=0):
    nblk = S // TQ
    s_rows = max(_tile_rows(nblk, span), 2 * _tile_rows(nblk, narrow_span))
    return [pltpu.VMEM((2, S, TQ), F32), pltpu.VMEM((2, S, 2 * LANES), BF16), pltpu.VMEM((2, S, 2 * LANES), BF16),
            pltpu.VMEM((S, 2 * LANES), BF16), pltpu.VMEM((s_rows, TQ), F32),
            pltpu.VMEM((nblk, TQ, span * TQ), BF16), pltpu.VMEM((2, S, LANES), F32)]


def _attn0(qkv, r_tab, e_onehot):
    B, S, _ = qkv.shape
    nblk = S // TQ
    return pl.pallas_call(
        _attn0_kernel,
        grid=(N_PAIRS, B),
        in_specs=[
            pl.BlockSpec((1, S, LANES), lambda p, b: (b, 0, p)),
            pl.BlockSpec((1, S, LANES), lambda p, b: (b, 0, N_PAIRS + p)),
            pl.BlockSpec((1, S, LANES), lambda p, b: (b, 0, 2 * N_PAIRS + p)),
            pl.BlockSpec((1, 2, nblk, 2 * TQ), lambda p, b: (p, 0, 0, 0)),
            pl.BlockSpec((2, S, LANES), lambda p, b: (0, 0, 0), pipeline_mode=pl.Buffered(1)),
        ],
        out_specs=pl.BlockSpec((1, S, LANES), lambda p, b: (b, 0, p)),
        out_shape=jax.ShapeDtypeStruct((B, S, D_MODEL), BF16),
        scratch_shapes=_attn_scratch(S, nblk),
        compiler_params=pltpu.CompilerParams(dimension_semantics=("arbitrary",) * 2, vmem_limit_bytes=ATTN_VMEM_LIMIT),
        name="attn_moba_dilated",
    )(qkv, qkv, qkv, r_tab, e_onehot)


def _compress_kernel(x_ref, pos_ref, w1_ref, b1_ref, w2_ref, b2_ref, gain_ref, o_ref, *, normed):
    n_chunk = x_ref.shape[1] // NSA_CMP_STRIDE
    first = second = None
    for a in range(NSA_CMP_STRIDE):
        t = x_ref[0, pl.ds(a, n_chunk, stride=NSA_CMP_STRIDE), :]
        fa = jnp.dot((t + pos_ref[a:a + 1, :]).astype(BF16), w1_ref[a], preferred_element_type=F32)
        sa = jnp.dot((t + pos_ref[NSA_CMP_STRIDE + a:NSA_CMP_STRIDE + a + 1, :]).astype(BF16),
                     w1_ref[NSA_CMP_STRIDE + a], preferred_element_type=F32)
        first = fa if first is None else first + fa
        second = sa if second is None else second + sa
    hid = first + pltpu.roll(second, n_chunk - 1, 0) + b1_ref[...]
    cdf = 0.5 * (1.0 + jnp.tanh(math.sqrt(2.0 / math.pi) * (hid + 0.044715 * (hid * hid * hid))))
    y = jnp.dot((hid * cdf).astype(BF16), w2_ref[...], preferred_element_type=F32) + b2_ref[...]
    if normed:
        lo = _lane_lo(y.shape)
        y2 = y * y
        s_lo = jnp.sum(jnp.where(lo, y2, 0.0), axis=1, keepdims=True)
        s_hi = jnp.sum(jnp.where(lo, 0.0, y2), axis=1, keepdims=True)
        ms = jnp.where(lo, s_lo, s_hi) * (1.0 / HEAD_DIM)
        y = y * lax.rsqrt(ms + RMS_EPS) * gain_ref[...]
    o_ref[0, 0] = y.astype(o_ref.dtype)


def _pair_block_diag(w):
    z = jnp.zeros_like(w)
    return jnp.concatenate([jnp.concatenate([w, z], axis=-1), jnp.concatenate([z, w], axis=-1)], axis=-2)


def _compress(raw, col_block0, pos, w1, b1, w2, b2, gain, normed):
    B, S, _ = raw.shape
    hid = w1.shape[1]
    n_pos = pos.shape[0]
    w1bd = _pair_block_diag(w1.reshape(n_pos, HEAD_DIM, hid)).astype(BF16)
    return pl.pallas_call(
        functools.partial(_compress_kernel, normed=normed),
        grid=(B, 2),
        in_specs=[
            pl.BlockSpec((1, S, LANES), lambda b, m: (b, 0, col_block0 + m)),
            pl.BlockSpec((n_pos, LANES), lambda b, m: (0, 0)),
            pl.BlockSpec((n_pos, LANES, 2 * hid), lambda b, m: (0, 0, 0)),
            pl.BlockSpec((1, 2 * hid), lambda b, m: (0, 0)),
            pl.BlockSpec((2 * hid, LANES), lambda b, m: (0, 0)),
            pl.BlockSpec((1, LANES), lambda b, m: (0, 0)),
            pl.BlockSpec((1, LANES), lambda b, m: (0, 0)),
        ],
        out_specs=pl.BlockSpec((1, 1, S // NSA_CMP_STRIDE, LANES), lambda b, m: (b, m, 0, 0)),
        out_shape=jax.ShapeDtypeStruct((B, 2, S // NSA_CMP_STRIDE, LANES), BF16),
        compiler_params=_cparams(2),
        name="nsa_compress",
    )(raw, jnp.tile(pos, (1, 2)), w1bd, jnp.tile(b1, 2).reshape(1, 2 * hid), _pair_block_diag(w2).astype(BF16),
      jnp.tile(b2, 2).reshape(1, LANES), jnp.tile(gain, 2).reshape(1, LANES))


def _cmp_kernel(q_ref, kc_ref, vc_ref, bias_ref, ovt_ref, o_ref, pen_ref):
    kc = kc_ref[0, 0]
    vc = vc_ref[0, 0]
    S = q_ref.shape[1]
    n_sel = ovt_ref.shape[0]
    lo = _lane_lo((CMP_ROWS, LANES))
    blk = lax.broadcasted_iota(jnp.int32, (n_sel, CMP_ROWS), 0)
    for c in range(S // CMP_ROWS):
        rows = slice(c * CMP_ROWS, (c + 1) * CMP_ROWS)
        psum = [jnp.zeros((CMP_ROWS, LANES), F32), jnp.zeros((CMP_ROWS, LANES), F32)]
        for r in range(4):
            heads = _split_heads(q_ref[0, rows, r * LANES:(r + 1) * LANES])
            outs = []
            for h in range(2):
                s = lax.dot_general(heads[h], kc, NT_DIMS, preferred_element_type=F32) + bias_ref[0, 2 * r + h, rows, :]
                m = jnp.max(s, axis=1, keepdims=True)
                e = jnp.exp(s - m)
                l = jnp.sum(e, axis=1, keepdims=True)
                pr = e * jnp.where(m > 0.5 * NEG, 1.0 / l, 0.0)
                psum[h] = psum[h] + pr
                outs.append(jnp.dot(pr.astype(BF16), vc, preferred_element_type=F32))
            o_ref[0, rows, r * LANES:(r + 1) * LANES] = jnp.where(lo, outs[0], outs[1])

        t = c * CMP_ROWS + lax.broadcasted_iota(jnp.int32, (n_sel, CMP_ROWS), 1)
        cur = lax.shift_right_logical(t, int(math.log2(NSA_SEL_BLOCK)))
        keeps = []
        for h in range(2):
            imp = lax.dot_general(ovt_ref[...], psum[h], NT_DIMS,
                                  precision=lax.Precision.HIGHEST, preferred_element_type=F32)
            forced = jnp.where(blk == 0, 1.0, jnp.where(blk == cur, 1.0, jnp.where(blk == cur - 1, 1.0, 0.0)))
            imp = jnp.where(blk <= cur, imp + forced * NSA_FORCE, -jnp.inf)
            groups = [imp[g * SUBLANES:(g + 1) * SUBLANES, :] for g in range(n_sel // SUBLANES)]
            ranks = [jnp.zeros(g.shape, F32) for g in groups]
            for m in range(n_sel):
                im = imp[m:m + 1, :]
                for g, sub in enumerate(groups):
                    ge = jnp.where(im >= sub, 1.0, 0.0)
                    gt = jnp.where(im > sub, 1.0, 0.0)
                    if g * SUBLANES > m:
                        inc = ge
                    elif (g + 1) * SUBLANES - 1 <= m:
                        inc = gt
                    else:
                        inc = jnp.where(lax.broadcasted_iota(jnp.int32, sub.shape, 0) + g * SUBLANES > m, ge, gt)
                    ranks[g] = ranks[g] + inc
            keeps.append(jnp.where(jnp.concatenate(ranks, axis=0) < NSA_SEL_TOPN, 1.0, 0.0))
        keep = jnp.concatenate(keeps + [jnp.zeros((LANES - 2 * n_sel, CMP_ROWS), F32)], axis=0).T
        lane = lax.broadcasted_iota(jnp.int32, keep.shape, 1)
        pen_ref[0, 0, rows, :] = jnp.where(lane < 2 * n_sel, (keep - 1.0) * (-NEG), 0.0).astype(pen_ref.dtype)


def _cmp_attention(qkv, kc, vc, bias, ovt):
    B, S, _ = qkv.shape
    n_sel = ovt.shape[0]
    return pl.pallas_call(
        _cmp_kernel,
        grid=(2, B),
        in_specs=[
            pl.BlockSpec((1, S, 4 * LANES), lambda m, b: (b, 0, m)),
            pl.BlockSpec((1, 1, LANES, LANES), lambda m, b: (b, m, 0, 0)),
            pl.BlockSpec((1, 1, LANES, LANES), lambda m, b: (b, m, 0, 0)),
            pl.BlockSpec((1, 8, S, LANES), lambda m, b: (m, 0, 0, 0), pipeline_mode=pl.Buffered(1)),
            pl.BlockSpec((n_sel, LANES), lambda m, b: (0, 0)),
        ],
        out_specs=[
            pl.BlockSpec((1, S, 4 * LANES), lambda m, b: (b, 0, m)),
            pl.BlockSpec((1, 1, S, LANES), lambda m, b: (b, m, 0, 0)),
        ],
        out_shape=[jax.ShapeDtypeStruct((B, S, D_MODEL), F32),
                   jax.ShapeDtypeStruct((B, 2, S, LANES), BF16)],
        compiler_params=_cparams(2),
        name="nsa_compressed_select",
    )(qkv, kc, vc, bias, ovt)


def _attn1_kernel(q_ref, ks_ref, vs_ref, kw_ref, vw_ref, rs_ref, rw_ref, e_ref, pen_ref, ocmp_ref, graw_ref,
                  gsel_ref, o_ref, strip_s_ref, qaug_ref, kaug_ref, vsaug_ref, s_ref, p_ref, oslc_ref,
                  strip_w_ref, vwaug_ref, owin_ref):
    p = pl.program_id(0)
    b = pl.program_id(1)
    S = q_ref.shape[1]
    nblk = S // TQ

    @pl.when(b == 0)
    def _():
        _fill_strips(strip_s_ref, rs_ref)
        _fill_strips(strip_w_ref, rw_ref)

    @pl.when(jnp.logical_and(p == 0, b == 0))
    def _():
        for h in range(2):
            kaug_ref[h, :, LANES:] = e_ref[h]
        _fill_value_ones(vsaug_ref)
        _fill_value_ones(vwaug_ref)

    ks = ks_ref[0]
    pen = pen_ref[0, 0]
    for h, qh in enumerate(_split_heads(q_ref[0])):
        kaug_ref[h, :, :LANES] = ks
        qaug_ref[h, :, :LANES] = qh
        qaug_ref[h, :, LANES:] = pen
    vsaug_ref[:, :LANES] = vs_ref[0]
    vwaug_ref[:, :LANES] = vw_ref[0]
    _pair_attention(lambda h, r0, r1: qaug_ref[h, r0:r1, :],
                    lambda h, j: kaug_ref[h, j * TQ:(j + 1) * TQ, :],
                    lambda r0, r1: vsaug_ref[r0:r1, :],
                    strip_s_ref, s_ref, p_ref, oslc_ref, nblk, nblk)
    _pair_attention(lambda h, r0, r1: qaug_ref[h, r0:r1, :LANES],
                    lambda h, j: kw_ref[0, j * TQ:(j + 1) * TQ, :],
                    lambda r0, r1: vwaug_ref[r0:r1, :],
                    strip_w_ref, s_ref, p_ref, owin_ref, nblk, strip_w_ref.shape[1] // TQ, skew=2)
    sig = _sigmoid(graw_ref[0])
    hi = sig.astype(BF16)
    lo = (sig - hi.astype(F32)).astype(BF16)
    gates = [jnp.dot(hi, gsel_ref[0, br], preferred_element_type=F32)
             + jnp.dot(lo, gsel_ref[0, br], preferred_element_type=F32) for br in range(3)]
    out = gates[0] * ocmp_ref[0] + gates[1] * _merge_heads(oslc_ref) + gates[2] * _merge_heads(owin_ref)
    o_ref[0] = out.astype(o_ref.dtype)


def _attn1(qkv, r_slc, r_win, e_onehot, pen, ocmp, graw, gsel):
    B, S, _ = qkv.shape
    kv0 = D_MODEL // LANES + 4
    n_win = r_win.shape[2]
    return pl.pallas_call(
        _attn1_kernel,
        grid=(N_PAIRS, B),
        in_specs=[
            pl.BlockSpec((1, S, LANES), lambda p, b: (b, 0, p)),
            pl.BlockSpec((1, S, LANES), lambda p, b: (b, 0, kv0 + p // 4)),
            pl.BlockSpec((1, S, LANES), lambda p, b: (b, 0, kv0 + 2 + p // 4)),
            pl.BlockSpec((1, S, LANES), lambda p, b: (b, 0, kv0 + 4 + p // 4)),
            pl.BlockSpec((1, S, LANES), lambda p, b: (b, 0, kv0 + 6 + p // 4)),
            pl.BlockSpec((1, 2, r_slc.shape[2], 2 * TQ), lambda p, b: (p, 0, 0, 0)),
            pl.BlockSpec((1, 2, n_win, 2 * TQ), lambda p, b: (p, 0, 0, 0)),
            pl.BlockSpec((2, S, LANES), lambda p, b: (0, 0, 0), pipeline_mode=pl.Buffered(1)),
            pl.BlockSpec((1, 1, S, LANES), lambda p, b: (b, p // 4, 0, 0)),
            pl.BlockSpec((1, S, LANES), lambda p, b: (b, 0, p)),
            pl.BlockSpec((1, S, LANES), lambda p, b: (b, 0, 0)),
            pl.BlockSpec((1, 3, LANES, LANES), lambda p, b: (p, 0, 0, 0)),
        ],
        out_specs=pl.BlockSpec((1, S, LANES), lambda p, b: (b, 0, p)),
        out_shape=jax.ShapeDtypeStruct((B, S, D_MODEL), BF16),
        scratch_shapes=_attn_scratch(S, S // TQ, n_win) + [pltpu.VMEM((2, n_win * TQ, TQ), F32),
                                                    pltpu.VMEM((S, 2 * LANES), BF16), pltpu.VMEM((2, S, LANES), F32)],
        compiler_params=pltpu.CompilerParams(dimension_semantics=("arbitrary",) * 2, vmem_limit_bytes=ATTN_VMEM_LIMIT),
        name="attn_nsa",
    )(qkv, qkv, qkv, qkv, qkv, r_slc, r_win, e_onehot, pen, ocmp, graw, gsel)


def _bucket(dist):
    n = np.maximum(dist, 0)
    exact = REL_BUCKETS // 2
    nf = np.maximum(n, 1).astype(np.float64)
    large = exact + (np.log(nf / exact) / math.log(REL_MAX_DIST / exact) * (REL_BUCKETS - exact)).astype(np.int64)
    return np.where(n < exact, n, np.minimum(large, REL_BUCKETS - 1))


def _strip_rows(tab, n_off, extra_of_dist):
    u = np.arange(2 * TQ)
    d = np.arange(n_off)[:, None] * TQ - np.where(u < TQ, u, u - 2 * TQ)[None, :]
    onehot = (_bucket(d)[None] == np.arange(REL_BUCKETS)[:, None, None]).astype(np.float32)
    vals = jnp.einsum("hb,bou->hou", tab, jnp.asarray(onehot), precision=lax.Precision.HIGHEST) + extra_of_dist(d)
    return vals.reshape(tab.shape[0] // 2, 2, n_off, 2 * TQ)


def _causal_mask(d):
    return np.where(d >= 0, 0.0, NEG).astype(np.float32)


def _window_mask(d):
    return np.where((d >= 0) & (d < NSA_WINDOW), 0.0, NEG).astype(np.float32)


def _dilation_log_count(d):
    c = ((d >= 0) & (d <= 128)).astype(np.float64)
    c += ((d >= 0) & (d % 4 == 0) & (d <= 512))
    c += ((d >= 0) & (d % 16 == 0) & (d <= 2048))
    return np.where(c > 0, np.log(np.maximum(c, 1.0)), NEG).astype(np.float32)


def _one_hot_blocks(S, block, per_head):
    e = np.zeros((2, S, LANES), np.float32)
    key = np.arange(S)
    for h in range(2):
        e[h, key, h * per_head + key // block] = 1.0
    return jnp.asarray(e, BF16)


_NSA_HEAD_ORDER = np.array([8 * (p // 4) + (p % 4) + 4 * h for p in range(N_PAIRS) for h in range(2)])


def _head_cols(heads):
    return (np.asarray(heads)[:, None] * HEAD_DIM + np.arange(HEAD_DIM)[None, :]).reshape(-1)


def _mixer_ab(h, g_mix, w_in, qn_a, kn_a, qn_b, kn_b, rel_bias, B, S):
    wa = MOBA_HEADS * HEAD_DIM
    ones = jnp.ones((wa,), F32)
    nh = MOBA_HEADS
    gain = jnp.concatenate([jnp.tile(qn_a, nh) * ATTN_SCALE, jnp.tile(qn_b, nh) * ATTN_SCALE,
                            jnp.tile(kn_a, nh), jnp.tile(kn_b, nh), ones, ones])
    per_sec = wa // PROJ_CHUNK
    src = [sec * per_sec + c for sec in (0, 3, 1, 4, 2, 5) for c in range(per_sec)]
    (qkv,) = _project(h, g_mix, w_in.astype(BF16), gain, [True] * (4 * per_sec) + [False] * (2 * per_sec),
                      src_chunks=src)
    qkv = qkv.reshape(B, S, 3 * D_MODEL)

    tab = rel_bias.T
    r_tab = jnp.concatenate([_strip_rows(tab[:MOBA_HEADS], S // TQ, _causal_mask),
                             _strip_rows(tab[MOBA_HEADS:], S // TQ, _dilation_log_count)])
    o = _attn0(qkv, r_tab, _one_hot_blocks(S, MOBA_BLOCK, S // MOBA_BLOCK))
    return o.reshape(B * S, D_MODEL)


def _mixer_nsa(h, g_mix, w_in, qn, kn_c, kn_s, kn_w, cmp_k, cmp_v, rel_bias, B, S):
    qw = N_HEADS * HEAD_DIM
    kvw = NSA_GROUPS * HEAD_DIM
    order = _NSA_HEAD_ORDER
    w_main = jnp.concatenate([w_in[:, _head_cols(order)], w_in[:, qw:qw + 6 * kvw]], axis=1).astype(BF16)
    ones = jnp.ones((kvw,), F32)
    gain = jnp.concatenate([jnp.tile(qn, N_HEADS) * ATTN_SCALE, ones, ones,
                            jnp.tile(kn_s, NSA_GROUPS), ones, jnp.tile(kn_w, NSA_GROUPS), ones])
    assert kvw == PROJ_CHUNK
    norm_chunks = [True] * (qw // PROJ_CHUNK) + [False, False, True, False, True, False]

    gate_stride = LANES // (2 * N_PAIRS)
    gcols = np.zeros((LANES,), np.int64)
    gused = np.zeros((LANES,), np.float32)
    gsel = np.zeros((N_PAIRS, 3, LANES, LANES), np.float32)
    for p in range(N_PAIRS):
        for br in range(3):
            for hh in range(2):
                c = gate_stride * p + 2 * br + hh
                gcols[c] = qw + 6 * kvw + 3 * order[2 * p + hh] + br
                gused[c] = 1.0
                gsel[p, br, c, hh * HEAD_DIM:(hh + 1) * HEAD_DIM] = 1.0
    w_gate = (w_in[:, gcols] * gused).astype(BF16)
    kc_chunk = qw // PROJ_CHUNK
    qkv, graw, raw = _project(h, g_mix, w_main, gain, norm_chunks, w_gate, raw_chunks=(kc_chunk, kc_chunk + 1))
    qkv = qkv.reshape(B, S, qw + 6 * kvw)
    graw = graw.reshape(B, S, LANES)
    raw = raw.reshape(B, S, 2 * kvw)
    kc = _compress(raw, 0, *cmp_k, kn_c, True)
    vc = _compress(raw, kvw // LANES, *cmp_v, kn_c, False)

    tab = rel_bias.T[order]
    n_cmp = (S - NSA_CMP_LEN) // NSA_CMP_STRIDE + 1
    t_pos = np.arange(S)[:, None]
    c_idx = np.arange(LANES)[None, :]
    dc = t_pos - (c_idx * NSA_CMP_STRIDE + NSA_CMP_LEN - 1)
    cmp_mask = np.where((dc >= 0) & (c_idx < n_cmp), 0.0, NEG).astype(np.float32)
    onehot = jnp.asarray(_bucket(dc).astype(np.int8))[None] == jnp.arange(REL_BUCKETS, dtype=jnp.int8)[:, None, None]
    bias_c = jnp.einsum("hb,bsc->hsc", tab, onehot.astype(F32), precision=lax.Precision.HIGHEST)
    bias_c = (bias_c + cmp_mask).reshape(2, 8, S, LANES)
    n_sel = S // NSA_SEL_BLOCK
    cstart = np.arange(LANES) * NSA_CMP_STRIDE
    sstart = np.arange(n_sel) * NSA_SEL_BLOCK
    ovt = np.maximum(np.minimum(cstart[None, :] + NSA_CMP_LEN, sstart[:, None] + NSA_SEL_BLOCK)
                     - np.maximum(cstart[None, :], sstart[:, None]), 0).astype(np.float32)
    ovt[:, n_cmp:] = 0.0
    ocmp, pen = _cmp_attention(qkv, kc, vc, bias_c, jnp.asarray(ovt))

    r_slc = _strip_rows(tab, S // TQ, _causal_mask)
    r_win = _strip_rows(tab, NSA_WINDOW // TQ + 1, _window_mask)
    o = _attn1(qkv, r_slc, r_win, _one_hot_blocks(S, NSA_SEL_BLOCK, n_sel), pen, ocmp, graw, jnp.asarray(gsel, BF16))
    return o.reshape(B * S, D_MODEL)


def kernel(x, p, rel_bias, norm_mix, norm_ffn, norm_ple, w_ffn_gate, w_ffn_up, w_ffn_down, w_ple_proj, w_ple_gate, w_in_ab, w_out_ab, qn_moba, kn_moba, qn_dil, kn_dil, w_in_nsa, w_out_nsa, qn_nsa, kn_cmp, kn_slc, kn_win, cmp_k_pos, cmp_k_w1, cmp_k_b1, cmp_k_w2, cmp_k_b2, cmp_v_pos, cmp_v_w1, cmp_v_b1, cmp_v_w2, cmp_v_b2):
    B, S, D = x.shape
    depth = p.shape[0]
    assert D == D_MODEL and S == MOBA_NBLK * MOBA_BLOCK and (B * S) % ROW_TILE == 0, (B, S, D)
    h = x.reshape(B * S, D)
    for i in range(depth):
        e = i // 2
        if i % 2 == 0:
            o = _mixer_ab(h, norm_mix[i], w_in_ab[e], qn_moba[e], kn_moba[e], qn_dil[e], kn_dil[e], rel_bias, B, S)
            w_out = w_out_ab[e]
        else:
            cmp_k = (cmp_k_pos[e], cmp_k_w1[e], cmp_k_b1[e], cmp_k_w2[e], cmp_k_b2[e])
            cmp_v = (cmp_v_pos[e], cmp_v_w1[e], cmp_v_b1[e], cmp_v_w2[e], cmp_v_b2[e])
            o = _mixer_nsa(h, norm_mix[i], w_in_nsa[e], qn_nsa[e], kn_cmp[e], kn_slc[e], kn_win[e],
                           cmp_k, cmp_v, rel_bias, B, S)
            w_out = w_out_nsa[e][_head_cols(_NSA_HEAD_ORDER), :]
        h = _post_attention(o, h, w_out.astype(BF16), norm_ffn[i], w_ffn_gate[i].astype(BF16),
                            w_ffn_up[i].astype(BF16), w_ffn_down[i].astype(BF16), norm_ple[i],
                            w_ple_gate[i].astype(BF16), p[i].reshape(B * S, -1), w_ple_proj[i].astype(BF16))
    return h.reshape(B, S, D)
```

```python
import functools
import math

import numpy as np
import jax
import jax.numpy as jnp
from jax import lax
from jax.experimental import pallas as pl
from jax.experimental.pallas import tpu as pltpu

F32 = jnp.float32
BF16 = jnp.bfloat16

D_MODEL = 1024
HEAD_DIM = 64
N_HEADS = 16
N_PAIRS = N_HEADS // 2
MOBA_HEADS = 8
MOBA_BLOCK = 256
MOBA_TOPK = 3
MOBA_NBLK = 8
NSA_GROUPS = 4
NSA_CMP_LEN = 32
NSA_CMP_STRIDE = 16
NSA_SEL_BLOCK = 64
NSA_SEL_TOPN = 16
NSA_WINDOW = 512
NSA_FORCE = 1.0e6
REL_BUCKETS = 32
REL_MAX_DIST = 2048
RMS_EPS = 1e-6
ATTN_SCALE = HEAD_DIM ** -0.5

LANES = 128
SUBLANES = 8
TQ = 256
NEG = -1.0e30
ROW_TILE = 512
PROJ_CHUNK = 256
POST_SPLIT = 2
GATE_STRIDE = LANES // N_HEADS
CMP_ROWS = 2048
VMEM_LIMIT = 48 * 1024 * 1024
POST_VMEM_LIMIT = 56 * 1024 * 1024
ATTN_VMEM_LIMIT = 56 * 1024 * 1024

NT_DIMS = (((1,), (1,)), ((), ()))


def _cparams(n_axes):
    return pltpu.CompilerParams(dimension_semantics=("arbitrary",) * n_axes,
                                vmem_limit_bytes=VMEM_LIMIT)


def _rms_rows(x, g):
    ms = jnp.mean(x * x, axis=-1, keepdims=True)
    return x * lax.rsqrt(ms + RMS_EPS) * g


def _lane_lo(shape):
    return lax.broadcasted_iota(jnp.int32, shape, len(shape) - 1) < HEAD_DIM


def _resident(shape):
    return pl.BlockSpec(shape, lambda i: (0,) * len(shape), pipeline_mode=pl.Buffered(1))


def _proj_kernel(x_ref, g_ref, w_ref, cg_ref, bd_ref, *rest, norm_chunks, src_chunks, raw_chunks, with_extra):
    if with_extra:
        wx_ref, o_ref, ox_ref, oraw_ref = rest
    else:
        (o_ref,) = rest
    xn = _rms_rows(x_ref[...], g_ref[...]).astype(BF16)
    bd = bd_ref[...]

    def finish(c, y):
        cols = slice(c * PROJ_CHUNK, (c + 1) * PROJ_CHUNK)
        if c in raw_chunks:
            k = raw_chunks.index(c)
            oraw_ref[:, k * PROJ_CHUNK:(k + 1) * PROJ_CHUNK] = y
        if norm_chunks[c]:
            y2 = y * y
            hi = y2.astype(BF16)
            lo = (y2 - hi.astype(F32)).astype(BF16)
            ssq = jnp.dot(hi, bd, preferred_element_type=F32) + jnp.dot(lo, bd, preferred_element_type=F32)
            y = y * lax.rsqrt(ssq * (1.0 / HEAD_DIM) + RMS_EPS) * cg_ref[:, cols]
        o_ref[:, cols] = y.astype(o_ref.dtype)

    prev = None
    for c in range(len(norm_chunks)):
        src = src_chunks[c] * PROJ_CHUNK
        y = jnp.dot(xn, w_ref[:, src:src + PROJ_CHUNK], preferred_element_type=F32)
        if prev is not None:
            finish(c - 1, prev)
        prev = y
    finish(len(norm_chunks) - 1, prev)
    if with_extra:
        ox_ref[...] = jnp.dot(xn, wx_ref[...], preferred_element_type=F32)


def _project(x, g, w, col_gain, norm_chunks, w_extra=None, raw_chunks=(), src_chunks=None):
    if src_chunks is None:
        src_chunks = range(len(norm_chunks))
    T, D = x.shape
    N = w.shape[1]
    blk = np.kron(np.eye(PROJ_CHUNK // HEAD_DIM), np.ones((HEAD_DIM, HEAD_DIM))).astype(np.float32)
    row_in = lambda n: pl.BlockSpec((ROW_TILE, n), lambda i: (i, 0))
    in_specs = [row_in(D), _resident((1, D)), _resident((D, N)), _resident((1, N)),
                _resident((PROJ_CHUNK, PROJ_CHUNK))]
    args = [x, g.reshape(1, D), w, col_gain.reshape(1, N), jnp.asarray(blk, BF16)]
    out_specs = [row_in(N)]
    out_shape = [jax.ShapeDtypeStruct((T, N), BF16)]
    if w_extra is not None:
        nx = w_extra.shape[1]
        in_specs.append(_resident((D, nx)))
        args.append(w_extra)
        out_specs += [row_in(nx), row_in(len(raw_chunks) * PROJ_CHUNK)]
        out_shape += [jax.ShapeDtypeStruct((T, nx), F32), jax.ShapeDtypeStruct((T, len(raw_chunks) * PROJ_CHUNK), F32)]
    return pl.pallas_call(
        functools.partial(_proj_kernel, norm_chunks=tuple(norm_chunks), src_chunks=tuple(src_chunks),
                          raw_chunks=tuple(raw_chunks),
                          with_extra=w_extra is not None),
        grid=(T // ROW_TILE,),
        in_specs=in_specs,
        out_specs=out_specs,
        out_shape=out_shape,
        compiler_params=_cparams(1),
        name="proj",
    )(*args)


def _sigmoid(z):
    return 1.0 / (1.0 + jnp.exp(-z))


def _post_kernel(o_ref, h_ref, wout_ref, gf_ref, wg_ref, wu_ref, wd_ref, gp_ref, wpg_ref, p_ref, wpp_ref, out_ref):
    halves = [slice(r * ROW_TILE // POST_SPLIT, (r + 1) * ROW_TILE // POST_SPLIT) for r in range(POST_SPLIT)]
    h1 = [h_ref[r, :] + jnp.dot(o_ref[r, :], wout_ref[...], preferred_element_type=F32) for r in halves]
    xn = [_rms_rows(x, gf_ref[...]).astype(BF16) for x in h1]
    a = [jnp.dot(x, wg_ref[...], preferred_element_type=F32) for x in xn]
    u = [jnp.dot(x, wu_ref[...], preferred_element_type=F32) for x in xn]
    act = [(ai * _sigmoid(ai) * ui).astype(BF16) for ai, ui in zip(a, u)]
    h2 = [x + jnp.dot(t, wd_ref[...], preferred_element_type=F32) for x, t in zip(h1, act)]
    hn = [_rms_rows(x, gp_ref[...]).astype(BF16) for x in h2]
    gate = [_sigmoid(jnp.dot(x, wpg_ref[...], preferred_element_type=F32)) for x in hn]
    for r, x, g in zip(halves, h2, gate):
        out_ref[r, :] = x + g * jnp.dot(p_ref[r, :].astype(BF16), wpp_ref[...], preferred_element_type=F32)


def _post_attention(o, h, w_out, g_ffn, wg, wu, wd, g_ple, w_pgate, p, w_pproj):
    T, D = h.shape
    Fh = wg.shape[1]
    Pd = p.shape[1]
    row_in = lambda n: pl.BlockSpec((ROW_TILE, n), lambda i: (i, 0))
    return pl.pallas_call(
        _post_kernel,
        grid=(T // ROW_TILE,),
        in_specs=[row_in(D), row_in(D), _resident((D, D)), _resident((1, D)), _resident((D, Fh)),
                  _resident((D, Fh)), _resident((Fh, D)), _resident((1, D)), _resident((D, D)),
                  row_in(Pd), _resident((Pd, D))],
        out_specs=row_in(D),
        out_shape=jax.ShapeDtypeStruct((T, D), F32),
        compiler_params=pltpu.CompilerParams(dimension_semantics=("arbitrary",), vmem_limit_bytes=POST_VMEM_LIMIT),
        name="post_attention",
    )(o, h, w_out, g_ffn.reshape(1, D), wg, wu, wd, g_ple.reshape(1, D), w_pgate, p, w_pproj)


def _fill_strips(strip_ref, r_ref):
    for h in range(2):
        for o in range(r_ref.shape[2]):
            x = jnp.broadcast_to(r_ref[0, h, o:o + 1, :], (TQ, 2 * TQ))
            strip_ref[h, o * TQ:(o + 1) * TQ, :] = pltpu.roll(x, 0, 1, stride=1, stride_axis=0)[:, :TQ]


def _tile_groups(nblk, span):
    cnt = [min(span, nblk - j) for j in range(nblk)]
    off = [0]
    for c in cnt:
        off.append(off[-1] + c * TQ)
    return cnt, off


def _tile_rows(nblk, span):
    return _tile_groups(nblk, span)[1][-1]


def _pair_attention(q_rows, k_rows, v_rows, strip_ref, s_ref, p_ref, o_ref, nblk, span, skew=1):
    cnt, off = _tile_groups(nblk, span)
    together = s_ref.shape[0] >= 2 * off[-1] and p_ref.shape[2] >= 2 * span * TQ
    s0 = [0, off[-1] if together else 0]
    p0 = [0, span * TQ if together else 0]

    def scores(h, j):
        n = cnt[j] * TQ
        s = lax.dot_general(q_rows(h, j * TQ, j * TQ + n), k_rows(h, j), NT_DIMS, preferred_element_type=F32)
        s_ref[s0[h] + off[j]:s0[h] + off[j] + n, :] = s + strip_ref[h, 0:n, :]

    def probs(h, i):
        j0 = max(0, i - span + 1)
        rows = [s0[h] + off[j] + (i - j) * TQ for j in range(j0, i + 1)]
        mx = s_ref[rows[0]:rows[0] + TQ, :]
        for r in rows[1:]:
            mx = jnp.maximum(mx, s_ref[r:r + TQ, :])
        m = jnp.max(mx, axis=1, keepdims=True)
        for t, r in enumerate(rows):
            p_ref[i, :, p0[h] + t * TQ:p0[h] + (t + 1) * TQ] = jnp.exp(s_ref[r:r + TQ, :] - m).astype(BF16)

    def values(h, i):
        j0 = max(0, i - span + 1)
        out = jnp.dot(p_ref[i, :, p0[h]:p0[h] + (i + 1 - j0) * TQ], v_rows(j0 * TQ, (i + 1) * TQ),
                      preferred_element_type=F32)
        o_ref[h, i * TQ:(i + 1) * TQ, :] = out[:, :LANES] / out[:, LANES:]

    for heads in ([(0, 1)] if together else [(0,), (1,)]):
        for j in range(nblk + skew):
            for h in heads:
                if j < nblk:
                    scores(h, j)
            for h in heads:
                if j >= skew:
                    values(h, j - skew)
            for h in heads:
                if j < nblk:
                    probs(h, j)


def _merge_heads(o_ref):
    return jnp.where(_lane_lo((o_ref.shape[1], LANES)), o_ref[0], o_ref[1])


def _split_heads(q):
    lo = _lane_lo(q.shape)
    zero = jnp.zeros_like(q)
    return jnp.where(lo, q, zero), jnp.where(lo, zero, q)


def _fill_value_ones(vaug_ref):
    vaug_ref[:, LANES:] = jnp.ones((vaug_ref.shape[0], LANES), vaug_ref.dtype)


def _moba_block_means(k_ref):
    shape = (2 * MOBA_NBLK, LANES)
    lo = _lane_lo(shape)
    row = lax.broadcasted_iota(jnp.int32, shape, 0)
    kmt = jnp.zeros(shape, F32)
    for n in range(MOBA_NBLK):
        mean = jnp.mean(k_ref[0, n * MOBA_BLOCK:(n + 1) * MOBA_BLOCK, :].astype(F32), axis=0, keepdims=True)
        kmt = jnp.where(row == n, jnp.where(lo, mean, 0.0), kmt)
        kmt = jnp.where(row == MOBA_NBLK + n, jnp.where(lo, 0.0, mean), kmt)
    return kmt


def _moba_penalty(q, kmt):
    nb = MOBA_NBLK
    S = q.shape[0]
    hi = kmt.astype(BF16)
    rest = kmt - hi.astype(F32)
    mid = rest.astype(BF16)
    lo = (rest - mid.astype(F32)).astype(BF16)
    g3 = lax.dot_general(jnp.concatenate([hi, mid, lo], axis=0), q, NT_DIMS, preferred_element_type=F32)
    g = g3[0:2 * nb] + g3[2 * nb:4 * nb] + g3[4 * nb:6 * nb]
    row = lax.broadcasted_iota(jnp.int32, g.shape, 0)
    n = row & (nb - 1)
    own = lax.shift_right_logical(lax.broadcasted_iota(jnp.int32, g.shape, 1), int(math.log2(MOBA_BLOCK)))
    rank = jnp.zeros(g.shape, F32)
    for m in range(nb - 1):
        gm = jnp.where(row < nb, g[m:m + 1, :], g[nb + m:nb + m + 1, :])
        tie = jnp.where(n > m, 1.0, 0.0)
        beats = jnp.where(gm > g, 1.0, jnp.where(gm == g, tie, 0.0))
        rank = rank + jnp.where(own > m, beats, 0.0)
    keep = jnp.where(n < own, jnp.where(rank < MOBA_TOPK, 1.0, 0.0), jnp.where(n == own, 1.0, 0.0))
    keep = jnp.concatenate([keep, jnp.zeros((LANES - 2 * nb, S), F32)], axis=0).T
    lane = lax.broadcasted_iota(jnp.int32, keep.shape, 1)
    return jnp.where(lane < 2 * nb, (keep - 1.0) * (-NEG), 0.0)


def _attn0_kernel(q_ref, k_ref, v_ref, r_ref, e_ref, o_ref, strip_ref, qaug_ref, kaug_ref, vaug_ref,
                  s_ref, p_ref, oh_ref):
    p = pl.program_id(0)
    b = pl.program_id(1)
    S = q_ref.shape[1]
    nblk = S // TQ

    @pl.when(b == 0)
    def _():
        _fill_strips(strip_ref, r_ref)

    @pl.when(jnp.logical_and(p == 0, b == 0))
    def _():
        for h in range(2):
            kaug_ref[h, :, LANES:] = e_ref[h]
        _fill_value_ones(vaug_ref)

    k = k_ref[0]
    for h, qh in enumerate(_split_heads(q_ref[0])):
        kaug_ref[h, :, :LANES] = k
        qaug_ref[h, :, :LANES] = qh
    vaug_ref[:, :LANES] = v_ref[0]
    is_moba = p < MOBA_HEADS // 2

    @pl.when(is_moba)
    def _():
        pen = _moba_penalty(q_ref[0], _moba_block_means(k_ref)).astype(BF16)
        for h in range(2):
            qaug_ref[h, :, LANES:] = pen

    @pl.when(jnp.logical_not(is_moba))
    def _():
        for h in range(2):
            qaug_ref[h, :, LANES:] = jnp.zeros((S, LANES), BF16)
    _pair_attention(lambda h, r0, r1: qaug_ref[h, r0:r1, :],
                    lambda h, j: kaug_ref[h, j * TQ:(j + 1) * TQ, :],
                    lambda r0, r1: vaug_ref[r0:r1, :],
                    strip_ref, s_ref, p_ref, oh_ref, nblk, nblk)
    o_ref[0] = _merge_heads(oh_ref).astype(o_ref.dtype)


def _attn_scratch(S, span, narrow_span=0):
    nblk = S // TQ
    s_rows = max(_tile_rows(nblk, span), 2 * _tile_rows(nblk, narrow_span))
    return [pltpu.VMEM((2, S, TQ), F32), pltpu.VMEM((2, S, 2 * LANES), BF16), pltpu.VMEM((2, S, 2 * LANES), BF16),
            pltpu.VMEM((S, 2 * LANES), BF16), pltpu.VMEM((s_rows, TQ), F32),
            pltpu.VMEM((nblk, TQ, span * TQ), BF16), pltpu.VMEM((2, S, LANES), F32)]


def _attn0(qkv, r_tab, e_onehot):
    B, S, _ = qkv.shape
    nblk = S // TQ
    return pl.pallas_call(
        _attn0_kernel,
        grid=(N_PAIRS, B),
        in_specs=[
            pl.BlockSpec((1, S, LANES), lambda p, b: (b, 0, p)),
            pl.BlockSpec((1, S, LANES), lambda p, b: (b, 0, N_PAIRS + p)),
            pl.BlockSpec((1, S, LANES), lambda p, b: (b, 0, 2 * N_PAIRS + p)),
            pl.BlockSpec((1, 2, nblk, 2 * TQ), lambda p, b: (p, 0, 0, 0)),
            pl.BlockSpec((2, S, LANES), lambda p, b: (0, 0, 0), pipeline_mode=pl.Buffered(1)),
        ],
        out_specs=pl.BlockSpec((1, S, LANES), lambda p, b: (b, 0, p)),
        out_shape=jax.ShapeDtypeStruct((B, S, D_MODEL), BF16),
        scratch_shapes=_attn_scratch(S, nblk),
        compiler_params=pltpu.CompilerParams(dimension_semantics=("arbitrary",) * 2, vmem_limit_bytes=ATTN_VMEM_LIMIT),
        name="attn_moba_dilated",
    )(qkv, qkv, qkv, r_tab, e_onehot)


def _compress_kernel(x_ref, pos_ref, w1_ref, b1_ref, w2_ref, b2_ref, gain_ref, o_ref, *, normed):
    n_chunk = x_ref.shape[1] // NSA_CMP_STRIDE
    first = second = None
    for a in range(NSA_CMP_STRIDE):
        t = x_ref[0, pl.ds(a, n_chunk, stride=NSA_CMP_STRIDE), :]
        fa = jnp.dot((t + pos_ref[a:a + 1, :]).astype(BF16), w1_ref[a], preferred_element_type=F32)
        sa = jnp.dot((t + pos_ref[NSA_CMP_STRIDE + a:NSA_CMP_STRIDE + a + 1, :]).astype(BF16),
                     w1_ref[NSA_CMP_STRIDE + a], preferred_element_type=F32)
        first = fa if first is None else first + fa
        second = sa if second is None else second + sa
    hid = first + pltpu.roll(second, n_chunk - 1, 0) + b1_ref[...]
    cdf = 0.5 * (1.0 + jnp.tanh(math.sqrt(2.0 / math.pi) * (hid + 0.044715 * (hid * hid * hid))))
    y = jnp.dot((hid * cdf).astype(BF16), w2_ref[...], preferred_element_type=F32) + b2_ref[...]
    if normed:
        lo = _lane_lo(y.shape)
        y2 = y * y
        s_lo = jnp.sum(jnp.where(lo, y2, 0.0), axis=1, keepdims=True)
        s_hi = jnp.sum(jnp.where(lo, 0.0, y2), axis=1, keepdims=True)
        ms = jnp.where(lo, s_lo, s_hi) * (1.0 / HEAD_DIM)
        y = y * lax.rsqrt(ms + RMS_EPS) * gain_ref[...]
    o_ref[0, 0] = y.astype(o_ref.dtype)


def _pair_block_diag(w):
    z = jnp.zeros_like(w)
    return jnp.concatenate([jnp.concatenate([w, z], axis=-1), jnp.concatenate([z, w], axis=-1)], axis=-2)


def _compress(raw, col_block0, pos, w1, b1, w2, b2, gain, normed):
    B, S, _ = raw.shape
    hid = w1.shape[1]
    n_pos = pos.shape[0]
    w1bd = _pair_block_diag(w1.reshape(n_pos, HEAD_DIM, hid)).astype(BF16)
    return pl.pallas_call(
        functools.partial(_compress_kernel, normed=normed),
        grid=(B, 2),
        in_specs=[
            pl.BlockSpec((1, S, LANES), lambda b, m: (b, 0, col_block0 + m)),
            pl.BlockSpec((n_pos, LANES), lambda b, m: (0, 0)),
            pl.BlockSpec((n_pos, LANES, 2 * hid), lambda b, m: (0, 0, 0)),
            pl.BlockSpec((1, 2 * hid), lambda b, m: (0, 0)),
            pl.BlockSpec((2 * hid, LANES), lambda b, m: (0, 0)),
            pl.BlockSpec((1, LANES), lambda b, m: (0, 0)),
            pl.BlockSpec((1, LANES), lambda b, m: (0, 0)),
        ],
        out_specs=pl.BlockSpec((1, 1, S // NSA_CMP_STRIDE, LANES), lambda b, m: (b, m, 0, 0)),
        out_shape=jax.ShapeDtypeStruct((B, 2, S // NSA_CMP_STRIDE, LANES), BF16),
        compiler_params=_cparams(2),
        name="nsa_compress",
    )(raw, jnp.tile(pos, (1, 2)), w1bd, jnp.tile(b1, 2).reshape(1, 2 * hid), _pair_block_diag(w2).astype(BF16),
      jnp.tile(b2, 2).reshape(1, LANES), jnp.tile(gain, 2).reshape(1, LANES))


def _cmp_kernel(q_ref, kc_ref, vc_ref, bias_ref, ovt_ref, o_ref, pen_ref):
    kc = kc_ref[0, 0]
    vc = vc_ref[0, 0]
    S = q_ref.shape[1]
    n_sel = ovt_ref.shape[0]
    lo = _lane_lo((CMP_ROWS, LANES))
    blk = lax.broadcasted_iota(jnp.int32, (n_sel, CMP_ROWS), 0)
    for c in range(S // CMP_ROWS):
        rows = slice(c * CMP_ROWS, (c + 1) * CMP_ROWS)
        psum = [jnp.zeros((CMP_ROWS, LANES), F32), jnp.zeros((CMP_ROWS, LANES), F32)]
        for r in range(4):
            heads = _split_heads(q_ref[0, rows, r * LANES:(r + 1) * LANES])
            outs = []
            for h in range(2):
                s = lax.dot_general(heads[h], kc, NT_DIMS, preferred_element_type=F32) + bias_ref[0, 2 * r + h, rows, :]
                m = jnp.max(s, axis=1, keepdims=True)
                e = jnp.exp(s - m)
                l = jnp.sum(e, axis=1, keepdims=True)
                pr = e * jnp.where(m > 0.5 * NEG, 1.0 / l, 0.0)
                psum[h] = psum[h] + pr
                outs.append(jnp.dot(pr.astype(BF16), vc, preferred_element_type=F32))
            o_ref[0, rows, r * LANES:(r + 1) * LANES] = jnp.where(lo, outs[0], outs[1])

        t = c * CMP_ROWS + lax.broadcasted_iota(jnp.int32, (n_sel, CMP_ROWS), 1)
        cur = lax.shift_right_logical(t, int(math.log2(NSA_SEL_BLOCK)))
        keeps = []
        for h in range(2):
            imp = lax.dot_general(ovt_ref[...], psum[h], NT_DIMS,
                                  precision=lax.Precision.HIGHEST, preferred_element_type=F32)
            forced = jnp.where(blk == 0, 1.0, jnp.where(blk == cur, 1.0, jnp.where(blk == cur - 1, 1.0, 0.0)))
            imp = jnp.where(blk <= cur, imp + forced * NSA_FORCE, -jnp.inf)
            groups = [imp[g * SUBLANES:(g + 1) * SUBLANES, :] for g in range(n_sel // SUBLANES)]
            ranks = [jnp.zeros(g.shape, F32) for g in groups]
            for m in range(n_sel):
                im = imp[m:m + 1, :]
                for g, sub in enumerate(groups):
                    ge = jnp.where(im >= sub, 1.0, 0.0)
                    gt = jnp.where(im > sub, 1.0, 0.0)
                    if g * SUBLANES > m:
                        inc = ge
                    elif (g + 1) * SUBLANES - 1 <= m:
                        inc = gt
                    else:
                        inc = jnp.where(lax.broadcasted_iota(jnp.int32, sub.shape, 0) + g * SUBLANES > m, ge, gt)
                    ranks[g] = ranks[g] + inc
            keeps.append(jnp.where(jnp.concatenate(ranks, axis=0) < NSA_SEL_TOPN, 1.0, 0.0))
        keep = jnp.concatenate(keeps + [jnp.zeros((LANES - 2 * n_sel, CMP_ROWS), F32)], axis=0).T
        lane = lax.broadcasted_iota(jnp.int32, keep.shape, 1)
        pen_ref[0, 0, rows, :] = jnp.where(lane < 2 * n_sel, (keep - 1.0) * (-NEG), 0.0).astype(pen_ref.dtype)


def _cmp_attention(qkv, kc, vc, bias, ovt):
    B, S, _ = qkv.shape
    n_sel = ovt.shape[0]
    return pl.pallas_call(
        _cmp_kernel,
        grid=(2, B),
        in_specs=[
            pl.BlockSpec((1, S, 4 * LANES), lambda m, b: (b, 0, m)),
            pl.BlockSpec((1, 1, LANES, LANES), lambda m, b: (b, m, 0, 0)),
            pl.BlockSpec((1, 1, LANES, LANES), lambda m, b: (b, m, 0, 0)),
            pl.BlockSpec((1, 8, S, LANES), lambda m, b: (m, 0, 0, 0), pipeline_mode=pl.Buffered(1)),
            pl.BlockSpec((n_sel, LANES), lambda m, b: (0, 0)),
        ],
        out_specs=[
            pl.BlockSpec((1, S, 4 * LANES), lambda m, b: (b, 0, m)),
            pl.BlockSpec((1, 1, S, LANES), lambda m, b: (b, m, 0, 0)),
        ],
        out_shape=[jax.ShapeDtypeStruct((B, S, D_MODEL), F32),
                   jax.ShapeDtypeStruct((B, 2, S, LANES), BF16)],
        compiler_params=_cparams(2),
        name="nsa_compressed_select",
    )(qkv, kc, vc, bias, ovt)


def _attn1_kernel(q_ref, ks_ref, vs_ref, kw_ref, vw_ref, rs_ref, rw_ref, e_ref, pen_ref, ocmp_ref, graw_ref,
                  o_ref, strip_s_ref, qaug_ref, kaug_ref, vsaug_ref, s_ref, p_ref, oslc_ref,
                  strip_w_ref, vwaug_ref, owin_ref):
    p = pl.program_id(0)
    b = pl.program_id(1)
    S = q_ref.shape[1]
    nblk = S // TQ

    @pl.when(b == 0)
    def _():
        _fill_strips(strip_s_ref, rs_ref)
        _fill_strips(strip_w_ref, rw_ref)

    @pl.when(jnp.logical_and(p == 0, b == 0))
    def _():
        for h in range(2):
            kaug_ref[h, :, LANES:] = e_ref[h]
        _fill_value_ones(vsaug_ref)
        _fill_value_ones(vwaug_ref)

    ks = ks_ref[0]
    pen = pen_ref[0, 0]
    for h, qh in enumerate(_split_heads(q_ref[0])):
        kaug_ref[h, :, :LANES] = ks
        qaug_ref[h, :, :LANES] = qh
        qaug_ref[h, :, LANES:] = pen
    vsaug_ref[:, :LANES] = vs_ref[0]
    vwaug_ref[:, :LANES] = vw_ref[0]
    _pair_attention(lambda h, r0, r1: qaug_ref[h, r0:r1, :],
                    lambda h, j: kaug_ref[h, j * TQ:(j + 1) * TQ, :],
                    lambda r0, r1: vsaug_ref[r0:r1, :],
                    strip_s_ref, s_ref, p_ref, oslc_ref, nblk, nblk)
    _pair_attention(lambda h, r0, r1: qaug_ref[h, r0:r1, :LANES],
                    lambda h, j: kw_ref[0, j * TQ:(j + 1) * TQ, :],
                    lambda r0, r1: vwaug_ref[r0:r1, :],
                    strip_w_ref, s_ref, p_ref, owin_ref, nblk, strip_w_ref.shape[1] // TQ, skew=2)
    sig = _sigmoid(pltpu.roll(graw_ref[0], lax.rem(LANES - GATE_STRIDE * p, LANES), 1))
    lo_lanes = _lane_lo((S, LANES))
    gates = [jnp.where(lo_lanes, sig[:, 2 * br:2 * br + 1], sig[:, 2 * br + 1:2 * br + 2]) for br in range(3)]
    out = gates[0] * ocmp_ref[0] + gates[1] * _merge_heads(oslc_ref) + gates[2] * _merge_heads(owin_ref)
    o_ref[0] = out.astype(o_ref.dtype)


def _attn1(qkv, r_slc, r_win, e_onehot, pen, ocmp, graw):
    B, S, _ = qkv.shape
    kv0 = D_MODEL // LANES + 4
    n_win = r_win.shape[2]
    return pl.pallas_call(
        _attn1_kernel,
        grid=(N_PAIRS, B),
        in_specs=[
            pl.BlockSpec((1, S, LANES), lambda p, b: (b, 0, p)),
            pl.BlockSpec((1, S, LANES), lambda p, b: (b, 0, kv0 + p // 4)),
            pl.BlockSpec((1, S, LANES), lambda p, b: (b, 0, kv0 + 2 + p // 4)),
            pl.BlockSpec((1, S, LANES), lambda p, b: (b, 0, kv0 + 4 + p // 4)),
            pl.BlockSpec((1, S, LANES), lambda p, b: (b, 0, kv0 + 6 + p // 4)),
            pl.BlockSpec((1, 2, r_slc.shape[2], 2 * TQ), lambda p, b: (p, 0, 0, 0)),
            pl.BlockSpec((1, 2, n_win, 2 * TQ), lambda p, b: (p, 0, 0, 0)),
            pl.BlockSpec((2, S, LANES), lambda p, b: (0, 0, 0), pipeline_mode=pl.Buffered(1)),
            pl.BlockSpec((1, 1, S, LANES), lambda p, b: (b, p // 4, 0, 0)),
            pl.BlockSpec((1, S, LANES), lambda p, b: (b, 0, p)),
            pl.BlockSpec((1, S, LANES), lambda p, b: (b, 0, 0)),
        ],
        out_specs=pl.BlockSpec((1, S, LANES), lambda p, b: (b, 0, p)),
        out_shape=jax.ShapeDtypeStruct((B, S, D_MODEL), BF16),
        scratch_shapes=_attn_scratch(S, S // TQ, n_win) + [pltpu.VMEM((2, n_win * TQ, TQ), F32),
                                                    pltpu.VMEM((S, 2 * LANES), BF16), pltpu.VMEM((2, S, LANES), F32)],
        compiler_params=pltpu.CompilerParams(dimension_semantics=("arbitrary",) * 2, vmem_limit_bytes=ATTN_VMEM_LIMIT),
        name="attn_nsa",
    )(qkv, qkv, qkv, qkv, qkv, r_slc, r_win, e_onehot, pen, ocmp, graw)


def _bucket(dist):
    n = np.maximum(dist, 0)
    exact = REL_BUCKETS // 2
    nf = np.maximum(n, 1).astype(np.float64)
    large = exact + (np.log(nf / exact) / math.log(REL_MAX_DIST / exact) * (REL_BUCKETS - exact)).astype(np.int64)
    return np.where(n < exact, n, np.minimum(large, REL_BUCKETS - 1))


def _strip_rows(tab, n_off, extra_of_dist):
    u = np.arange(2 * TQ)
    d = np.arange(n_off)[:, None] * TQ - np.where(u < TQ, u, u - 2 * TQ)[None, :]
    onehot = (_bucket(d)[None] == np.arange(REL_BUCKETS)[:, None, None]).astype(np.float32)
    vals = jnp.einsum("hb,bou->hou", tab, jnp.asarray(onehot), precision=lax.Precision.HIGHEST) + extra_of_dist(d)
    return vals.reshape(tab.shape[0] // 2, 2, n_off, 2 * TQ)


def _causal_mask(d):
    return np.where(d >= 0, 0.0, NEG).astype(np.float32)


def _window_mask(d):
    return np.where((d >= 0) & (d < NSA_WINDOW), 0.0, NEG).astype(np.float32)


def _dilation_log_count(d):
    c = ((d >= 0) & (d <= 128)).astype(np.float64)
    c += ((d >= 0) & (d % 4 == 0) & (d <= 512))
    c += ((d >= 0) & (d % 16 == 0) & (d <= 2048))
    return np.where(c > 0, np.log(np.maximum(c, 1.0)), NEG).astype(np.float32)


def _one_hot_blocks(S, block, per_head):
    e = np.zeros((2, S, LANES), np.float32)
    key = np.arange(S)
    for h in range(2):
        e[h, key, h * per_head + key // block] = 1.0
    return jnp.asarray(e, BF16)


_NSA_HEAD_ORDER = np.array([8 * (p // 4) + (p % 4) + 4 * h for p in range(N_PAIRS) for h in range(2)])


def _head_cols(heads):
    return (np.asarray(heads)[:, None] * HEAD_DIM + np.arange(HEAD_DIM)[None, :]).reshape(-1)


def _mixer_ab(h, g_mix, w_in, qn_a, kn_a, qn_b, kn_b, rel_bias, B, S):
    wa = MOBA_HEADS * HEAD_DIM
    ones = jnp.ones((wa,), F32)
    nh = MOBA_HEADS
    gain = jnp.concatenate([jnp.tile(qn_a, nh) * ATTN_SCALE, jnp.tile(qn_b, nh) * ATTN_SCALE,
                            jnp.tile(kn_a, nh), jnp.tile(kn_b, nh), ones, ones])
    per_sec = wa // PROJ_CHUNK
    src = [sec * per_sec + c for sec in (0, 3, 1, 4, 2, 5) for c in range(per_sec)]
    (qkv,) = _project(h, g_mix, w_in.astype(BF16), gain, [True] * (4 * per_sec) + [False] * (2 * per_sec),
                      src_chunks=src)
    qkv = qkv.reshape(B, S, 3 * D_MODEL)

    tab = rel_bias.T
    r_tab = jnp.concatenate([_strip_rows(tab[:MOBA_HEADS], S // TQ, _causal_mask),
                             _strip_rows(tab[MOBA_HEADS:], S // TQ, _dilation_log_count)])
    o = _attn0(qkv, r_tab, _one_hot_blocks(S, MOBA_BLOCK, S // MOBA_BLOCK))
    return o.reshape(B * S, D_MODEL)


def _mixer_nsa(h, g_mix, w_in, qn, kn_c, kn_s, kn_w, cmp_k, cmp_v, rel_bias, B, S):
    qw = N_HEADS * HEAD_DIM
    kvw = NSA_GROUPS * HEAD_DIM
    order = _NSA_HEAD_ORDER
    w_main = jnp.concatenate([w_in[:, _head_cols(order)], w_in[:, qw:qw + 6 * kvw]], axis=1).astype(BF16)
    ones = jnp.ones((kvw,), F32)
    gain = jnp.concatenate([jnp.tile(qn, N_HEADS) * ATTN_SCALE, ones, ones,
                            jnp.tile(kn_s, NSA_GROUPS), ones, jnp.tile(kn_w, NSA_GROUPS), ones])
    assert kvw == PROJ_CHUNK
    norm_chunks = [True] * (qw // PROJ_CHUNK) + [False, False, True, False, True, False]

    gcols = np.zeros((LANES,), np.int64)
    gused = np.zeros((LANES,), np.float32)
    for p in range(N_PAIRS):
        for br in range(3):
            for hh in range(2):
                c = GATE_STRIDE * p + 2 * br + hh
                gcols[c] = qw + 6 * kvw + 3 * order[2 * p + hh] + br
                gused[c] = 1.0
    w_gate = (w_in[:, gcols] * gused).astype(BF16)
    kc_chunk = qw // PROJ_CHUNK
    qkv, graw, raw = _project(h, g_mix, w_main, gain, norm_chunks, w_gate, raw_chunks=(kc_chunk, kc_chunk + 1))
    qkv = qkv.reshape(B, S, qw + 6 * kvw)
    graw = graw.reshape(B, S, LANES)
    raw = raw.reshape(B, S, 2 * kvw)
    kc = _compress(raw, 0, *cmp_k, kn_c, True)
    vc = _compress(raw, kvw // LANES, *cmp_v, kn_c, False)

    tab = rel_bias.T[order]
    n_cmp = (S - NSA_CMP_LEN) // NSA_CMP_STRIDE + 1
    t_pos = np.arange(S)[:, None]
    c_idx = np.arange(LANES)[None, :]
    dc = t_pos - (c_idx * NSA_CMP_STRIDE + NSA_CMP_LEN - 1)
    cmp_mask = np.where((dc >= 0) & (c_idx < n_cmp), 0.0, NEG).astype(np.float32)
    onehot = jnp.asarray(_bucket(dc).astype(np.int8))[None] == jnp.arange(REL_BUCKETS, dtype=jnp.int8)[:, None, None]
    bias_c = jnp.einsum("hb,bsc->hsc", tab, onehot.astype(F32), precision=lax.Precision.HIGHEST)
    bias_c = (bias_c + cmp_mask).reshape(2, 8, S, LANES)
    n_sel = S // NSA_SEL_BLOCK
    cstart = np.arange(LANES) * NSA_CMP_STRIDE
    sstart = np.arange(n_sel) * NSA_SEL_BLOCK
    ovt = np.maximum(np.minimum(cstart[None, :] + NSA_CMP_LEN, sstart[:, None] + NSA_SEL_BLOCK)
                     - np.maximum(cstart[None, :], sstart[:, None]), 0).astype(np.float32)
    ovt[:, n_cmp:] = 0.0
    ocmp, pen = _cmp_attention(qkv, kc, vc, bias_c, jnp.asarray(ovt))

    r_slc = _strip_rows(tab, S // TQ, _causal_mask)
    r_win = _strip_rows(tab, NSA_WINDOW // TQ + 1, _window_mask)
    o = _attn1(qkv, r_slc, r_win, _one_hot_blocks(S, NSA_SEL_BLOCK, n_sel), pen, ocmp, graw)
    return o.reshape(B * S, D_MODEL)


def kernel(x, p, rel_bias, norm_mix, norm_ffn, norm_ple, w_ffn_gate, w_ffn_up, w_ffn_down, w_ple_proj, w_ple_gate, w_in_ab, w_out_ab, qn_moba, kn_moba, qn_dil, kn_dil, w_in_nsa, w_out_nsa, qn_nsa, kn_cmp, kn_slc, kn_win, cmp_k_pos, cmp_k_w1, cmp_k_b1, cmp_k_w2, cmp_k_b2, cmp_v_pos, cmp_v_w1, cmp_v_b1, cmp_v_w2, cmp_v_b2):
    B, S, D = x.shape
    depth = p.shape[0]
    assert D == D_MODEL and S == MOBA_NBLK * MOBA_BLOCK and (B * S) % ROW_TILE == 0, (B, S, D)
    h = x.reshape(B * S, D)
    for i in range(depth):
        e = i // 2
        if i % 2 == 0:
            o = _mixer_ab(h, norm_mix[i], w_in_ab[e], qn_moba[e], kn_moba[e], qn_dil[e], kn_dil[e], rel_bias, B, S)
            w_out = w_out_ab[e]
        else:
            cmp_k = (cmp_k_pos[e], cmp_k_w1[e], cmp_k_b1[e], cmp_k_w2[e], cmp_k_b2[e])
            cmp_v = (cmp_v_pos[e], cmp_v_w1[e], cmp_v_b1[e], cmp_v_w2[e], cmp_v_b2[e])
            o = _mixer_nsa(h, norm_mix[i], w_in_nsa[e], qn_nsa[e], kn_cmp[e], kn_slc[e], kn_win[e],
                           cmp_k, cmp_v, rel_bias, B, S)
            w_out = w_out_nsa[e][_head_cols(_NSA_HEAD_ORDER), :]
        h = _post_attention(o, h, w_out.astype(BF16), norm_ffn[i], w_ffn_gate[i].astype(BF16),
                            w_ffn_up[i].astype(BF16), w_ffn_down[i].astype(BF16), norm_ple[i],
                            w_ple_gate[i].astype(BF16), p[i].reshape(B * S, -1), w_ple_proj[i].astype(BF16))
    return h.reshape(B, S, D)
```

```python
import functools
import math

import numpy as np
import jax
import jax.numpy as jnp
from jax import lax
from jax.experimental import pallas as pl
from jax.experimental.pallas import tpu as pltpu

F32 = jnp.float32
BF16 = jnp.bfloat16

D_MODEL = 1024
HEAD_DIM = 64
N_HEADS = 16
N_PAIRS = N_HEADS // 2
MOBA_HEADS = 8
MOBA_BLOCK = 256
MOBA_TOPK = 3
MOBA_NBLK = 8
NSA_GROUPS = 4
NSA_CMP_LEN = 32
NSA_CMP_STRIDE = 16
NSA_SEL_BLOCK = 64
NSA_SEL_TOPN = 16
NSA_WINDOW = 512
NSA_FORCE = 1.0e6
REL_BUCKETS = 32
REL_MAX_DIST = 2048
RMS_EPS = 1e-6
ATTN_SCALE = HEAD_DIM ** -0.5

LANES = 128
SUBLANES = 8
TQ = 256
NEG = -1.0e30
ROW_TILE = 512
PROJ_CHUNK = 256
POST_SPLIT = 2
GATE_STRIDE = LANES // N_HEADS
CMP_ROWS = 2048
CAST_ROWS = 256
VMEM_LIMIT = 48 * 1024 * 1024
POST_VMEM_LIMIT = 56 * 1024 * 1024
ATTN_VMEM_LIMIT = 56 * 1024 * 1024

NT_DIMS = (((1,), (1,)), ((), ()))


def _cparams(n_axes):
    return pltpu.CompilerParams(dimension_semantics=("arbitrary",) * n_axes,
                                vmem_limit_bytes=VMEM_LIMIT)


def _rms_rows(x, g):
    ms = jnp.mean(x * x, axis=-1, keepdims=True)
    return x * lax.rsqrt(ms + RMS_EPS) * g


def _lane_lo(shape):
    return lax.broadcasted_iota(jnp.int32, shape, len(shape) - 1) < HEAD_DIM


def _resident(shape):
    return pl.BlockSpec(shape, lambda i: (0,) * len(shape), pipeline_mode=pl.Buffered(1))


def _proj_kernel(x_ref, g_ref, w_ref, cg_ref, bd_ref, *rest, norm_chunks, src_chunks, raw_chunks, with_extra):
    if with_extra:
        wx_ref, o_ref, ox_ref, oraw_ref = rest
    else:
        (o_ref,) = rest
    xn = _rms_rows(x_ref[...], g_ref[...]).astype(BF16)
    bd = bd_ref[...]

    def finish(c, y):
        cols = slice(c * PROJ_CHUNK, (c + 1) * PROJ_CHUNK)
        if c in raw_chunks:
            k = raw_chunks.index(c)
            oraw_ref[:, k * PROJ_CHUNK:(k + 1) * PROJ_CHUNK] = y
        if norm_chunks[c]:
            y2 = y * y
            hi = y2.astype(BF16)
            lo = (y2 - hi.astype(F32)).astype(BF16)
            ssq = jnp.dot(hi, bd, preferred_element_type=F32) + jnp.dot(lo, bd, preferred_element_type=F32)
            y = y * lax.rsqrt(ssq * (1.0 / HEAD_DIM) + RMS_EPS) * cg_ref[:, cols]
        o_ref[:, cols] = y.astype(o_ref.dtype)

    prev = None
    for c in range(len(norm_chunks)):
        src = src_chunks[c] * PROJ_CHUNK
        y = jnp.dot(xn, w_ref[:, src:src + PROJ_CHUNK], preferred_element_type=F32)
        if prev is not None:
            finish(c - 1, prev)
        prev = y
    finish(len(norm_chunks) - 1, prev)
    if with_extra:
        ox_ref[...] = jnp.dot(xn, wx_ref[...], preferred_element_type=F32)


def _project(x, g, w, col_gain, norm_chunks, w_extra=None, raw_chunks=(), src_chunks=None):
    if src_chunks is None:
        src_chunks = range(len(norm_chunks))
    T, D = x.shape
    N = w.shape[1]
    blk = np.kron(np.eye(PROJ_CHUNK // HEAD_DIM), np.ones((HEAD_DIM, HEAD_DIM))).astype(np.float32)
    row_in = lambda n: pl.BlockSpec((ROW_TILE, n), lambda i: (i, 0))
    in_specs = [row_in(D), _resident((1, D)), _resident((D, N)), _resident((1, N)),
                _resident((PROJ_CHUNK, PROJ_CHUNK))]
    args = [x, g.reshape(1, D), w, col_gain.reshape(1, N), jnp.asarray(blk, BF16)]
    out_specs = [row_in(N)]
    out_shape = [jax.ShapeDtypeStruct((T, N), BF16)]
    if w_extra is not None:
        nx = w_extra.shape[1]
        in_specs.append(_resident((D, nx)))
        args.append(w_extra)
        out_specs += [row_in(nx), row_in(len(raw_chunks) * PROJ_CHUNK)]
        out_shape += [jax.ShapeDtypeStruct((T, nx), F32), jax.ShapeDtypeStruct((T, len(raw_chunks) * PROJ_CHUNK), F32)]
    return pl.pallas_call(
        functools.partial(_proj_kernel, norm_chunks=tuple(norm_chunks), src_chunks=tuple(src_chunks),
                          raw_chunks=tuple(raw_chunks),
                          with_extra=w_extra is not None),
        grid=(T // ROW_TILE,),
        in_specs=in_specs,
        out_specs=out_specs,
        out_shape=out_shape,
        compiler_params=_cparams(1),
        name="proj",
    )(*args)


def _cast_kernel(w_ref, o_ref):
    o_ref[...] = w_ref[...].astype(o_ref.dtype)


def _to_bf16(w):
    L, R, C = w.shape
    spec = pl.BlockSpec((1, CAST_ROWS, C), lambda l, r: (l, r, 0))
    return pl.pallas_call(_cast_kernel, grid=(L, R // CAST_ROWS), in_specs=[spec], out_specs=spec,
                          out_shape=jax.ShapeDtypeStruct(w.shape, BF16), compiler_params=_cparams(2),
                          name="weights_to_bf16")(w)


def _sigmoid(z):
    return 1.0 / (1.0 + jnp.exp(-z))


def _post_kernel(o_ref, h_ref, wout_ref, gf_ref, wg_ref, wu_ref, wd_ref, gp_ref, wpg_ref, p_ref, wpp_ref, out_ref):
    halves = [slice(r * ROW_TILE // POST_SPLIT, (r + 1) * ROW_TILE // POST_SPLIT) for r in range(POST_SPLIT)]
    h1 = [h_ref[r, :] + jnp.dot(o_ref[r, :], wout_ref[...], preferred_element_type=F32) for r in halves]
    xn = [_rms_rows(x, gf_ref[...]).astype(BF16) for x in h1]
    a = [jnp.dot(x, wg_ref[...], preferred_element_type=F32) for x in xn]
    u = [jnp.dot(x, wu_ref[...], preferred_element_type=F32) for x in xn]
    act = [(ai * _sigmoid(ai) * ui).astype(BF16) for ai, ui in zip(a, u)]
    h2 = [x + jnp.dot(t, wd_ref[...], preferred_element_type=F32) for x, t in zip(h1, act)]
    hn = [_rms_rows(x, gp_ref[...]).astype(BF16) for x in h2]
    gate = [_sigmoid(jnp.dot(x, wpg_ref[...], preferred_element_type=F32)) for x in hn]
    for r, x, g in zip(halves, h2, gate):
        out_ref[r, :] = x + g * jnp.dot(p_ref[r, :].astype(BF16), wpp_ref[...], preferred_element_type=F32)


def _post_attention(o, h, w_out, g_ffn, wg, wu, wd, g_ple, w_pgate, p, w_pproj):
    T, D = h.shape
    Fh = wg.shape[1]
    Pd = p.shape[1]
    row_in = lambda n: pl.BlockSpec((ROW_TILE, n), lambda i: (i, 0))
    return pl.pallas_call(
        _post_kernel,
        grid=(T // ROW_TILE,),
        in_specs=[row_in(D), row_in(D), _resident((D, D)), _resident((1, D)), _resident((D, Fh)),
                  _resident((D, Fh)), _resident((Fh, D)), _resident((1, D)), _resident((D, D)),
                  row_in(Pd), _resident((Pd, D))],
        out_specs=row_in(D),
        out_shape=jax.ShapeDtypeStruct((T, D), F32),
        compiler_params=pltpu.CompilerParams(dimension_semantics=("arbitrary",), vmem_limit_bytes=POST_VMEM_LIMIT),
        name="post_attention",
    )(o, h, w_out, g_ffn.reshape(1, D), wg, wu, wd, g_ple.reshape(1, D), w_pgate, p, w_pproj)


def _fill_strips(strip_ref, r_ref):
    for h in range(2):
        for o in range(r_ref.shape[2]):
            x = jnp.broadcast_to(r_ref[0, h, o:o + 1, :], (TQ, 2 * TQ))
            strip_ref[h, o * TQ:(o + 1) * TQ, :] = pltpu.roll(x, 0, 1, stride=1, stride_axis=0)[:, :TQ]


def _tile_groups(nblk, span):
    cnt = [min(span, nblk - j) for j in range(nblk)]
    off = [0]
    for c in cnt:
        off.append(off[-1] + c * TQ)
    return cnt, off


def _tile_rows(nblk, span):
    return _tile_groups(nblk, span)[1][-1]


def _pair_attention(q_rows, k_rows, v_rows, strip_ref, s_ref, p_ref, o_ref, nblk, span, skew=1):
    cnt, off = _tile_groups(nblk, span)
    together = s_ref.shape[0] >= 2 * off[-1] and p_ref.shape[2] >= 2 * span * TQ
    s0 = [0, off[-1] if together else 0]
    p0 = [0, span * TQ if together else 0]

    def scores(h, j):
        n = cnt[j] * TQ
        s = lax.dot_general(q_rows(h, j * TQ, j * TQ + n), k_rows(h, j), NT_DIMS, preferred_element_type=F32)
        s_ref[s0[h] + off[j]:s0[h] + off[j] + n, :] = s + strip_ref[h, 0:n, :]

    def probs(h, i):
        j0 = max(0, i - span + 1)
        rows = [s0[h] + off[j] + (i - j) * TQ for j in range(j0, i + 1)]
        mx = s_ref[rows[0]:rows[0] + TQ, :]
        for r in rows[1:]:
            mx = jnp.maximum(mx, s_ref[r:r + TQ, :])
        m = jnp.max(mx, axis=1, keepdims=True)
        for t, r in enumerate(rows):
            p_ref[i, :, p0[h] + t * TQ:p0[h] + (t + 1) * TQ] = jnp.exp(s_ref[r:r + TQ, :] - m).astype(BF16)

    def values(h, i):
        j0 = max(0, i - span + 1)
        out = jnp.dot(p_ref[i, :, p0[h]:p0[h] + (i + 1 - j0) * TQ], v_rows(j0 * TQ, (i + 1) * TQ),
                      preferred_element_type=F32)
        o_ref[h, i * TQ:(i + 1) * TQ, :] = out[:, :LANES] / out[:, LANES:]

    for heads in ([(0, 1)] if together else [(0,), (1,)]):
        for j in range(nblk + skew):
            for h in heads:
                if j < nblk:
                    scores(h, j)
            for h in heads:
                if j >= skew:
                    values(h, j - skew)
            for h in heads:
                if j < nblk:
                    probs(h, j)


def _merge_heads(o_ref):
    return jnp.where(_lane_lo((o_ref.shape[1], LANES)), o_ref[0], o_ref[1])


def _split_heads(q):
    lo = _lane_lo(q.shape)
    zero = jnp.zeros_like(q)
    return jnp.where(lo, q, zero), jnp.where(lo, zero, q)


def _fill_value_ones(vaug_ref):
    vaug_ref[:, LANES:] = jnp.ones((vaug_ref.shape[0], LANES), vaug_ref.dtype)


def _moba_block_means(k_ref):
    shape = (2 * MOBA_NBLK, LANES)
    lo = _lane_lo(shape)
    row = lax.broadcasted_iota(jnp.int32, shape, 0)
    kmt = jnp.zeros(shape, F32)
    for n in range(MOBA_NBLK):
        mean = jnp.mean(k_ref[0, n * MOBA_BLOCK:(n + 1) * MOBA_BLOCK, :].astype(F32), axis=0, keepdims=True)
        kmt = jnp.where(row == n, jnp.where(lo, mean, 0.0), kmt)
        kmt = jnp.where(row == MOBA_NBLK + n, jnp.where(lo, 0.0, mean), kmt)
    return kmt


def _moba_penalty(q, kmt):
    nb = MOBA_NBLK
    S = q.shape[0]
    hi = kmt.astype(BF16)
    rest = kmt - hi.astype(F32)
    mid = rest.astype(BF16)
    lo = (rest - mid.astype(F32)).astype(BF16)
    g3 = lax.dot_general(jnp.concatenate([hi, mid, lo], axis=0), q, NT_DIMS, preferred_element_type=F32)
    g = g3[0:2 * nb] + g3[2 * nb:4 * nb] + g3[4 * nb:6 * nb]
    row = lax.broadcasted_iota(jnp.int32, g.shape, 0)
    n = row & (nb - 1)
    own = lax.shift_right_logical(lax.broadcasted_iota(jnp.int32, g.shape, 1), int(math.log2(MOBA_BLOCK)))
    rank = jnp.zeros(g.shape, F32)
    for m in range(nb - 1):
        gm = jnp.where(row < nb, g[m:m + 1, :], g[nb + m:nb + m + 1, :])
        tie = jnp.where(n > m, 1.0, 0.0)
        beats = jnp.where(gm > g, 1.0, jnp.where(gm == g, tie, 0.0))
        rank = rank + jnp.where(own > m, beats, 0.0)
    keep = jnp.where(n < own, jnp.where(rank < MOBA_TOPK, 1.0, 0.0), jnp.where(n == own, 1.0, 0.0))
    keep = jnp.concatenate([keep, jnp.zeros((LANES - 2 * nb, S), F32)], axis=0).T
    lane = lax.broadcasted_iota(jnp.int32, keep.shape, 1)
    return jnp.where(lane < 2 * nb, (keep - 1.0) * (-NEG), 0.0)


def _attn0_kernel(q_ref, k_ref, v_ref, r_ref, e_ref, o_ref, strip_ref, qaug_ref, kaug_ref, vaug_ref,
                  s_ref, p_ref, oh_ref):
    p = pl.program_id(0)
    b = pl.program_id(1)
    S = q_ref.shape[1]
    nblk = S // TQ

    @pl.when(b == 0)
    def _():
        _fill_strips(strip_ref, r_ref)

    @pl.when(jnp.logical_and(p == 0, b == 0))
    def _():
        for h in range(2):
            kaug_ref[h, :, LANES:] = e_ref[h]
        _fill_value_ones(vaug_ref)

    k = k_ref[0]
    for h, qh in enumerate(_split_heads(q_ref[0])):
        kaug_ref[h, :, :LANES] = k
        qaug_ref[h, :, :LANES] = qh
    vaug_ref[:, :LANES] = v_ref[0]
    is_moba = p < MOBA_HEADS // 2

    @pl.when(is_moba)
    def _():
        pen = _moba_penalty(q_ref[0], _moba_block_means(k_ref)).astype(BF16)
        for h in range(2):
            qaug_ref[h, :, LANES:] = pen

    @pl.when(jnp.logical_not(is_moba))
    def _():
        for h in range(2):
            qaug_ref[h, :, LANES:] = jnp.zeros((S, LANES), BF16)
    _pair_attention(lambda h, r0, r1: qaug_ref[h, r0:r1, :],
                    lambda h, j: kaug_ref[h, j * TQ:(j + 1) * TQ, :],
                    lambda r0, r1: vaug_ref[r0:r1, :],
                    strip_ref, s_ref, p_ref, oh_ref, nblk, nblk)
    o_ref[0] = _merge_heads(oh_ref).astype(o_ref.dtype)


def _attn_scratch(S, span, narrow_span=0):
    nblk = S // TQ
    s_rows = max(_tile_rows(nblk, span), 2 * _tile_rows(nblk, narrow_span))
    return [pltpu.VMEM((2, S, TQ), F32), pltpu.VMEM((2, S, 2 * LANES), BF16), pltpu.VMEM((2, S, 2 * LANES), BF16),
            pltpu.VMEM((S, 2 * LANES), BF16), pltpu.VMEM((s_rows, TQ), F32),
            pltpu.VMEM((nblk, TQ, span * TQ), BF16), pltpu.VMEM((2, S, LANES), F32)]


def _attn0(qkv, r_tab, e_onehot):
    B, S, _ = qkv.shape
    nblk = S // TQ
    return pl.pallas_call(
        _attn0_kernel,
        grid=(N_PAIRS, B),
        in_specs=[
            pl.BlockSpec((1, S, LANES), lambda p, b: (b, 0, p)),
            pl.BlockSpec((1, S, LANES), lambda p, b: (b, 0, N_PAIRS + p)),
            pl.BlockSpec((1, S, LANES), lambda p, b: (b, 0, 2 * N_PAIRS + p)),
            pl.BlockSpec((1, 2, nblk, 2 * TQ), lambda p, b: (p, 0, 0, 0)),
            pl.BlockSpec((2, S, LANES), lambda p, b: (0, 0, 0), pipeline_mode=pl.Buffered(1)),
        ],
        out_specs=pl.BlockSpec((1, S, LANES), lambda p, b: (b, 0, p)),
        out_shape=jax.ShapeDtypeStruct((B, S, D_MODEL), BF16),
        scratch_shapes=_attn_scratch(S, nblk),
        compiler_params=pltpu.CompilerParams(dimension_semantics=("arbitrary",) * 2, vmem_limit_bytes=ATTN_VMEM_LIMIT),
        name="attn_moba_dilated",
    )(qkv, qkv, qkv, r_tab, e_onehot)


def _compress_kernel(x_ref, pos_ref, w1_ref, b1_ref, w2_ref, b2_ref, gain_ref, o_ref, *, normed):
    n_chunk = x_ref.shape[1] // NSA_CMP_STRIDE
    first = second = None
    for a in range(NSA_CMP_STRIDE):
        t = x_ref[0, pl.ds(a, n_chunk, stride=NSA_CMP_STRIDE), :]
        fa = jnp.dot((t + pos_ref[a:a + 1, :]).astype(BF16), w1_ref[a], preferred_element_type=F32)
        sa = jnp.dot((t + pos_ref[NSA_CMP_STRIDE + a:NSA_CMP_STRIDE + a + 1, :]).astype(BF16),
                     w1_ref[NSA_CMP_STRIDE + a], preferred_element_type=F32)
        first = fa if first is None else first + fa
        second = sa if second is None else second + sa
    hid = first + pltpu.roll(second, n_chunk - 1, 0) + b1_ref[...]
    cdf = 0.5 * (1.0 + jnp.tanh(math.sqrt(2.0 / math.pi) * (hid + 0.044715 * (hid * hid * hid))))
    y = jnp.dot((hid * cdf).astype(BF16), w2_ref[...], preferred_element_type=F32) + b2_ref[...]
    if normed:
        lo = _lane_lo(y.shape)
        y2 = y * y
        s_lo = jnp.sum(jnp.where(lo, y2, 0.0), axis=1, keepdims=True)
        s_hi = jnp.sum(jnp.where(lo, 0.0, y2), axis=1, keepdims=True)
        ms = jnp.where(lo, s_lo, s_hi) * (1.0 / HEAD_DIM)
        y = y * lax.rsqrt(ms + RMS_EPS) * gain_ref[...]
    o_ref[0, 0] = y.astype(o_ref.dtype)


def _pair_block_diag(w):
    z = jnp.zeros_like(w)
    return jnp.concatenate([jnp.concatenate([w, z], axis=-1), jnp.concatenate([z, w], axis=-1)], axis=-2)


def _compress(raw, col_block0, pos, w1, b1, w2, b2, gain, normed):
    B, S, _ = raw.shape
    hid = w1.shape[1]
    n_pos = pos.shape[0]
    w1bd = _pair_block_diag(w1.reshape(n_pos, HEAD_DIM, hid)).astype(BF16)
    return pl.pallas_call(
        functools.partial(_compress_kernel, normed=normed),
        grid=(B, 2),
        in_specs=[
            pl.BlockSpec((1, S, LANES), lambda b, m: (b, 0, col_block0 + m)),
            pl.BlockSpec((n_pos, LANES), lambda b, m: (0, 0)),
            pl.BlockSpec((n_pos, LANES, 2 * hid), lambda b, m: (0, 0, 0)),
            pl.BlockSpec((1, 2 * hid), lambda b, m: (0, 0)),
            pl.BlockSpec((2 * hid, LANES), lambda b, m: (0, 0)),
            pl.BlockSpec((1, LANES), lambda b, m: (0, 0)),
            pl.BlockSpec((1, LANES), lambda b, m: (0, 0)),
        ],
        out_specs=pl.BlockSpec((1, 1, S // NSA_CMP_STRIDE, LANES), lambda b, m: (b, m, 0, 0)),
        out_shape=jax.ShapeDtypeStruct((B, 2, S // NSA_CMP_STRIDE, LANES), BF16),
        compiler_params=_cparams(2),
        name="nsa_compress",
    )(raw, jnp.tile(pos, (1, 2)), w1bd, jnp.tile(b1, 2).reshape(1, 2 * hid), _pair_block_diag(w2).astype(BF16),
      jnp.tile(b2, 2).reshape(1, LANES), jnp.tile(gain, 2).reshape(1, LANES))


def _cmp_kernel(q_ref, kc_ref, vc_ref, bias_ref, ovt_ref, o_ref, pen_ref):
    kc = kc_ref[0, 0]
    vc = vc_ref[0, 0]
    S = q_ref.shape[1]
    n_sel = ovt_ref.shape[0]
    lo = _lane_lo((CMP_ROWS, LANES))
    blk = lax.broadcasted_iota(jnp.int32, (n_sel, CMP_ROWS), 0)
    for c in range(S // CMP_ROWS):
        rows = slice(c * CMP_ROWS, (c + 1) * CMP_ROWS)
        psum = [jnp.zeros((CMP_ROWS, LANES), F32), jnp.zeros((CMP_ROWS, LANES), F32)]
        for r in range(4):
            heads = _split_heads(q_ref[0, rows, r * LANES:(r + 1) * LANES])
            outs = []
            for h in range(2):
                s = lax.dot_general(heads[h], kc, NT_DIMS, preferred_element_type=F32) + bias_ref[0, 2 * r + h, rows, :]
                m = jnp.max(s, axis=1, keepdims=True)
                e = jnp.exp(s - m)
                l = jnp.sum(e, axis=1, keepdims=True)
                pr = e * jnp.where(m > 0.5 * NEG, 1.0 / l, 0.0)
                psum[h] = psum[h] + pr
                outs.append(jnp.dot(pr.astype(BF16), vc, preferred_element_type=F32))
            o_ref[0, rows, r * LANES:(r + 1) * LANES] = jnp.where(lo, outs[0], outs[1])

        t = c * CMP_ROWS + lax.broadcasted_iota(jnp.int32, (n_sel, CMP_ROWS), 1)
        cur = lax.shift_right_logical(t, int(math.log2(NSA_SEL_BLOCK)))
        keeps = []
        for h in range(2):
            imp = lax.dot_general(ovt_ref[...], psum[h], NT_DIMS,
                                  precision=lax.Precision.HIGHEST, preferred_element_type=F32)
            forced = jnp.where(blk == 0, 1.0, jnp.where(blk == cur, 1.0, jnp.where(blk == cur - 1, 1.0, 0.0)))
            imp = jnp.where(blk <= cur, imp + forced * NSA_FORCE, -jnp.inf)
            groups = [imp[g * SUBLANES:(g + 1) * SUBLANES, :] for g in range(n_sel // SUBLANES)]
            ranks = [jnp.zeros(g.shape, F32) for g in groups]
            for m in range(n_sel):
                im = imp[m:m + 1, :]
                for g, sub in enumerate(groups):
                    ge = jnp.where(im >= sub, 1.0, 0.0)
                    gt = jnp.where(im > sub, 1.0, 0.0)
                    if g * SUBLANES > m:
                        inc = ge
                    elif (g + 1) * SUBLANES - 1 <= m:
                        inc = gt
                    else:
                        inc = jnp.where(lax.broadcasted_iota(jnp.int32, sub.shape, 0) + g * SUBLANES > m, ge, gt)
                    ranks[g] = ranks[g] + inc
            keeps.append(jnp.where(jnp.concatenate(ranks, axis=0) < NSA_SEL_TOPN, 1.0, 0.0))
        keep = jnp.concatenate(keeps + [jnp.zeros((LANES - 2 * n_sel, CMP_ROWS), F32)], axis=0).T
        lane = lax.broadcasted_iota(jnp.int32, keep.shape, 1)
        pen_ref[0, 0, rows, :] = jnp.where(lane < 2 * n_sel, (keep - 1.0) * (-NEG), 0.0).astype(pen_ref.dtype)


def _cmp_attention(qkv, kc, vc, bias, ovt):
    B, S, _ = qkv.shape
    n_sel = ovt.shape[0]
    return pl.pallas_call(
        _cmp_kernel,
        grid=(2, B),
        in_specs=[
            pl.BlockSpec((1, S, 4 * LANES), lambda m, b: (b, 0, m)),
            pl.BlockSpec((1, 1, LANES, LANES), lambda m, b: (b, m, 0, 0)),
            pl.BlockSpec((1, 1, LANES, LANES), lambda m, b: (b, m, 0, 0)),
            pl.BlockSpec((1, 8, S, LANES), lambda m, b: (m, 0, 0, 0), pipeline_mode=pl.Buffered(1)),
            pl.BlockSpec((n_sel, LANES), lambda m, b: (0, 0)),
        ],
        out_specs=[
            pl.BlockSpec((1, S, 4 * LANES), lambda m, b: (b, 0, m)),
            pl.BlockSpec((1, 1, S, LANES), lambda m, b: (b, m, 0, 0)),
        ],
        out_shape=[jax.ShapeDtypeStruct((B, S, D_MODEL), F32),
                   jax.ShapeDtypeStruct((B, 2, S, LANES), BF16)],
        compiler_params=_cparams(2),
        name="nsa_compressed_select",
    )(qkv, kc, vc, bias, ovt)


def _attn1_kernel(q_ref, ks_ref, vs_ref, kw_ref, vw_ref, rs_ref, rw_ref, e_ref, pen_ref, ocmp_ref, graw_ref,
                  o_ref, strip_s_ref, qaug_ref, kaug_ref, vsaug_ref, s_ref, p_ref, oslc_ref,
                  strip_w_ref, vwaug_ref, owin_ref):
    p = pl.program_id(0)
    b = pl.program_id(1)
    S = q_ref.shape[1]
    nblk = S // TQ

    @pl.when(b == 0)
    def _():
        _fill_strips(strip_s_ref, rs_ref)
        _fill_strips(strip_w_ref, rw_ref)

    @pl.when(jnp.logical_and(p == 0, b == 0))
    def _():
        for h in range(2):
            kaug_ref[h, :, LANES:] = e_ref[h]
        _fill_value_ones(vsaug_ref)
        _fill_value_ones(vwaug_ref)

    ks = ks_ref[0]
    pen = pen_ref[0, 0]
    for h, qh in enumerate(_split_heads(q_ref[0])):
        kaug_ref[h, :, :LANES] = ks
        qaug_ref[h, :, :LANES] = qh
        qaug_ref[h, :, LANES:] = pen
    vsaug_ref[:, :LANES] = vs_ref[0]
    vwaug_ref[:, :LANES] = vw_ref[0]
    _pair_attention(lambda h, r0, r1: qaug_ref[h, r0:r1, :],
                    lambda h, j: kaug_ref[h, j * TQ:(j + 1) * TQ, :],
                    lambda r0, r1: vsaug_ref[r0:r1, :],
                    strip_s_ref, s_ref, p_ref, oslc_ref, nblk, nblk)
    _pair_attention(lambda h, r0, r1: qaug_ref[h, r0:r1, :LANES],
                    lambda h, j: kw_ref[0, j * TQ:(j + 1) * TQ, :],
                    lambda r0, r1: vwaug_ref[r0:r1, :],
                    strip_w_ref, s_ref, p_ref, owin_ref, nblk, strip_w_ref.shape[1] // TQ, skew=2)
    sig = _sigmoid(pltpu.roll(graw_ref[0], lax.rem(LANES - GATE_STRIDE * p, LANES), 1))
    lo_lanes = _lane_lo((S, LANES))
    gates = [jnp.where(lo_lanes, sig[:, 2 * br:2 * br + 1], sig[:, 2 * br + 1:2 * br + 2]) for br in range(3)]
    out = gates[0] * ocmp_ref[0] + gates[1] * _merge_heads(oslc_ref) + gates[2] * _merge_heads(owin_ref)
    o_ref[0] = out.astype(o_ref.dtype)


def _attn1(qkv, r_slc, r_win, e_onehot, pen, ocmp, graw):
    B, S, _ = qkv.shape
    kv0 = D_MODEL // LANES + 4
    n_win = r_win.shape[2]
    return pl.pallas_call(
        _attn1_kernel,
        grid=(N_PAIRS, B),
        in_specs=[
            pl.BlockSpec((1, S, LANES), lambda p, b: (b, 0, p)),
            pl.BlockSpec((1, S, LANES), lambda p, b: (b, 0, kv0 + p // 4)),
            pl.BlockSpec((1, S, LANES), lambda p, b: (b, 0, kv0 + 2 + p // 4)),
            pl.BlockSpec((1, S, LANES), lambda p, b: (b, 0, kv0 + 4 + p // 4)),
            pl.BlockSpec((1, S, LANES), lambda p, b: (b, 0, kv0 + 6 + p // 4)),
            pl.BlockSpec((1, 2, r_slc.shape[2], 2 * TQ), lambda p, b: (p, 0, 0, 0)),
            pl.BlockSpec((1, 2, n_win, 2 * TQ), lambda p, b: (p, 0, 0, 0)),
            pl.BlockSpec((2, S, LANES), lambda p, b: (0, 0, 0), pipeline_mode=pl.Buffered(1)),
            pl.BlockSpec((1, 1, S, LANES), lambda p, b: (b, p // 4, 0, 0)),
            pl.BlockSpec((1, S, LANES), lambda p, b: (b, 0, p)),
            pl.BlockSpec((1, S, LANES), lambda p, b: (b, 0, 0)),
        ],
        out_specs=pl.BlockSpec((1, S, LANES), lambda p, b: (b, 0, p)),
        out_shape=jax.ShapeDtypeStruct((B, S, D_MODEL), BF16),
        scratch_shapes=_attn_scratch(S, S // TQ, n_win) + [pltpu.VMEM((2, n_win * TQ, TQ), F32),
                                                    pltpu.VMEM((S, 2 * LANES), BF16), pltpu.VMEM((2, S, LANES), F32)],
        compiler_params=pltpu.CompilerParams(dimension_semantics=("arbitrary",) * 2, vmem_limit_bytes=ATTN_VMEM_LIMIT),
        name="attn_nsa",
    )(qkv, qkv, qkv, qkv, qkv, r_slc, r_win, e_onehot, pen, ocmp, graw)


def _bucket(dist):
    n = np.maximum(dist, 0)
    exact = REL_BUCKETS // 2
    nf = np.maximum(n, 1).astype(np.float64)
    large = exact + (np.log(nf / exact) / math.log(REL_MAX_DIST / exact) * (REL_BUCKETS - exact)).astype(np.int64)
    return np.where(n < exact, n, np.minimum(large, REL_BUCKETS - 1))


def _strip_rows(tab, n_off, extra_of_dist):
    u = np.arange(2 * TQ)
    d = np.arange(n_off)[:, None] * TQ - np.where(u < TQ, u, u - 2 * TQ)[None, :]
    onehot = (_bucket(d)[None] == np.arange(REL_BUCKETS)[:, None, None]).astype(np.float32)
    vals = jnp.einsum("hb,bou->hou", tab, jnp.asarray(onehot), precision=lax.Precision.HIGHEST) + extra_of_dist(d)
    return vals.reshape(tab.shape[0] // 2, 2, n_off, 2 * TQ)


def _causal_mask(d):
    return np.where(d >= 0, 0.0, NEG).astype(np.float32)


def _window_mask(d):
    return np.where((d >= 0) & (d < NSA_WINDOW), 0.0, NEG).astype(np.float32)


def _dilation_log_count(d):
    c = ((d >= 0) & (d <= 128)).astype(np.float64)
    c += ((d >= 0) & (d % 4 == 0) & (d <= 512))
    c += ((d >= 0) & (d % 16 == 0) & (d <= 2048))
    return np.where(c > 0, np.log(np.maximum(c, 1.0)), NEG).astype(np.float32)


def _one_hot_blocks(S, block, per_head):
    e = np.zeros((2, S, LANES), np.float32)
    key = np.arange(S)
    for h in range(2):
        e[h, key, h * per_head + key // block] = 1.0
    return jnp.asarray(e, BF16)


_NSA_HEAD_ORDER = np.array([8 * (p // 4) + (p % 4) + 4 * h for p in range(N_PAIRS) for h in range(2)])


def _head_cols(heads):
    return (np.asarray(heads)[:, None] * HEAD_DIM + np.arange(HEAD_DIM)[None, :]).reshape(-1)


def _mixer_ab(h, g_mix, w_in, qn_a, kn_a, qn_b, kn_b, rel_bias, B, S):
    wa = MOBA_HEADS * HEAD_DIM
    ones = jnp.ones((wa,), F32)
    nh = MOBA_HEADS
    gain = jnp.concatenate([jnp.tile(qn_a, nh) * ATTN_SCALE, jnp.tile(qn_b, nh) * ATTN_SCALE,
                            jnp.tile(kn_a, nh), jnp.tile(kn_b, nh), ones, ones])
    per_sec = wa // PROJ_CHUNK
    src = [sec * per_sec + c for sec in (0, 3, 1, 4, 2, 5) for c in range(per_sec)]
    (qkv,) = _project(h, g_mix, w_in.astype(BF16), gain, [True] * (4 * per_sec) + [False] * (2 * per_sec),
                      src_chunks=src)
    qkv = qkv.reshape(B, S, 3 * D_MODEL)

    tab = rel_bias.T
    r_tab = jnp.concatenate([_strip_rows(tab[:MOBA_HEADS], S // TQ, _causal_mask),
                             _strip_rows(tab[MOBA_HEADS:], S // TQ, _dilation_log_count)])
    o = _attn0(qkv, r_tab, _one_hot_blocks(S, MOBA_BLOCK, S // MOBA_BLOCK))
    return o.reshape(B * S, D_MODEL)


def _mixer_nsa(h, g_mix, w_in, qn, kn_c, kn_s, kn_w, cmp_k, cmp_v, rel_bias, B, S):
    qw = N_HEADS * HEAD_DIM
    kvw = NSA_GROUPS * HEAD_DIM
    order = _NSA_HEAD_ORDER
    w_main = jnp.concatenate([w_in[:, _head_cols(order)], w_in[:, qw:qw + 6 * kvw]], axis=1).astype(BF16)
    ones = jnp.ones((kvw,), F32)
    gain = jnp.concatenate([jnp.tile(qn, N_HEADS) * ATTN_SCALE, ones, ones,
                            jnp.tile(kn_s, NSA_GROUPS), ones, jnp.tile(kn_w, NSA_GROUPS), ones])
    assert kvw == PROJ_CHUNK
    norm_chunks = [True] * (qw // PROJ_CHUNK) + [False, False, True, False, True, False]

    gcols = np.zeros((LANES,), np.int64)
    gused = np.zeros((LANES,), np.float32)
    for p in range(N_PAIRS):
        for br in range(3):
            for hh in range(2):
                c = GATE_STRIDE * p + 2 * br + hh
                gcols[c] = qw + 6 * kvw + 3 * order[2 * p + hh] + br
                gused[c] = 1.0
    w_gate = (w_in[:, gcols] * gused).astype(BF16)
    kc_chunk = qw // PROJ_CHUNK
    qkv, graw, raw = _project(h, g_mix, w_main, gain, norm_chunks, w_gate, raw_chunks=(kc_chunk, kc_chunk + 1))
    qkv = qkv.reshape(B, S, qw + 6 * kvw)
    graw = graw.reshape(B, S, LANES)
    raw = raw.reshape(B, S, 2 * kvw)
    kc = _compress(raw, 0, *cmp_k, kn_c, True)
    vc = _compress(raw, kvw // LANES, *cmp_v, kn_c, False)

    tab = rel_bias.T[order]
    n_cmp = (S - NSA_CMP_LEN) // NSA_CMP_STRIDE + 1
    t_pos = np.arange(S)[:, None]
    c_idx = np.arange(LANES)[None, :]
    dc = t_pos - (c_idx * NSA_CMP_STRIDE + NSA_CMP_LEN - 1)
    cmp_mask = np.where((dc >= 0) & (c_idx < n_cmp), 0.0, NEG).astype(np.float32)
    onehot = jnp.asarray(_bucket(dc).astype(np.int8))[None] == jnp.arange(REL_BUCKETS, dtype=jnp.int8)[:, None, None]
    bias_c = jnp.einsum("hb,bsc->hsc", tab, onehot.astype(F32), precision=lax.Precision.HIGHEST)
    bias_c = (bias_c + cmp_mask).reshape(2, 8, S, LANES)
    n_sel = S // NSA_SEL_BLOCK
    cstart = np.arange(LANES) * NSA_CMP_STRIDE
    sstart = np.arange(n_sel) * NSA_SEL_BLOCK
    ovt = np.maximum(np.minimum(cstart[None, :] + NSA_CMP_LEN, sstart[:, None] + NSA_SEL_BLOCK)
                     - np.maximum(cstart[None, :], sstart[:, None]), 0).astype(np.float32)
    ovt[:, n_cmp:] = 0.0
    ocmp, pen = _cmp_attention(qkv, kc, vc, bias_c, jnp.asarray(ovt))

    r_slc = _strip_rows(tab, S // TQ, _causal_mask)
    r_win = _strip_rows(tab, NSA_WINDOW // TQ + 1, _window_mask)
    o = _attn1(qkv, r_slc, r_win, _one_hot_blocks(S, NSA_SEL_BLOCK, n_sel), pen, ocmp, graw)
    return o.reshape(B * S, D_MODEL)


def kernel(x, p, rel_bias, norm_mix, norm_ffn, norm_ple, w_ffn_gate, w_ffn_up, w_ffn_down, w_ple_proj, w_ple_gate, w_in_ab, w_out_ab, qn_moba, kn_moba, qn_dil, kn_dil, w_in_nsa, w_out_nsa, qn_nsa, kn_cmp, kn_slc, kn_win, cmp_k_pos, cmp_k_w1, cmp_k_b1, cmp_k_w2, cmp_k_b2, cmp_v_pos, cmp_v_w1, cmp_v_b1, cmp_v_w2, cmp_v_b2):
    B, S, D = x.shape
    depth = p.shape[0]
    assert D == D_MODEL and S == MOBA_NBLK * MOBA_BLOCK and (B * S) % ROW_TILE == 0, (B, S, D)
    h = x.reshape(B * S, D)
    wg, wu, wd, wpg = (_to_bf16(w) for w in (w_ffn_gate, w_ffn_up, w_ffn_down, w_ple_gate))
    for i in range(depth):
        e = i // 2
        if i % 2 == 0:
            o = _mixer_ab(h, norm_mix[i], w_in_ab[e], qn_moba[e], kn_moba[e], qn_dil[e], kn_dil[e], rel_bias, B, S)
            w_out = w_out_ab[e]
        else:
            cmp_k = (cmp_k_pos[e], cmp_k_w1[e], cmp_k_b1[e], cmp_k_w2[e], cmp_k_b2[e])
            cmp_v = (cmp_v_pos[e], cmp_v_w1[e], cmp_v_b1[e], cmp_v_w2[e], cmp_v_b2[e])
            o = _mixer_nsa(h, norm_mix[i], w_in_nsa[e], qn_nsa[e], kn_cmp[e], kn_slc[e], kn_win[e],
                           cmp_k, cmp_v, rel_bias, B, S)
            w_out = w_out_nsa[e][_head_cols(_NSA_HEAD_ORDER), :]
        h = _post_attention(o, h, w_out.astype(BF16), norm_ffn[i], wg[i], wu[i], wd[i], norm_ple[i],
                            wpg[i], p[i].reshape(B * S, -1), w_ple_proj[i].astype(BF16))
    return h.reshape(B, S, D)
```

```python
import functools
import math

import numpy as np
import jax
import jax.numpy as jnp
from jax import lax
from jax.experimental import pallas as pl
from jax.experimental.pallas import tpu as pltpu

F32 = jnp.float32
BF16 = jnp.bfloat16

D_MODEL = 1024
HEAD_DIM = 64
N_HEADS = 16
N_PAIRS = N_HEADS // 2
MOBA_HEADS = 8
MOBA_BLOCK = 256
MOBA_TOPK = 3
MOBA_NBLK = 8
NSA_GROUPS = 4
NSA_CMP_LEN = 32
NSA_CMP_STRIDE = 16
NSA_SEL_BLOCK = 64
NSA_SEL_TOPN = 16
NSA_WINDOW = 512
NSA_FORCE = 1.0e6
REL_BUCKETS = 32
REL_MAX_DIST = 2048
RMS_EPS = 1e-6
ATTN_SCALE = HEAD_DIM ** -0.5

LANES = 128
SUBLANES = 8
TQ = 256
NEG = -1.0e30
ROW_TILE = 512
PROJ_CHUNK = 256
POST_SPLIT = 2
GATE_STRIDE = LANES // N_HEADS
CMP_ROWS = 2048
VMEM_LIMIT = 48 * 1024 * 1024
POST_VMEM_LIMIT = 56 * 1024 * 1024
ATTN_VMEM_LIMIT = 56 * 1024 * 1024

NT_DIMS = (((1,), (1,)), ((), ()))


def _cparams(n_axes):
    return pltpu.CompilerParams(dimension_semantics=("arbitrary",) * n_axes,
                                vmem_limit_bytes=VMEM_LIMIT)


def _rms_rows(x, g):
    ms = jnp.mean(x * x, axis=-1, keepdims=True)
    return x * lax.rsqrt(ms + RMS_EPS) * g


def _lane_lo(shape):
    return lax.broadcasted_iota(jnp.int32, shape, len(shape) - 1) < HEAD_DIM


def _resident(shape):
    return pl.BlockSpec(shape, lambda i: (0,) * len(shape), pipeline_mode=pl.Buffered(1))


def _head_sum_squares(y):
    y2 = y * y
    parts = []
    for blk in range(y.shape[1] // LANES):
        yb = y2[:, blk * LANES:(blk + 1) * LANES]
        lo = _lane_lo(yb.shape)
        s_lo = jnp.sum(jnp.where(lo, yb, 0.0), axis=1, keepdims=True)
        s_hi = jnp.sum(jnp.where(lo, 0.0, yb), axis=1, keepdims=True)
        parts.append(jnp.where(lo, s_lo, s_hi))
    return jnp.concatenate(parts, axis=1)


def _proj_kernel(x_ref, g_ref, w_ref, cg_ref, *rest, norm_chunks, src_chunks, raw_chunks, with_extra):
    if with_extra:
        wx_ref, o_ref, ox_ref, oraw_ref = rest
    else:
        (o_ref,) = rest
    xn = _rms_rows(x_ref[...], g_ref[...]).astype(BF16)

    def finish(c, y):
        cols = slice(c * PROJ_CHUNK, (c + 1) * PROJ_CHUNK)
        if c in raw_chunks:
            k = raw_chunks.index(c)
            oraw_ref[:, k * PROJ_CHUNK:(k + 1) * PROJ_CHUNK] = y
        if norm_chunks[c]:
            y = y * lax.rsqrt(_head_sum_squares(y) * (1.0 / HEAD_DIM) + RMS_EPS) * cg_ref[:, cols]
        o_ref[:, cols] = y.astype(o_ref.dtype)

    prev = None
    for c in range(len(norm_chunks)):
        src = src_chunks[c] * PROJ_CHUNK
        y = jnp.dot(xn, w_ref[:, src:src + PROJ_CHUNK], preferred_element_type=F32)
        if prev is not None:
            finish(c - 1, prev)
        prev = y
    finish(len(norm_chunks) - 1, prev)
    if with_extra:
        ox_ref[...] = jnp.dot(xn, wx_ref[...], preferred_element_type=F32)


def _project(x, g, w, col_gain, norm_chunks, w_extra=None, raw_chunks=(), src_chunks=None):
    if src_chunks is None:
        src_chunks = range(len(norm_chunks))
    T, D = x.shape
    N = w.shape[1]
    row_in = lambda n: pl.BlockSpec((ROW_TILE, n), lambda i: (i, 0))
    in_specs = [row_in(D), _resident((1, D)), _resident((D, N)), _resident((1, N))]
    args = [x, g.reshape(1, D), w, col_gain.reshape(1, N)]
    out_specs = [row_in(N)]
    out_shape = [jax.ShapeDtypeStruct((T, N), BF16)]
    if w_extra is not None:
        nx = w_extra.shape[1]
        in_specs.append(_resident((D, nx)))
        args.append(w_extra)
        out_specs += [row_in(nx), row_in(len(raw_chunks) * PROJ_CHUNK)]
        out_shape += [jax.ShapeDtypeStruct((T, nx), F32), jax.ShapeDtypeStruct((T, len(raw_chunks) * PROJ_CHUNK), F32)]
    return pl.pallas_call(
        functools.partial(_proj_kernel, norm_chunks=tuple(norm_chunks), src_chunks=tuple(src_chunks),
                          raw_chunks=tuple(raw_chunks),
                          with_extra=w_extra is not None),
        grid=(T // ROW_TILE,),
        in_specs=in_specs,
        out_specs=out_specs,
        out_shape=out_shape,
        compiler_params=_cparams(1),
        name="proj",
    )(*args)


def _sigmoid(z):
    return 1.0 / (1.0 + jnp.exp(-z))


def _post_kernel(o_ref, h_ref, wout_ref, gf_ref, wg_ref, wu_ref, wd_ref, gp_ref, wpg_ref, p_ref, wpp_ref, out_ref):
    halves = [slice(r * ROW_TILE // POST_SPLIT, (r + 1) * ROW_TILE // POST_SPLIT) for r in range(POST_SPLIT)]
    h1 = [h_ref[r, :] + jnp.dot(o_ref[r, :], wout_ref[...], preferred_element_type=F32) for r in halves]
    xn = [_rms_rows(x, gf_ref[...]).astype(BF16) for x in h1]
    a = [jnp.dot(x, wg_ref[...], preferred_element_type=F32) for x in xn]
    u = [jnp.dot(x, wu_ref[...], preferred_element_type=F32) for x in xn]
    act = [(ai * _sigmoid(ai) * ui).astype(BF16) for ai, ui in zip(a, u)]
    h2 = [x + jnp.dot(t, wd_ref[...], preferred_element_type=F32) for x, t in zip(h1, act)]
    hn = [_rms_rows(x, gp_ref[...]).astype(BF16) for x in h2]
    gate = [_sigmoid(jnp.dot(x, wpg_ref[...], preferred_element_type=F32)) for x in hn]
    for r, x, g in zip(halves, h2, gate):
        out_ref[r, :] = x + g * jnp.dot(p_ref[r, :].astype(BF16), wpp_ref[...], preferred_element_type=F32)


def _post_attention(o, h, w_out, g_ffn, wg, wu, wd, g_ple, w_pgate, p, w_pproj):
    T, D = h.shape
    Fh = wg.shape[1]
    Pd = p.shape[1]
    row_in = lambda n: pl.BlockSpec((ROW_TILE, n), lambda i: (i, 0))
    return pl.pallas_call(
        _post_kernel,
        grid=(T // ROW_TILE,),
        in_specs=[row_in(D), row_in(D), _resident((D, D)), _resident((1, D)), _resident((D, Fh)),
                  _resident((D, Fh)), _resident((Fh, D)), _resident((1, D)), _resident((D, D)),
                  row_in(Pd), _resident((Pd, D))],
        out_specs=row_in(D),
        out_shape=jax.ShapeDtypeStruct((T, D), F32),
        compiler_params=pltpu.CompilerParams(dimension_semantics=("arbitrary",), vmem_limit_bytes=POST_VMEM_LIMIT),
        name="post_attention",
    )(o, h, w_out, g_ffn.reshape(1, D), wg, wu, wd, g_ple.reshape(1, D), w_pgate, p, w_pproj)


def _fill_strips(strip_ref, r_ref):
    for h in range(2):
        for o in range(r_ref.shape[2]):
            x = jnp.broadcast_to(r_ref[0, h, o:o + 1, :], (TQ, 2 * TQ))
            strip_ref[h, o * TQ:(o + 1) * TQ, :] = pltpu.roll(x, 0, 1, stride=1, stride_axis=0)[:, :TQ]


def _tile_groups(nblk, span):
    cnt = [min(span, nblk - j) for j in range(nblk)]
    off = [0]
    for c in cnt:
        off.append(off[-1] + c * TQ)
    return cnt, off


def _tile_rows(nblk, span):
    return _tile_groups(nblk, span)[1][-1]


def _pair_attention(q_rows, k_rows, v_rows, strip_ref, s_ref, p_ref, o_ref, nblk, span, skew=1):
    cnt, off = _tile_groups(nblk, span)
    together = s_ref.shape[0] >= 2 * off[-1] and p_ref.shape[2] >= 2 * span * TQ
    s0 = [0, off[-1] if together else 0]
    p0 = [0, span * TQ if together else 0]

    def scores(h, j):
        n = cnt[j] * TQ
        s = lax.dot_general(q_rows(h, j * TQ, j * TQ + n), k_rows(h, j), NT_DIMS, preferred_element_type=F32)
        s_ref[s0[h] + off[j]:s0[h] + off[j] + n, :] = s + strip_ref[h, 0:n, :]

    def probs(h, i):
        j0 = max(0, i - span + 1)
        rows = [s0[h] + off[j] + (i - j) * TQ for j in range(j0, i + 1)]
        mx = s_ref[rows[0]:rows[0] + TQ, :]
        for r in rows[1:]:
            mx = jnp.maximum(mx, s_ref[r:r + TQ, :])
        m = jnp.max(mx, axis=1, keepdims=True)
        for t, r in enumerate(rows):
            p_ref[i, :, p0[h] + t * TQ:p0[h] + (t + 1) * TQ] = jnp.exp(s_ref[r:r + TQ, :] - m).astype(BF16)

    def values(h, i):
        j0 = max(0, i - span + 1)
        out = jnp.dot(p_ref[i, :, p0[h]:p0[h] + (i + 1 - j0) * TQ], v_rows(j0 * TQ, (i + 1) * TQ),
                      preferred_element_type=F32)
        o_ref[h, i * TQ:(i + 1) * TQ, :] = out[:, :LANES] / out[:, LANES:]

    for heads in ([(0, 1)] if together else [(0,), (1,)]):
        for j in range(nblk + skew):
            for h in heads:
                if j < nblk:
                    scores(h, j)
            for h in heads:
                if j >= skew:
                    values(h, j - skew)
            for h in heads:
                if j < nblk:
                    probs(h, j)


def _merge_heads(o_ref):
    return jnp.where(_lane_lo((o_ref.shape[1], LANES)), o_ref[0], o_ref[1])


def _split_heads(q):
    lo = _lane_lo(q.shape)
    zero = jnp.zeros_like(q)
    return jnp.where(lo, q, zero), jnp.where(lo, zero, q)


def _fill_value_ones(vaug_ref):
    vaug_ref[:, LANES:] = jnp.ones((vaug_ref.shape[0], LANES), vaug_ref.dtype)


def _moba_block_means(k_ref):
    shape = (2 * MOBA_NBLK, LANES)
    lo = _lane_lo(shape)
    row = lax.broadcasted_iota(jnp.int32, shape, 0)
    kmt = jnp.zeros(shape, F32)
    for n in range(MOBA_NBLK):
        mean = jnp.mean(k_ref[0, n * MOBA_BLOCK:(n + 1) * MOBA_BLOCK, :].astype(F32), axis=0, keepdims=True)
        kmt = jnp.where(row == n, jnp.where(lo, mean, 0.0), kmt)
        kmt = jnp.where(row == MOBA_NBLK + n, jnp.where(lo, 0.0, mean), kmt)
    return kmt


def _moba_penalty(q, kmt):
    nb = MOBA_NBLK
    S = q.shape[0]
    hi = kmt.astype(BF16)
    rest = kmt - hi.astype(F32)
    mid = rest.astype(BF16)
    lo = (rest - mid.astype(F32)).astype(BF16)
    g3 = lax.dot_general(jnp.concatenate([hi, mid, lo], axis=0), q, NT_DIMS, preferred_element_type=F32)
    g = g3[0:2 * nb] + g3[2 * nb:4 * nb] + g3[4 * nb:6 * nb]
    row = lax.broadcasted_iota(jnp.int32, g.shape, 0)
    n = row & (nb - 1)
    own = lax.shift_right_logical(lax.broadcasted_iota(jnp.int32, g.shape, 1), int(math.log2(MOBA_BLOCK)))
    rank = jnp.zeros(g.shape, F32)
    for m in range(nb - 1):
        gm = jnp.where(row < nb, g[m:m + 1, :], g[nb + m:nb + m + 1, :])
        tie = jnp.where(n > m, 1.0, 0.0)
        beats = jnp.where(gm > g, 1.0, jnp.where(gm == g, tie, 0.0))
        rank = rank + jnp.where(own > m, beats, 0.0)
    keep = jnp.where(n < own, jnp.where(rank < MOBA_TOPK, 1.0, 0.0), jnp.where(n == own, 1.0, 0.0))
    keep = jnp.concatenate([keep, jnp.zeros((LANES - 2 * nb, S), F32)], axis=0).T
    lane = lax.broadcasted_iota(jnp.int32, keep.shape, 1)
    return jnp.where(lane < 2 * nb, (keep - 1.0) * (-NEG), 0.0)


def _attn0_kernel(q_ref, k_ref, v_ref, r_ref, e_ref, o_ref, strip_ref, qaug_ref, kaug_ref, vaug_ref,
                  s_ref, p_ref, oh_ref):
    p = pl.program_id(0)
    b = pl.program_id(1)
    S = q_ref.shape[1]
    nblk = S // TQ

    @pl.when(b == 0)
    def _():
        _fill_strips(strip_ref, r_ref)

    @pl.when(jnp.logical_and(p == 0, b == 0))
    def _():
        for h in range(2):
            kaug_ref[h, :, LANES:] = e_ref[h]
        _fill_value_ones(vaug_ref)

    k = k_ref[0]
    for h, qh in enumerate(_split_heads(q_ref[0])):
        kaug_ref[h, :, :LANES] = k
        qaug_ref[h, :, :LANES] = qh
    vaug_ref[:, :LANES] = v_ref[0]
    is_moba = p < MOBA_HEADS // 2

    @pl.when(is_moba)
    def _():
        pen = _moba_penalty(q_ref[0], _moba_block_means(k_ref)).astype(BF16)
        for h in range(2):
            qaug_ref[h, :, LANES:] = pen

    @pl.when(jnp.logical_not(is_moba))
    def _():
        for h in range(2):
            qaug_ref[h, :, LANES:] = jnp.zeros((S, LANES), BF16)
    _pair_attention(lambda h, r0, r1: qaug_ref[h, r0:r1, :],
                    lambda h, j: kaug_ref[h, j * TQ:(j + 1) * TQ, :],
                    lambda r0, r1: vaug_ref[r0:r1, :],
                    strip_ref, s_ref, p_ref, oh_ref, nblk, nblk)
    o_ref[0] = _merge_heads(oh_ref).astype(o_ref.dtype)


def _attn_scratch(S, span, narrow_span=0):
    nblk = S // TQ
    s_rows = max(_tile_rows(nblk, span), 2 * _tile_rows(nblk, narrow_span))
    return [pltpu.VMEM((2, S, TQ), F32), pltpu.VMEM((2, S, 2 * LANES), BF16), pltpu.VMEM((2, S, 2 * LANES), BF16),
            pltpu.VMEM((S, 2 * LANES), BF16), pltpu.VMEM((s_rows, TQ), F32),
            pltpu.VMEM((nblk, TQ, span * TQ), BF16), pltpu.VMEM((2, S, LANES), F32)]


def _attn0(qkv, r_tab, e_onehot):
    B, S, _ = qkv.shape
    nblk = S // TQ
    return pl.pallas_call(
        _attn0_kernel,
        grid=(N_PAIRS, B),
        in_specs=[
            pl.BlockSpec((1, S, LANES), lambda p, b: (b, 0, p)),
            pl.BlockSpec((1, S, LANES), lambda p, b: (b, 0, N_PAIRS + p)),
            pl.BlockSpec((1, S, LANES), lambda p, b: (b, 0, 2 * N_PAIRS + p)),
            pl.BlockSpec((1, 2, nblk, 2 * TQ), lambda p, b: (p, 0, 0, 0)),
            pl.BlockSpec((2, S, LANES), lambda p, b: (0, 0, 0), pipeline_mode=pl.Buffered(1)),
        ],
        out_specs=pl.BlockSpec((1, S, LANES), lambda p, b: (b, 0, p)),
        out_shape=jax.ShapeDtypeStruct((B, S, D_MODEL), BF16),
        scratch_shapes=_attn_scratch(S, nblk),
        compiler_params=pltpu.CompilerParams(dimension_semantics=("arbitrary",) * 2, vmem_limit_bytes=ATTN_VMEM_LIMIT),
        name="attn_moba_dilated",
    )(qkv, qkv, qkv, r_tab, e_onehot)


def _compress_kernel(x_ref, pos_ref, w1_ref, b1_ref, w2_ref, b2_ref, gain_ref, o_ref, *, normed):
    n_chunk = x_ref.shape[1] // NSA_CMP_STRIDE
    first = second = None
    for a in range(NSA_CMP_STRIDE):
        t = x_ref[0, pl.ds(a, n_chunk, stride=NSA_CMP_STRIDE), :]
        fa = jnp.dot((t + pos_ref[a:a + 1, :]).astype(BF16), w1_ref[a], preferred_element_type=F32)
        sa = jnp.dot((t + pos_ref[NSA_CMP_STRIDE + a:NSA_CMP_STRIDE + a + 1, :]).astype(BF16),
                     w1_ref[NSA_CMP_STRIDE + a], preferred_element_type=F32)
        first = fa if first is None else first + fa
        second = sa if second is None else second + sa
    hid = first + pltpu.roll(second, n_chunk - 1, 0) + b1_ref[...]
    cdf = 0.5 * (1.0 + jnp.tanh(math.sqrt(2.0 / math.pi) * (hid + 0.044715 * (hid * hid * hid))))
    y = jnp.dot((hid * cdf).astype(BF16), w2_ref[...], preferred_element_type=F32) + b2_ref[...]
    if normed:
        lo = _lane_lo(y.shape)
        y2 = y * y
        s_lo = jnp.sum(jnp.where(lo, y2, 0.0), axis=1, keepdims=True)
        s_hi = jnp.sum(jnp.where(lo, 0.0, y2), axis=1, keepdims=True)
        ms = jnp.where(lo, s_lo, s_hi) * (1.0 / HEAD_DIM)
        y = y * lax.rsqrt(ms + RMS_EPS) * gain_ref[...]
    o_ref[0, 0] = y.astype(o_ref.dtype)


def _pair_block_diag(w):
    z = jnp.zeros_like(w)
    return jnp.concatenate([jnp.concatenate([w, z], axis=-1), jnp.concatenate([z, w], axis=-1)], axis=-2)


def _compress(raw, col_block0, pos, w1, b1, w2, b2, gain, normed):
    B, S, _ = raw.shape
    hid = w1.shape[1]
    n_pos = pos.shape[0]
    w1bd = _pair_block_diag(w1.reshape(n_pos, HEAD_DIM, hid)).astype(BF16)
    return pl.pallas_call(
        functools.partial(_compress_kernel, normed=normed),
        grid=(B, 2),
        in_specs=[
            pl.BlockSpec((1, S, LANES), lambda b, m: (b, 0, col_block0 + m)),
            pl.BlockSpec((n_pos, LANES), lambda b, m: (0, 0)),
            pl.BlockSpec((n_pos, LANES, 2 * hid), lambda b, m: (0, 0, 0)),
            pl.BlockSpec((1, 2 * hid), lambda b, m: (0, 0)),
            pl.BlockSpec((2 * hid, LANES), lambda b, m: (0, 0)),
            pl.BlockSpec((1, LANES), lambda b, m: (0, 0)),
            pl.BlockSpec((1, LANES), lambda b, m: (0, 0)),
        ],
        out_specs=pl.BlockSpec((1, 1, S // NSA_CMP_STRIDE, LANES), lambda b, m: (b, m, 0, 0)),
        out_shape=jax.ShapeDtypeStruct((B, 2, S // NSA_CMP_STRIDE, LANES), BF16),
        compiler_params=_cparams(2),
        name="nsa_compress",
    )(raw, jnp.tile(pos, (1, 2)), w1bd, jnp.tile(b1, 2).reshape(1, 2 * hid), _pair_block_diag(w2).astype(BF16),
      jnp.tile(b2, 2).reshape(1, LANES), jnp.tile(gain, 2).reshape(1, LANES))


def _cmp_kernel(q_ref, kc_ref, vc_ref, bias_ref, ovt_ref, o_ref, pen_ref):
    kc = kc_ref[0, 0]
    vc = vc_ref[0, 0]
    S = q_ref.shape[1]
    n_sel = ovt_ref.shape[0]
    lo = _lane_lo((CMP_ROWS, LANES))
    blk = lax.broadcasted_iota(jnp.int32, (n_sel, CMP_ROWS), 0)
    for c in range(S // CMP_ROWS):
        rows = slice(c * CMP_ROWS, (c + 1) * CMP_ROWS)
        psum = [jnp.zeros((CMP_ROWS, LANES), F32), jnp.zeros((CMP_ROWS, LANES), F32)]
        for r in range(4):
            heads = _split_heads(q_ref[0, rows, r * LANES:(r + 1) * LANES])
            outs = []
            for h in range(2):
                s = lax.dot_general(heads[h], kc, NT_DIMS, preferred_element_type=F32) + bias_ref[0, 2 * r + h, rows, :]
                m = jnp.max(s, axis=1, keepdims=True)
                e = jnp.exp(s - m)
                l = jnp.sum(e, axis=1, keepdims=True)
                pr = e * jnp.where(m > 0.5 * NEG, 1.0 / l, 0.0)
                psum[h] = psum[h] + pr
                outs.append(jnp.dot(pr.astype(BF16), vc, preferred_element_type=F32))
            o_ref[0, rows, r * LANES:(r + 1) * LANES] = jnp.where(lo, outs[0], outs[1])

        t = c * CMP_ROWS + lax.broadcasted_iota(jnp.int32, (n_sel, CMP_ROWS), 1)
        cur = lax.shift_right_logical(t, int(math.log2(NSA_SEL_BLOCK)))
        keeps = []
        for h in range(2):
            imp = lax.dot_general(ovt_ref[...], psum[h], NT_DIMS,
                                  precision=lax.Precision.HIGHEST, preferred_element_type=F32)
            forced = jnp.where(blk == 0, 1.0, jnp.where(blk == cur, 1.0, jnp.where(blk == cur - 1, 1.0, 0.0)))
            imp = jnp.where(blk <= cur, imp + forced * NSA_FORCE, -jnp.inf)
            groups = [imp[g * SUBLANES:(g + 1) * SUBLANES, :] for g in range(n_sel // SUBLANES)]
            ranks = [jnp.zeros(g.shape, F32) for g in groups]
            for m in range(n_sel):
                im = imp[m:m + 1, :]
                for g, sub in enumerate(groups):
                    ge = jnp.where(im >= sub, 1.0, 0.0)
                    gt = jnp.where(im > sub, 1.0, 0.0)
                    if g * SUBLANES > m:
                        inc = ge
                    elif (g + 1) * SUBLANES - 1 <= m:
                        inc = gt
                    else:
                        inc = jnp.where(lax.broadcasted_iota(jnp.int32, sub.shape, 0) + g * SUBLANES > m, ge, gt)
                    ranks[g] = ranks[g] + inc
            keeps.append(jnp.where(jnp.concatenate(ranks, axis=0) < NSA_SEL_TOPN, 1.0, 0.0))
        keep = jnp.concatenate(keeps + [jnp.zeros((LANES - 2 * n_sel, CMP_ROWS), F32)], axis=0).T
        lane = lax.broadcasted_iota(jnp.int32, keep.shape, 1)
        pen_ref[0, 0, rows, :] = jnp.where(lane < 2 * n_sel, (keep - 1.0) * (-NEG), 0.0).astype(pen_ref.dtype)


def _cmp_attention(qkv, kc, vc, bias, ovt):
    B, S, _ = qkv.shape
    n_sel = ovt.shape[0]
    return pl.pallas_call(
        _cmp_kernel,
        grid=(2, B),
        in_specs=[
            pl.BlockSpec((1, S, 4 * LANES), lambda m, b: (b, 0, m)),
            pl.BlockSpec((1, 1, LANES, LANES), lambda m, b: (b, m, 0, 0)),
            pl.BlockSpec((1, 1, LANES, LANES), lambda m, b: (b, m, 0, 0)),
            pl.BlockSpec((1, 8, S, LANES), lambda m, b: (m, 0, 0, 0), pipeline_mode=pl.Buffered(1)),
            pl.BlockSpec((n_sel, LANES), lambda m, b: (0, 0)),
        ],
        out_specs=[
            pl.BlockSpec((1, S, 4 * LANES), lambda m, b: (b, 0, m)),
            pl.BlockSpec((1, 1, S, LANES), lambda m, b: (b, m, 0, 0)),
        ],
        out_shape=[jax.ShapeDtypeStruct((B, S, D_MODEL), F32),
                   jax.ShapeDtypeStruct((B, 2, S, LANES), BF16)],
        compiler_params=_cparams(2),
        name="nsa_compressed_select",
    )(qkv, kc, vc, bias, ovt)


def _attn1_kernel(q_ref, ks_ref, vs_ref, kw_ref, vw_ref, rs_ref, rw_ref, e_ref, pen_ref, ocmp_ref, graw_ref,
                  o_ref, strip_s_ref, qaug_ref, kaug_ref, vsaug_ref, s_ref, p_ref, oslc_ref,
                  strip_w_ref, vwaug_ref, owin_ref):
    p = pl.program_id(0)
    b = pl.program_id(1)
    S = q_ref.shape[1]
    nblk = S // TQ

    @pl.when(b == 0)
    def _():
        _fill_strips(strip_s_ref, rs_ref)
        _fill_strips(strip_w_ref, rw_ref)

    @pl.when(jnp.logical_and(p == 0, b == 0))
    def _():
        for h in range(2):
            kaug_ref[h, :, LANES:] = e_ref[h]
        _fill_value_ones(vsaug_ref)
        _fill_value_ones(vwaug_ref)

    ks = ks_ref[0]
    pen = pen_ref[0, 0]
    for h, qh in enumerate(_split_heads(q_ref[0])):
        kaug_ref[h, :, :LANES] = ks
        qaug_ref[h, :, :LANES] = qh
        qaug_ref[h, :, LANES:] = pen
    vsaug_ref[:, :LANES] = vs_ref[0]
    vwaug_ref[:, :LANES] = vw_ref[0]
    _pair_attention(lambda h, r0, r1: qaug_ref[h, r0:r1, :],
                    lambda h, j: kaug_ref[h, j * TQ:(j + 1) * TQ, :],
                    lambda r0, r1: vsaug_ref[r0:r1, :],
                    strip_s_ref, s_ref, p_ref, oslc_ref, nblk, nblk)
    _pair_attention(lambda h, r0, r1: qaug_ref[h, r0:r1, :LANES],
                    lambda h, j: kw_ref[0, j * TQ:(j + 1) * TQ, :],
                    lambda r0, r1: vwaug_ref[r0:r1, :],
                    strip_w_ref, s_ref, p_ref, owin_ref, nblk, strip_w_ref.shape[1] // TQ, skew=2)
    sig = _sigmoid(pltpu.roll(graw_ref[0], lax.rem(LANES - GATE_STRIDE * p, LANES), 1))
    lo_lanes = _lane_lo((S, LANES))
    gates = [jnp.where(lo_lanes, sig[:, 2 * br:2 * br + 1], sig[:, 2 * br + 1:2 * br + 2]) for br in range(3)]
    out = gates[0] * ocmp_ref[0] + gates[1] * _merge_heads(oslc_ref) + gates[2] * _merge_heads(owin_ref)
    o_ref[0] = out.astype(o_ref.dtype)


def _attn1(qkv, r_slc, r_win, e_onehot, pen, ocmp, graw):
    B, S, _ = qkv.shape
    kv0 = D_MODEL // LANES + 4
    n_win = r_win.shape[2]
    return pl.pallas_call(
        _attn1_kernel,
        grid=(N_PAIRS, B),
        in_specs=[
            pl.BlockSpec((1, S, LANES), lambda p, b: (b, 0, p)),
            pl.BlockSpec((1, S, LANES), lambda p, b: (b, 0, kv0 + p // 4)),
            pl.BlockSpec((1, S, LANES), lambda p, b: (b, 0, kv0 + 2 + p // 4)),
            pl.BlockSpec((1, S, LANES), lambda p, b: (b, 0, kv0 + 4 + p // 4)),
            pl.BlockSpec((1, S, LANES), lambda p, b: (b, 0, kv0 + 6 + p // 4)),
            pl.BlockSpec((1, 2, r_slc.shape[2], 2 * TQ), lambda p, b: (p, 0, 0, 0)),
            pl.BlockSpec((1, 2, n_win, 2 * TQ), lambda p, b: (p, 0, 0, 0)),
            pl.BlockSpec((2, S, LANES), lambda p, b: (0, 0, 0), pipeline_mode=pl.Buffered(1)),
            pl.BlockSpec((1, 1, S, LANES), lambda p, b: (b, p // 4, 0, 0)),
            pl.BlockSpec((1, S, LANES), lambda p, b: (b, 0, p)),
            pl.BlockSpec((1, S, LANES), lambda p, b: (b, 0, 0)),
        ],
        out_specs=pl.BlockSpec((1, S, LANES), lambda p, b: (b, 0, p)),
        out_shape=jax.ShapeDtypeStruct((B, S, D_MODEL), BF16),
        scratch_shapes=_attn_scratch(S, S // TQ, n_win) + [pltpu.VMEM((2, n_win * TQ, TQ), F32),
                                                    pltpu.VMEM((S, 2 * LANES), BF16), pltpu.VMEM((2, S, LANES), F32)],
        compiler_params=pltpu.CompilerParams(dimension_semantics=("arbitrary",) * 2, vmem_limit_bytes=ATTN_VMEM_LIMIT),
        name="attn_nsa",
    )(qkv, qkv, qkv, qkv, qkv, r_slc, r_win, e_onehot, pen, ocmp, graw)


def _bucket(dist):
    n = np.maximum(dist, 0)
    exact = REL_BUCKETS // 2
    nf = np.maximum(n, 1).astype(np.float64)
    large = exact + (np.log(nf / exact) / math.log(REL_MAX_DIST / exact) * (REL_BUCKETS - exact)).astype(np.int64)
    return np.where(n < exact, n, np.minimum(large, REL_BUCKETS - 1))


def _strip_rows(tab, n_off, extra_of_dist):
    u = np.arange(2 * TQ)
    d = np.arange(n_off)[:, None] * TQ - np.where(u < TQ, u, u - 2 * TQ)[None, :]
    onehot = (_bucket(d)[None] == np.arange(REL_BUCKETS)[:, None, None]).astype(np.float32)
    vals = jnp.einsum("hb,bou->hou", tab, jnp.asarray(onehot), precision=lax.Precision.HIGHEST) + extra_of_dist(d)
    return vals.reshape(tab.shape[0] // 2, 2, n_off, 2 * TQ)


def _causal_mask(d):
    return np.where(d >= 0, 0.0, NEG).astype(np.float32)


def _window_mask(d):
    return np.where((d >= 0) & (d < NSA_WINDOW), 0.0, NEG).astype(np.float32)


def _dilation_log_count(d):
    c = ((d >= 0) & (d <= 128)).astype(np.float64)
    c += ((d >= 0) & (d % 4 == 0) & (d <= 512))
    c += ((d >= 0) & (d % 16 == 0) & (d <= 2048))
    return np.where(c > 0, np.log(np.maximum(c, 1.0)), NEG).astype(np.float32)


def _one_hot_blocks(S, block, per_head):
    e = np.zeros((2, S, LANES), np.float32)
    key = np.arange(S)
    for h in range(2):
        e[h, key, h * per_head + key // block] = 1.0
    return jnp.asarray(e, BF16)


_NSA_HEAD_ORDER = np.array([8 * (p // 4) + (p % 4) + 4 * h for p in range(N_PAIRS) for h in range(2)])


def _head_cols(heads):
    return (np.asarray(heads)[:, None] * HEAD_DIM + np.arange(HEAD_DIM)[None, :]).reshape(-1)


def _mixer_ab(h, g_mix, w_in, qn_a, kn_a, qn_b, kn_b, rel_bias, B, S):
    wa = MOBA_HEADS * HEAD_DIM
    ones = jnp.ones((wa,), F32)
    nh = MOBA_HEADS
    gain = jnp.concatenate([jnp.tile(qn_a, nh) * ATTN_SCALE, jnp.tile(qn_b, nh) * ATTN_SCALE,
                            jnp.tile(kn_a, nh), jnp.tile(kn_b, nh), ones, ones])
    per_sec = wa // PROJ_CHUNK
    src = [sec * per_sec + c for sec in (0, 3, 1, 4, 2, 5) for c in range(per_sec)]
    (qkv,) = _project(h, g_mix, w_in.astype(BF16), gain, [True] * (4 * per_sec) + [False] * (2 * per_sec),
                      src_chunks=src)
    qkv = qkv.reshape(B, S, 3 * D_MODEL)

    tab = rel_bias.T
    r_tab = jnp.concatenate([_strip_rows(tab[:MOBA_HEADS], S // TQ, _causal_mask),
                             _strip_rows(tab[MOBA_HEADS:], S // TQ, _dilation_log_count)])
    o = _attn0(qkv, r_tab, _one_hot_blocks(S, MOBA_BLOCK, S // MOBA_BLOCK))
    return o.reshape(B * S, D_MODEL)


def _mixer_nsa(h, g_mix, w_in, qn, kn_c, kn_s, kn_w, cmp_k, cmp_v, rel_bias, B, S):
    qw = N_HEADS * HEAD_DIM
    kvw = NSA_GROUPS * HEAD_DIM
    order = _NSA_HEAD_ORDER
    w_main = jnp.concatenate([w_in[:, _head_cols(order)], w_in[:, qw:qw + 6 * kvw]], axis=1).astype(BF16)
    ones = jnp.ones((kvw,), F32)
    gain = jnp.concatenate([jnp.tile(qn, N_HEADS) * ATTN_SCALE, ones, ones,
                            jnp.tile(kn_s, NSA_GROUPS), ones, jnp.tile(kn_w, NSA_GROUPS), ones])
    assert kvw == PROJ_CHUNK
    norm_chunks = [True] * (qw // PROJ_CHUNK) + [False, False, True, False, True, False]

    gcols = np.zeros((LANES,), np.int64)
    gused = np.zeros((LANES,), np.float32)
    for p in range(N_PAIRS):
        for br in range(3):
            for hh in range(2):
                c = GATE_STRIDE * p + 2 * br + hh
                gcols[c] = qw + 6 * kvw + 3 * order[2 * p + hh] + br
                gused[c] = 1.0
    w_gate = (w_in[:, gcols] * gused).astype(BF16)
    kc_chunk = qw // PROJ_CHUNK
    qkv, graw, raw = _project(h, g_mix, w_main, gain, norm_chunks, w_gate, raw_chunks=(kc_chunk, kc_chunk + 1))
    qkv = qkv.reshape(B, S, qw + 6 * kvw)
    graw = graw.reshape(B, S, LANES)
    raw = raw.reshape(B, S, 2 * kvw)
    kc = _compress(raw, 0, *cmp_k, kn_c, True)
    vc = _compress(raw, kvw // LANES, *cmp_v, kn_c, False)

    tab = rel_bias.T[order]
    n_cmp = (S - NSA_CMP_LEN) // NSA_CMP_STRIDE + 1
    t_pos = np.arange(S)[:, None]
    c_idx = np.arange(LANES)[None, :]
    dc = t_pos - (c_idx * NSA_CMP_STRIDE + NSA_CMP_LEN - 1)
    cmp_mask = np.where((dc >= 0) & (c_idx < n_cmp), 0.0, NEG).astype(np.float32)
    onehot = jnp.asarray(_bucket(dc).astype(np.int8))[None] == jnp.arange(REL_BUCKETS, dtype=jnp.int8)[:, None, None]
    bias_c = jnp.einsum("hb,bsc->hsc", tab, onehot.astype(F32), precision=lax.Precision.HIGHEST)
    bias_c = (bias_c + cmp_mask).reshape(2, 8, S, LANES)
    n_sel = S // NSA_SEL_BLOCK
    cstart = np.arange(LANES) * NSA_CMP_STRIDE
    sstart = np.arange(n_sel) * NSA_SEL_BLOCK
    ovt = np.maximum(np.minimum(cstart[None, :] + NSA_CMP_LEN, sstart[:, None] + NSA_SEL_BLOCK)
                     - np.maximum(cstart[None, :], sstart[:, None]), 0).astype(np.float32)
    ovt[:, n_cmp:] = 0.0
    ocmp, pen = _cmp_attention(qkv, kc, vc, bias_c, jnp.asarray(ovt))

    r_slc = _strip_rows(tab, S // TQ, _causal_mask)
    r_win = _strip_rows(tab, NSA_WINDOW // TQ + 1, _window_mask)
    o = _attn1(qkv, r_slc, r_win, _one_hot_blocks(S, NSA_SEL_BLOCK, n_sel), pen, ocmp, graw)
    return o.reshape(B * S, D_MODEL)


def kernel(x, p, rel_bias, norm_mix, norm_ffn, norm_ple, w_ffn_gate, w_ffn_up, w_ffn_down, w_ple_proj, w_ple_gate, w_in_ab, w_out_ab, qn_moba, kn_moba, qn_dil, kn_dil, w_in_nsa, w_out_nsa, qn_nsa, kn_cmp, kn_slc, kn_win, cmp_k_pos, cmp_k_w1, cmp_k_b1, cmp_k_w2, cmp_k_b2, cmp_v_pos, cmp_v_w1, cmp_v_b1, cmp_v_w2, cmp_v_b2):
    B, S, D = x.shape
    depth = p.shape[0]
    assert D == D_MODEL and S == MOBA_NBLK * MOBA_BLOCK and (B * S) % ROW_TILE == 0, (B, S, D)
    h = x.reshape(B * S, D)
    for i in range(depth):
        e = i // 2
        if i % 2 == 0:
            o = _mixer_ab(h, norm_mix[i], w_in_ab[e], qn_moba[e], kn_moba[e], qn_dil[e], kn_dil[e], rel_bias, B, S)
            w_out = w_out_ab[e]
        else:
            cmp_k = (cmp_k_pos[e], cmp_k_w1[e], cmp_k_b1[e], cmp_k_w2[e], cmp_k_b2[e])
            cmp_v = (cmp_v_pos[e], cmp_v_w1[e], cmp_v_b1[e], cmp_v_w2[e], cmp_v_b2[e])
            o = _mixer_nsa(h, norm_mix[i], w_in_nsa[e], qn_nsa[e], kn_cmp[e], kn_slc[e], kn_win[e],
                           cmp_k, cmp_v, rel_bias, B, S)
            w_out = w_out_nsa[e][_head_cols(_NSA_HEAD_ORDER), :]
        h = _post_attention(o, h, w_out.astype(BF16), norm_ffn[i], w_ffn_gate[i].astype(BF16),
                            w_ffn_up[i].astype(BF16), w_ffn_down[i].astype(BF16), norm_ple[i],
                            w_ple_gate[i].astype(BF16), p[i].reshape(B * S, -1), w_ple_proj[i].astype(BF16))
    return h.reshape(B, S, D)
```

```python
import functools
import math

import numpy as np
import jax
import jax.numpy as jnp
from jax import lax
from jax.experimental import pallas as pl
from jax.experimental.pallas import tpu as pltpu

F32 = jnp.float32
BF16 = jnp.bfloat16

D_MODEL = 1024
HEAD_DIM = 64
N_HEADS = 16
N_PAIRS = N_HEADS // 2
MOBA_HEADS = 8
MOBA_BLOCK = 256
MOBA_TOPK = 3
MOBA_NBLK = 8
NSA_GROUPS = 4
NSA_CMP_LEN = 32
NSA_CMP_STRIDE = 16
NSA_SEL_BLOCK = 64
NSA_SEL_TOPN = 16
NSA_WINDOW = 512
NSA_FORCE = 1.0e6
REL_BUCKETS = 32
REL_MAX_DIST = 2048
RMS_EPS = 1e-6
ATTN_SCALE = HEAD_DIM ** -0.5

LANES = 128
SUBLANES = 8
TQ = 256
NEG = -1.0e30
ROW_TILE = 512
PROJ_CHUNK = 256
POST_SPLIT = 2
GATE_STRIDE = LANES // N_HEADS
CMP_ROWS = 2048
VMEM_LIMIT = 48 * 1024 * 1024
POST_VMEM_LIMIT = 56 * 1024 * 1024
ATTN_VMEM_LIMIT = 56 * 1024 * 1024

NT_DIMS = (((1,), (1,)), ((), ()))


def _cparams(n_axes):
    return pltpu.CompilerParams(dimension_semantics=("arbitrary",) * n_axes,
                                vmem_limit_bytes=VMEM_LIMIT)


def _rms_rows(x, g):
    ms = jnp.mean(x * x, axis=-1, keepdims=True)
    return x * lax.rsqrt(ms + RMS_EPS) * g


def _lane_lo(shape):
    return lax.broadcasted_iota(jnp.int32, shape, len(shape) - 1) < HEAD_DIM


def _resident(shape):
    return pl.BlockSpec(shape, lambda i: (0,) * len(shape), pipeline_mode=pl.Buffered(1))


def _head_sum_squares(y):
    y2 = y * y
    parts = []
    for blk in range(y.shape[1] // LANES):
        yb = y2[:, blk * LANES:(blk + 1) * LANES]
        lo = _lane_lo(yb.shape)
        s_lo = jnp.sum(jnp.where(lo, yb, 0.0), axis=1, keepdims=True)
        s_hi = jnp.sum(jnp.where(lo, 0.0, yb), axis=1, keepdims=True)
        parts.append(jnp.where(lo, s_lo, s_hi))
    return jnp.concatenate(parts, axis=1)


def _proj_kernel(x_ref, g_ref, w_ref, cg_ref, *rest, norm_chunks, src_chunks, raw_chunks, with_extra):
    if with_extra:
        wx_ref, o_ref, ox_ref, oraw_ref = rest
    else:
        (o_ref,) = rest
    xn = _rms_rows(x_ref[...], g_ref[...]).astype(BF16)

    def finish(c, y):
        cols = slice(c * PROJ_CHUNK, (c + 1) * PROJ_CHUNK)
        if c in raw_chunks:
            k = raw_chunks.index(c)
            oraw_ref[:, k * PROJ_CHUNK:(k + 1) * PROJ_CHUNK] = y
        if norm_chunks[c]:
            y = y * lax.rsqrt(_head_sum_squares(y) * (1.0 / HEAD_DIM) + RMS_EPS) * cg_ref[:, cols]
        o_ref[:, cols] = y.astype(o_ref.dtype)

    prev = None
    for c in range(len(norm_chunks)):
        src = src_chunks[c] * PROJ_CHUNK
        y = jnp.dot(xn, w_ref[:, src:src + PROJ_CHUNK], preferred_element_type=F32)
        if prev is not None:
            finish(c - 1, prev)
        prev = y
    finish(len(norm_chunks) - 1, prev)
    if with_extra:
        ox_ref[...] = jnp.dot(xn, wx_ref[...], preferred_element_type=F32)


def _project(x, g, w, col_gain, norm_chunks, w_extra=None, raw_chunks=(), src_chunks=None):
    if src_chunks is None:
        src_chunks = range(len(norm_chunks))
    T, D = x.shape
    N = w.shape[1]
    row_in = lambda n: pl.BlockSpec((ROW_TILE, n), lambda i: (i, 0))
    in_specs = [row_in(D), _resident((1, D)), _resident((D, N)), _resident((1, N))]
    args = [x, g.reshape(1, D), w, col_gain.reshape(1, N)]
    out_specs = [row_in(N)]
    out_shape = [jax.ShapeDtypeStruct((T, N), BF16)]
    if w_extra is not None:
        nx = w_extra.shape[1]
        in_specs.append(_resident((D, nx)))
        args.append(w_extra)
        out_specs += [row_in(nx), row_in(len(raw_chunks) * PROJ_CHUNK)]
        out_shape += [jax.ShapeDtypeStruct((T, nx), F32), jax.ShapeDtypeStruct((T, len(raw_chunks) * PROJ_CHUNK), F32)]
    return pl.pallas_call(
        functools.partial(_proj_kernel, norm_chunks=tuple(norm_chunks), src_chunks=tuple(src_chunks),
                          raw_chunks=tuple(raw_chunks),
                          with_extra=w_extra is not None),
        grid=(T // ROW_TILE,),
        in_specs=in_specs,
        out_specs=out_specs,
        out_shape=out_shape,
        compiler_params=_cparams(1),
        name="proj",
    )(*args)


def _sigmoid(z):
    return 1.0 / (1.0 + jnp.exp(-z))


def _post_kernel(o_ref, h_ref, wout_ref, gf_ref, wg_ref, wu_ref, wd_ref, gp_ref, wpg_ref, p_ref, wpp_ref, out_ref):
    halves = [slice(r * ROW_TILE // POST_SPLIT, (r + 1) * ROW_TILE // POST_SPLIT) for r in range(POST_SPLIT)]
    h1 = [h_ref[r, :] + jnp.dot(o_ref[r, :], wout_ref[...], preferred_element_type=F32) for r in halves]
    xn = [_rms_rows(x, gf_ref[...]).astype(BF16) for x in h1]
    a = [jnp.dot(x, wg_ref[...], preferred_element_type=F32) for x in xn]
    u = [jnp.dot(x, wu_ref[...], preferred_element_type=F32) for x in xn]
    act = [(ai * _sigmoid(ai) * ui).astype(BF16) for ai, ui in zip(a, u)]
    h2 = [x + jnp.dot(t, wd_ref[...], preferred_element_type=F32) for x, t in zip(h1, act)]
    hn = [_rms_rows(x, gp_ref[...]).astype(BF16) for x in h2]
    gate = [_sigmoid(jnp.dot(x, wpg_ref[...], preferred_element_type=F32)) for x in hn]
    for r, x, g in zip(halves, h2, gate):
        out_ref[r, :] = x + g * jnp.dot(p_ref[r, :].astype(BF16), wpp_ref[...], preferred_element_type=F32)


def _post_attention(o, h, w_out, g_ffn, wg, wu, wd, g_ple, w_pgate, p, w_pproj, layer):
    T, D = h.shape
    Fh = wg.shape[2]
    Pd = p.shape[2]
    row_in = lambda n: pl.BlockSpec((ROW_TILE, n), lambda i: (i, 0))
    slab = lambda r, c: pl.BlockSpec((None, r, c), lambda i: (layer, 0, 0), pipeline_mode=pl.Buffered(1))
    return pl.pallas_call(
        _post_kernel,
        grid=(T // ROW_TILE,),
        in_specs=[row_in(D), row_in(D), _resident((D, D)), _resident((1, D)), slab(D, Fh),
                  slab(D, Fh), slab(Fh, D), _resident((1, D)), slab(D, D),
                  pl.BlockSpec((None, ROW_TILE, Pd), lambda i: (layer, i, 0)), slab(Pd, D)],
        out_specs=row_in(D),
        out_shape=jax.ShapeDtypeStruct((T, D), F32),
        compiler_params=pltpu.CompilerParams(dimension_semantics=("arbitrary",), vmem_limit_bytes=POST_VMEM_LIMIT),
        name="post_attention",
    )(o, h, w_out, g_ffn.reshape(1, D), wg, wu, wd, g_ple.reshape(1, D), w_pgate, p, w_pproj)


def _fill_strips(strip_ref, r_ref):
    for h in range(2):
        for o in range(r_ref.shape[2]):
            x = jnp.broadcast_to(r_ref[0, h, o:o + 1, :], (TQ, 2 * TQ))
            strip_ref[h, o * TQ:(o + 1) * TQ, :] = pltpu.roll(x, 0, 1, stride=1, stride_axis=0)[:, :TQ]


def _tile_groups(nblk, span):
    cnt = [min(span, nblk - j) for j in range(nblk)]
    off = [0]
    for c in cnt:
        off.append(off[-1] + c * TQ)
    return cnt, off


def _tile_rows(nblk, span):
    return _tile_groups(nblk, span)[1][-1]


def _pair_attention(q_rows, k_rows, v_rows, strip_ref, s_ref, p_ref, o_ref, nblk, span, skew=1):
    cnt, off = _tile_groups(nblk, span)
    together = s_ref.shape[0] >= 2 * off[-1] and p_ref.shape[2] >= 2 * span * TQ
    s0 = [0, off[-1] if together else 0]
    p0 = [0, span * TQ if together else 0]

    def scores(h, j):
        n = cnt[j] * TQ
        s = lax.dot_general(q_rows(h, j * TQ, j * TQ + n), k_rows(h, j), NT_DIMS, preferred_element_type=F32)
        s_ref[s0[h] + off[j]:s0[h] + off[j] + n, :] = s + strip_ref[h, 0:n, :]

    def probs(h, i):
        j0 = max(0, i - span + 1)
        rows = [s0[h] + off[j] + (i - j) * TQ for j in range(j0, i + 1)]
        mx = s_ref[rows[0]:rows[0] + TQ, :]
        for r in rows[1:]:
            mx = jnp.maximum(mx, s_ref[r:r + TQ, :])
        m = jnp.max(mx, axis=1, keepdims=True)
        for t, r in enumerate(rows):
            p_ref[i, :, p0[h] + t * TQ:p0[h] + (t + 1) * TQ] = jnp.exp(s_ref[r:r + TQ, :] - m).astype(BF16)

    def values(h, i):
        j0 = max(0, i - span + 1)
        out = jnp.dot(p_ref[i, :, p0[h]:p0[h] + (i + 1 - j0) * TQ], v_rows(j0 * TQ, (i + 1) * TQ),
                      preferred_element_type=F32)
        o_ref[h, i * TQ:(i + 1) * TQ, :] = out[:, :LANES] / out[:, LANES:]

    for heads in ([(0, 1)] if together else [(0,), (1,)]):
        for j in range(nblk + skew):
            for h in heads:
                if j < nblk:
                    scores(h, j)
            for h in heads:
                if j >= skew:
                    values(h, j - skew)
            for h in heads:
                if j < nblk:
                    probs(h, j)


def _merge_heads(o_ref):
    return jnp.where(_lane_lo((o_ref.shape[1], LANES)), o_ref[0], o_ref[1])


def _split_heads(q):
    lo = _lane_lo(q.shape)
    zero = jnp.zeros_like(q)
    return jnp.where(lo, q, zero), jnp.where(lo, zero, q)


def _fill_value_ones(vaug_ref):
    vaug_ref[:, LANES:] = jnp.ones((vaug_ref.shape[0], LANES), vaug_ref.dtype)


def _moba_block_means(k_ref):
    shape = (2 * MOBA_NBLK, LANES)
    lo = _lane_lo(shape)
    row = lax.broadcasted_iota(jnp.int32, shape, 0)
    kmt = jnp.zeros(shape, F32)
    for n in range(MOBA_NBLK):
        mean = jnp.mean(k_ref[0, n * MOBA_BLOCK:(n + 1) * MOBA_BLOCK, :].astype(F32), axis=0, keepdims=True)
        kmt = jnp.where(row == n, jnp.where(lo, mean, 0.0), kmt)
        kmt = jnp.where(row == MOBA_NBLK + n, jnp.where(lo, 0.0, mean), kmt)
    return kmt


def _moba_penalty(q, kmt):
    nb = MOBA_NBLK
    S = q.shape[0]
    hi = kmt.astype(BF16)
    rest = kmt - hi.astype(F32)
    mid = rest.astype(BF16)
    lo = (rest - mid.astype(F32)).astype(BF16)
    g3 = lax.dot_general(jnp.concatenate([hi, mid, lo], axis=0), q, NT_DIMS, preferred_element_type=F32)
    g = g3[0:2 * nb] + g3[2 * nb:4 * nb] + g3[4 * nb:6 * nb]
    row = lax.broadcasted_iota(jnp.int32, g.shape, 0)
    n = row & (nb - 1)
    own = lax.shift_right_logical(lax.broadcasted_iota(jnp.int32, g.shape, 1), int(math.log2(MOBA_BLOCK)))
    rank = jnp.zeros(g.shape, F32)
    for m in range(nb - 1):
        gm = jnp.where(row < nb, g[m:m + 1, :], g[nb + m:nb + m + 1, :])
        tie = jnp.where(n > m, 1.0, 0.0)
        beats = jnp.where(gm > g, 1.0, jnp.where(gm == g, tie, 0.0))
        rank = rank + jnp.where(own > m, beats, 0.0)
    keep = jnp.where(n < own, jnp.where(rank < MOBA_TOPK, 1.0, 0.0), jnp.where(n == own, 1.0, 0.0))
    keep = jnp.concatenate([keep, jnp.zeros((LANES - 2 * nb, S), F32)], axis=0).T
    lane = lax.broadcasted_iota(jnp.int32, keep.shape, 1)
    return jnp.where(lane < 2 * nb, (keep - 1.0) * (-NEG), 0.0)


def _attn0_kernel(q_ref, k_ref, v_ref, r_ref, e_ref, o_ref, strip_ref, qaug_ref, kaug_ref, vaug_ref,
                  s_ref, p_ref, oh_ref):
    p = pl.program_id(0)
    b = pl.program_id(1)
    S = q_ref.shape[1]
    nblk = S // TQ

    @pl.when(b == 0)
    def _():
        _fill_strips(strip_ref, r_ref)

    @pl.when(jnp.logical_and(p == 0, b == 0))
    def _():
        for h in range(2):
            kaug_ref[h, :, LANES:] = e_ref[h]
        _fill_value_ones(vaug_ref)

    k = k_ref[0]
    for h, qh in enumerate(_split_heads(q_ref[0])):
        kaug_ref[h, :, :LANES] = k
        qaug_ref[h, :, :LANES] = qh
    vaug_ref[:, :LANES] = v_ref[0]
    is_moba = p < MOBA_HEADS // 2

    @pl.when(is_moba)
    def _():
        pen = _moba_penalty(q_ref[0], _moba_block_means(k_ref)).astype(BF16)
        for h in range(2):
            qaug_ref[h, :, LANES:] = pen

    @pl.when(jnp.logical_not(is_moba))
    def _():
        for h in range(2):
            qaug_ref[h, :, LANES:] = jnp.zeros((S, LANES), BF16)
    _pair_attention(lambda h, r0, r1: qaug_ref[h, r0:r1, :],
                    lambda h, j: kaug_ref[h, j * TQ:(j + 1) * TQ, :],
                    lambda r0, r1: vaug_ref[r0:r1, :],
                    strip_ref, s_ref, p_ref, oh_ref, nblk, nblk)
    o_ref[0] = _merge_heads(oh_ref).astype(o_ref.dtype)


def _attn_scratch(S, span, narrow_span=0):
    nblk = S // TQ
    s_rows = max(_tile_rows(nblk, span), 2 * _tile_rows(nblk, narrow_span))
    return [pltpu.VMEM((2, S, TQ), F32), pltpu.VMEM((2, S, 2 * LANES), BF16), pltpu.VMEM((2, S, 2 * LANES), BF16),
            pltpu.VMEM((S, 2 * LANES), BF16), pltpu.VMEM((s_rows, TQ), F32),
            pltpu.VMEM((nblk, TQ, span * TQ), BF16), pltpu.VMEM((2, S, LANES), F32)]


def _attn0(qkv, r_tab, e_onehot):
    B, S, _ = qkv.shape
    nblk = S // TQ
    return pl.pallas_call(
        _attn0_kernel,
        grid=(N_PAIRS, B),
        in_specs=[
            pl.BlockSpec((1, S, LANES), lambda p, b: (b, 0, p)),
            pl.BlockSpec((1, S, LANES), lambda p, b: (b, 0, N_PAIRS + p)),
            pl.BlockSpec((1, S, LANES), lambda p, b: (b, 0, 2 * N_PAIRS + p)),
            pl.BlockSpec((1, 2, nblk, 2 * TQ), lambda p, b: (p, 0, 0, 0)),
            pl.BlockSpec((2, S, LANES), lambda p, b: (0, 0, 0), pipeline_mode=pl.Buffered(1)),
        ],
        out_specs=pl.BlockSpec((1, S, LANES), lambda p, b: (b, 0, p)),
        out_shape=jax.ShapeDtypeStruct((B, S, D_MODEL), BF16),
        scratch_shapes=_attn_scratch(S, nblk),
        compiler_params=pltpu.CompilerParams(dimension_semantics=("arbitrary",) * 2, vmem_limit_bytes=ATTN_VMEM_LIMIT),
        name="attn_moba_dilated",
    )(qkv, qkv, qkv, r_tab, e_onehot)


def _compress_kernel(x_ref, pos_ref, w1_ref, b1_ref, w2_ref, b2_ref, gain_ref, o_ref, *, normed):
    n_chunk = x_ref.shape[1] // NSA_CMP_STRIDE
    first = second = None
    for a in range(NSA_CMP_STRIDE):
        t = x_ref[0, pl.ds(a, n_chunk, stride=NSA_CMP_STRIDE), :]
        fa = jnp.dot((t + pos_ref[a:a + 1, :]).astype(BF16), w1_ref[a], preferred_element_type=F32)
        sa = jnp.dot((t + pos_ref[NSA_CMP_STRIDE + a:NSA_CMP_STRIDE + a + 1, :]).astype(BF16),
                     w1_ref[NSA_CMP_STRIDE + a], preferred_element_type=F32)
        first = fa if first is None else first + fa
        second = sa if second is None else second + sa
    hid = first + pltpu.roll(second, n_chunk - 1, 0) + b1_ref[...]
    cdf = 0.5 * (1.0 + jnp.tanh(math.sqrt(2.0 / math.pi) * (hid + 0.044715 * (hid * hid * hid))))
    y = jnp.dot((hid * cdf).astype(BF16), w2_ref[...], preferred_element_type=F32) + b2_ref[...]
    if normed:
        lo = _lane_lo(y.shape)
        y2 = y * y
        s_lo = jnp.sum(jnp.where(lo, y2, 0.0), axis=1, keepdims=True)
        s_hi = jnp.sum(jnp.where(lo, 0.0, y2), axis=1, keepdims=True)
        ms = jnp.where(lo, s_lo, s_hi) * (1.0 / HEAD_DIM)
        y = y * lax.rsqrt(ms + RMS_EPS) * gain_ref[...]
    o_ref[0, 0] = y.astype(o_ref.dtype)


def _pair_block_diag(w):
    z = jnp.zeros_like(w)
    return jnp.concatenate([jnp.concatenate([w, z], axis=-1), jnp.concatenate([z, w], axis=-1)], axis=-2)


def _compress(raw, col_block0, pos, w1, b1, w2, b2, gain, normed):
    B, S, _ = raw.shape
    hid = w1.shape[1]
    n_pos = pos.shape[0]
    w1bd = _pair_block_diag(w1.reshape(n_pos, HEAD_DIM, hid)).astype(BF16)
    return pl.pallas_call(
        functools.partial(_compress_kernel, normed=normed),
        grid=(B, 2),
        in_specs=[
            pl.BlockSpec((1, S, LANES), lambda b, m: (b, 0, col_block0 + m)),
            pl.BlockSpec((n_pos, LANES), lambda b, m: (0, 0)),
            pl.BlockSpec((n_pos, LANES, 2 * hid), lambda b, m: (0, 0, 0)),
            pl.BlockSpec((1, 2 * hid), lambda b, m: (0, 0)),
            pl.BlockSpec((2 * hid, LANES), lambda b, m: (0, 0)),
            pl.BlockSpec((1, LANES), lambda b, m: (0, 0)),
            pl.BlockSpec((1, LANES), lambda b, m: (0, 0)),
        ],
        out_specs=pl.BlockSpec((1, 1, S // NSA_CMP_STRIDE, LANES), lambda b, m: (b, m, 0, 0)),
        out_shape=jax.ShapeDtypeStruct((B, 2, S // NSA_CMP_STRIDE, LANES), BF16),
        compiler_params=_cparams(2),
        name="nsa_compress",
    )(raw, jnp.tile(pos, (1, 2)), w1bd, jnp.tile(b1, 2).reshape(1, 2 * hid), _pair_block_diag(w2).astype(BF16),
      jnp.tile(b2, 2).reshape(1, LANES), jnp.tile(gain, 2).reshape(1, LANES))


def _cmp_kernel(q_ref, kc_ref, vc_ref, bias_ref, ovt_ref, o_ref, pen_ref):
    kc = kc_ref[0, 0]
    vc = vc_ref[0, 0]
    S = q_ref.shape[1]
    n_sel = ovt_ref.shape[0]
    lo = _lane_lo((CMP_ROWS, LANES))
    blk = lax.broadcasted_iota(jnp.int32, (n_sel, CMP_ROWS), 0)
    for c in range(S // CMP_ROWS):
        rows = slice(c * CMP_ROWS, (c + 1) * CMP_ROWS)
        psum = [jnp.zeros((CMP_ROWS, LANES), F32), jnp.zeros((CMP_ROWS, LANES), F32)]
        for r in range(4):
            heads = _split_heads(q_ref[0, rows, r * LANES:(r + 1) * LANES])
            outs = []
            for h in range(2):
                s = lax.dot_general(heads[h], kc, NT_DIMS, preferred_element_type=F32) + bias_ref[0, 2 * r + h, rows, :]
                m = jnp.max(s, axis=1, keepdims=True)
                e = jnp.exp(s - m)
                l = jnp.sum(e, axis=1, keepdims=True)
                pr = e * jnp.where(m > 0.5 * NEG, 1.0 / l, 0.0)
                psum[h] = psum[h] + pr
                outs.append(jnp.dot(pr.astype(BF16), vc, preferred_element_type=F32))
            o_ref[0, rows, r * LANES:(r + 1) * LANES] = jnp.where(lo, outs[0], outs[1])

        t = c * CMP_ROWS + lax.broadcasted_iota(jnp.int32, (n_sel, CMP_ROWS), 1)
        cur = lax.shift_right_logical(t, int(math.log2(NSA_SEL_BLOCK)))
        keeps = []
        for h in range(2):
            imp = lax.dot_general(ovt_ref[...], psum[h], NT_DIMS,
                                  precision=lax.Precision.HIGHEST, preferred_element_type=F32)
            forced = jnp.where(blk == 0, 1.0, jnp.where(blk == cur, 1.0, jnp.where(blk == cur - 1, 1.0, 0.0)))
            imp = jnp.where(blk <= cur, imp + forced * NSA_FORCE, -jnp.inf)
            groups = [imp[g * SUBLANES:(g + 1) * SUBLANES, :] for g in range(n_sel // SUBLANES)]
            ranks = [jnp.zeros(g.shape, F32) for g in groups]
            for m in range(n_sel):
                im = imp[m:m + 1, :]
                for g, sub in enumerate(groups):
                    ge = jnp.where(im >= sub, 1.0, 0.0)
                    gt = jnp.where(im > sub, 1.0, 0.0)
                    if g * SUBLANES > m:
                        inc = ge
                    elif (g + 1) * SUBLANES - 1 <= m:
                        inc = gt
                    else:
                        inc = jnp.where(lax.broadcasted_iota(jnp.int32, sub.shape, 0) + g * SUBLANES > m, ge, gt)
                    ranks[g] = ranks[g] + inc
            keeps.append(jnp.where(jnp.concatenate(ranks, axis=0) < NSA_SEL_TOPN, 1.0, 0.0))
        keep = jnp.concatenate(keeps + [jnp.zeros((LANES - 2 * n_sel, CMP_ROWS), F32)], axis=0).T
        lane = lax.broadcasted_iota(jnp.int32, keep.shape, 1)
        pen_ref[0, 0, rows, :] = jnp.where(lane < 2 * n_sel, (keep - 1.0) * (-NEG), 0.0).astype(pen_ref.dtype)


def _cmp_attention(qkv, kc, vc, bias, ovt):
    B, S, _ = qkv.shape
    n_sel = ovt.shape[0]
    return pl.pallas_call(
        _cmp_kernel,
        grid=(2, B),
        in_specs=[
            pl.BlockSpec((1, S, 4 * LANES), lambda m, b: (b, 0, m)),
            pl.BlockSpec((1, 1, LANES, LANES), lambda m, b: (b, m, 0, 0)),
            pl.BlockSpec((1, 1, LANES, LANES), lambda m, b: (b, m, 0, 0)),
            pl.BlockSpec((1, 8, S, LANES), lambda m, b: (m, 0, 0, 0), pipeline_mode=pl.Buffered(1)),
            pl.BlockSpec((n_sel, LANES), lambda m, b: (0, 0)),
        ],
        out_specs=[
            pl.BlockSpec((1, S, 4 * LANES), lambda m, b: (b, 0, m)),
            pl.BlockSpec((1, 1, S, LANES), lambda m, b: (b, m, 0, 0)),
        ],
        out_shape=[jax.ShapeDtypeStruct((B, S, D_MODEL), F32),
                   jax.ShapeDtypeStruct((B, 2, S, LANES), BF16)],
        compiler_params=_cparams(2),
        name="nsa_compressed_select",
    )(qkv, kc, vc, bias, ovt)


def _attn1_kernel(q_ref, ks_ref, vs_ref, kw_ref, vw_ref, rs_ref, rw_ref, e_ref, pen_ref, ocmp_ref, graw_ref,
                  o_ref, strip_s_ref, qaug_ref, kaug_ref, vsaug_ref, s_ref, p_ref, oslc_ref,
                  strip_w_ref, vwaug_ref, owin_ref):
    p = pl.program_id(0)
    b = pl.program_id(1)
    S = q_ref.shape[1]
    nblk = S // TQ

    @pl.when(b == 0)
    def _():
        _fill_strips(strip_s_ref, rs_ref)
        _fill_strips(strip_w_ref, rw_ref)

    @pl.when(jnp.logical_and(p == 0, b == 0))
    def _():
        for h in range(2):
            kaug_ref[h, :, LANES:] = e_ref[h]
        _fill_value_ones(vsaug_ref)
        _fill_value_ones(vwaug_ref)

    ks = ks_ref[0]
    pen = pen_ref[0, 0]
    for h, qh in enumerate(_split_heads(q_ref[0])):
        kaug_ref[h, :, :LANES] = ks
        qaug_ref[h, :, :LANES] = qh
        qaug_ref[h, :, LANES:] = pen
    vsaug_ref[:, :LANES] = vs_ref[0]
    vwaug_ref[:, :LANES] = vw_ref[0]
    _pair_attention(lambda h, r0, r1: qaug_ref[h, r0:r1, :],
                    lambda h, j: kaug_ref[h, j * TQ:(j + 1) * TQ, :],
                    lambda r0, r1: vsaug_ref[r0:r1, :],
                    strip_s_ref, s_ref, p_ref, oslc_ref, nblk, nblk)
    _pair_attention(lambda h, r0, r1: qaug_ref[h, r0:r1, :LANES],
                    lambda h, j: kw_ref[0, j * TQ:(j + 1) * TQ, :],
                    lambda r0, r1: vwaug_ref[r0:r1, :],
                    strip_w_ref, s_ref, p_ref, owin_ref, nblk, strip_w_ref.shape[1] // TQ, skew=2)
    sig = _sigmoid(pltpu.roll(graw_ref[0], lax.rem(LANES - GATE_STRIDE * p, LANES), 1))
    lo_lanes = _lane_lo((S, LANES))
    gates = [jnp.where(lo_lanes, sig[:, 2 * br:2 * br + 1], sig[:, 2 * br + 1:2 * br + 2]) for br in range(3)]
    out = gates[0] * ocmp_ref[0] + gates[1] * _merge_heads(oslc_ref) + gates[2] * _merge_heads(owin_ref)
    o_ref[0] = out.astype(o_ref.dtype)


def _attn1(qkv, r_slc, r_win, e_onehot, pen, ocmp, graw):
    B, S, _ = qkv.shape
    kv0 = D_MODEL // LANES + 4
    n_win = r_win.shape[2]
    return pl.pallas_call(
        _attn1_kernel,
        grid=(N_PAIRS, B),
        in_specs=[
            pl.BlockSpec((1, S, LANES), lambda p, b: (b, 0, p)),
            pl.BlockSpec((1, S, LANES), lambda p, b: (b, 0, kv0 + p // 4)),
            pl.BlockSpec((1, S, LANES), lambda p, b: (b, 0, kv0 + 2 + p // 4)),
            pl.BlockSpec((1, S, LANES), lambda p, b: (b, 0, kv0 + 4 + p // 4)),
            pl.BlockSpec((1, S, LANES), lambda p, b: (b, 0, kv0 + 6 + p // 4)),
            pl.BlockSpec((1, 2, r_slc.shape[2], 2 * TQ), lambda p, b: (p, 0, 0, 0)),
            pl.BlockSpec((1, 2, n_win, 2 * TQ), lambda p, b: (p, 0, 0, 0)),
            pl.BlockSpec((2, S, LANES), lambda p, b: (0, 0, 0), pipeline_mode=pl.Buffered(1)),
            pl.BlockSpec((1, 1, S, LANES), lambda p, b: (b, p // 4, 0, 0)),
            pl.BlockSpec((1, S, LANES), lambda p, b: (b, 0, p)),
            pl.BlockSpec((1, S, LANES), lambda p, b: (b, 0, 0)),
        ],
        out_specs=pl.BlockSpec((1, S, LANES), lambda p, b: (b, 0, p)),
        out_shape=jax.ShapeDtypeStruct((B, S, D_MODEL), BF16),
        scratch_shapes=_attn_scratch(S, S // TQ, n_win) + [pltpu.VMEM((2, n_win * TQ, TQ), F32),
                                                    pltpu.VMEM((S, 2 * LANES), BF16), pltpu.VMEM((2, S, LANES), F32)],
        compiler_params=pltpu.CompilerParams(dimension_semantics=("arbitrary",) * 2, vmem_limit_bytes=ATTN_VMEM_LIMIT),
        name="attn_nsa",
    )(qkv, qkv, qkv, qkv, qkv, r_slc, r_win, e_onehot, pen, ocmp, graw)


def _bucket(dist):
    n = np.maximum(dist, 0)
    exact = REL_BUCKETS // 2
    nf = np.maximum(n, 1).astype(np.float64)
    large = exact + (np.log(nf / exact) / math.log(REL_MAX_DIST / exact) * (REL_BUCKETS - exact)).astype(np.int64)
    return np.where(n < exact, n, np.minimum(large, REL_BUCKETS - 1))


def _strip_rows(tab, n_off, extra_of_dist):
    u = np.arange(2 * TQ)
    d = np.arange(n_off)[:, None] * TQ - np.where(u < TQ, u, u - 2 * TQ)[None, :]
    onehot = (_bucket(d)[None] == np.arange(REL_BUCKETS)[:, None, None]).astype(np.float32)
    vals = jnp.einsum("hb,bou->hou", tab, jnp.asarray(onehot), precision=lax.Precision.HIGHEST) + extra_of_dist(d)
    return vals.reshape(tab.shape[0] // 2, 2, n_off, 2 * TQ)


def _causal_mask(d):
    return np.where(d >= 0, 0.0, NEG).astype(np.float32)


def _window_mask(d):
    return np.where((d >= 0) & (d < NSA_WINDOW), 0.0, NEG).astype(np.float32)


def _dilation_log_count(d):
    c = ((d >= 0) & (d <= 128)).astype(np.float64)
    c += ((d >= 0) & (d % 4 == 0) & (d <= 512))
    c += ((d >= 0) & (d % 16 == 0) & (d <= 2048))
    return np.where(c > 0, np.log(np.maximum(c, 1.0)), NEG).astype(np.float32)


def _one_hot_blocks(S, block, per_head):
    e = np.zeros((2, S, LANES), np.float32)
    key = np.arange(S)
    for h in range(2):
        e[h, key, h * per_head + key // block] = 1.0
    return jnp.asarray(e, BF16)


_NSA_HEAD_ORDER = np.array([8 * (p // 4) + (p % 4) + 4 * h for p in range(N_PAIRS) for h in range(2)])


def _head_cols(heads):
    return (np.asarray(heads)[:, None] * HEAD_DIM + np.arange(HEAD_DIM)[None, :]).reshape(-1)


def _mixer_ab(h, g_mix, w_in, qn_a, kn_a, qn_b, kn_b, rel_bias, B, S):
    wa = MOBA_HEADS * HEAD_DIM
    ones = jnp.ones((wa,), F32)
    nh = MOBA_HEADS
    gain = jnp.concatenate([jnp.tile(qn_a, nh) * ATTN_SCALE, jnp.tile(qn_b, nh) * ATTN_SCALE,
                            jnp.tile(kn_a, nh), jnp.tile(kn_b, nh), ones, ones])
    per_sec = wa // PROJ_CHUNK
    src = [sec * per_sec + c for sec in (0, 3, 1, 4, 2, 5) for c in range(per_sec)]
    (qkv,) = _project(h, g_mix, w_in.astype(BF16), gain, [True] * (4 * per_sec) + [False] * (2 * per_sec),
                      src_chunks=src)
    qkv = qkv.reshape(B, S, 3 * D_MODEL)

    tab = rel_bias.T
    r_tab = jnp.concatenate([_strip_rows(tab[:MOBA_HEADS], S // TQ, _causal_mask),
                             _strip_rows(tab[MOBA_HEADS:], S // TQ, _dilation_log_count)])
    o = _attn0(qkv, r_tab, _one_hot_blocks(S, MOBA_BLOCK, S // MOBA_BLOCK))
    return o.reshape(B * S, D_MODEL)


def _mixer_nsa(h, g_mix, w_in, qn, kn_c, kn_s, kn_w, cmp_k, cmp_v, rel_bias, B, S):
    qw = N_HEADS * HEAD_DIM
    kvw = NSA_GROUPS * HEAD_DIM
    order = _NSA_HEAD_ORDER
    w_main = jnp.concatenate([w_in[:, hd * HEAD_DIM:(hd + 1) * HEAD_DIM] for hd in order]
                             + [w_in[:, qw:qw + 6 * kvw]], axis=1).astype(BF16)
    ones = jnp.ones((kvw,), F32)
    gain = jnp.concatenate([jnp.tile(qn, N_HEADS) * ATTN_SCALE, ones, ones,
                            jnp.tile(kn_s, NSA_GROUPS), ones, jnp.tile(kn_w, NSA_GROUPS), ones])
    assert kvw == PROJ_CHUNK
    norm_chunks = [True] * (qw // PROJ_CHUNK) + [False, False, True, False, True, False]

    gcols = np.zeros((LANES,), np.int64)
    gused = np.zeros((LANES,), np.float32)
    for p in range(N_PAIRS):
        for br in range(3):
            for hh in range(2):
                c = GATE_STRIDE * p + 2 * br + hh
                gcols[c] = qw + 6 * kvw + 3 * order[2 * p + hh] + br
                gused[c] = 1.0
    w_gate = (w_in[:, gcols] * gused).astype(BF16)
    kc_chunk = qw // PROJ_CHUNK
    qkv, graw, raw = _project(h, g_mix, w_main, gain, norm_chunks, w_gate, raw_chunks=(kc_chunk, kc_chunk + 1))
    qkv = qkv.reshape(B, S, qw + 6 * kvw)
    graw = graw.reshape(B, S, LANES)
    raw = raw.reshape(B, S, 2 * kvw)
    kc = _compress(raw, 0, *cmp_k, kn_c, True)
    vc = _compress(raw, kvw // LANES, *cmp_v, kn_c, False)

    tab = rel_bias.T[order]
    n_cmp = (S - NSA_CMP_LEN) // NSA_CMP_STRIDE + 1
    t_pos = np.arange(S)[:, None]
    c_idx = np.arange(LANES)[None, :]
    dc = t_pos - (c_idx * NSA_CMP_STRIDE + NSA_CMP_LEN - 1)
    cmp_mask = np.where((dc >= 0) & (c_idx < n_cmp), 0.0, NEG).astype(np.float32)
    onehot = jnp.asarray(_bucket(dc).astype(np.int8)).reshape(1, -1) == jnp.arange(REL_BUCKETS, dtype=jnp.int8)[:, None]
    bias_c = jnp.dot(tab, onehot.astype(F32), precision=lax.Precision.HIGHEST) + cmp_mask.reshape(1, -1)
    bias_c = bias_c.reshape(2, 8, S, LANES)
    n_sel = S // NSA_SEL_BLOCK
    cstart = np.arange(LANES) * NSA_CMP_STRIDE
    sstart = np.arange(n_sel) * NSA_SEL_BLOCK
    ovt = np.maximum(np.minimum(cstart[None, :] + NSA_CMP_LEN, sstart[:, None] + NSA_SEL_BLOCK)
                     - np.maximum(cstart[None, :], sstart[:, None]), 0).astype(np.float32)
    ovt[:, n_cmp:] = 0.0
    ocmp, pen = _cmp_attention(qkv, kc, vc, bias_c, jnp.asarray(ovt))

    r_slc = _strip_rows(tab, S // TQ, _causal_mask)
    r_win = _strip_rows(tab, NSA_WINDOW // TQ + 1, _window_mask)
    o = _attn1(qkv, r_slc, r_win, _one_hot_blocks(S, NSA_SEL_BLOCK, n_sel), pen, ocmp, graw)
    return o.reshape(B * S, D_MODEL)


def kernel(x, p, rel_bias, norm_mix, norm_ffn, norm_ple, w_ffn_gate, w_ffn_up, w_ffn_down, w_ple_proj, w_ple_gate, w_in_ab, w_out_ab, qn_moba, kn_moba, qn_dil, kn_dil, w_in_nsa, w_out_nsa, qn_nsa, kn_cmp, kn_slc, kn_win, cmp_k_pos, cmp_k_w1, cmp_k_b1, cmp_k_w2, cmp_k_b2, cmp_v_pos, cmp_v_w1, cmp_v_b1, cmp_v_w2, cmp_v_b2):
    B, S, D = x.shape
    depth = p.shape[0]
    assert D == D_MODEL and S == MOBA_NBLK * MOBA_BLOCK and (B * S) % ROW_TILE == 0, (B, S, D)
    h = x.reshape(B * S, D)
    wg, wu, wd, wpg, wpp = (w.astype(BF16) for w in (w_ffn_gate, w_ffn_up, w_ffn_down, w_ple_gate, w_ple_proj))
    for i in range(depth):
        e = i // 2
        if i % 2 == 0:
            o = _mixer_ab(h, norm_mix[i], w_in_ab[e], qn_moba[e], kn_moba[e], qn_dil[e], kn_dil[e], rel_bias, B, S)
            w_out = w_out_ab[e]
        else:
            cmp_k = (cmp_k_pos[e], cmp_k_w1[e], cmp_k_b1[e], cmp_k_w2[e], cmp_k_b2[e])
            cmp_v = (cmp_v_pos[e], cmp_v_w1[e], cmp_v_b1[e], cmp_v_w2[e], cmp_v_b2[e])
            o = _mixer_nsa(h, norm_mix[i], w_in_nsa[e], qn_nsa[e], kn_cmp[e], kn_slc[e], kn_win[e],
                           cmp_k, cmp_v, rel_bias, B, S)
            w_out = w_out_nsa[e][_head_cols(_NSA_HEAD_ORDER), :]
        h = _post_attention(o, h, w_out.astype(BF16), norm_ffn[i], wg, wu, wd, norm_ple[i], wpg,
                            p.reshape(depth, B * S, -1), wpp, i)
    return h.reshape(B, S, D)
```

```python
import functools
import math

import numpy as np
import jax
import jax.numpy as jnp
from jax import lax
from jax.experimental import pallas as pl
from jax.experimental.pallas import tpu as pltpu

F32 = jnp.float32
BF16 = jnp.bfloat16

D_MODEL = 1024
HEAD_DIM = 64
N_HEADS = 16
N_PAIRS = N_HEADS // 2
MOBA_HEADS = 8
MOBA_BLOCK = 256
MOBA_TOPK = 3
MOBA_NBLK = 8
NSA_GROUPS = 4
NSA_CMP_LEN = 32
NSA_CMP_STRIDE = 16
NSA_SEL_BLOCK = 64
NSA_SEL_TOPN = 16
NSA_WINDOW = 512
NSA_FORCE = 1.0e6
REL_BUCKETS = 32
REL_MAX_DIST = 2048
RMS_EPS = 1e-6
ATTN_SCALE = HEAD_DIM ** -0.5

LANES = 128
SUBLANES = 8
TQ = 256
NEG = -1.0e30
ROW_TILE = 512
PROJ_CHUNK = 256
POST_SPLIT = 2
GATE_STRIDE = LANES // N_HEADS
ATTN0_ROWS = 2
CMP_ROWS = 2048
VMEM_LIMIT = 48 * 1024 * 1024
POST_VMEM_LIMIT = 56 * 1024 * 1024
ATTN_VMEM_LIMIT = 56 * 1024 * 1024

NT_DIMS = (((1,), (1,)), ((), ()))


def _cparams(n_axes):
    return pltpu.CompilerParams(dimension_semantics=("arbitrary",) * n_axes,
                                vmem_limit_bytes=VMEM_LIMIT)


def _rms_rows(x, g):
    ms = jnp.mean(x * x, axis=-1, keepdims=True)
    return x * lax.rsqrt(ms + RMS_EPS) * g


def _lane_lo(shape):
    return lax.broadcasted_iota(jnp.int32, shape, len(shape) - 1) < HEAD_DIM


def _resident(shape):
    return pl.BlockSpec(shape, lambda i: (0,) * len(shape), pipeline_mode=pl.Buffered(1))


def _head_sum_squares(y):
    y2 = y * y
    parts = []
    for blk in range(y.shape[1] // LANES):
        yb = y2[:, blk * LANES:(blk + 1) * LANES]
        lo = _lane_lo(yb.shape)
        s_lo = jnp.sum(jnp.where(lo, yb, 0.0), axis=1, keepdims=True)
        s_hi = jnp.sum(jnp.where(lo, 0.0, yb), axis=1, keepdims=True)
        parts.append(jnp.where(lo, s_lo, s_hi))
    return jnp.concatenate(parts, axis=1)


def _proj_kernel(x_ref, g_ref, w_ref, cg_ref, *rest, norm_chunks, src_chunks, raw_chunks, with_extra):
    if with_extra:
        wx_ref, o_ref, ox_ref, oraw_ref = rest
    else:
        (o_ref,) = rest
    xn = _rms_rows(x_ref[...], g_ref[...]).astype(BF16)

    def finish(c, y):
        cols = slice(c * PROJ_CHUNK, (c + 1) * PROJ_CHUNK)
        if c in raw_chunks:
            k = raw_chunks.index(c)
            oraw_ref[:, k * PROJ_CHUNK:(k + 1) * PROJ_CHUNK] = y
        if norm_chunks[c]:
            y = y * lax.rsqrt(_head_sum_squares(y) * (1.0 / HEAD_DIM) + RMS_EPS) * cg_ref[:, cols]
        o_ref[:, cols] = y.astype(o_ref.dtype)

    prev = None
    for c in range(len(norm_chunks)):
        src = src_chunks[c] * PROJ_CHUNK
        y = jnp.dot(xn, w_ref[:, src:src + PROJ_CHUNK], preferred_element_type=F32)
        if prev is not None:
            finish(c - 1, prev)
        prev = y
    finish(len(norm_chunks) - 1, prev)
    if with_extra:
        ox_ref[...] = jnp.dot(xn, wx_ref[...], preferred_element_type=F32)


def _project(x, g, w, col_gain, norm_chunks, w_extra=None, raw_chunks=(), src_chunks=None):
    if src_chunks is None:
        src_chunks = range(len(norm_chunks))
    T, D = x.shape
    N = w.shape[1]
    row_in = lambda n: pl.BlockSpec((ROW_TILE, n), lambda i: (i, 0))
    in_specs = [row_in(D), _resident((1, D)), _resident((D, N)), _resident((1, N))]
    args = [x, g.reshape(1, D), w, col_gain.reshape(1, N)]
    out_specs = [row_in(N)]
    out_shape = [jax.ShapeDtypeStruct((T, N), BF16)]
    if w_extra is not None:
        nx = w_extra.shape[1]
        in_specs.append(_resident((D, nx)))
        args.append(w_extra)
        out_specs += [row_in(nx), row_in(len(raw_chunks) * PROJ_CHUNK)]
        out_shape += [jax.ShapeDtypeStruct((T, nx), F32), jax.ShapeDtypeStruct((T, len(raw_chunks) * PROJ_CHUNK), F32)]
    return pl.pallas_call(
        functools.partial(_proj_kernel, norm_chunks=tuple(norm_chunks), src_chunks=tuple(src_chunks),
                          raw_chunks=tuple(raw_chunks),
                          with_extra=w_extra is not None),
        grid=(T // ROW_TILE,),
        in_specs=in_specs,
        out_specs=out_specs,
        out_shape=out_shape,
        compiler_params=_cparams(1),
        name="proj",
    )(*args)


def _sigmoid(z):
    return 1.0 / (1.0 + jnp.exp(-z))


def _post_kernel(o_ref, h_ref, wout_ref, gf_ref, wg_ref, wu_ref, wd_ref, gp_ref, wpg_ref, p_ref, wpp_ref, out_ref):
    halves = [slice(r * ROW_TILE // POST_SPLIT, (r + 1) * ROW_TILE // POST_SPLIT) for r in range(POST_SPLIT)]
    h1 = [h_ref[r, :] + jnp.dot(o_ref[r, :], wout_ref[...], preferred_element_type=F32) for r in halves]
    xn = [_rms_rows(x, gf_ref[...]).astype(BF16) for x in h1]
    a = [jnp.dot(x, wg_ref[...], preferred_element_type=F32) for x in xn]
    u = [jnp.dot(x, wu_ref[...], preferred_element_type=F32) for x in xn]
    act = [(ai * _sigmoid(ai) * ui).astype(BF16) for ai, ui in zip(a, u)]
    h2 = [x + jnp.dot(t, wd_ref[...], preferred_element_type=F32) for x, t in zip(h1, act)]
    hn = [_rms_rows(x, gp_ref[...]).astype(BF16) for x in h2]
    gate = [_sigmoid(jnp.dot(x, wpg_ref[...], preferred_element_type=F32)) for x in hn]
    for r, x, g in zip(halves, h2, gate):
        out_ref[r, :] = x + g * jnp.dot(p_ref[r, :].astype(BF16), wpp_ref[...], preferred_element_type=F32)


def _post_attention(o, h, w_out, g_ffn, wg, wu, wd, g_ple, w_pgate, p, w_pproj, layer):
    T, D = h.shape
    Fh = wg.shape[2]
    Pd = p.shape[2]
    row_in = lambda n: pl.BlockSpec((ROW_TILE, n), lambda i: (i, 0))
    slab = lambda r, c: pl.BlockSpec((None, r, c), lambda i: (layer, 0, 0), pipeline_mode=pl.Buffered(1))
    return pl.pallas_call(
        _post_kernel,
        grid=(T // ROW_TILE,),
        in_specs=[row_in(D), row_in(D), _resident((D, D)), _resident((1, D)), slab(D, Fh),
                  slab(D, Fh), slab(Fh, D), _resident((1, D)), slab(D, D),
                  pl.BlockSpec((None, ROW_TILE, Pd), lambda i: (layer, i, 0)), slab(Pd, D)],
        out_specs=row_in(D),
        out_shape=jax.ShapeDtypeStruct((T, D), F32),
        compiler_params=pltpu.CompilerParams(dimension_semantics=("arbitrary",), vmem_limit_bytes=POST_VMEM_LIMIT),
        name="post_attention",
    )(o, h, w_out, g_ffn.reshape(1, D), wg, wu, wd, g_ple.reshape(1, D), w_pgate, p, w_pproj)


def _fill_strips(strip_ref, r_ref):
    for h in range(2):
        for o in range(r_ref.shape[2]):
            x = jnp.broadcast_to(r_ref[0, h, o:o + 1, :], (TQ, 2 * TQ))
            strip_ref[h, o * TQ:(o + 1) * TQ, :] = pltpu.roll(x, 0, 1, stride=1, stride_axis=0)[:, :TQ]


def _tile_groups(nblk, span):
    cnt = [min(span, nblk - j) for j in range(nblk)]
    off = [0]
    for c in cnt:
        off.append(off[-1] + c * TQ)
    return cnt, off


def _tile_rows(nblk, span):
    return _tile_groups(nblk, span)[1][-1]


def _pair_attention(q_rows, k_rows, v_rows, strip_ref, s_ref, p_ref, o_ref, nblk, span, skew=1):
    cnt, off = _tile_groups(nblk, span)
    together = s_ref.shape[0] >= 2 * off[-1] and p_ref.shape[2] >= 2 * span * TQ
    s0 = [0, off[-1] if together else 0]
    p0 = [0, span * TQ if together else 0]

    def scores(h, j):
        n = cnt[j] * TQ
        s = lax.dot_general(q_rows(h, j * TQ, j * TQ + n), k_rows(h, j), NT_DIMS, preferred_element_type=F32)
        s_ref[s0[h] + off[j]:s0[h] + off[j] + n, :] = s + strip_ref[h, 0:n, :]

    def probs(h, i):
        j0 = max(0, i - span + 1)
        rows = [s0[h] + off[j] + (i - j) * TQ for j in range(j0, i + 1)]
        mx = s_ref[rows[0]:rows[0] + TQ, :]
        for r in rows[1:]:
            mx = jnp.maximum(mx, s_ref[r:r + TQ, :])
        m = jnp.max(mx, axis=1, keepdims=True)
        for t, r in enumerate(rows):
            p_ref[i, :, p0[h] + t * TQ:p0[h] + (t + 1) * TQ] = jnp.exp(s_ref[r:r + TQ, :] - m).astype(BF16)

    def values(h, i):
        j0 = max(0, i - span + 1)
        out = jnp.dot(p_ref[i, :, p0[h]:p0[h] + (i + 1 - j0) * TQ], v_rows(j0 * TQ, (i + 1) * TQ),
                      preferred_element_type=F32)
        o_ref[h, i * TQ:(i + 1) * TQ, :] = out[:, :LANES] / out[:, LANES:]

    for heads in ([(0, 1)] if together else [(0,), (1,)]):
        for j in range(nblk + skew):
            for h in heads:
                if j < nblk:
                    scores(h, j)
            for h in heads:
                if j >= skew:
                    values(h, j - skew)
            for h in heads:
                if j < nblk:
                    probs(h, j)


def _merge_heads(o_ref):
    return jnp.where(_lane_lo((o_ref.shape[1], LANES)), o_ref[0], o_ref[1])


def _split_heads(q):
    lo = _lane_lo(q.shape)
    zero = jnp.zeros_like(q)
    return jnp.where(lo, q, zero), jnp.where(lo, zero, q)


def _fill_value_ones(vaug_ref):
    vaug_ref[:, LANES:] = jnp.ones((vaug_ref.shape[0], LANES), vaug_ref.dtype)


def _moba_block_means(k_ref, bb):
    shape = (2 * MOBA_NBLK, LANES)
    lo = _lane_lo(shape)
    row = lax.broadcasted_iota(jnp.int32, shape, 0)
    kmt = jnp.zeros(shape, F32)
    for n in range(MOBA_NBLK):
        mean = jnp.mean(k_ref[bb, n * MOBA_BLOCK:(n + 1) * MOBA_BLOCK, :].astype(F32), axis=0, keepdims=True)
        kmt = jnp.where(row == n, jnp.where(lo, mean, 0.0), kmt)
        kmt = jnp.where(row == MOBA_NBLK + n, jnp.where(lo, 0.0, mean), kmt)
    return kmt


def _moba_penalty(q, kmt):
    nb = MOBA_NBLK
    S = q.shape[0]
    hi = kmt.astype(BF16)
    rest = kmt - hi.astype(F32)
    mid = rest.astype(BF16)
    lo = (rest - mid.astype(F32)).astype(BF16)
    g3 = lax.dot_general(jnp.concatenate([hi, mid, lo], axis=0), q, NT_DIMS, preferred_element_type=F32)
    g = g3[0:2 * nb] + g3[2 * nb:4 * nb] + g3[4 * nb:6 * nb]
    row = lax.broadcasted_iota(jnp.int32, g.shape, 0)
    n = row & (nb - 1)
    own = lax.shift_right_logical(lax.broadcasted_iota(jnp.int32, g.shape, 1), int(math.log2(MOBA_BLOCK)))
    rank = jnp.zeros(g.shape, F32)
    for m in range(nb - 1):
        gm = jnp.where(row < nb, g[m:m + 1, :], g[nb + m:nb + m + 1, :])
        tie = jnp.where(n > m, 1.0, 0.0)
        beats = jnp.where(gm > g, 1.0, jnp.where(gm == g, tie, 0.0))
        rank = rank + jnp.where(own > m, beats, 0.0)
    keep = jnp.where(n < own, jnp.where(rank < MOBA_TOPK, 1.0, 0.0), jnp.where(n == own, 1.0, 0.0))
    keep = jnp.concatenate([keep, jnp.zeros((LANES - 2 * nb, S), F32)], axis=0).T
    lane = lax.broadcasted_iota(jnp.int32, keep.shape, 1)
    return jnp.where(lane < 2 * nb, (keep - 1.0) * (-NEG), 0.0)


def _attn0_kernel(q_ref, k_ref, v_ref, r_ref, e_ref, o_ref, strip_ref, *scratch):
    p = pl.program_id(0)
    b = pl.program_id(1)
    S = q_ref.shape[1]
    nblk = S // TQ
    n_set = len(scratch) // ATTN0_ROWS
    sets = [scratch[t * n_set:(t + 1) * n_set] for t in range(ATTN0_ROWS)]
    is_moba = p < MOBA_HEADS // 2

    @pl.when(b == 0)
    def _():
        _fill_strips(strip_ref, r_ref)

    @pl.when(jnp.logical_and(p == 0, b == 0))
    def _():
        for _, kaug_ref, vaug_ref, _, _, _ in sets:
            for h in range(2):
                kaug_ref[h, :, LANES:] = e_ref[h]
            _fill_value_ones(vaug_ref)

    for bb, (qaug_ref, kaug_ref, vaug_ref, _, _, _) in enumerate(sets):
        k = k_ref[bb]
        for h, qh in enumerate(_split_heads(q_ref[bb])):
            kaug_ref[h, :, :LANES] = k
            qaug_ref[h, :, :LANES] = qh
        vaug_ref[:, :LANES] = v_ref[bb]

    @pl.when(is_moba)
    def _():
        for bb, (qaug_ref, _, _, _, _, _) in enumerate(sets):
            pen = _moba_penalty(q_ref[bb], _moba_block_means(k_ref, bb)).astype(BF16)
            for h in range(2):
                qaug_ref[h, :, LANES:] = pen

    @pl.when(jnp.logical_not(is_moba))
    def _():
        for qaug_ref, _, _, _, _, _ in sets:
            for h in range(2):
                qaug_ref[h, :, LANES:] = jnp.zeros((S, LANES), BF16)

    for bb, (qaug_ref, kaug_ref, vaug_ref, s_ref, p_ref, oh_ref) in enumerate(sets):
        _pair_attention(lambda h, r0, r1, ref=qaug_ref: ref[h, r0:r1, :],
                        lambda h, j, ref=kaug_ref: ref[h, j * TQ:(j + 1) * TQ, :],
                        lambda r0, r1, ref=vaug_ref: ref[r0:r1, :],
                        strip_ref, s_ref, p_ref, oh_ref, nblk, nblk)
        o_ref[bb] = _merge_heads(oh_ref).astype(o_ref.dtype)


def _strip_scratch(rows):
    return pltpu.VMEM((2, rows, TQ), F32)


def _attn_set(S, span, narrow_span=0):
    nblk = S // TQ
    s_rows = max(_tile_rows(nblk, span), 2 * _tile_rows(nblk, narrow_span))
    return [pltpu.VMEM((2, S, 2 * LANES), BF16), pltpu.VMEM((2, S, 2 * LANES), BF16),
            pltpu.VMEM((S, 2 * LANES), BF16), pltpu.VMEM((s_rows, TQ), F32),
            pltpu.VMEM((nblk, TQ, span * TQ), BF16), pltpu.VMEM((2, S, LANES), F32)]


def _attn0(qkv, r_tab, e_onehot):
    B, S, _ = qkv.shape
    nblk = S // TQ
    return pl.pallas_call(
        _attn0_kernel,
        grid=(N_PAIRS, B // ATTN0_ROWS),
        in_specs=[
            pl.BlockSpec((ATTN0_ROWS, S, LANES), lambda p, b: (b, 0, p)),
            pl.BlockSpec((ATTN0_ROWS, S, LANES), lambda p, b: (b, 0, N_PAIRS + p)),
            pl.BlockSpec((ATTN0_ROWS, S, LANES), lambda p, b: (b, 0, 2 * N_PAIRS + p)),
            pl.BlockSpec((1, 2, nblk, 2 * TQ), lambda p, b: (p, 0, 0, 0)),
            pl.BlockSpec((2, S, LANES), lambda p, b: (0, 0, 0), pipeline_mode=pl.Buffered(1)),
        ],
        out_specs=pl.BlockSpec((ATTN0_ROWS, S, LANES), lambda p, b: (b, 0, p)),
        out_shape=jax.ShapeDtypeStruct((B, S, D_MODEL), BF16),
        scratch_shapes=[_strip_scratch(S)] + _attn_set(S, nblk) * ATTN0_ROWS,
        compiler_params=pltpu.CompilerParams(dimension_semantics=("arbitrary",) * 2, vmem_limit_bytes=ATTN_VMEM_LIMIT),
        name="attn_moba_dilated",
    )(qkv, qkv, qkv, r_tab, e_onehot)


def _compress_kernel(x_ref, pos_ref, w1_ref, b1_ref, w2_ref, b2_ref, gain_ref, o_ref, *, normed):
    n_chunk = x_ref.shape[1] // NSA_CMP_STRIDE
    first = second = None
    for a in range(NSA_CMP_STRIDE):
        t = x_ref[0, pl.ds(a, n_chunk, stride=NSA_CMP_STRIDE), :]
        fa = jnp.dot((t + pos_ref[a:a + 1, :]).astype(BF16), w1_ref[a], preferred_element_type=F32)
        sa = jnp.dot((t + pos_ref[NSA_CMP_STRIDE + a:NSA_CMP_STRIDE + a + 1, :]).astype(BF16),
                     w1_ref[NSA_CMP_STRIDE + a], preferred_element_type=F32)
        first = fa if first is None else first + fa
        second = sa if second is None else second + sa
    hid = first + pltpu.roll(second, n_chunk - 1, 0) + b1_ref[...]
    cdf = 0.5 * (1.0 + jnp.tanh(math.sqrt(2.0 / math.pi) * (hid + 0.044715 * (hid * hid * hid))))
    y = jnp.dot((hid * cdf).astype(BF16), w2_ref[...], preferred_element_type=F32) + b2_ref[...]
    if normed:
        lo = _lane_lo(y.shape)
        y2 = y * y
        s_lo = jnp.sum(jnp.where(lo, y2, 0.0), axis=1, keepdims=True)
        s_hi = jnp.sum(jnp.where(lo, 0.0, y2), axis=1, keepdims=True)
        ms = jnp.where(lo, s_lo, s_hi) * (1.0 / HEAD_DIM)
        y = y * lax.rsqrt(ms + RMS_EPS) * gain_ref[...]
    o_ref[0, 0] = y.astype(o_ref.dtype)


def _pair_block_diag(w):
    z = jnp.zeros_like(w)
    return jnp.concatenate([jnp.concatenate([w, z], axis=-1), jnp.concatenate([z, w], axis=-1)], axis=-2)


def _compress(raw, col_block0, pos, w1, b1, w2, b2, gain, normed):
    B, S, _ = raw.shape
    hid = w1.shape[1]
    n_pos = pos.shape[0]
    w1bd = _pair_block_diag(w1.reshape(n_pos, HEAD_DIM, hid)).astype(BF16)
    return pl.pallas_call(
        functools.partial(_compress_kernel, normed=normed),
        grid=(B, 2),
        in_specs=[
            pl.BlockSpec((1, S, LANES), lambda b, m: (b, 0, col_block0 + m)),
            pl.BlockSpec((n_pos, LANES), lambda b, m: (0, 0)),
            pl.BlockSpec((n_pos, LANES, 2 * hid), lambda b, m: (0, 0, 0)),
            pl.BlockSpec((1, 2 * hid), lambda b, m: (0, 0)),
            pl.BlockSpec((2 * hid, LANES), lambda b, m: (0, 0)),
            pl.BlockSpec((1, LANES), lambda b, m: (0, 0)),
            pl.BlockSpec((1, LANES), lambda b, m: (0, 0)),
        ],
        out_specs=pl.BlockSpec((1, 1, S // NSA_CMP_STRIDE, LANES), lambda b, m: (b, m, 0, 0)),
        out_shape=jax.ShapeDtypeStruct((B, 2, S // NSA_CMP_STRIDE, LANES), BF16),
        compiler_params=_cparams(2),
        name="nsa_compress",
    )(raw, jnp.tile(pos, (1, 2)), w1bd, jnp.tile(b1, 2).reshape(1, 2 * hid), _pair_block_diag(w2).astype(BF16),
      jnp.tile(b2, 2).reshape(1, LANES), jnp.tile(gain, 2).reshape(1, LANES))


def _cmp_kernel(q_ref, kc_ref, vc_ref, bias_ref, ovt_ref, o_ref, pen_ref):
    kc = kc_ref[0, 0]
    vc = vc_ref[0, 0]
    S = q_ref.shape[1]
    n_sel = ovt_ref.shape[0]
    lo = _lane_lo((CMP_ROWS, LANES))
    blk = lax.broadcasted_iota(jnp.int32, (n_sel, CMP_ROWS), 0)
    for c in range(S // CMP_ROWS):
        rows = slice(c * CMP_ROWS, (c + 1) * CMP_ROWS)
        psum = [jnp.zeros((CMP_ROWS, LANES), F32), jnp.zeros((CMP_ROWS, LANES), F32)]
        for r in range(4):
            heads = _split_heads(q_ref[0, rows, r * LANES:(r + 1) * LANES])
            outs = []
            for h in range(2):
                s = lax.dot_general(heads[h], kc, NT_DIMS, preferred_element_type=F32) + bias_ref[0, 2 * r + h, rows, :]
                m = jnp.max(s, axis=1, keepdims=True)
                e = jnp.exp(s - m)
                l = jnp.sum(e, axis=1, keepdims=True)
                pr = e * jnp.where(m > 0.5 * NEG, 1.0 / l, 0.0)
                psum[h] = psum[h] + pr
                outs.append(jnp.dot(pr.astype(BF16), vc, preferred_element_type=F32))
            o_ref[0, rows, r * LANES:(r + 1) * LANES] = jnp.where(lo, outs[0], outs[1])

        t = c * CMP_ROWS + lax.broadcasted_iota(jnp.int32, (n_sel, CMP_ROWS), 1)
        cur = lax.shift_right_logical(t, int(math.log2(NSA_SEL_BLOCK)))
        keeps = []
        for h in range(2):
            imp = lax.dot_general(ovt_ref[...], psum[h], NT_DIMS,
                                  precision=lax.Precision.HIGHEST, preferred_element_type=F32)
            forced = jnp.where(blk == 0, 1.0, jnp.where(blk == cur, 1.0, jnp.where(blk == cur - 1, 1.0, 0.0)))
            imp = jnp.where(blk <= cur, imp + forced * NSA_FORCE, -jnp.inf)
            groups = [imp[g * SUBLANES:(g + 1) * SUBLANES, :] for g in range(n_sel // SUBLANES)]
            ranks = [jnp.zeros(g.shape, F32) for g in groups]
            for m in range(n_sel):
                im = imp[m:m + 1, :]
                for g, sub in enumerate(groups):
                    ge = jnp.where(im >= sub, 1.0, 0.0)
                    gt = jnp.where(im > sub, 1.0, 0.0)
                    if g * SUBLANES > m:
                        inc = ge
                    elif (g + 1) * SUBLANES - 1 <= m:
                        inc = gt
                    else:
                        inc = jnp.where(lax.broadcasted_iota(jnp.int32, sub.shape, 0) + g * SUBLANES > m, ge, gt)
                    ranks[g] = ranks[g] + inc
            keeps.append(jnp.where(jnp.concatenate(ranks, axis=0) < NSA_SEL_TOPN, 1.0, 0.0))
        keep = jnp.concatenate(keeps + [jnp.zeros((LANES - 2 * n_sel, CMP_ROWS), F32)], axis=0).T
        lane = lax.broadcasted_iota(jnp.int32, keep.shape, 1)
        pen_ref[0, 0, rows, :] = jnp.where(lane < 2 * n_sel, (keep - 1.0) * (-NEG), 0.0).astype(pen_ref.dtype)


def _cmp_attention(qkv, kc, vc, bias, ovt):
    B, S, _ = qkv.shape
    n_sel = ovt.shape[0]
    return pl.pallas_call(
        _cmp_kernel,
        grid=(2, B),
        in_specs=[
            pl.BlockSpec((1, S, 4 * LANES), lambda m, b: (b, 0, m)),
            pl.BlockSpec((1, 1, LANES, LANES), lambda m, b: (b, m, 0, 0)),
            pl.BlockSpec((1, 1, LANES, LANES), lambda m, b: (b, m, 0, 0)),
            pl.BlockSpec((1, 8, S, LANES), lambda m, b: (m, 0, 0, 0), pipeline_mode=pl.Buffered(1)),
            pl.BlockSpec((n_sel, LANES), lambda m, b: (0, 0)),
        ],
        out_specs=[
            pl.BlockSpec((1, S, 4 * LANES), lambda m, b: (b, 0, m)),
            pl.BlockSpec((1, 1, S, LANES), lambda m, b: (b, m, 0, 0)),
        ],
        out_shape=[jax.ShapeDtypeStruct((B, S, D_MODEL), F32),
                   jax.ShapeDtypeStruct((B, 2, S, LANES), BF16)],
        compiler_params=_cparams(2),
        name="nsa_compressed_select",
    )(qkv, kc, vc, bias, ovt)


def _attn1_kernel(q_ref, ks_ref, vs_ref, kw_ref, vw_ref, rs_ref, rw_ref, e_ref, pen_ref, ocmp_ref, graw_ref,
                  o_ref, strip_s_ref, qaug_ref, kaug_ref, vsaug_ref, s_ref, p_ref, oslc_ref,
                  strip_w_ref, vwaug_ref, owin_ref):
    p = pl.program_id(0)
    b = pl.program_id(1)
    S = q_ref.shape[1]
    nblk = S // TQ

    @pl.when(b == 0)
    def _():
        _fill_strips(strip_s_ref, rs_ref)
        _fill_strips(strip_w_ref, rw_ref)

    @pl.when(jnp.logical_and(p == 0, b == 0))
    def _():
        for h in range(2):
            kaug_ref[h, :, LANES:] = e_ref[h]
        _fill_value_ones(vsaug_ref)
        _fill_value_ones(vwaug_ref)

    ks = ks_ref[0]
    pen = pen_ref[0, 0]
    for h, qh in enumerate(_split_heads(q_ref[0])):
        kaug_ref[h, :, :LANES] = ks
        qaug_ref[h, :, :LANES] = qh
        qaug_ref[h, :, LANES:] = pen
    vsaug_ref[:, :LANES] = vs_ref[0]
    vwaug_ref[:, :LANES] = vw_ref[0]
    _pair_attention(lambda h, r0, r1: qaug_ref[h, r0:r1, :],
                    lambda h, j: kaug_ref[h, j * TQ:(j + 1) * TQ, :],
                    lambda r0, r1: vsaug_ref[r0:r1, :],
                    strip_s_ref, s_ref, p_ref, oslc_ref, nblk, nblk)
    _pair_attention(lambda h, r0, r1: qaug_ref[h, r0:r1, :LANES],
                    lambda h, j: kw_ref[0, j * TQ:(j + 1) * TQ, :],
                    lambda r0, r1: vwaug_ref[r0:r1, :],
                    strip_w_ref, s_ref, p_ref, owin_ref, nblk, strip_w_ref.shape[1] // TQ, skew=2)
    sig = _sigmoid(pltpu.roll(graw_ref[0], lax.rem(LANES - GATE_STRIDE * p, LANES), 1))
    lo_lanes = _lane_lo((S, LANES))
    gates = [jnp.where(lo_lanes, sig[:, 2 * br:2 * br + 1], sig[:, 2 * br + 1:2 * br + 2]) for br in range(3)]
    out = gates[0] * ocmp_ref[0] + gates[1] * _merge_heads(oslc_ref) + gates[2] * _merge_heads(owin_ref)
    o_ref[0] = out.astype(o_ref.dtype)


def _attn1(qkv, r_slc, r_win, e_onehot, pen, ocmp, graw):
    B, S, _ = qkv.shape
    kv0 = D_MODEL // LANES + 4
    n_win = r_win.shape[2]
    return pl.pallas_call(
        _attn1_kernel,
        grid=(N_PAIRS, B),
        in_specs=[
            pl.BlockSpec((1, S, LANES), lambda p, b: (b, 0, p)),
            pl.BlockSpec((1, S, LANES), lambda p, b: (b, 0, kv0 + p // 4)),
            pl.BlockSpec((1, S, LANES), lambda p, b: (b, 0, kv0 + 2 + p // 4)),
            pl.BlockSpec((1, S, LANES), lambda p, b: (b, 0, kv0 + 4 + p // 4)),
            pl.BlockSpec((1, S, LANES), lambda p, b: (b, 0, kv0 + 6 + p // 4)),
            pl.BlockSpec((1, 2, r_slc.shape[2], 2 * TQ), lambda p, b: (p, 0, 0, 0)),
            pl.BlockSpec((1, 2, n_win, 2 * TQ), lambda p, b: (p, 0, 0, 0)),
            pl.BlockSpec((2, S, LANES), lambda p, b: (0, 0, 0), pipeline_mode=pl.Buffered(1)),
            pl.BlockSpec((1, 1, S, LANES), lambda p, b: (b, p // 4, 0, 0)),
            pl.BlockSpec((1, S, LANES), lambda p, b: (b, 0, p)),
            pl.BlockSpec((1, S, LANES), lambda p, b: (b, 0, 0)),
        ],
        out_specs=pl.BlockSpec((1, S, LANES), lambda p, b: (b, 0, p)),
        out_shape=jax.ShapeDtypeStruct((B, S, D_MODEL), BF16),
        scratch_shapes=[_strip_scratch(S)] + _attn_set(S, S // TQ, n_win) + [_strip_scratch(n_win * TQ),
                                                    pltpu.VMEM((S, 2 * LANES), BF16), pltpu.VMEM((2, S, LANES), F32)],
        compiler_params=pltpu.CompilerParams(dimension_semantics=("arbitrary",) * 2, vmem_limit_bytes=ATTN_VMEM_LIMIT),
        name="attn_nsa",
    )(qkv, qkv, qkv, qkv, qkv, r_slc, r_win, e_onehot, pen, ocmp, graw)


def _bucket(dist):
    n = np.maximum(dist, 0)
    exact = REL_BUCKETS // 2
    nf = np.maximum(n, 1).astype(np.float64)
    large = exact + (np.log(nf / exact) / math.log(REL_MAX_DIST / exact) * (REL_BUCKETS - exact)).astype(np.int64)
    return np.where(n < exact, n, np.minimum(large, REL_BUCKETS - 1))


def _strip_rows(tab, n_off, extra_of_dist):
    u = np.arange(2 * TQ)
    d = np.arange(n_off)[:, None] * TQ - np.where(u < TQ, u, u - 2 * TQ)[None, :]
    onehot = (_bucket(d)[None] == np.arange(REL_BUCKETS)[:, None, None]).astype(np.float32)
    vals = jnp.einsum("hb,bou->hou", tab, jnp.asarray(onehot), precision=lax.Precision.HIGHEST) + extra_of_dist(d)
    return vals.reshape(tab.shape[0] // 2, 2, n_off, 2 * TQ)


def _causal_mask(d):
    return np.where(d >= 0, 0.0, NEG).astype(np.float32)


def _window_mask(d):
    return np.where((d >= 0) & (d < NSA_WINDOW), 0.0, NEG).astype(np.float32)


def _dilation_log_count(d):
    c = ((d >= 0) & (d <= 128)).astype(np.float64)
    c += ((d >= 0) & (d % 4 == 0) & (d <= 512))
    c += ((d >= 0) & (d % 16 == 0) & (d <= 2048))
    return np.where(c > 0, np.log(np.maximum(c, 1.0)), NEG).astype(np.float32)


def _one_hot_blocks(S, block, per_head):
    e = np.zeros((2, S, LANES), np.float32)
    key = np.arange(S)
    for h in range(2):
        e[h, key, h * per_head + key // block] = 1.0
    return jnp.asarray(e, BF16)


_NSA_HEAD_ORDER = np.array([8 * (p // 4) + (p % 4) + 4 * h for p in range(N_PAIRS) for h in range(2)])


def _head_cols(heads):
    return (np.asarray(heads)[:, None] * HEAD_DIM + np.arange(HEAD_DIM)[None, :]).reshape(-1)


def _mixer_ab(h, g_mix, w_in, qn_a, kn_a, qn_b, kn_b, rel_bias, B, S):
    wa = MOBA_HEADS * HEAD_DIM
    ones = jnp.ones((wa,), F32)
    nh = MOBA_HEADS
    gain = jnp.concatenate([jnp.tile(qn_a, nh) * ATTN_SCALE, jnp.tile(qn_b, nh) * ATTN_SCALE,
                            jnp.tile(kn_a, nh), jnp.tile(kn_b, nh), ones, ones])
    per_sec = wa // PROJ_CHUNK
    src = [sec * per_sec + c for sec in (0, 3, 1, 4, 2, 5) for c in range(per_sec)]
    (qkv,) = _project(h, g_mix, w_in.astype(BF16), gain, [True] * (4 * per_sec) + [False] * (2 * per_sec),
                      src_chunks=src)
    qkv = qkv.reshape(B, S, 3 * D_MODEL)

    tab = rel_bias.T
    r_tab = jnp.concatenate([_strip_rows(tab[:MOBA_HEADS], S // TQ, _causal_mask),
                             _strip_rows(tab[MOBA_HEADS:], S // TQ, _dilation_log_count)])
    o = _attn0(qkv, r_tab, _one_hot_blocks(S, MOBA_BLOCK, S // MOBA_BLOCK))
    return o.reshape(B * S, D_MODEL)


def _mixer_nsa(h, g_mix, w_in, qn, kn_c, kn_s, kn_w, cmp_k, cmp_v, rel_bias, B, S):
    qw = N_HEADS * HEAD_DIM
    kvw = NSA_GROUPS * HEAD_DIM
    order = _NSA_HEAD_ORDER
    w_main = jnp.concatenate([w_in[:, hd * HEAD_DIM:(hd + 1) * HEAD_DIM] for hd in order]
                             + [w_in[:, qw:qw + 6 * kvw]], axis=1).astype(BF16)
    ones = jnp.ones((kvw,), F32)
    gain = jnp.concatenate([jnp.tile(qn, N_HEADS) * ATTN_SCALE, ones, ones,
                            jnp.tile(kn_s, NSA_GROUPS), ones, jnp.tile(kn_w, NSA_GROUPS), ones])
    assert kvw == PROJ_CHUNK
    norm_chunks = [True] * (qw // PROJ_CHUNK) + [False, False, True, False, True, False]

    gcols = np.zeros((LANES,), np.int64)
    gused = np.zeros((LANES,), np.float32)
    for p in range(N_PAIRS):
        for br in range(3):
            for hh in range(2):
                c = GATE_STRIDE * p + 2 * br + hh
                gcols[c] = qw + 6 * kvw + 3 * order[2 * p + hh] + br
                gused[c] = 1.0
    w_gate = (w_in[:, gcols] * gused).astype(BF16)
    kc_chunk = qw // PROJ_CHUNK
    qkv, graw, raw = _project(h, g_mix, w_main, gain, norm_chunks, w_gate, raw_chunks=(kc_chunk, kc_chunk + 1))
    qkv = qkv.reshape(B, S, qw + 6 * kvw)
    graw = graw.reshape(B, S, LANES)
    raw = raw.reshape(B, S, 2 * kvw)
    kc = _compress(raw, 0, *cmp_k, kn_c, True)
    vc = _compress(raw, kvw // LANES, *cmp_v, kn_c, False)

    tab = rel_bias.T[order]
    n_cmp = (S - NSA_CMP_LEN) // NSA_CMP_STRIDE + 1
    t_pos = np.arange(S)[:, None]
    c_idx = np.arange(LANES)[None, :]
    dc = t_pos - (c_idx * NSA_CMP_STRIDE + NSA_CMP_LEN - 1)
    cmp_mask = np.where((dc >= 0) & (c_idx < n_cmp), 0.0, NEG).astype(np.float32)
    onehot = jnp.asarray(_bucket(dc).astype(np.int8)).reshape(1, -1) == jnp.arange(REL_BUCKETS, dtype=jnp.int8)[:, None]
    bias_c = jnp.dot(tab, onehot.astype(F32), precision=lax.Precision.HIGHEST) + cmp_mask.reshape(1, -1)
    bias_c = bias_c.reshape(2, 8, S, LANES)
    n_sel = S // NSA_SEL_BLOCK
    cstart = np.arange(LANES) * NSA_CMP_STRIDE
    sstart = np.arange(n_sel) * NSA_SEL_BLOCK
    ovt = np.maximum(np.minimum(cstart[None, :] + NSA_CMP_LEN, sstart[:, None] + NSA_SEL_BLOCK)
                     - np.maximum(cstart[None, :], sstart[:, None]), 0).astype(np.float32)
    ovt[:, n_cmp:] = 0.0
    ocmp, pen = _cmp_attention(qkv, kc, vc, bias_c, jnp.asarray(ovt))

    r_slc = _strip_rows(tab, S // TQ, _causal_mask)
    r_win = _strip_rows(tab, NSA_WINDOW // TQ + 1, _window_mask)
    o = _attn1(qkv, r_slc, r_win, _one_hot_blocks(S, NSA_SEL_BLOCK, n_sel), pen, ocmp, graw)
    return o.reshape(B * S, D_MODEL)


def kernel(x, p, rel_bias, norm_mix, norm_ffn, norm_ple, w_ffn_gate, w_ffn_up, w_ffn_down, w_ple_proj, w_ple_gate, w_in_ab, w_out_ab, qn_moba, kn_moba, qn_dil, kn_dil, w_in_nsa, w_out_nsa, qn_nsa, kn_cmp, kn_slc, kn_win, cmp_k_pos, cmp_k_w1, cmp_k_b1, cmp_k_w2, cmp_k_b2, cmp_v_pos, cmp_v_w1, cmp_v_b1, cmp_v_w2, cmp_v_b2):
    B, S, D = x.shape
    depth = p.shape[0]
    assert D == D_MODEL and S == MOBA_NBLK * MOBA_BLOCK and (B * S) % ROW_TILE == 0 and B % ATTN0_ROWS == 0, (B, S, D)
    h = x.reshape(B * S, D)
    wg, wu, wd, wpg, wpp = (w.astype(BF16) for w in (w_ffn_gate, w_ffn_up, w_ffn_down, w_ple_gate, w_ple_proj))
    for i in range(depth):
        e = i // 2
        if i % 2 == 0:
            o = _mixer_ab(h, norm_mix[i], w_in_ab[e], qn_moba[e], kn_moba[e], qn_dil[e], kn_dil[e], rel_bias, B, S)
            w_out = w_out_ab[e]
        else:
            cmp_k = (cmp_k_pos[e], cmp_k_w1[e], cmp_k_b1[e], cmp_k_w2[e], cmp_k_b2[e])
            cmp_v = (cmp_v_pos[e], cmp_v_w1[e], cmp_v_b1[e], cmp_v_w2[e], cmp_v_b2[e])
            o = _mixer_nsa(h, norm_mix[i], w_in_nsa[e], qn_nsa[e], kn_cmp[e], kn_slc[e], kn_win[e],
                           cmp_k, cmp_v, rel_bias, B, S)
            w_out = w_out_nsa[e][_head_cols(_NSA_HEAD_ORDER), :]
        h = _post_attention(o, h, w_out.astype(BF16), norm_ffn[i], wg, wu, wd, norm_ple[i], wpg,
                            p.reshape(depth, B * S, -1), wpp, i)
    return h.reshape(B, S, D)
```

```python
import functools
import math

import numpy as np
import jax
import jax.numpy as jnp
from jax import lax
from jax.experimental import pallas as pl
from jax.experimental.pallas import tpu as pltpu

F32 = jnp.float32
BF16 = jnp.bfloat16

D_MODEL = 1024
HEAD_DIM = 64
N_HEADS = 16
N_PAIRS = N_HEADS // 2
MOBA_HEADS = 8
MOBA_BLOCK = 256
MOBA_TOPK = 3
MOBA_NBLK = 8
NSA_GROUPS = 4
NSA_CMP_LEN = 32
NSA_CMP_STRIDE = 16
NSA_SEL_BLOCK = 64
NSA_SEL_TOPN = 16
NSA_WINDOW = 512
NSA_FORCE = 1.0e6
REL_BUCKETS = 32
REL_MAX_DIST = 2048
RMS_EPS = 1e-6
ATTN_SCALE = HEAD_DIM ** -0.5

LANES = 128
SUBLANES = 8
TQ = 256
NEG = -1.0e30
ROW_TILE = 512
PROJ_CHUNK = 256
POST_SPLIT = 2
GATE_STRIDE = LANES // N_HEADS
ATTN0_ROWS = 2
CMP_ROWS = 2048
VMEM_LIMIT = 48 * 1024 * 1024
POST_VMEM_LIMIT = 56 * 1024 * 1024
ATTN_VMEM_LIMIT = 56 * 1024 * 1024

NT_DIMS = (((1,), (1,)), ((), ()))


def _cparams(n_axes):
    return pltpu.CompilerParams(dimension_semantics=("arbitrary",) * n_axes,
                                vmem_limit_bytes=VMEM_LIMIT)


def _rms_rows(x, g):
    ms = jnp.mean(x * x, axis=-1, keepdims=True)
    return x * lax.rsqrt(ms + RMS_EPS) * g


def _lane_lo(shape):
    return lax.broadcasted_iota(jnp.int32, shape, len(shape) - 1) < HEAD_DIM


def _resident(shape):
    return pl.BlockSpec(shape, lambda i: (0,) * len(shape), pipeline_mode=pl.Buffered(1))


def _head_sum_squares(y):
    y2 = y * y
    parts = []
    for blk in range(y.shape[1] // LANES):
        yb = y2[:, blk * LANES:(blk + 1) * LANES]
        lo = _lane_lo(yb.shape)
        s_lo = jnp.sum(jnp.where(lo, yb, 0.0), axis=1, keepdims=True)
        s_hi = jnp.sum(jnp.where(lo, 0.0, yb), axis=1, keepdims=True)
        parts.append(jnp.where(lo, s_lo, s_hi))
    return jnp.concatenate(parts, axis=1)


def _proj_kernel(x_ref, g_ref, w_ref, cg_ref, *rest, norm_chunks, src_chunks, raw_chunks, with_extra):
    if with_extra:
        wx_ref, o_ref, ox_ref, oraw_ref = rest
    else:
        (o_ref,) = rest
    xn = _rms_rows(x_ref[...], g_ref[...]).astype(BF16)

    def finish(c, y):
        cols = slice(c * PROJ_CHUNK, (c + 1) * PROJ_CHUNK)
        if c in raw_chunks:
            k = raw_chunks.index(c)
            oraw_ref[:, k * PROJ_CHUNK:(k + 1) * PROJ_CHUNK] = y
        if norm_chunks[c]:
            y = y * lax.rsqrt(_head_sum_squares(y) * (1.0 / HEAD_DIM) + RMS_EPS) * cg_ref[:, cols]
        o_ref[:, cols] = y.astype(o_ref.dtype)

    prev = None
    for c in range(len(norm_chunks)):
        src = src_chunks[c] * PROJ_CHUNK
        y = jnp.dot(xn, w_ref[:, src:src + PROJ_CHUNK], preferred_element_type=F32)
        if prev is not None:
            finish(c - 1, prev)
        prev = y
    finish(len(norm_chunks) - 1, prev)
    if with_extra:
        ox_ref[...] = jnp.dot(xn, wx_ref[...], preferred_element_type=F32)


def _project(x, g, w, col_gain, norm_chunks, w_extra=None, raw_chunks=(), src_chunks=None):
    if src_chunks is None:
        src_chunks = range(len(norm_chunks))
    T, D = x.shape
    N = w.shape[1]
    row_in = lambda n: pl.BlockSpec((ROW_TILE, n), lambda i: (i, 0))
    in_specs = [row_in(D), _resident((1, D)), _resident((D, N)), _resident((1, N))]
    args = [x, g.reshape(1, D), w, col_gain.reshape(1, N)]
    out_specs = [row_in(N)]
    out_shape = [jax.ShapeDtypeStruct((T, N), BF16)]
    if w_extra is not None:
        nx = w_extra.shape[1]
        in_specs.append(_resident((D, nx)))
        args.append(w_extra)
        out_specs += [row_in(nx), row_in(len(raw_chunks) * PROJ_CHUNK)]
        out_shape += [jax.ShapeDtypeStruct((T, nx), F32), jax.ShapeDtypeStruct((T, len(raw_chunks) * PROJ_CHUNK), F32)]
    return pl.pallas_call(
        functools.partial(_proj_kernel, norm_chunks=tuple(norm_chunks), src_chunks=tuple(src_chunks),
                          raw_chunks=tuple(raw_chunks),
                          with_extra=w_extra is not None),
        grid=(T // ROW_TILE,),
        in_specs=in_specs,
        out_specs=out_specs,
        out_shape=out_shape,
        compiler_params=_cparams(1),
        name="proj",
    )(*args)


def _sigmoid(z):
    return 1.0 / (1.0 + jnp.exp(-z))


def _post_kernel(o_ref, h_ref, wout_ref, gf_ref, wg_ref, wu_ref, wd_ref, gp_ref, wpg_ref, p_ref, wpp_ref, out_ref):
    halves = [slice(r * ROW_TILE // POST_SPLIT, (r + 1) * ROW_TILE // POST_SPLIT) for r in range(POST_SPLIT)]
    h1 = [h_ref[r, :] + jnp.dot(o_ref[r, :], wout_ref[...], preferred_element_type=F32) for r in halves]
    xn = [_rms_rows(x, gf_ref[...]).astype(BF16) for x in h1]
    a = [jnp.dot(x, wg_ref[...], preferred_element_type=F32) for x in xn]
    u = [jnp.dot(x, wu_ref[...], preferred_element_type=F32) for x in xn]
    act = [(ai * _sigmoid(ai) * ui).astype(BF16) for ai, ui in zip(a, u)]
    h2 = [x + jnp.dot(t, wd_ref[...], preferred_element_type=F32) for x, t in zip(h1, act)]
    hn = [_rms_rows(x, gp_ref[...]).astype(BF16) for x in h2]
    gate = [_sigmoid(jnp.dot(x, wpg_ref[...], preferred_element_type=F32)) for x in hn]
    for r, x, g in zip(halves, h2, gate):
        out_ref[r, :] = x + g * jnp.dot(p_ref[r, :].astype(BF16), wpp_ref[...], preferred_element_type=F32)


def _post_attention(o, h, w_out, g_ffn, wg, wu, wd, g_ple, w_pgate, p, w_pproj, layer):
    T, D = h.shape
    Fh = wg.shape[2]
    Pd = p.shape[2]
    row_in = lambda n: pl.BlockSpec((ROW_TILE, n), lambda i: (i, 0))
    slab = lambda r, c: pl.BlockSpec((None, r, c), lambda i: (layer, 0, 0), pipeline_mode=pl.Buffered(1))
    return pl.pallas_call(
        _post_kernel,
        grid=(T // ROW_TILE,),
        in_specs=[row_in(D), row_in(D), _resident((D, D)), _resident((1, D)), slab(D, Fh),
                  slab(D, Fh), slab(Fh, D), _resident((1, D)), slab(D, D),
                  pl.BlockSpec((None, ROW_TILE, Pd), lambda i: (layer, i, 0)), slab(Pd, D)],
        out_specs=row_in(D),
        out_shape=jax.ShapeDtypeStruct((T, D), F32),
        compiler_params=pltpu.CompilerParams(dimension_semantics=("arbitrary",), vmem_limit_bytes=POST_VMEM_LIMIT),
        name="post_attention",
    )(o, h, w_out, g_ffn.reshape(1, D), wg, wu, wd, g_ple.reshape(1, D), w_pgate, p, w_pproj)


def _fill_strips(strip_ref, r_ref):
    for h in range(2):
        for o in range(r_ref.shape[2]):
            x = jnp.broadcast_to(r_ref[0, h, o:o + 1, :], (TQ, 2 * TQ))
            strip_ref[h, o * TQ:(o + 1) * TQ, :] = pltpu.roll(x, 0, 1, stride=1, stride_axis=0)[:, :TQ]


def _tile_groups(nblk, span):
    cnt = [min(span, nblk - j) for j in range(nblk)]
    off = [0]
    for c in cnt:
        off.append(off[-1] + c * TQ)
    return cnt, off


def _tile_rows(nblk, span):
    return _tile_groups(nblk, span)[1][-1]


def _pair_attention(q_rows, k_rows, v_rows, strip_ref, s_ref, p_ref, o_ref, nblk, span, skew=1):
    cnt, off = _tile_groups(nblk, span)
    together = s_ref.shape[0] >= 2 * off[-1] and p_ref.shape[2] >= 2 * span * TQ
    s0 = [0, off[-1] if together else 0]
    p0 = [0, span * TQ if together else 0]

    def scores(h, j):
        n = cnt[j] * TQ
        s = lax.dot_general(q_rows(h, j * TQ, j * TQ + n), k_rows(h, j), NT_DIMS, preferred_element_type=F32)
        s_ref[s0[h] + off[j]:s0[h] + off[j] + n, :] = s + strip_ref[h, 0:n, :]

    def probs(h, i):
        j0 = max(0, i - span + 1)
        rows = [s0[h] + off[j] + (i - j) * TQ for j in range(j0, i + 1)]
        mx = s_ref[rows[0]:rows[0] + TQ, :]
        for r in rows[1:]:
            mx = jnp.maximum(mx, s_ref[r:r + TQ, :])
        m = jnp.max(mx, axis=1, keepdims=True)
        for t, r in enumerate(rows):
            p_ref[i, :, p0[h] + t * TQ:p0[h] + (t + 1) * TQ] = jnp.exp(s_ref[r:r + TQ, :] - m).astype(BF16)

    def values(h, i):
        j0 = max(0, i - span + 1)
        out = jnp.dot(p_ref[i, :, p0[h]:p0[h] + (i + 1 - j0) * TQ], v_rows(j0 * TQ, (i + 1) * TQ),
                      preferred_element_type=F32)
        o_ref[h, i * TQ:(i + 1) * TQ, :] = out[:, :LANES] / out[:, LANES:]

    for heads in ([(0, 1)] if together else [(0,), (1,)]):
        for j in range(nblk + skew):
            for h in heads:
                if j < nblk:
                    scores(h, j)
            for h in heads:
                if j >= skew:
                    values(h, j - skew)
            for h in heads:
                if j < nblk:
                    probs(h, j)


def _merge_heads(o_ref):
    return jnp.where(_lane_lo((o_ref.shape[1], LANES)), o_ref[0], o_ref[1])


def _split_heads(q):
    lo = _lane_lo(q.shape)
    zero = jnp.zeros_like(q)
    return jnp.where(lo, q, zero), jnp.where(lo, zero, q)


def _fill_value_ones(vaug_ref):
    vaug_ref[:, LANES:] = jnp.ones((vaug_ref.shape[0], LANES), vaug_ref.dtype)


def _moba_block_means(k_ref, bb):
    shape = (2 * MOBA_NBLK, LANES)
    lo = _lane_lo(shape)
    row = lax.broadcasted_iota(jnp.int32, shape, 0)
    kmt = jnp.zeros(shape, F32)
    for n in range(MOBA_NBLK):
        mean = jnp.mean(k_ref[bb, n * MOBA_BLOCK:(n + 1) * MOBA_BLOCK, :].astype(F32), axis=0, keepdims=True)
        kmt = jnp.where(row == n, jnp.where(lo, mean, 0.0), kmt)
        kmt = jnp.where(row == MOBA_NBLK + n, jnp.where(lo, 0.0, mean), kmt)
    return kmt


def _moba_penalty(q, kmt):
    nb = MOBA_NBLK
    S = q.shape[0]
    hi = kmt.astype(BF16)
    rest = kmt - hi.astype(F32)
    mid = rest.astype(BF16)
    lo = (rest - mid.astype(F32)).astype(BF16)
    g3 = lax.dot_general(jnp.concatenate([hi, mid, lo], axis=0), q, NT_DIMS, preferred_element_type=F32)
    g = g3[0:2 * nb] + g3[2 * nb:4 * nb] + g3[4 * nb:6 * nb]
    row = lax.broadcasted_iota(jnp.int32, g.shape, 0)
    n = row & (nb - 1)
    own = lax.shift_right_logical(lax.broadcasted_iota(jnp.int32, g.shape, 1), int(math.log2(MOBA_BLOCK)))
    rank = jnp.zeros(g.shape, F32)
    for m in range(nb - 1):
        gm = jnp.where(row < nb, g[m:m + 1, :], g[nb + m:nb + m + 1, :])
        tie = jnp.where(n > m, 1.0, 0.0)
        beats = jnp.where(gm > g, 1.0, jnp.where(gm == g, tie, 0.0))
        rank = rank + jnp.where(own > m, beats, 0.0)
    keep = jnp.where(n < own, jnp.where(rank < MOBA_TOPK, 1.0, 0.0), jnp.where(n == own, 1.0, 0.0))
    keep = jnp.concatenate([keep, jnp.zeros((LANES - 2 * nb, S), F32)], axis=0).T
    lane = lax.broadcasted_iota(jnp.int32, keep.shape, 1)
    return jnp.where(lane < 2 * nb, (keep - 1.0) * (-NEG), 0.0)


def _attn0_kernel(q_ref, k_ref, v_ref, r_ref, e_ref, o_ref, strip_ref, *scratch):
    p = pl.program_id(0)
    b = pl.program_id(1)
    S = q_ref.shape[1]
    nblk = S // TQ
    n_set = len(scratch) // ATTN0_ROWS
    sets = [scratch[t * n_set:(t + 1) * n_set] for t in range(ATTN0_ROWS)]
    is_moba = p < MOBA_HEADS // 2

    @pl.when(b == 0)
    def _():
        _fill_strips(strip_ref, r_ref)

    @pl.when(jnp.logical_and(p == 0, b == 0))
    def _():
        for _, kaug_ref, vaug_ref, _, _, _ in sets:
            for h in range(2):
                kaug_ref[h, :, LANES:] = e_ref[h]
            _fill_value_ones(vaug_ref)

    for bb, (qaug_ref, kaug_ref, vaug_ref, _, _, _) in enumerate(sets):
        k = k_ref[bb]
        for h, qh in enumerate(_split_heads(q_ref[bb])):
            kaug_ref[h, :, :LANES] = k
            qaug_ref[h, :, :LANES] = qh
        vaug_ref[:, :LANES] = v_ref[bb]

    @pl.when(is_moba)
    def _():
        for bb, (qaug_ref, _, _, _, _, _) in enumerate(sets):
            pen = _moba_penalty(q_ref[bb], _moba_block_means(k_ref, bb)).astype(BF16)
            for h in range(2):
                qaug_ref[h, :, LANES:] = pen

    @pl.when(jnp.logical_not(is_moba))
    def _():
        for qaug_ref, _, _, _, _, _ in sets:
            for h in range(2):
                qaug_ref[h, :, LANES:] = jnp.zeros((S, LANES), BF16)

    for bb, (qaug_ref, kaug_ref, vaug_ref, s_ref, p_ref, oh_ref) in enumerate(sets):
        _pair_attention(lambda h, r0, r1, ref=qaug_ref: ref[h, r0:r1, :],
                        lambda h, j, ref=kaug_ref: ref[h, j * TQ:(j + 1) * TQ, :],
                        lambda r0, r1, ref=vaug_ref: ref[r0:r1, :],
                        strip_ref, s_ref, p_ref, oh_ref, nblk, nblk)
        o_ref[bb] = _merge_heads(oh_ref).astype(o_ref.dtype)


def _strip_scratch(rows):
    return pltpu.VMEM((2, rows, TQ), F32)


def _attn_set(S, span, narrow_span=0):
    nblk = S // TQ
    s_rows = max(_tile_rows(nblk, span), 2 * _tile_rows(nblk, narrow_span))
    return [pltpu.VMEM((2, S, 2 * LANES), BF16), pltpu.VMEM((2, S, 2 * LANES), BF16),
            pltpu.VMEM((S, 2 * LANES), BF16), pltpu.VMEM((s_rows, TQ), F32),
            pltpu.VMEM((nblk, TQ, span * TQ), BF16), pltpu.VMEM((2, S, LANES), F32)]


def _attn0(qkv, r_tab, e_onehot):
    B, S, _ = qkv.shape
    nblk = S // TQ
    return pl.pallas_call(
        _attn0_kernel,
        grid=(N_PAIRS, B // ATTN0_ROWS),
        in_specs=[
            pl.BlockSpec((ATTN0_ROWS, S, LANES), lambda p, b: (b, 0, p)),
            pl.BlockSpec((ATTN0_ROWS, S, LANES), lambda p, b: (b, 0, N_PAIRS + p)),
            pl.BlockSpec((ATTN0_ROWS, S, LANES), lambda p, b: (b, 0, 2 * N_PAIRS + p)),
            pl.BlockSpec((1, 2, nblk, 2 * TQ), lambda p, b: (p, 0, 0, 0)),
            pl.BlockSpec((2, S, LANES), lambda p, b: (0, 0, 0), pipeline_mode=pl.Buffered(1)),
        ],
        out_specs=pl.BlockSpec((ATTN0_ROWS, S, LANES), lambda p, b: (b, 0, p)),
        out_shape=jax.ShapeDtypeStruct((B, S, D_MODEL), BF16),
        scratch_shapes=[_strip_scratch(S)] + _attn_set(S, nblk) * ATTN0_ROWS,
        compiler_params=pltpu.CompilerParams(dimension_semantics=("arbitrary",) * 2, vmem_limit_bytes=ATTN_VMEM_LIMIT),
        name="attn_moba_dilated",
    )(qkv, qkv, qkv, r_tab, e_onehot)


def _compress_pair(x_ref, pos_ref, w1_ref, b1_ref, w2_ref, b2_ref, gain_ref, normed):
    n_chunk = x_ref.shape[1] // NSA_CMP_STRIDE
    first = second = None
    for a in range(NSA_CMP_STRIDE):
        t = x_ref[0, pl.ds(a, n_chunk, stride=NSA_CMP_STRIDE), :]
        fa = jnp.dot((t + pos_ref[a:a + 1, :]).astype(BF16), w1_ref[a], preferred_element_type=F32)
        sa = jnp.dot((t + pos_ref[NSA_CMP_STRIDE + a:NSA_CMP_STRIDE + a + 1, :]).astype(BF16),
                     w1_ref[NSA_CMP_STRIDE + a], preferred_element_type=F32)
        first = fa if first is None else first + fa
        second = sa if second is None else second + sa
    hid = first + pltpu.roll(second, n_chunk - 1, 0) + b1_ref[...]
    cdf = 0.5 * (1.0 + jnp.tanh(math.sqrt(2.0 / math.pi) * (hid + 0.044715 * (hid * hid * hid))))
    y = jnp.dot((hid * cdf).astype(BF16), w2_ref[...], preferred_element_type=F32) + b2_ref[...]
    if normed:
        lo = _lane_lo(y.shape)
        y2 = y * y
        s_lo = jnp.sum(jnp.where(lo, y2, 0.0), axis=1, keepdims=True)
        s_hi = jnp.sum(jnp.where(lo, 0.0, y2), axis=1, keepdims=True)
        ms = jnp.where(lo, s_lo, s_hi) * (1.0 / HEAD_DIM)
        y = y * lax.rsqrt(ms + RMS_EPS) * gain_ref[...]
    return y.astype(BF16)


def _pair_block_diag(w):
    z = jnp.zeros_like(w)
    return jnp.concatenate([jnp.concatenate([w, z], axis=-1), jnp.concatenate([z, w], axis=-1)], axis=-2)


def _compress_operands(pos, w1, b1, w2, b2, gain):
    hid = w1.shape[1]
    n_pos = pos.shape[0]
    return [jnp.tile(pos, (1, 2)), _pair_block_diag(w1.reshape(n_pos, HEAD_DIM, hid)).astype(BF16),
            jnp.tile(b1, 2).reshape(1, 2 * hid), _pair_block_diag(w2).astype(BF16),
            jnp.tile(b2, 2).reshape(1, LANES), jnp.tile(gain, 2).reshape(1, LANES)]


def _cmp_kernel(q_ref, rawk_ref, rawv_ref, *rest):
    k_par, v_par = rest[0:6], rest[6:12]
    bias_ref, ovt_ref, o_ref, pen_ref = rest[12:]
    kc = _compress_pair(rawk_ref, *k_par, True)
    vc = _compress_pair(rawv_ref, *v_par, False)
    S = q_ref.shape[1]
    n_sel = ovt_ref.shape[0]
    lo = _lane_lo((CMP_ROWS, LANES))
    blk = lax.broadcasted_iota(jnp.int32, (n_sel, CMP_ROWS), 0)
    for c in range(S // CMP_ROWS):
        rows = slice(c * CMP_ROWS, (c + 1) * CMP_ROWS)
        psum = [jnp.zeros((CMP_ROWS, LANES), F32), jnp.zeros((CMP_ROWS, LANES), F32)]
        for r in range(4):
            heads = _split_heads(q_ref[0, rows, r * LANES:(r + 1) * LANES])
            outs = []
            for h in range(2):
                s = lax.dot_general(heads[h], kc, NT_DIMS, preferred_element_type=F32) + bias_ref[0, 2 * r + h, rows, :]
                m = jnp.max(s, axis=1, keepdims=True)
                e = jnp.exp(s - m)
                l = jnp.sum(e, axis=1, keepdims=True)
                pr = e * jnp.where(m > 0.5 * NEG, 1.0 / l, 0.0)
                psum[h] = psum[h] + pr
                outs.append(jnp.dot(pr.astype(BF16), vc, preferred_element_type=F32))
            o_ref[0, rows, r * LANES:(r + 1) * LANES] = jnp.where(lo, outs[0], outs[1])

        t = c * CMP_ROWS + lax.broadcasted_iota(jnp.int32, (n_sel, CMP_ROWS), 1)
        cur = lax.shift_right_logical(t, int(math.log2(NSA_SEL_BLOCK)))
        keeps = []
        for h in range(2):
            imp = lax.dot_general(ovt_ref[...], psum[h], NT_DIMS,
                                  precision=lax.Precision.HIGHEST, preferred_element_type=F32)
            forced = jnp.where(blk == 0, 1.0, jnp.where(blk == cur, 1.0, jnp.where(blk == cur - 1, 1.0, 0.0)))
            imp = jnp.where(blk <= cur, imp + forced * NSA_FORCE, -jnp.inf)
            groups = [imp[g * SUBLANES:(g + 1) * SUBLANES, :] for g in range(n_sel // SUBLANES)]
            ranks = [jnp.zeros(g.shape, F32) for g in groups]
            for m in range(n_sel):
                im = imp[m:m + 1, :]
                for g, sub in enumerate(groups):
                    ge = jnp.where(im >= sub, 1.0, 0.0)
                    gt = jnp.where(im > sub, 1.0, 0.0)
                    if g * SUBLANES > m:
                        inc = ge
                    elif (g + 1) * SUBLANES - 1 <= m:
                        inc = gt
                    else:
                        inc = jnp.where(lax.broadcasted_iota(jnp.int32, sub.shape, 0) + g * SUBLANES > m, ge, gt)
                    ranks[g] = ranks[g] + inc
            keeps.append(jnp.where(jnp.concatenate(ranks, axis=0) < NSA_SEL_TOPN, 1.0, 0.0))
        keep = jnp.concatenate(keeps + [jnp.zeros((LANES - 2 * n_sel, CMP_ROWS), F32)], axis=0).T
        lane = lax.broadcasted_iota(jnp.int32, keep.shape, 1)
        pen_ref[0, 0, rows, :] = jnp.where(lane < 2 * n_sel, (keep - 1.0) * (-NEG), 0.0).astype(pen_ref.dtype)


def _cmp_attention(qkv, raw, k_params, v_params, bias, ovt):
    B, S, _ = qkv.shape
    n_sel = ovt.shape[0]
    v_block0 = raw.shape[2] // (2 * LANES)
    params = _compress_operands(*k_params) + _compress_operands(*v_params)
    whole = lambda a: pl.BlockSpec(a.shape, lambda m, b: (0,) * a.ndim)
    return pl.pallas_call(
        _cmp_kernel,
        grid=(2, B),
        in_specs=[
            pl.BlockSpec((1, S, 4 * LANES), lambda m, b: (b, 0, m)),
            pl.BlockSpec((1, S, LANES), lambda m, b: (b, 0, m)),
            pl.BlockSpec((1, S, LANES), lambda m, b: (b, 0, v_block0 + m)),
        ] + [whole(a) for a in params] + [
            pl.BlockSpec((1, 8, S, LANES), lambda m, b: (m, 0, 0, 0), pipeline_mode=pl.Buffered(1)),
            pl.BlockSpec((n_sel, LANES), lambda m, b: (0, 0)),
        ],
        out_specs=[
            pl.BlockSpec((1, S, 4 * LANES), lambda m, b: (b, 0, m)),
            pl.BlockSpec((1, 1, S, LANES), lambda m, b: (b, m, 0, 0)),
        ],
        out_shape=[jax.ShapeDtypeStruct((B, S, D_MODEL), F32),
                   jax.ShapeDtypeStruct((B, 2, S, LANES), BF16)],
        compiler_params=_cparams(2),
        name="nsa_compressed_select",
    )(qkv, raw, raw, *params, bias, ovt)


def _attn1_kernel(q_ref, ks_ref, vs_ref, kw_ref, vw_ref, rs_ref, rw_ref, e_ref, pen_ref, ocmp_ref, graw_ref,
                  o_ref, strip_s_ref, qaug_ref, kaug_ref, vsaug_ref, s_ref, p_ref, oslc_ref,
                  strip_w_ref, vwaug_ref, owin_ref):
    p = pl.program_id(0)
    b = pl.program_id(1)
    S = q_ref.shape[1]
    nblk = S // TQ

    @pl.when(b == 0)
    def _():
        _fill_strips(strip_s_ref, rs_ref)
        _fill_strips(strip_w_ref, rw_ref)

    @pl.when(jnp.logical_and(p == 0, b == 0))
    def _():
        for h in range(2):
            kaug_ref[h, :, LANES:] = e_ref[h]
        _fill_value_ones(vsaug_ref)
        _fill_value_ones(vwaug_ref)

    ks = ks_ref[0]
    pen = pen_ref[0, 0]
    for h, qh in enumerate(_split_heads(q_ref[0])):
        kaug_ref[h, :, :LANES] = ks
        qaug_ref[h, :, :LANES] = qh
        qaug_ref[h, :, LANES:] = pen
    vsaug_ref[:, :LANES] = vs_ref[0]
    vwaug_ref[:, :LANES] = vw_ref[0]
    _pair_attention(lambda h, r0, r1: qaug_ref[h, r0:r1, :],
                    lambda h, j: kaug_ref[h, j * TQ:(j + 1) * TQ, :],
                    lambda r0, r1: vsaug_ref[r0:r1, :],
                    strip_s_ref, s_ref, p_ref, oslc_ref, nblk, nblk)
    _pair_attention(lambda h, r0, r1: qaug_ref[h, r0:r1, :LANES],
                    lambda h, j: kw_ref[0, j * TQ:(j + 1) * TQ, :],
                    lambda r0, r1: vwaug_ref[r0:r1, :],
                    strip_w_ref, s_ref, p_ref, owin_ref, nblk, strip_w_ref.shape[1] // TQ, skew=2)
    sig = _sigmoid(pltpu.roll(graw_ref[0], lax.rem(LANES - GATE_STRIDE * p, LANES), 1))
    lo_lanes = _lane_lo((S, LANES))
    gates = [jnp.where(lo_lanes, sig[:, 2 * br:2 * br + 1], sig[:, 2 * br + 1:2 * br + 2]) for br in range(3)]
    out = gates[0] * ocmp_ref[0] + gates[1] * _merge_heads(oslc_ref) + gates[2] * _merge_heads(owin_ref)
    o_ref[0] = out.astype(o_ref.dtype)


def _attn1(qkv, r_slc, r_win, e_onehot, pen, ocmp, graw):
    B, S, _ = qkv.shape
    kv0 = D_MODEL // LANES + 4
    n_win = r_win.shape[2]
    return pl.pallas_call(
        _attn1_kernel,
        grid=(N_PAIRS, B),
        in_specs=[
            pl.BlockSpec((1, S, LANES), lambda p, b: (b, 0, p)),
            pl.BlockSpec((1, S, LANES), lambda p, b: (b, 0, kv0 + p // 4)),
            pl.BlockSpec((1, S, LANES), lambda p, b: (b, 0, kv0 + 2 + p // 4)),
            pl.BlockSpec((1, S, LANES), lambda p, b: (b, 0, kv0 + 4 + p // 4)),
            pl.BlockSpec((1, S, LANES), lambda p, b: (b, 0, kv0 + 6 + p // 4)),
            pl.BlockSpec((1, 2, r_slc.shape[2], 2 * TQ), lambda p, b: (p, 0, 0, 0)),
            pl.BlockSpec((1, 2, n_win, 2 * TQ), lambda p, b: (p, 0, 0, 0)),
            pl.BlockSpec((2, S, LANES), lambda p, b: (0, 0, 0), pipeline_mode=pl.Buffered(1)),
            pl.BlockSpec((1, 1, S, LANES), lambda p, b: (b, p // 4, 0, 0)),
            pl.BlockSpec((1, S, LANES), lambda p, b: (b, 0, p)),
            pl.BlockSpec((1, S, LANES), lambda p, b: (b, 0, 0)),
        ],
        out_specs=pl.BlockSpec((1, S, LANES), lambda p, b: (b, 0, p)),
        out_shape=jax.ShapeDtypeStruct((B, S, D_MODEL), BF16),
        scratch_shapes=[_strip_scratch(S)] + _attn_set(S, S // TQ, n_win) + [_strip_scratch(n_win * TQ),
                                                    pltpu.VMEM((S, 2 * LANES), BF16), pltpu.VMEM((2, S, LANES), F32)],
        compiler_params=pltpu.CompilerParams(dimension_semantics=("arbitrary",) * 2, vmem_limit_bytes=ATTN_VMEM_LIMIT),
        name="attn_nsa",
    )(qkv, qkv, qkv, qkv, qkv, r_slc, r_win, e_onehot, pen, ocmp, graw)


def _bucket(dist):
    n = np.maximum(dist, 0)
    exact = REL_BUCKETS // 2
    nf = np.maximum(n, 1).astype(np.float64)
    large = exact + (np.log(nf / exact) / math.log(REL_MAX_DIST / exact) * (REL_BUCKETS - exact)).astype(np.int64)
    return np.where(n < exact, n, np.minimum(large, REL_BUCKETS - 1))


def _strip_rows(tab, n_off, extra_of_dist):
    u = np.arange(2 * TQ)
    d = np.arange(n_off)[:, None] * TQ - np.where(u < TQ, u, u - 2 * TQ)[None, :]
    onehot = (_bucket(d)[None] == np.arange(REL_BUCKETS)[:, None, None]).astype(np.float32)
    vals = jnp.einsum("hb,bou->hou", tab, jnp.asarray(onehot), precision=lax.Precision.HIGHEST) + extra_of_dist(d)
    return vals.reshape(tab.shape[0] // 2, 2, n_off, 2 * TQ)


def _causal_mask(d):
    return np.where(d >= 0, 0.0, NEG).astype(np.float32)


def _window_mask(d):
    return np.where((d >= 0) & (d < NSA_WINDOW), 0.0, NEG).astype(np.float32)


def _dilation_log_count(d):
    c = ((d >= 0) & (d <= 128)).astype(np.float64)
    c += ((d >= 0) & (d % 4 == 0) & (d <= 512))
    c += ((d >= 0) & (d % 16 == 0) & (d <= 2048))
    return np.where(c > 0, np.log(np.maximum(c, 1.0)), NEG).astype(np.float32)


def _one_hot_blocks(S, block, per_head):
    e = np.zeros((2, S, LANES), np.float32)
    key = np.arange(S)
    for h in range(2):
        e[h, key, h * per_head + key // block] = 1.0
    return jnp.asarray(e, BF16)


_NSA_HEAD_ORDER = np.array([8 * (p // 4) + (p % 4) + 4 * h for p in range(N_PAIRS) for h in range(2)])


def _head_cols(heads):
    return (np.asarray(heads)[:, None] * HEAD_DIM + np.arange(HEAD_DIM)[None, :]).reshape(-1)


def _mixer_ab(h, g_mix, w_in, qn_a, kn_a, qn_b, kn_b, rel_bias, B, S):
    wa = MOBA_HEADS * HEAD_DIM
    ones = jnp.ones((wa,), F32)
    nh = MOBA_HEADS
    gain = jnp.concatenate([jnp.tile(qn_a, nh) * ATTN_SCALE, jnp.tile(qn_b, nh) * ATTN_SCALE,
                            jnp.tile(kn_a, nh), jnp.tile(kn_b, nh), ones, ones])
    per_sec = wa // PROJ_CHUNK
    src = [sec * per_sec + c for sec in (0, 3, 1, 4, 2, 5) for c in range(per_sec)]
    (qkv,) = _project(h, g_mix, w_in.astype(BF16), gain, [True] * (4 * per_sec) + [False] * (2 * per_sec),
                      src_chunks=src)
    qkv = qkv.reshape(B, S, 3 * D_MODEL)

    tab = rel_bias.T
    r_tab = jnp.concatenate([_strip_rows(tab[:MOBA_HEADS], S // TQ, _causal_mask),
                             _strip_rows(tab[MOBA_HEADS:], S // TQ, _dilation_log_count)])
    o = _attn0(qkv, r_tab, _one_hot_blocks(S, MOBA_BLOCK, S // MOBA_BLOCK))
    return o.reshape(B * S, D_MODEL)


def _mixer_nsa(h, g_mix, w_in, qn, kn_c, kn_s, kn_w, cmp_k, cmp_v, rel_bias, B, S):
    qw = N_HEADS * HEAD_DIM
    kvw = NSA_GROUPS * HEAD_DIM
    order = _NSA_HEAD_ORDER
    w_main = jnp.concatenate([w_in[:, hd * HEAD_DIM:(hd + 1) * HEAD_DIM] for hd in order]
                             + [w_in[:, qw:qw + 6 * kvw]], axis=1).astype(BF16)
    ones = jnp.ones((kvw,), F32)
    gain = jnp.concatenate([jnp.tile(qn, N_HEADS) * ATTN_SCALE, ones, ones,
                            jnp.tile(kn_s, NSA_GROUPS), ones, jnp.tile(kn_w, NSA_GROUPS), ones])
    assert kvw == PROJ_CHUNK
    norm_chunks = [True] * (qw // PROJ_CHUNK) + [False, False, True, False, True, False]

    gcols = np.zeros((LANES,), np.int64)
    gused = np.zeros((LANES,), np.float32)
    for p in range(N_PAIRS):
        for br in range(3):
            for hh in range(2):
                c = GATE_STRIDE * p + 2 * br + hh
                gcols[c] = qw + 6 * kvw + 3 * order[2 * p + hh] + br
                gused[c] = 1.0
    w_gate = (w_in[:, gcols] * gused).astype(BF16)
    kc_chunk = qw // PROJ_CHUNK
    qkv, graw, raw = _project(h, g_mix, w_main, gain, norm_chunks, w_gate, raw_chunks=(kc_chunk, kc_chunk + 1))
    qkv = qkv.reshape(B, S, qw + 6 * kvw)
    graw = graw.reshape(B, S, LANES)
    raw = raw.reshape(B, S, 2 * kvw)

    tab = rel_bias.T[order]
    n_cmp = (S - NSA_CMP_LEN) // NSA_CMP_STRIDE + 1
    t_pos = np.arange(S)[:, None]
    c_idx = np.arange(LANES)[None, :]
    dc = t_pos - (c_idx * NSA_CMP_STRIDE + NSA_CMP_LEN - 1)
    cmp_mask = np.where((dc >= 0) & (c_idx < n_cmp), 0.0, NEG).astype(np.float32)
    onehot = jnp.asarray(_bucket(dc).astype(np.int8)).reshape(1, -1) == jnp.arange(REL_BUCKETS, dtype=jnp.int8)[:, None]
    bias_c = jnp.dot(tab, onehot.astype(F32), precision=lax.Precision.HIGHEST) + cmp_mask.reshape(1, -1)
    bias_c = bias_c.reshape(2, 8, S, LANES)
    n_sel = S // NSA_SEL_BLOCK
    cstart = np.arange(LANES) * NSA_CMP_STRIDE
    sstart = np.arange(n_sel) * NSA_SEL_BLOCK
    ovt = np.maximum(np.minimum(cstart[None, :] + NSA_CMP_LEN, sstart[:, None] + NSA_SEL_BLOCK)
                     - np.maximum(cstart[None, :], sstart[:, None]), 0).astype(np.float32)
    ovt[:, n_cmp:] = 0.0
    ocmp, pen = _cmp_attention(qkv, raw, (*cmp_k, kn_c), (*cmp_v, kn_c), bias_c, jnp.asarray(ovt))

    r_slc = _strip_rows(tab, S // TQ, _causal_mask)
    r_win = _strip_rows(tab, NSA_WINDOW // TQ + 1, _window_mask)
    o = _attn1(qkv, r_slc, r_win, _one_hot_blocks(S, NSA_SEL_BLOCK, n_sel), pen, ocmp, graw)
    return o.reshape(B * S, D_MODEL)


def kernel(x, p, rel_bias, norm_mix, norm_ffn, norm_ple, w_ffn_gate, w_ffn_up, w_ffn_down, w_ple_proj, w_ple_gate, w_in_ab, w_out_ab, qn_moba, kn_moba, qn_dil, kn_dil, w_in_nsa, w_out_nsa, qn_nsa, kn_cmp, kn_slc, kn_win, cmp_k_pos, cmp_k_w1, cmp_k_b1, cmp_k_w2, cmp_k_b2, cmp_v_pos, cmp_v_w1, cmp_v_b1, cmp_v_w2, cmp_v_b2):
    B, S, D = x.shape
    depth = p.shape[0]
    assert D == D_MODEL and S == MOBA_NBLK * MOBA_BLOCK and (B * S) % ROW_TILE == 0 and B % ATTN0_ROWS == 0, (B, S, D)
    h = x.reshape(B * S, D)
    wg, wu, wd, wpg, wpp = (w.astype(BF16) for w in (w_ffn_gate, w_ffn_up, w_ffn_down, w_ple_gate, w_ple_proj))
    for i in range(depth):
        e = i // 2
        if i % 2 == 0:
            o = _mixer_ab(h, norm_mix[i], w_in_ab[e], qn_moba[e], kn_moba[e], qn_dil[e], kn_dil[e], rel_bias, B, S)
            w_out = w_out_ab[e]
        else:
            cmp_k = (cmp_k_pos[e], cmp_k_w1[e], cmp_k_b1[e], cmp_k_w2[e], cmp_k_b2[e])
            cmp_v = (cmp_v_pos[e], cmp_v_w1[e], cmp_v_b1[e], cmp_v_w2[e], cmp_v_b2[e])
            o = _mixer_nsa(h, norm_mix[i], w_in_nsa[e], qn_nsa[e], kn_cmp[e], kn_slc[e], kn_win[e],
                           cmp_k, cmp_v, rel_bias, B, S)
            w_out = w_out_nsa[e][_head_cols(_NSA_HEAD_ORDER), :]
        h = _post_attention(o, h, w_out.astype(BF16), norm_ffn[i], wg, wu, wd, norm_ple[i], wpg,
                            p.reshape(depth, B * S, -1), wpp, i)
    return h.reshape(B, S, D)
```

```python
import functools
import math

import numpy as np
import jax
import jax.numpy as jnp
from jax import lax
from jax.experimental import pallas as pl
from jax.experimental.pallas import tpu as pltpu

F32 = jnp.float32
BF16 = jnp.bfloat16

D_MODEL = 1024
HEAD_DIM = 64
N_HEADS = 16
N_PAIRS = N_HEADS // 2
MOBA_HEADS = 8
MOBA_BLOCK = 256
MOBA_TOPK = 3
MOBA_NBLK = 8
NSA_GROUPS = 4
NSA_CMP_LEN = 32
NSA_CMP_STRIDE = 16
NSA_SEL_BLOCK = 64
NSA_SEL_TOPN = 16
NSA_WINDOW = 512
NSA_FORCE = 1.0e6
REL_BUCKETS = 32
REL_MAX_DIST = 2048
RMS_EPS = 1e-6
ATTN_SCALE = HEAD_DIM ** -0.5

LANES = 128
SUBLANES = 8
TQ = 256
NEG = -1.0e30
ROW_TILE = 512
PROJ_CHUNK = 256
POST_SPLIT = 2
GATE_STRIDE = LANES // N_HEADS
ATTN0_ROWS = 2
CMP_ROWS = 2048
VMEM_LIMIT = 48 * 1024 * 1024
POST_VMEM_LIMIT = 56 * 1024 * 1024
ATTN_VMEM_LIMIT = 56 * 1024 * 1024

NT_DIMS = (((1,), (1,)), ((), ()))


def _cparams(n_axes):
    return pltpu.CompilerParams(dimension_semantics=("arbitrary",) * n_axes,
                                vmem_limit_bytes=VMEM_LIMIT)


def _rms_rows(x, g):
    ms = jnp.mean(x * x, axis=-1, keepdims=True)
    return x * lax.rsqrt(ms + RMS_EPS) * g


def _lane_lo(shape):
    return lax.broadcasted_iota(jnp.int32, shape, len(shape) - 1) < HEAD_DIM


def _resident(shape):
    return pl.BlockSpec(shape, lambda i: (0,) * len(shape), pipeline_mode=pl.Buffered(1))


def _head_sum_squares(y):
    y2 = y * y
    parts = []
    for blk in range(y.shape[1] // LANES):
        yb = y2[:, blk * LANES:(blk + 1) * LANES]
        lo = _lane_lo(yb.shape)
        s_lo = jnp.sum(jnp.where(lo, yb, 0.0), axis=1, keepdims=True)
        s_hi = jnp.sum(jnp.where(lo, 0.0, yb), axis=1, keepdims=True)
        parts.append(jnp.where(lo, s_lo, s_hi))
    return jnp.concatenate(parts, axis=1)


def _proj_kernel(x_ref, g_ref, w_ref, cg_ref, *rest, norm_chunks, src_chunks, raw_chunks, with_extra):
    if with_extra:
        wx_ref, o_ref, ox_ref, oraw_ref = rest
    else:
        (o_ref,) = rest
    xn = _rms_rows(x_ref[...], g_ref[...]).astype(BF16)

    def finish(c, y):
        cols = slice(c * PROJ_CHUNK, (c + 1) * PROJ_CHUNK)
        if c in raw_chunks:
            k = raw_chunks.index(c)
            oraw_ref[:, k * PROJ_CHUNK:(k + 1) * PROJ_CHUNK] = y
        if norm_chunks[c]:
            y = y * lax.rsqrt(_head_sum_squares(y) * (1.0 / HEAD_DIM) + RMS_EPS) * cg_ref[:, cols]
        o_ref[:, cols] = y.astype(o_ref.dtype)

    prev = None
    for c in range(len(norm_chunks)):
        src = src_chunks[c] * PROJ_CHUNK
        y = jnp.dot(xn, w_ref[:, src:src + PROJ_CHUNK], preferred_element_type=F32)
        if prev is not None:
            finish(c - 1, prev)
        prev = y
    finish(len(norm_chunks) - 1, prev)
    if with_extra:
        ox_ref[...] = jnp.dot(xn, wx_ref[...], preferred_element_type=F32)


def _project(x, g, w, col_gain, norm_chunks, w_extra=None, raw_chunks=(), src_chunks=None):
    if src_chunks is None:
        src_chunks = range(len(norm_chunks))
    T, D = x.shape
    N = w.shape[1]
    row_in = lambda n: pl.BlockSpec((ROW_TILE, n), lambda i: (i, 0))
    in_specs = [row_in(D), _resident((1, D)), _resident((D, N)), _resident((1, N))]
    args = [x, g.reshape(1, D), w, col_gain.reshape(1, N)]
    out_specs = [row_in(N)]
    out_shape = [jax.ShapeDtypeStruct((T, N), BF16)]
    if w_extra is not None:
        nx = w_extra.shape[1]
        in_specs.append(_resident((D, nx)))
        args.append(w_extra)
        out_specs += [row_in(nx), row_in(len(raw_chunks) * PROJ_CHUNK)]
        out_shape += [jax.ShapeDtypeStruct((T, nx), F32), jax.ShapeDtypeStruct((T, len(raw_chunks) * PROJ_CHUNK), F32)]
    return pl.pallas_call(
        functools.partial(_proj_kernel, norm_chunks=tuple(norm_chunks), src_chunks=tuple(src_chunks),
                          raw_chunks=tuple(raw_chunks),
                          with_extra=w_extra is not None),
        grid=(T // ROW_TILE,),
        in_specs=in_specs,
        out_specs=out_specs,
        out_shape=out_shape,
        compiler_params=_cparams(1),
        name="proj",
    )(*args)


def _sigmoid(z):
    return 1.0 / (1.0 + jnp.exp(-z))


def _post_kernel(o_ref, h_ref, wout_ref, gf_ref, wg_ref, wu_ref, wd_ref, gp_ref, wpg_ref, p_ref, wpp_ref, out_ref):
    halves = [slice(r * ROW_TILE // POST_SPLIT, (r + 1) * ROW_TILE // POST_SPLIT) for r in range(POST_SPLIT)]
    h1 = [h_ref[r, :] + jnp.dot(o_ref[r, :], wout_ref[...], preferred_element_type=F32) for r in halves]
    xn = [_rms_rows(x, gf_ref[...]).astype(BF16) for x in h1]
    a = [jnp.dot(x, wg_ref[...], preferred_element_type=F32) for x in xn]
    u = [jnp.dot(x, wu_ref[...], preferred_element_type=F32) for x in xn]
    act = [(ai * _sigmoid(ai) * ui).astype(BF16) for ai, ui in zip(a, u)]
    h2 = [x + jnp.dot(t, wd_ref[...], preferred_element_type=F32) for x, t in zip(h1, act)]
    hn = [_rms_rows(x, gp_ref[...]).astype(BF16) for x in h2]
    gate = [_sigmoid(jnp.dot(x, wpg_ref[...], preferred_element_type=F32)) for x in hn]
    for r, x, g in zip(halves, h2, gate):
        out_ref[r, :] = x + g * jnp.dot(p_ref[r, :].astype(BF16), wpp_ref[...], preferred_element_type=F32)


def _post_attention(o, h, w_out, g_ffn, wg, wu, wd, g_ple, w_pgate, p, w_pproj, layer):
    T, D = h.shape
    Fh = wg.shape[2]
    Pd = p.shape[2]
    row_in = lambda n: pl.BlockSpec((ROW_TILE, n), lambda i: (i, 0))
    slab = lambda r, c: pl.BlockSpec((None, r, c), lambda i: (layer, 0, 0), pipeline_mode=pl.Buffered(1))
    return pl.pallas_call(
        _post_kernel,
        grid=(T // ROW_TILE,),
        in_specs=[row_in(D), row_in(D), _resident((D, D)), _resident((1, D)), slab(D, Fh),
                  slab(D, Fh), slab(Fh, D), _resident((1, D)), slab(D, D),
                  pl.BlockSpec((None, ROW_TILE, Pd), lambda i: (layer, i, 0)), slab(Pd, D)],
        out_specs=row_in(D),
        out_shape=jax.ShapeDtypeStruct((T, D), F32),
        compiler_params=pltpu.CompilerParams(dimension_semantics=("arbitrary",), vmem_limit_bytes=POST_VMEM_LIMIT),
        name="post_attention",
    )(o, h, w_out, g_ffn.reshape(1, D), wg, wu, wd, g_ple.reshape(1, D), w_pgate, p, w_pproj)


def _fill_strips(strip_ref, r_ref):
    for h in range(2):
        for o in range(r_ref.shape[2]):
            x = jnp.broadcast_to(r_ref[0, h, o:o + 1, :], (TQ, 2 * TQ))
            strip_ref[h, o * TQ:(o + 1) * TQ, :] = pltpu.roll(x, 0, 1, stride=1, stride_axis=0)[:, :TQ]


def _tile_groups(nblk, span):
    cnt = [min(span, nblk - j) for j in range(nblk)]
    off = [0]
    for c in cnt:
        off.append(off[-1] + c * TQ)
    return cnt, off


def _tile_rows(nblk, span):
    return _tile_groups(nblk, span)[1][-1]


def _pair_attention(q_rows, k_rows, v_rows, strip_ref, s_ref, p_ref, o_ref, nblk, span, skew=1):
    cnt, off = _tile_groups(nblk, span)
    together = s_ref.shape[0] >= 2 * off[-1] and p_ref.shape[2] >= 2 * span * TQ
    s0 = [0, off[-1] if together else 0]
    p0 = [0, span * TQ if together else 0]

    def scores(h, j):
        n = cnt[j] * TQ
        s = lax.dot_general(q_rows(h, j * TQ, j * TQ + n), k_rows(h, j), NT_DIMS, preferred_element_type=F32)
        s_ref[s0[h] + off[j]:s0[h] + off[j] + n, :] = s + strip_ref[h, 0:n, :]

    def probs(h, i):
        j0 = max(0, i - span + 1)
        rows = [s0[h] + off[j] + (i - j) * TQ for j in range(j0, i + 1)]
        mx = s_ref[rows[0]:rows[0] + TQ, :]
        for r in rows[1:]:
            mx = jnp.maximum(mx, s_ref[r:r + TQ, :])
        m = jnp.max(mx, axis=1, keepdims=True)
        for t, r in enumerate(rows):
            p_ref[i, :, p0[h] + t * TQ:p0[h] + (t + 1) * TQ] = jnp.exp(s_ref[r:r + TQ, :] - m).astype(BF16)

    def values(h, i):
        j0 = max(0, i - span + 1)
        out = jnp.dot(p_ref[i, :, p0[h]:p0[h] + (i + 1 - j0) * TQ], v_rows(j0 * TQ, (i + 1) * TQ),
                      preferred_element_type=F32)
        o_ref[h, i * TQ:(i + 1) * TQ, :] = out[:, :LANES] / out[:, LANES:]

    for heads in ([(0, 1)] if together else [(0,), (1,)]):
        for j in range(nblk + skew):
            for h in heads:
                if j < nblk:
                    scores(h, j)
            for h in heads:
                if j >= skew:
                    values(h, j - skew)
            for h in heads:
                if j < nblk:
                    probs(h, j)


def _merge_heads(o_ref):
    return jnp.where(_lane_lo((o_ref.shape[1], LANES)), o_ref[0], o_ref[1])


def _split_heads(q):
    lo = _lane_lo(q.shape)
    zero = jnp.zeros_like(q)
    return jnp.where(lo, q, zero), jnp.where(lo, zero, q)


def _fill_value_ones(vaug_ref):
    vaug_ref[:, LANES:] = jnp.ones((vaug_ref.shape[0], LANES), vaug_ref.dtype)


def _moba_block_means(k_ref, bb):
    shape = (2 * MOBA_NBLK, LANES)
    lo = _lane_lo(shape)
    row = lax.broadcasted_iota(jnp.int32, shape, 0)
    kmt = jnp.zeros(shape, F32)
    for n in range(MOBA_NBLK):
        mean = jnp.mean(k_ref[bb, n * MOBA_BLOCK:(n + 1) * MOBA_BLOCK, :].astype(F32), axis=0, keepdims=True)
        kmt = jnp.where(row == n, jnp.where(lo, mean, 0.0), kmt)
        kmt = jnp.where(row == MOBA_NBLK + n, jnp.where(lo, 0.0, mean), kmt)
    return kmt


def _moba_penalty(q, kmt):
    nb = MOBA_NBLK
    S = q.shape[0]
    hi = kmt.astype(BF16)
    rest = kmt - hi.astype(F32)
    mid = rest.astype(BF16)
    lo = (rest - mid.astype(F32)).astype(BF16)
    g3 = lax.dot_general(jnp.concatenate([hi, mid, lo], axis=0), q, NT_DIMS, preferred_element_type=F32)
    g = g3[0:2 * nb] + g3[2 * nb:4 * nb] + g3[4 * nb:6 * nb]
    row = lax.broadcasted_iota(jnp.int32, g.shape, 0)
    n = row & (nb - 1)
    own = lax.shift_right_logical(lax.broadcasted_iota(jnp.int32, g.shape, 1), int(math.log2(MOBA_BLOCK)))
    rank = jnp.zeros(g.shape, F32)
    for m in range(nb - 1):
        gm = jnp.where(row < nb, g[m:m + 1, :], g[nb + m:nb + m + 1, :])
        tie = jnp.where(n > m, 1.0, 0.0)
        beats = jnp.where(gm > g, 1.0, jnp.where(gm == g, tie, 0.0))
        rank = rank + jnp.where(own > m, beats, 0.0)
    keep = jnp.where(n < own, jnp.where(rank < MOBA_TOPK, 1.0, 0.0), jnp.where(n == own, 1.0, 0.0))
    keep = jnp.concatenate([keep, jnp.zeros((LANES - 2 * nb, S), F32)], axis=0).T
    lane = lax.broadcasted_iota(jnp.int32, keep.shape, 1)
    return jnp.where(lane < 2 * nb, (keep - 1.0) * (-NEG), 0.0)


def _attn0_kernel(q_ref, k_ref, v_ref, r_ref, e_ref, o_ref, strip_ref, *scratch):
    p = pl.program_id(0)
    b = pl.program_id(1)
    S = q_ref.shape[1]
    nblk = S // TQ
    n_set = len(scratch) // ATTN0_ROWS
    sets = [scratch[t * n_set:(t + 1) * n_set] for t in range(ATTN0_ROWS)]
    is_moba = p < MOBA_HEADS // 2

    @pl.when(b == 0)
    def _():
        _fill_strips(strip_ref, r_ref)

    @pl.when(jnp.logical_and(p == 0, b == 0))
    def _():
        for _, kaug_ref, vaug_ref, _, _, _ in sets:
            for h in range(2):
                kaug_ref[h, :, LANES:] = e_ref[h]
            _fill_value_ones(vaug_ref)

    for bb, (qaug_ref, kaug_ref, vaug_ref, _, _, _) in enumerate(sets):
        k = k_ref[bb]
        for h, qh in enumerate(_split_heads(q_ref[bb])):
            kaug_ref[h, :, :LANES] = k
            qaug_ref[h, :, :LANES] = qh
        vaug_ref[:, :LANES] = v_ref[bb]

    @pl.when(is_moba)
    def _():
        for bb, (qaug_ref, _, _, _, _, _) in enumerate(sets):
            pen = _moba_penalty(q_ref[bb], _moba_block_means(k_ref, bb)).astype(BF16)
            for h in range(2):
                qaug_ref[h, :, LANES:] = pen

    @pl.when(jnp.logical_not(is_moba))
    def _():
        for qaug_ref, _, _, _, _, _ in sets:
            for h in range(2):
                qaug_ref[h, :, LANES:] = jnp.zeros((S, LANES), BF16)

    for bb, (qaug_ref, kaug_ref, vaug_ref, s_ref, p_ref, oh_ref) in enumerate(sets):
        _pair_attention(lambda h, r0, r1, ref=qaug_ref: ref[h, r0:r1, :],
                        lambda h, j, ref=kaug_ref: ref[h, j * TQ:(j + 1) * TQ, :],
                        lambda r0, r1, ref=vaug_ref: ref[r0:r1, :],
                        strip_ref, s_ref, p_ref, oh_ref, nblk, nblk)
        o_ref[bb] = _merge_heads(oh_ref).astype(o_ref.dtype)


def _strip_scratch(rows):
    return pltpu.VMEM((2, rows, TQ), F32)


def _attn_set(S, span, narrow_span=0):
    nblk = S // TQ
    s_rows = max(_tile_rows(nblk, span), 2 * _tile_rows(nblk, narrow_span))
    return [pltpu.VMEM((2, S, 2 * LANES), BF16), pltpu.VMEM((2, S, 2 * LANES), BF16),
            pltpu.VMEM((S, 2 * LANES), BF16), pltpu.VMEM((s_rows, TQ), F32),
            pltpu.VMEM((nblk, TQ, span * TQ), BF16), pltpu.VMEM((2, S, LANES), F32)]


def _attn0(qkv, r_tab, e_onehot):
    B, S, _ = qkv.shape
    nblk = S // TQ
    return pl.pallas_call(
        _attn0_kernel,
        grid=(N_PAIRS, B // ATTN0_ROWS),
        in_specs=[
            pl.BlockSpec((ATTN0_ROWS, S, LANES), lambda p, b: (b, 0, p)),
            pl.BlockSpec((ATTN0_ROWS, S, LANES), lambda p, b: (b, 0, N_PAIRS + p)),
            pl.BlockSpec((ATTN0_ROWS, S, LANES), lambda p, b: (b, 0, 2 * N_PAIRS + p)),
            pl.BlockSpec((1, 2, nblk, 2 * TQ), lambda p, b: (p, 0, 0, 0)),
            pl.BlockSpec((2, S, LANES), lambda p, b: (0, 0, 0), pipeline_mode=pl.Buffered(1)),
        ],
        out_specs=pl.BlockSpec((ATTN0_ROWS, S, LANES), lambda p, b: (b, 0, p)),
        out_shape=jax.ShapeDtypeStruct((B, S, D_MODEL), BF16),
        scratch_shapes=[_strip_scratch(S)] + _attn_set(S, nblk) * ATTN0_ROWS,
        compiler_params=pltpu.CompilerParams(dimension_semantics=("arbitrary",) * 2, vmem_limit_bytes=ATTN_VMEM_LIMIT),
        name="attn_moba_dilated",
    )(qkv, qkv, qkv, r_tab, e_onehot)


def _compress_pair(x_ref, pos_ref, w1_ref, b1_ref, w2_ref, b2_ref, gain_ref, normed):
    n_chunk = x_ref.shape[1] // NSA_CMP_STRIDE
    first = second = None
    for a in range(NSA_CMP_STRIDE):
        t = x_ref[0, pl.ds(a, n_chunk, stride=NSA_CMP_STRIDE), :]
        fa = jnp.dot((t + pos_ref[a:a + 1, :]).astype(BF16), w1_ref[a], preferred_element_type=F32)
        sa = jnp.dot((t + pos_ref[NSA_CMP_STRIDE + a:NSA_CMP_STRIDE + a + 1, :]).astype(BF16),
                     w1_ref[NSA_CMP_STRIDE + a], preferred_element_type=F32)
        first = fa if first is None else first + fa
        second = sa if second is None else second + sa
    hid = first + pltpu.roll(second, n_chunk - 1, 0) + b1_ref[...]
    cdf = 0.5 * (1.0 + jnp.tanh(math.sqrt(2.0 / math.pi) * (hid + 0.044715 * (hid * hid * hid))))
    y = jnp.dot((hid * cdf).astype(BF16), w2_ref[...], preferred_element_type=F32) + b2_ref[...]
    if normed:
        lo = _lane_lo(y.shape)
        y2 = y * y
        s_lo = jnp.sum(jnp.where(lo, y2, 0.0), axis=1, keepdims=True)
        s_hi = jnp.sum(jnp.where(lo, 0.0, y2), axis=1, keepdims=True)
        ms = jnp.where(lo, s_lo, s_hi) * (1.0 / HEAD_DIM)
        y = y * lax.rsqrt(ms + RMS_EPS) * gain_ref[...]
    return y.astype(BF16)


def _pair_block_diag(w):
    z = jnp.zeros_like(w)
    return jnp.concatenate([jnp.concatenate([w, z], axis=-1), jnp.concatenate([z, w], axis=-1)], axis=-2)


def _compress_operands(pos, w1, b1, w2, b2, gain):
    hid = w1.shape[1]
    n_pos = pos.shape[0]
    return [jnp.tile(pos, (1, 2)), _pair_block_diag(w1.reshape(n_pos, HEAD_DIM, hid)).astype(BF16),
            jnp.tile(b1, 2).reshape(1, 2 * hid), _pair_block_diag(w2).astype(BF16),
            jnp.tile(b2, 2).reshape(1, LANES), jnp.tile(gain, 2).reshape(1, LANES)]


def _cmp_kernel(q_ref, rawk_ref, rawv_ref, *rest):
    k_par, v_par = rest[0:6], rest[6:12]
    bias_ref, ovt_ref, o_ref, pen_ref = rest[12:]
    kc = _compress_pair(rawk_ref, *k_par, True)
    vc = _compress_pair(rawv_ref, *v_par, False)
    S = q_ref.shape[1]
    n_sel = ovt_ref.shape[0]
    lo = _lane_lo((CMP_ROWS, LANES))
    blk = lax.broadcasted_iota(jnp.int32, (n_sel, CMP_ROWS), 0)
    for c in range(S // CMP_ROWS):
        rows = slice(c * CMP_ROWS, (c + 1) * CMP_ROWS)
        psum = [jnp.zeros((CMP_ROWS, LANES), F32), jnp.zeros((CMP_ROWS, LANES), F32)]
        for r in range(4):
            heads = _split_heads(q_ref[0, rows, r * LANES:(r + 1) * LANES])
            outs = []
            for h in range(2):
                s = lax.dot_general(heads[h], kc, NT_DIMS, preferred_element_type=F32) + bias_ref[0, 2 * r + h, rows, :]
                m = jnp.max(s, axis=1, keepdims=True)
                e = jnp.exp(s - m)
                l = jnp.sum(e, axis=1, keepdims=True)
                pr = e * jnp.where(m > 0.5 * NEG, 1.0 / l, 0.0)
                psum[h] = psum[h] + pr
                outs.append(jnp.dot(pr.astype(BF16), vc, preferred_element_type=F32))
            o_ref[0, rows, r * LANES:(r + 1) * LANES] = jnp.where(lo, outs[0], outs[1])

        t = c * CMP_ROWS + lax.broadcasted_iota(jnp.int32, (n_sel, CMP_ROWS), 1)
        cur = lax.shift_right_logical(t, int(math.log2(NSA_SEL_BLOCK)))
        keeps = []
        for h in range(2):
            hi = psum[h].astype(BF16)
            rest = psum[h] - hi.astype(F32)
            mid = rest.astype(BF16)
            low = (rest - mid.astype(F32)).astype(BF16)
            ov = ovt_ref[...].astype(BF16)
            imp = lax.dot_general(jnp.concatenate([ov, ov, ov], axis=1), jnp.concatenate([hi, mid, low], axis=1),
                                  NT_DIMS, preferred_element_type=F32)
            forced = jnp.where(blk == 0, 1.0, jnp.where(blk == cur, 1.0, jnp.where(blk == cur - 1, 1.0, 0.0)))
            imp = jnp.where(blk <= cur, imp + forced * NSA_FORCE, -jnp.inf)
            groups = [imp[g * SUBLANES:(g + 1) * SUBLANES, :] for g in range(n_sel // SUBLANES)]
            ranks = [jnp.zeros(g.shape, F32) for g in groups]
            for m in range(n_sel):
                im = imp[m:m + 1, :]
                for g, sub in enumerate(groups):
                    ge = jnp.where(im >= sub, 1.0, 0.0)
                    gt = jnp.where(im > sub, 1.0, 0.0)
                    if g * SUBLANES > m:
                        inc = ge
                    elif (g + 1) * SUBLANES - 1 <= m:
                        inc = gt
                    else:
                        inc = jnp.where(lax.broadcasted_iota(jnp.int32, sub.shape, 0) + g * SUBLANES > m, ge, gt)
                    ranks[g] = ranks[g] + inc
            keeps.append(jnp.where(jnp.concatenate(ranks, axis=0) < NSA_SEL_TOPN, 1.0, 0.0))
        keep = jnp.concatenate(keeps + [jnp.zeros((LANES - 2 * n_sel, CMP_ROWS), F32)], axis=0).T
        lane = lax.broadcasted_iota(jnp.int32, keep.shape, 1)
        pen_ref[0, 0, rows, :] = jnp.where(lane < 2 * n_sel, (keep - 1.0) * (-NEG), 0.0).astype(pen_ref.dtype)


def _cmp_attention(qkv, raw, k_params, v_params, bias, ovt):
    B, S, _ = qkv.shape
    n_sel = ovt.shape[0]
    v_block0 = raw.shape[2] // (2 * LANES)
    params = _compress_operands(*k_params) + _compress_operands(*v_params)
    whole = lambda a: pl.BlockSpec(a.shape, lambda m, b: (0,) * a.ndim)
    return pl.pallas_call(
        _cmp_kernel,
        grid=(2, B),
        in_specs=[
            pl.BlockSpec((1, S, 4 * LANES), lambda m, b: (b, 0, m)),
            pl.BlockSpec((1, S, LANES), lambda m, b: (b, 0, m)),
            pl.BlockSpec((1, S, LANES), lambda m, b: (b, 0, v_block0 + m)),
        ] + [whole(a) for a in params] + [
            pl.BlockSpec((1, 8, S, LANES), lambda m, b: (m, 0, 0, 0), pipeline_mode=pl.Buffered(1)),
            pl.BlockSpec((n_sel, LANES), lambda m, b: (0, 0)),
        ],
        out_specs=[
            pl.BlockSpec((1, S, 4 * LANES), lambda m, b: (b, 0, m)),
            pl.BlockSpec((1, 1, S, LANES), lambda m, b: (b, m, 0, 0)),
        ],
        out_shape=[jax.ShapeDtypeStruct((B, S, D_MODEL), F32),
                   jax.ShapeDtypeStruct((B, 2, S, LANES), BF16)],
        compiler_params=_cparams(2),
        name="nsa_compressed_select",
    )(qkv, raw, raw, *params, bias, ovt)


def _attn1_kernel(q_ref, ks_ref, vs_ref, kw_ref, vw_ref, rs_ref, rw_ref, e_ref, pen_ref, ocmp_ref, graw_ref,
                  o_ref, strip_s_ref, qaug_ref, kaug_ref, vsaug_ref, s_ref, p_ref, oslc_ref,
                  strip_w_ref, vwaug_ref, owin_ref):
    p = pl.program_id(0)
    b = pl.program_id(1)
    S = q_ref.shape[1]
    nblk = S // TQ

    @pl.when(b == 0)
    def _():
        _fill_strips(strip_s_ref, rs_ref)
        _fill_strips(strip_w_ref, rw_ref)

    @pl.when(jnp.logical_and(p == 0, b == 0))
    def _():
        for h in range(2):
            kaug_ref[h, :, LANES:] = e_ref[h]
        _fill_value_ones(vsaug_ref)
        _fill_value_ones(vwaug_ref)

    ks = ks_ref[0]
    pen = pen_ref[0, 0]
    for h, qh in enumerate(_split_heads(q_ref[0])):
        kaug_ref[h, :, :LANES] = ks
        qaug_ref[h, :, :LANES] = qh
        qaug_ref[h, :, LANES:] = pen
    vsaug_ref[:, :LANES] = vs_ref[0]
    vwaug_ref[:, :LANES] = vw_ref[0]
    _pair_attention(lambda h, r0, r1: qaug_ref[h, r0:r1, :],
                    lambda h, j: kaug_ref[h, j * TQ:(j + 1) * TQ, :],
                    lambda r0, r1: vsaug_ref[r0:r1, :],
                    strip_s_ref, s_ref, p_ref, oslc_ref, nblk, nblk)
    _pair_attention(lambda h, r0, r1: qaug_ref[h, r0:r1, :LANES],
                    lambda h, j: kw_ref[0, j * TQ:(j + 1) * TQ, :],
                    lambda r0, r1: vwaug_ref[r0:r1, :],
                    strip_w_ref, s_ref, p_ref, owin_ref, nblk, strip_w_ref.shape[1] // TQ, skew=2)
    sig = _sigmoid(pltpu.roll(graw_ref[0], lax.rem(LANES - GATE_STRIDE * p, LANES), 1))
    lo_lanes = _lane_lo((S, LANES))
    gates = [jnp.where(lo_lanes, sig[:, 2 * br:2 * br + 1], sig[:, 2 * br + 1:2 * br + 2]) for br in range(3)]
    out = gates[0] * ocmp_ref[0] + gates[1] * _merge_heads(oslc_ref) + gates[2] * _merge_heads(owin_ref)
    o_ref[0] = out.astype(o_ref.dtype)


def _attn1(qkv, r_slc, r_win, e_onehot, pen, ocmp, graw):
    B, S, _ = qkv.shape
    kv0 = D_MODEL // LANES + 4
    n_win = r_win.shape[2]
    return pl.pallas_call(
        _attn1_kernel,
        grid=(N_PAIRS, B),
        in_specs=[
            pl.BlockSpec((1, S, LANES), lambda p, b: (b, 0, p)),
            pl.BlockSpec((1, S, LANES), lambda p, b: (b, 0, kv0 + p // 4)),
            pl.BlockSpec((1, S, LANES), lambda p, b: (b, 0, kv0 + 2 + p // 4)),
            pl.BlockSpec((1, S, LANES), lambda p, b: (b, 0, kv0 + 4 + p // 4)),
            pl.BlockSpec((1, S, LANES), lambda p, b: (b, 0, kv0 + 6 + p // 4)),
            pl.BlockSpec((1, 2, r_slc.shape[2], 2 * TQ), lambda p, b: (p, 0, 0, 0)),
            pl.BlockSpec((1, 2, n_win, 2 * TQ), lambda p, b: (p, 0, 0, 0)),
            pl.BlockSpec((2, S, LANES), lambda p, b: (0, 0, 0), pipeline_mode=pl.Buffered(1)),
            pl.BlockSpec((1, 1, S, LANES), lambda p, b: (b, p // 4, 0, 0)),
            pl.BlockSpec((1, S, LANES), lambda p, b: (b, 0, p)),
            pl.BlockSpec((1, S, LANES), lambda p, b: (b, 0, 0)),
        ],
        out_specs=pl.BlockSpec((1, S, LANES), lambda p, b: (b, 0, p)),
        out_shape=jax.ShapeDtypeStruct((B, S, D_MODEL), BF16),
        scratch_shapes=[_strip_scratch(S)] + _attn_set(S, S // TQ, n_win) + [_strip_scratch(n_win * TQ),
                                                    pltpu.VMEM((S, 2 * LANES), BF16), pltpu.VMEM((2, S, LANES), F32)],
        compiler_params=pltpu.CompilerParams(dimension_semantics=("arbitrary",) * 2, vmem_limit_bytes=ATTN_VMEM_LIMIT),
        name="attn_nsa",
    )(qkv, qkv, qkv, qkv, qkv, r_slc, r_win, e_onehot, pen, ocmp, graw)


def _bucket(dist):
    n = np.maximum(dist, 0)
    exact = REL_BUCKETS // 2
    nf = np.maximum(n, 1).astype(np.float64)
    large = exact + (np.log(nf / exact) / math.log(REL_MAX_DIST / exact) * (REL_BUCKETS - exact)).astype(np.int64)
    return np.where(n < exact, n, np.minimum(large, REL_BUCKETS - 1))


def _strip_rows(tab, n_off, extra_of_dist):
    u = np.arange(2 * TQ)
    d = np.arange(n_off)[:, None] * TQ - np.where(u < TQ, u, u - 2 * TQ)[None, :]
    onehot = (_bucket(d)[None] == np.arange(REL_BUCKETS)[:, None, None]).astype(np.float32)
    vals = jnp.einsum("hb,bou->hou", tab, jnp.asarray(onehot), precision=lax.Precision.HIGHEST) + extra_of_dist(d)
    return vals.reshape(tab.shape[0] // 2, 2, n_off, 2 * TQ)


def _causal_mask(d):
    return np.where(d >= 0, 0.0, NEG).astype(np.float32)


def _window_mask(d):
    return np.where((d >= 0) & (d < NSA_WINDOW), 0.0, NEG).astype(np.float32)


def _dilation_log_count(d):
    c = ((d >= 0) & (d <= 128)).astype(np.float64)
    c += ((d >= 0) & (d % 4 == 0) & (d <= 512))
    c += ((d >= 0) & (d % 16 == 0) & (d <= 2048))
    return np.where(c > 0, np.log(np.maximum(c, 1.0)), NEG).astype(np.float32)


def _one_hot_blocks(S, block, per_head):
    e = np.zeros((2, S, LANES), np.float32)
    key = np.arange(S)
    for h in range(2):
        e[h, key, h * per_head + key // block] = 1.0
    return jnp.asarray(e, BF16)


_NSA_HEAD_ORDER = np.array([8 * (p // 4) + (p % 4) + 4 * h for p in range(N_PAIRS) for h in range(2)])


def _head_cols(heads):
    return (np.asarray(heads)[:, None] * HEAD_DIM + np.arange(HEAD_DIM)[None, :]).reshape(-1)


def _mixer_ab(h, g_mix, w_in, qn_a, kn_a, qn_b, kn_b, rel_bias, B, S):
    wa = MOBA_HEADS * HEAD_DIM
    ones = jnp.ones((wa,), F32)
    nh = MOBA_HEADS
    gain = jnp.concatenate([jnp.tile(qn_a, nh) * ATTN_SCALE, jnp.tile(qn_b, nh) * ATTN_SCALE,
                            jnp.tile(kn_a, nh), jnp.tile(kn_b, nh), ones, ones])
    per_sec = wa // PROJ_CHUNK
    src = [sec * per_sec + c for sec in (0, 3, 1, 4, 2, 5) for c in range(per_sec)]
    (qkv,) = _project(h, g_mix, w_in.astype(BF16), gain, [True] * (4 * per_sec) + [False] * (2 * per_sec),
                      src_chunks=src)
    qkv = qkv.reshape(B, S, 3 * D_MODEL)

    tab = rel_bias.T
    r_tab = jnp.concatenate([_strip_rows(tab[:MOBA_HEADS], S // TQ, _causal_mask),
                             _strip_rows(tab[MOBA_HEADS:], S // TQ, _dilation_log_count)])
    o = _attn0(qkv, r_tab, _one_hot_blocks(S, MOBA_BLOCK, S // MOBA_BLOCK))
    return o.reshape(B * S, D_MODEL)


def _mixer_nsa(h, g_mix, w_in, qn, kn_c, kn_s, kn_w, cmp_k, cmp_v, rel_bias, B, S):
    qw = N_HEADS * HEAD_DIM
    kvw = NSA_GROUPS * HEAD_DIM
    order = _NSA_HEAD_ORDER
    w_main = jnp.concatenate([w_in[:, hd * HEAD_DIM:(hd + 1) * HEAD_DIM] for hd in order]
                             + [w_in[:, qw:qw + 6 * kvw]], axis=1).astype(BF16)
    ones = jnp.ones((kvw,), F32)
    gain = jnp.concatenate([jnp.tile(qn, N_HEADS) * ATTN_SCALE, ones, ones,
                            jnp.tile(kn_s, NSA_GROUPS), ones, jnp.tile(kn_w, NSA_GROUPS), ones])
    assert kvw == PROJ_CHUNK
    norm_chunks = [True] * (qw // PROJ_CHUNK) + [False, False, True, False, True, False]

    gcols = np.zeros((LANES,), np.int64)
    gused = np.zeros((LANES,), np.float32)
    for p in range(N_PAIRS):
        for br in range(3):
            for hh in range(2):
                c = GATE_STRIDE * p + 2 * br + hh
                gcols[c] = qw + 6 * kvw + 3 * order[2 * p + hh] + br
                gused[c] = 1.0
    w_gate = (w_in[:, gcols] * gused).astype(BF16)
    kc_chunk = qw // PROJ_CHUNK
    qkv, graw, raw = _project(h, g_mix, w_main, gain, norm_chunks, w_gate, raw_chunks=(kc_chunk, kc_chunk + 1))
    qkv = qkv.reshape(B, S, qw + 6 * kvw)
    graw = graw.reshape(B, S, LANES)
    raw = raw.reshape(B, S, 2 * kvw)

    tab = rel_bias.T[order]
    n_cmp = (S - NSA_CMP_LEN) // NSA_CMP_STRIDE + 1
    t_pos = np.arange(S)[:, None]
    c_idx = np.arange(LANES)[None, :]
    dc = t_pos - (c_idx * NSA_CMP_STRIDE + NSA_CMP_LEN - 1)
    cmp_mask = np.where((dc >= 0) & (c_idx < n_cmp), 0.0, NEG).astype(np.float32)
    onehot = jnp.asarray(_bucket(dc).astype(np.int8)).reshape(1, -1) == jnp.arange(REL_BUCKETS, dtype=jnp.int8)[:, None]
    bias_c = jnp.dot(tab, onehot.astype(F32), precision=lax.Precision.HIGHEST) + cmp_mask.reshape(1, -1)
    bias_c = bias_c.reshape(2, 8, S, LANES)
    n_sel = S // NSA_SEL_BLOCK
    cstart = np.arange(LANES) * NSA_CMP_STRIDE
    sstart = np.arange(n_sel) * NSA_SEL_BLOCK
    ovt = np.maximum(np.minimum(cstart[None, :] + NSA_CMP_LEN, sstart[:, None] + NSA_SEL_BLOCK)
                     - np.maximum(cstart[None, :], sstart[:, None]), 0).astype(np.float32)
    ovt[:, n_cmp:] = 0.0
    ocmp, pen = _cmp_attention(qkv, raw, (*cmp_k, kn_c), (*cmp_v, kn_c), bias_c, jnp.asarray(ovt))

    r_slc = _strip_rows(tab, S // TQ, _causal_mask)
    r_win = _strip_rows(tab, NSA_WINDOW // TQ + 1, _window_mask)
    o = _attn1(qkv, r_slc, r_win, _one_hot_blocks(S, NSA_SEL_BLOCK, n_sel), pen, ocmp, graw)
    return o.reshape(B * S, D_MODEL)


def kernel(x, p, rel_bias, norm_mix, norm_ffn, norm_ple, w_ffn_gate, w_ffn_up, w_ffn_down, w_ple_proj, w_ple_gate, w_in_ab, w_out_ab, qn_moba, kn_moba, qn_dil, kn_dil, w_in_nsa, w_out_nsa, qn_nsa, kn_cmp, kn_slc, kn_win, cmp_k_pos, cmp_k_w1, cmp_k_b1, cmp_k_w2, cmp_k_b2, cmp_v_pos, cmp_v_w1, cmp_v_b1, cmp_v_w2, cmp_v_b2):
    B, S, D = x.shape
    depth = p.shape[0]
    assert D == D_MODEL and S == MOBA_NBLK * MOBA_BLOCK and (B * S) % ROW_TILE == 0 and B % ATTN0_ROWS == 0, (B, S, D)
    h = x.reshape(B * S, D)
    wg, wu, wd, wpg, wpp = (w.astype(BF16) for w in (w_ffn_gate, w_ffn_up, w_ffn_down, w_ple_gate, w_ple_proj))
    for i in range(depth):
        e = i // 2
        if i % 2 == 0:
            o = _mixer_ab(h, norm_mix[i], w_in_ab[e], qn_moba[e], kn_moba[e], qn_dil[e], kn_dil[e], rel_bias, B, S)
            w_out = w_out_ab[e]
        else:
            cmp_k = (cmp_k_pos[e], cmp_k_w1[e], cmp_k_b1[e], cmp_k_w2[e], cmp_k_b2[e])
            cmp_v = (cmp_v_pos[e], cmp_v_w1[e], cmp_v_b1[e], cmp_v_w2[e], cmp_v_b2[e])
            o = _mixer_nsa(h, norm_mix[i], w_in_nsa[e], qn_nsa[e], kn_cmp[e], kn_slc[e], kn_win[e],
                           cmp_k, cmp_v, rel_bias, B, S)
            w_out = w_out_nsa[e][_head_cols(_NSA_HEAD_ORDER), :]
        h = _post_attention(o, h, w_out.astype(BF16), norm_ffn[i], wg, wu, wd, norm_ple[i], wpg,
                            p.reshape(depth, B * S, -1), wpp, i)
    return h.reshape(B, S, D)
```

```python
import functools
import math

import numpy as np
import jax
import jax.numpy as jnp
from jax import lax
from jax.experimental import pallas as pl
from jax.experimental.pallas import tpu as pltpu

F32 = jnp.float32
BF16 = jnp.bfloat16

D_MODEL = 1024
HEAD_DIM = 64
N_HEADS = 16
N_PAIRS = N_HEADS // 2
MOBA_HEADS = 8
MOBA_BLOCK = 256
MOBA_TOPK = 3
MOBA_NBLK = 8
NSA_GROUPS = 4
NSA_CMP_LEN = 32
NSA_CMP_STRIDE = 16
NSA_SEL_BLOCK = 64
NSA_SEL_TOPN = 16
NSA_WINDOW = 512
NSA_FORCE = 1.0e6
REL_BUCKETS = 32
REL_MAX_DIST = 2048
RMS_EPS = 1e-6
ATTN_SCALE = HEAD_DIM ** -0.5

LANES = 128
SUBLANES = 8
TQ = 256
NEG = -1.0e30
ROW_TILE = 512
PROJ_CHUNK = 256
POST_SPLIT = 2
GATE_STRIDE = LANES // N_HEADS
ATTN0_ROWS = 2
CMP_ROWS = 2048
VMEM_LIMIT = 48 * 1024 * 1024
POST_VMEM_LIMIT = 56 * 1024 * 1024
ATTN_VMEM_LIMIT = 56 * 1024 * 1024

NT_DIMS = (((1,), (1,)), ((), ()))


def _cparams(n_axes):
    return pltpu.CompilerParams(dimension_semantics=("arbitrary",) * n_axes,
                                vmem_limit_bytes=VMEM_LIMIT)


def _rms_rows(x, g):
    ms = jnp.mean(x * x, axis=-1, keepdims=True)
    return x * lax.rsqrt(ms + RMS_EPS) * g


def _lane_lo(shape):
    return lax.broadcasted_iota(jnp.int32, shape, len(shape) - 1) < HEAD_DIM


def _resident(shape):
    return pl.BlockSpec(shape, lambda i: (0,) * len(shape), pipeline_mode=pl.Buffered(1))


def _head_sum_squares(y):
    y2 = y * y
    parts = []
    for blk in range(y.shape[1] // LANES):
        yb = y2[:, blk * LANES:(blk + 1) * LANES]
        lo = _lane_lo(yb.shape)
        s_lo = jnp.sum(jnp.where(lo, yb, 0.0), axis=1, keepdims=True)
        s_hi = jnp.sum(jnp.where(lo, 0.0, yb), axis=1, keepdims=True)
        parts.append(jnp.where(lo, s_lo, s_hi))
    return jnp.concatenate(parts, axis=1)


def _proj_kernel(x_ref, g_ref, w_ref, cg_ref, *rest, norm_chunks, src_chunks, raw_chunks, with_extra):
    if with_extra:
        wx_ref, o_ref, ox_ref, oraw_ref = rest
    else:
        (o_ref,) = rest
    xn = _rms_rows(x_ref[...], g_ref[...]).astype(BF16)

    def finish(c, y):
        cols = slice(c * PROJ_CHUNK, (c + 1) * PROJ_CHUNK)
        if c in raw_chunks:
            k = raw_chunks.index(c)
            oraw_ref[:, k * PROJ_CHUNK:(k + 1) * PROJ_CHUNK] = y
        if norm_chunks[c]:
            y = y * lax.rsqrt(_head_sum_squares(y) * (1.0 / HEAD_DIM) + RMS_EPS) * cg_ref[:, cols]
        o_ref[:, cols] = y.astype(o_ref.dtype)

    prev = None
    for c in range(len(norm_chunks)):
        src = src_chunks[c] * PROJ_CHUNK
        y = jnp.dot(xn, w_ref[:, src:src + PROJ_CHUNK], preferred_element_type=F32)
        if prev is not None:
            finish(c - 1, prev)
        prev = y
    finish(len(norm_chunks) - 1, prev)
    if with_extra:
        ox_ref[...] = jnp.dot(xn, wx_ref[...], preferred_element_type=F32)


def _project(x, g, w, col_gain, norm_chunks, w_extra=None, raw_chunks=(), src_chunks=None):
    if src_chunks is None:
        src_chunks = range(len(norm_chunks))
    T, D = x.shape
    N = w.shape[1]
    row_in = lambda n: pl.BlockSpec((ROW_TILE, n), lambda i: (i, 0))
    in_specs = [row_in(D), _resident((1, D)), _resident((D, N)), _resident((1, N))]
    args = [x, g.reshape(1, D), w, col_gain.reshape(1, N)]
    out_specs = [row_in(N)]
    out_shape = [jax.ShapeDtypeStruct((T, N), BF16)]
    if w_extra is not None:
        nx = w_extra.shape[1]
        in_specs.append(_resident((D, nx)))
        args.append(w_extra)
        out_specs += [row_in(nx), row_in(len(raw_chunks) * PROJ_CHUNK)]
        out_shape += [jax.ShapeDtypeStruct((T, nx), F32), jax.ShapeDtypeStruct((T, len(raw_chunks) * PROJ_CHUNK), F32)]
    return pl.pallas_call(
        functools.partial(_proj_kernel, norm_chunks=tuple(norm_chunks), src_chunks=tuple(src_chunks),
                          raw_chunks=tuple(raw_chunks),
                          with_extra=w_extra is not None),
        grid=(T // ROW_TILE,),
        in_specs=in_specs,
        out_specs=out_specs,
        out_shape=out_shape,
        compiler_params=_cparams(1),
        name="proj",
    )(*args)


def _sigmoid(z):
    return 1.0 / (1.0 + jnp.exp(-z))


def _post_kernel(o_ref, h_ref, wout_ref, gf_ref, wg_ref, wu_ref, wd_ref, gp_ref, wpg_ref, p_ref, wpp_ref, out_ref):
    halves = [slice(r * ROW_TILE // POST_SPLIT, (r + 1) * ROW_TILE // POST_SPLIT) for r in range(POST_SPLIT)]
    h1 = [h_ref[r, :] + jnp.dot(o_ref[r, :], wout_ref[...], preferred_element_type=F32) for r in halves]
    xn = [_rms_rows(x, gf_ref[...]).astype(BF16) for x in h1]
    act = []
    for x in xn:
        ai = jnp.dot(x, wg_ref[...], preferred_element_type=F32)
        ui = jnp.dot(x, wu_ref[...], preferred_element_type=F32)
        act.append((ai * _sigmoid(ai) * ui).astype(BF16))
    h2 = [x + jnp.dot(t, wd_ref[...], preferred_element_type=F32) for x, t in zip(h1, act)]
    hn = [_rms_rows(x, gp_ref[...]).astype(BF16) for x in h2]
    gate = [_sigmoid(jnp.dot(x, wpg_ref[...], preferred_element_type=F32)) for x in hn]
    for r, x, g in zip(halves, h2, gate):
        out_ref[r, :] = x + g * jnp.dot(p_ref[r, :].astype(BF16), wpp_ref[...], preferred_element_type=F32)


def _post_attention(o, h, w_out, g_ffn, wg, wu, wd, g_ple, w_pgate, p, w_pproj, layer):
    T, D = h.shape
    Fh = wg.shape[2]
    Pd = p.shape[2]
    row_in = lambda n: pl.BlockSpec((ROW_TILE, n), lambda i: (i, 0))
    slab = lambda r, c: pl.BlockSpec((None, r, c), lambda i: (layer, 0, 0), pipeline_mode=pl.Buffered(1))
    return pl.pallas_call(
        _post_kernel,
        grid=(T // ROW_TILE,),
        in_specs=[row_in(D), row_in(D), _resident((D, D)), _resident((1, D)), slab(D, Fh),
                  slab(D, Fh), slab(Fh, D), _resident((1, D)), slab(D, D),
                  pl.BlockSpec((None, ROW_TILE, Pd), lambda i: (layer, i, 0)), slab(Pd, D)],
        out_specs=row_in(D),
        out_shape=jax.ShapeDtypeStruct((T, D), F32),
        compiler_params=pltpu.CompilerParams(dimension_semantics=("arbitrary",), vmem_limit_bytes=POST_VMEM_LIMIT),
        name="post_attention",
    )(o, h, w_out, g_ffn.reshape(1, D), wg, wu, wd, g_ple.reshape(1, D), w_pgate, p, w_pproj)


def _fill_strips(strip_ref, r_ref):
    for h in range(2):
        for o in range(r_ref.shape[2]):
            x = jnp.broadcast_to(r_ref[0, h, o:o + 1, :], (TQ, 2 * TQ))
            strip_ref[h, o * TQ:(o + 1) * TQ, :] = pltpu.roll(x, 0, 1, stride=1, stride_axis=0)[:, :TQ]


def _tile_groups(nblk, span):
    cnt = [min(span, nblk - j) for j in range(nblk)]
    off = [0]
    for c in cnt:
        off.append(off[-1] + c * TQ)
    return cnt, off


def _tile_rows(nblk, span):
    return _tile_groups(nblk, span)[1][-1]


def _pair_attention(q_rows, k_rows, v_rows, strip_ref, s_ref, p_ref, o_ref, nblk, span, skew=1):
    cnt, off = _tile_groups(nblk, span)
    together = s_ref.shape[0] >= 2 * off[-1] and p_ref.shape[2] >= 2 * span * TQ
    s0 = [0, off[-1] if together else 0]
    p0 = [0, span * TQ if together else 0]

    def scores(h, j):
        n = cnt[j] * TQ
        s = lax.dot_general(q_rows(h, j * TQ, j * TQ + n), k_rows(h, j), NT_DIMS, preferred_element_type=F32)
        s_ref[s0[h] + off[j]:s0[h] + off[j] + n, :] = s + strip_ref[h, 0:n, :]

    def probs(h, i):
        j0 = max(0, i - span + 1)
        rows = [s0[h] + off[j] + (i - j) * TQ for j in range(j0, i + 1)]
        mx = s_ref[rows[0]:rows[0] + TQ, :]
        for r in rows[1:]:
            mx = jnp.maximum(mx, s_ref[r:r + TQ, :])
        m = jnp.max(mx, axis=1, keepdims=True)
        for t, r in enumerate(rows):
            p_ref[i, :, p0[h] + t * TQ:p0[h] + (t + 1) * TQ] = jnp.exp(s_ref[r:r + TQ, :] - m).astype(BF16)

    def values(h, i):
        j0 = max(0, i - span + 1)
        out = jnp.dot(p_ref[i, :, p0[h]:p0[h] + (i + 1 - j0) * TQ], v_rows(j0 * TQ, (i + 1) * TQ),
                      preferred_element_type=F32)
        o_ref[h, i * TQ:(i + 1) * TQ, :] = out[:, :LANES] / out[:, LANES:]

    for heads in ([(0, 1)] if together else [(0,), (1,)]):
        for j in range(nblk + skew):
            for h in heads:
                if j < nblk:
                    scores(h, j)
            for h in heads:
                if j >= skew:
                    values(h, j - skew)
            for h in heads:
                if j < nblk:
                    probs(h, j)


def _merge_heads(o_ref):
    return jnp.where(_lane_lo((o_ref.shape[1], LANES)), o_ref[0], o_ref[1])


def _split_heads(q):
    lo = _lane_lo(q.shape)
    zero = jnp.zeros_like(q)
    return jnp.where(lo, q, zero), jnp.where(lo, zero, q)


def _fill_value_ones(vaug_ref):
    vaug_ref[:, LANES:] = jnp.ones((vaug_ref.shape[0], LANES), vaug_ref.dtype)


def _moba_block_means(k_ref, bb):
    shape = (2 * MOBA_NBLK, LANES)
    lo = _lane_lo(shape)
    row = lax.broadcasted_iota(jnp.int32, shape, 0)
    kmt = jnp.zeros(shape, F32)
    for n in range(MOBA_NBLK):
        mean = jnp.mean(k_ref[bb, n * MOBA_BLOCK:(n + 1) * MOBA_BLOCK, :].astype(F32), axis=0, keepdims=True)
        kmt = jnp.where(row == n, jnp.where(lo, mean, 0.0), kmt)
        kmt = jnp.where(row == MOBA_NBLK + n, jnp.where(lo, 0.0, mean), kmt)
    return kmt


def _moba_penalty(q, kmt):
    nb = MOBA_NBLK
    S = q.shape[0]
    hi = kmt.astype(BF16)
    rest = kmt - hi.astype(F32)
    mid = rest.astype(BF16)
    lo = (rest - mid.astype(F32)).astype(BF16)
    g3 = lax.dot_general(jnp.concatenate([hi, mid, lo], axis=0), q, NT_DIMS, preferred_element_type=F32)
    g = g3[0:2 * nb] + g3[2 * nb:4 * nb] + g3[4 * nb:6 * nb]
    row = lax.broadcasted_iota(jnp.int32, g.shape, 0)
    n = row & (nb - 1)
    own = lax.shift_right_logical(lax.broadcasted_iota(jnp.int32, g.shape, 1), int(math.log2(MOBA_BLOCK)))
    rank = jnp.zeros(g.shape, F32)
    for m in range(nb - 1):
        gm = jnp.where(row < nb, g[m:m + 1, :], g[nb + m:nb + m + 1, :])
        tie = jnp.where(n > m, 1.0, 0.0)
        beats = jnp.where(gm > g, 1.0, jnp.where(gm == g, tie, 0.0))
        rank = rank + jnp.where(own > m, beats, 0.0)
    keep = jnp.where(n < own, jnp.where(rank < MOBA_TOPK, 1.0, 0.0), jnp.where(n == own, 1.0, 0.0))
    keep = jnp.concatenate([keep, jnp.zeros((LANES - 2 * nb, S), F32)], axis=0).T
    lane = lax.broadcasted_iota(jnp.int32, keep.shape, 1)
    return jnp.where(lane < 2 * nb, (keep - 1.0) * (-NEG), 0.0)


def _attn0_kernel(q_ref, k_ref, v_ref, r_ref, e_ref, o_ref, strip_ref, *scratch):
    p = pl.program_id(0)
    b = pl.program_id(1)
    S = q_ref.shape[1]
    nblk = S // TQ
    n_set = len(scratch) // ATTN0_ROWS
    sets = [scratch[t * n_set:(t + 1) * n_set] for t in range(ATTN0_ROWS)]
    is_moba = p < MOBA_HEADS // 2

    @pl.when(b == 0)
    def _():
        _fill_strips(strip_ref, r_ref)

    @pl.when(jnp.logical_and(p == 0, b == 0))
    def _():
        for _, kaug_ref, vaug_ref, _, _, _ in sets:
            for h in range(2):
                kaug_ref[h, :, LANES:] = e_ref[h]
            _fill_value_ones(vaug_ref)

    for bb, (qaug_ref, kaug_ref, vaug_ref, _, _, _) in enumerate(sets):
        k = k_ref[bb]
        for h, qh in enumerate(_split_heads(q_ref[bb])):
            kaug_ref[h, :, :LANES] = k
            qaug_ref[h, :, :LANES] = qh
        vaug_ref[:, :LANES] = v_ref[bb]

    @pl.when(is_moba)
    def _():
        for bb, (qaug_ref, _, _, _, _, _) in enumerate(sets):
            pen = _moba_penalty(q_ref[bb], _moba_block_means(k_ref, bb)).astype(BF16)
            for h in range(2):
                qaug_ref[h, :, LANES:] = pen

    @pl.when(jnp.logical_not(is_moba))
    def _():
        for qaug_ref, _, _, _, _, _ in sets:
            for h in range(2):
                qaug_ref[h, :, LANES:] = jnp.zeros((S, LANES), BF16)

    for bb, (qaug_ref, kaug_ref, vaug_ref, s_ref, p_ref, oh_ref) in enumerate(sets):
        _pair_attention(lambda h, r0, r1, ref=qaug_ref: ref[h, r0:r1, :],
                        lambda h, j, ref=kaug_ref: ref[h, j * TQ:(j + 1) * TQ, :],
                        lambda r0, r1, ref=vaug_ref: ref[r0:r1, :],
                        strip_ref, s_ref, p_ref, oh_ref, nblk, nblk)
        o_ref[bb] = _merge_heads(oh_ref).astype(o_ref.dtype)


def _strip_scratch(rows):
    return pltpu.VMEM((2, rows, TQ), F32)


def _attn_set(S, span, narrow_span=0):
    nblk = S // TQ
    s_rows = max(_tile_rows(nblk, span), 2 * _tile_rows(nblk, narrow_span))
    return [pltpu.VMEM((2, S, 2 * LANES), BF16), pltpu.VMEM((2, S, 2 * LANES), BF16),
            pltpu.VMEM((S, 2 * LANES), BF16), pltpu.VMEM((s_rows, TQ), F32),
            pltpu.VMEM((nblk, TQ, span * TQ), BF16), pltpu.VMEM((2, S, LANES), F32)]


def _attn0(qkv, r_tab, e_onehot):
    B, S, _ = qkv.shape
    nblk = S // TQ
    return pl.pallas_call(
        _attn0_kernel,
        grid=(N_PAIRS, B // ATTN0_ROWS),
        in_specs=[
            pl.BlockSpec((ATTN0_ROWS, S, LANES), lambda p, b: (b, 0, p)),
            pl.BlockSpec((ATTN0_ROWS, S, LANES), lambda p, b: (b, 0, N_PAIRS + p)),
            pl.BlockSpec((ATTN0_ROWS, S, LANES), lambda p, b: (b, 0, 2 * N_PAIRS + p)),
            pl.BlockSpec((1, 2, nblk, 2 * TQ), lambda p, b: (p, 0, 0, 0)),
            pl.BlockSpec((2, S, LANES), lambda p, b: (0, 0, 0), pipeline_mode=pl.Buffered(1)),
        ],
        out_specs=pl.BlockSpec((ATTN0_ROWS, S, LANES), lambda p, b: (b, 0, p)),
        out_shape=jax.ShapeDtypeStruct((B, S, D_MODEL), BF16),
        scratch_shapes=[_strip_scratch(S)] + _attn_set(S, nblk) * ATTN0_ROWS,
        compiler_params=pltpu.CompilerParams(dimension_semantics=("arbitrary",) * 2, vmem_limit_bytes=ATTN_VMEM_LIMIT),
        name="attn_moba_dilated",
    )(qkv, qkv, qkv, r_tab, e_onehot)


def _compress_pair(x_ref, pos_ref, w1_ref, b1_ref, w2_ref, b2_ref, gain_ref, normed):
    n_chunk = x_ref.shape[1] // NSA_CMP_STRIDE
    first = second = None
    for a in range(NSA_CMP_STRIDE):
        t = x_ref[0, pl.ds(a, n_chunk, stride=NSA_CMP_STRIDE), :]
        fa = jnp.dot((t + pos_ref[a:a + 1, :]).astype(BF16), w1_ref[a], preferred_element_type=F32)
        sa = jnp.dot((t + pos_ref[NSA_CMP_STRIDE + a:NSA_CMP_STRIDE + a + 1, :]).astype(BF16),
                     w1_ref[NSA_CMP_STRIDE + a], preferred_element_type=F32)
        first = fa if first is None else first + fa
        second = sa if second is None else second + sa
    hid = first + pltpu.roll(second, n_chunk - 1, 0) + b1_ref[...]
    cdf = 0.5 * (1.0 + jnp.tanh(math.sqrt(2.0 / math.pi) * (hid + 0.044715 * (hid * hid * hid))))
    y = jnp.dot((hid * cdf).astype(BF16), w2_ref[...], preferred_element_type=F32) + b2_ref[...]
    if normed:
        lo = _lane_lo(y.shape)
        y2 = y * y
        s_lo = jnp.sum(jnp.where(lo, y2, 0.0), axis=1, keepdims=True)
        s_hi = jnp.sum(jnp.where(lo, 0.0, y2), axis=1, keepdims=True)
        ms = jnp.where(lo, s_lo, s_hi) * (1.0 / HEAD_DIM)
        y = y * lax.rsqrt(ms + RMS_EPS) * gain_ref[...]
    return y.astype(BF16)


def _pair_block_diag(w):
    z = jnp.zeros_like(w)
    return jnp.concatenate([jnp.concatenate([w, z], axis=-1), jnp.concatenate([z, w], axis=-1)], axis=-2)


def _compress_operands(pos, w1, b1, w2, b2, gain):
    hid = w1.shape[1]
    n_pos = pos.shape[0]
    return [jnp.tile(pos, (1, 2)), _pair_block_diag(w1.reshape(n_pos, HEAD_DIM, hid)).astype(BF16),
            jnp.tile(b1, 2).reshape(1, 2 * hid), _pair_block_diag(w2).astype(BF16),
            jnp.tile(b2, 2).reshape(1, LANES), jnp.tile(gain, 2).reshape(1, LANES)]


def _cmp_kernel(q_ref, rawk_ref, rawv_ref, *rest):
    k_par, v_par = rest[0:6], rest[6:12]
    bias_ref, ovt_ref, o_ref, pen_ref = rest[12:]
    kc = _compress_pair(rawk_ref, *k_par, True)
    vc = _compress_pair(rawv_ref, *v_par, False)
    S = q_ref.shape[1]
    n_sel = ovt_ref.shape[0]
    lo = _lane_lo((CMP_ROWS, LANES))
    blk = lax.broadcasted_iota(jnp.int32, (n_sel, CMP_ROWS), 0)
    for c in range(S // CMP_ROWS):
        rows = slice(c * CMP_ROWS, (c + 1) * CMP_ROWS)
        psum = [jnp.zeros((CMP_ROWS, LANES), F32), jnp.zeros((CMP_ROWS, LANES), F32)]
        for r in range(4):
            heads = _split_heads(q_ref[0, rows, r * LANES:(r + 1) * LANES])
            outs = []
            for h in range(2):
                s = lax.dot_general(heads[h], kc, NT_DIMS, preferred_element_type=F32) + bias_ref[0, 2 * r + h, rows, :]
                m = jnp.max(s, axis=1, keepdims=True)
                e = jnp.exp(s - m)
                l = jnp.sum(e, axis=1, keepdims=True)
                pr = e * jnp.where(m > 0.5 * NEG, 1.0 / l, 0.0)
                psum[h] = psum[h] + pr
                outs.append(jnp.dot(pr.astype(BF16), vc, preferred_element_type=F32))
            o_ref[0, rows, r * LANES:(r + 1) * LANES] = jnp.where(lo, outs[0], outs[1])

        t = c * CMP_ROWS + lax.broadcasted_iota(jnp.int32, (n_sel, CMP_ROWS), 1)
        cur = lax.shift_right_logical(t, int(math.log2(NSA_SEL_BLOCK)))
        keeps = []
        for h in range(2):
            hi = psum[h].astype(BF16)
            rest = psum[h] - hi.astype(F32)
            mid = rest.astype(BF16)
            low = (rest - mid.astype(F32)).astype(BF16)
            ov = ovt_ref[...].astype(BF16)
            imp = lax.dot_general(jnp.concatenate([ov, ov, ov], axis=1), jnp.concatenate([hi, mid, low], axis=1),
                                  NT_DIMS, preferred_element_type=F32)
            forced = jnp.where(blk == 0, 1.0, jnp.where(blk == cur, 1.0, jnp.where(blk == cur - 1, 1.0, 0.0)))
            imp = jnp.where(blk <= cur, imp + forced * NSA_FORCE, -jnp.inf)
            groups = [imp[g * SUBLANES:(g + 1) * SUBLANES, :] for g in range(n_sel // SUBLANES)]
            ranks = [jnp.zeros(g.shape, F32) for g in groups]
            for m in range(n_sel):
                im = imp[m:m + 1, :]
                for g, sub in enumerate(groups):
                    ge = jnp.where(im >= sub, 1.0, 0.0)
                    gt = jnp.where(im > sub, 1.0, 0.0)
                    if g * SUBLANES > m:
                        inc = ge
                    elif (g + 1) * SUBLANES - 1 <= m:
                        inc = gt
                    else:
                        inc = jnp.where(lax.broadcasted_iota(jnp.int32, sub.shape, 0) + g * SUBLANES > m, ge, gt)
                    ranks[g] = ranks[g] + inc
            keeps.append(jnp.where(jnp.concatenate(ranks, axis=0) < NSA_SEL_TOPN, 1.0, 0.0))
        keep = jnp.concatenate(keeps + [jnp.zeros((LANES - 2 * n_sel, CMP_ROWS), F32)], axis=0).T
        lane = lax.broadcasted_iota(jnp.int32, keep.shape, 1)
        pen_ref[0, 0, rows, :] = jnp.where(lane < 2 * n_sel, (keep - 1.0) * (-NEG), 0.0).astype(pen_ref.dtype)


def _cmp_attention(qkv, raw, k_params, v_params, bias, ovt):
    B, S, _ = qkv.shape
    n_sel = ovt.shape[0]
    v_block0 = raw.shape[2] // (2 * LANES)
    params = _compress_operands(*k_params) + _compress_operands(*v_params)
    whole = lambda a: pl.BlockSpec(a.shape, lambda m, b: (0,) * a.ndim)
    return pl.pallas_call(
        _cmp_kernel,
        grid=(2, B),
        in_specs=[
            pl.BlockSpec((1, S, 4 * LANES), lambda m, b: (b, 0, m)),
            pl.BlockSpec((1, S, LANES), lambda m, b: (b, 0, m)),
            pl.BlockSpec((1, S, LANES), lambda m, b: (b, 0, v_block0 + m)),
        ] + [whole(a) for a in params] + [
            pl.BlockSpec((1, 8, S, LANES), lambda m, b: (m, 0, 0, 0), pipeline_mode=pl.Buffered(1)),
            pl.BlockSpec((n_sel, LANES), lambda m, b: (0, 0)),
        ],
        out_specs=[
            pl.BlockSpec((1, S, 4 * LANES), lambda m, b: (b, 0, m)),
            pl.BlockSpec((1, 1, S, LANES), lambda m, b: (b, m, 0, 0)),
        ],
        out_shape=[jax.ShapeDtypeStruct((B, S, D_MODEL), F32),
                   jax.ShapeDtypeStruct((B, 2, S, LANES), BF16)],
        compiler_params=_cparams(2),
        name="nsa_compressed_select",
    )(qkv, raw, raw, *params, bias, ovt)


def _attn1_kernel(q_ref, ks_ref, vs_ref, kw_ref, vw_ref, rs_ref, rw_ref, e_ref, pen_ref, ocmp_ref, graw_ref,
                  o_ref, strip_s_ref, qaug_ref, kaug_ref, vsaug_ref, s_ref, p_ref, oslc_ref,
                  strip_w_ref, vwaug_ref, owin_ref):
    p = pl.program_id(0)
    b = pl.program_id(1)
    S = q_ref.shape[1]
    nblk = S // TQ

    @pl.when(b == 0)
    def _():
        _fill_strips(strip_s_ref, rs_ref)
        _fill_strips(strip_w_ref, rw_ref)

    @pl.when(jnp.logical_and(p == 0, b == 0))
    def _():
        for h in range(2):
            kaug_ref[h, :, LANES:] = e_ref[h]
        _fill_value_ones(vsaug_ref)
        _fill_value_ones(vwaug_ref)

    ks = ks_ref[0]
    pen = pen_ref[0, 0]
    for h, qh in enumerate(_split_heads(q_ref[0])):
        kaug_ref[h, :, :LANES] = ks
        qaug_ref[h, :, :LANES] = qh
        qaug_ref[h, :, LANES:] = pen
    vsaug_ref[:, :LANES] = vs_ref[0]
    vwaug_ref[:, :LANES] = vw_ref[0]
    _pair_attention(lambda h, r0, r1: qaug_ref[h, r0:r1, :],
                    lambda h, j: kaug_ref[h, j * TQ:(j + 1) * TQ, :],
                    lambda r0, r1: vsaug_ref[r0:r1, :],
                    strip_s_ref, s_ref, p_ref, oslc_ref, nblk, nblk)
    _pair_attention(lambda h, r0, r1: qaug_ref[h, r0:r1, :LANES],
                    lambda h, j: kw_ref[0, j * TQ:(j + 1) * TQ, :],
                    lambda r0, r1: vwaug_ref[r0:r1, :],
                    strip_w_ref, s_ref, p_ref, owin_ref, nblk, strip_w_ref.shape[1] // TQ, skew=2)
    sig = _sigmoid(pltpu.roll(graw_ref[0], lax.rem(LANES - GATE_STRIDE * p, LANES), 1))
    lo_lanes = _lane_lo((S, LANES))
    gates = [jnp.where(lo_lanes, sig[:, 2 * br:2 * br + 1], sig[:, 2 * br + 1:2 * br + 2]) for br in range(3)]
    out = gates[0] * ocmp_ref[0] + gates[1] * _merge_heads(oslc_ref) + gates[2] * _merge_heads(owin_ref)
    o_ref[0] = out.astype(o_ref.dtype)


def _attn1(qkv, r_slc, r_win, e_onehot, pen, ocmp, graw):
    B, S, _ = qkv.shape
    kv0 = D_MODEL // LANES + 4
    n_win = r_win.shape[2]
    return pl.pallas_call(
        _attn1_kernel,
        grid=(N_PAIRS, B),
        in_specs=[
            pl.BlockSpec((1, S, LANES), lambda p, b: (b, 0, p)),
            pl.BlockSpec((1, S, LANES), lambda p, b: (b, 0, kv0 + p // 4)),
            pl.BlockSpec((1, S, LANES), lambda p, b: (b, 0, kv0 + 2 + p // 4)),
            pl.BlockSpec((1, S, LANES), lambda p, b: (b, 0, kv0 + 4 + p // 4)),
            pl.BlockSpec((1, S, LANES), lambda p, b: (b, 0, kv0 + 6 + p // 4)),
            pl.BlockSpec((1, 2, r_slc.shape[2], 2 * TQ), lambda p, b: (p, 0, 0, 0)),
            pl.BlockSpec((1, 2, n_win, 2 * TQ), lambda p, b: (p, 0, 0, 0)),
            pl.BlockSpec((2, S, LANES), lambda p, b: (0, 0, 0), pipeline_mode=pl.Buffered(1)),
            pl.BlockSpec((1, 1, S, LANES), lambda p, b: (b, p // 4, 0, 0)),
            pl.BlockSpec((1, S, LANES), lambda p, b: (b, 0, p)),
            pl.BlockSpec((1, S, LANES), lambda p, b: (b, 0, 0)),
        ],
        out_specs=pl.BlockSpec((1, S, LANES), lambda p, b: (b, 0, p)),
        out_shape=jax.ShapeDtypeStruct((B, S, D_MODEL), BF16),
        scratch_shapes=[_strip_scratch(S)] + _attn_set(S, S // TQ, n_win) + [_strip_scratch(n_win * TQ),
                                                    pltpu.VMEM((S, 2 * LANES), BF16), pltpu.VMEM((2, S, LANES), F32)],
        compiler_params=pltpu.CompilerParams(dimension_semantics=("arbitrary",) * 2, vmem_limit_bytes=ATTN_VMEM_LIMIT),
        name="attn_nsa",
    )(qkv, qkv, qkv, qkv, qkv, r_slc, r_win, e_onehot, pen, ocmp, graw)


def _bucket(dist):
    n = np.maximum(dist, 0)
    exact = REL_BUCKETS // 2
    nf = np.maximum(n, 1).astype(np.float64)
    large = exact + (np.log(nf / exact) / math.log(REL_MAX_DIST / exact) * (REL_BUCKETS - exact)).astype(np.int64)
    return np.where(n < exact, n, np.minimum(large, REL_BUCKETS - 1))


def _strip_rows(tab, n_off, extra_of_dist):
    u = np.arange(2 * TQ)
    d = np.arange(n_off)[:, None] * TQ - np.where(u < TQ, u, u - 2 * TQ)[None, :]
    onehot = (_bucket(d)[None] == np.arange(REL_BUCKETS)[:, None, None]).astype(np.float32)
    vals = jnp.einsum("hb,bou->hou", tab, jnp.asarray(onehot), precision=lax.Precision.HIGHEST) + extra_of_dist(d)
    return vals.reshape(tab.shape[0] // 2, 2, n_off, 2 * TQ)


def _causal_mask(d):
    return np.where(d >= 0, 0.0, NEG).astype(np.float32)


def _window_mask(d):
    return np.where((d >= 0) & (d < NSA_WINDOW), 0.0, NEG).astype(np.float32)


def _dilation_log_count(d):
    c = ((d >= 0) & (d <= 128)).astype(np.float64)
    c += ((d >= 0) & (d % 4 == 0) & (d <= 512))
    c += ((d >= 0) & (d % 16 == 0) & (d <= 2048))
    return np.where(c > 0, np.log(np.maximum(c, 1.0)), NEG).astype(np.float32)


def _one_hot_blocks(S, block, per_head):
    e = np.zeros((2, S, LANES), np.float32)
    key = np.arange(S)
    for h in range(2):
        e[h, key, h * per_head + key // block] = 1.0
    return jnp.asarray(e, BF16)


_NSA_HEAD_ORDER = np.array([8 * (p // 4) + (p % 4) + 4 * h for p in range(N_PAIRS) for h in range(2)])


def _head_cols(heads):
    return (np.asarray(heads)[:, None] * HEAD_DIM + np.arange(HEAD_DIM)[None, :]).reshape(-1)


def _mixer_ab(h, g_mix, w_in, qn_a, kn_a, qn_b, kn_b, rel_bias, B, S):
    wa = MOBA_HEADS * HEAD_DIM
    ones = jnp.ones((wa,), F32)
    nh = MOBA_HEADS
    gain = jnp.concatenate([jnp.tile(qn_a, nh) * ATTN_SCALE, jnp.tile(qn_b, nh) * ATTN_SCALE,
                            jnp.tile(kn_a, nh), jnp.tile(kn_b, nh), ones, ones])
    per_sec = wa // PROJ_CHUNK
    src = [sec * per_sec + c for sec in (0, 3, 1, 4, 2, 5) for c in range(per_sec)]
    (qkv,) = _project(h, g_mix, w_in.astype(BF16), gain, [True] * (4 * per_sec) + [False] * (2 * per_sec),
                      src_chunks=src)
    qkv = qkv.reshape(B, S, 3 * D_MODEL)

    tab = rel_bias.T
    r_tab = jnp.concatenate([_strip_rows(tab[:MOBA_HEADS], S // TQ, _causal_mask),
                             _strip_rows(tab[MOBA_HEADS:], S // TQ, _dilation_log_count)])
    o = _attn0(qkv, r_tab, _one_hot_blocks(S, MOBA_BLOCK, S // MOBA_BLOCK))
    return o.reshape(B * S, D_MODEL)


def _mixer_nsa(h, g_mix, w_in, qn, kn_c, kn_s, kn_w, cmp_k, cmp_v, rel_bias, B, S):
    qw = N_HEADS * HEAD_DIM
    kvw = NSA_GROUPS * HEAD_DIM
    order = _NSA_HEAD_ORDER
    w_main = jnp.concatenate([w_in[:, hd * HEAD_DIM:(hd + 1) * HEAD_DIM] for hd in order]
                             + [w_in[:, qw:qw + 6 * kvw]], axis=1).astype(BF16)
    ones = jnp.ones((kvw,), F32)
    gain = jnp.concatenate([jnp.tile(qn, N_HEADS) * ATTN_SCALE, ones, ones,
                            jnp.tile(kn_s, NSA_GROUPS), ones, jnp.tile(kn_w, NSA_GROUPS), ones])
    assert kvw == PROJ_CHUNK
    norm_chunks = [True] * (qw // PROJ_CHUNK) + [False, False, True, False, True, False]

    gcols = np.zeros((LANES,), np.int64)
    gused = np.zeros((LANES,), np.float32)
    for p in range(N_PAIRS):
        for br in range(3):
            for hh in range(2):
                c = GATE_STRIDE * p + 2 * br + hh
                gcols[c] = qw + 6 * kvw + 3 * order[2 * p + hh] + br
                gused[c] = 1.0
    w_gate = (w_in[:, gcols] * gused).astype(BF16)
    kc_chunk = qw // PROJ_CHUNK
    qkv, graw, raw = _project(h, g_mix, w_main, gain, norm_chunks, w_gate, raw_chunks=(kc_chunk, kc_chunk + 1))
    qkv = qkv.reshape(B, S, qw + 6 * kvw)
    graw = graw.reshape(B, S, LANES)
    raw = raw.reshape(B, S, 2 * kvw)

    tab = rel_bias.T[order]
    n_cmp = (S - NSA_CMP_LEN) // NSA_CMP_STRIDE + 1
    t_pos = np.arange(S)[:, None]
    c_idx = np.arange(LANES)[None, :]
    dc = t_pos - (c_idx * NSA_CMP_STRIDE + NSA_CMP_LEN - 1)
    cmp_mask = np.where((dc >= 0) & (c_idx < n_cmp), 0.0, NEG).astype(np.float32)
    onehot = jnp.asarray(_bucket(dc).astype(np.int8)).reshape(1, -1) == jnp.arange(REL_BUCKETS, dtype=jnp.int8)[:, None]
    bias_c = jnp.dot(tab, onehot.astype(F32), precision=lax.Precision.HIGHEST) + cmp_mask.reshape(1, -1)
    bias_c = bias_c.reshape(2, 8, S, LANES)
    n_sel = S // NSA_SEL_BLOCK
    cstart = np.arange(LANES) * NSA_CMP_STRIDE
    sstart = np.arange(n_sel) * NSA_SEL_BLOCK
    ovt = np.maximum(np.minimum(cstart[None, :] + NSA_CMP_LEN, sstart[:, None] + NSA_SEL_BLOCK)
                     - np.maximum(cstart[None, :], sstart[:, None]), 0).astype(np.float32)
    ovt[:, n_cmp:] = 0.0
    ocmp, pen = _cmp_attention(qkv, raw, (*cmp_k, kn_c), (*cmp_v, kn_c), bias_c, jnp.asarray(ovt))

    r_slc = _strip_rows(tab, S // TQ, _causal_mask)
    r_win = _strip_rows(tab, NSA_WINDOW // TQ + 1, _window_mask)
    o = _attn1(qkv, r_slc, r_win, _one_hot_blocks(S, NSA_SEL_BLOCK, n_sel), pen, ocmp, graw)
    return o.reshape(B * S, D_MODEL)


def kernel(x, p, rel_bias, norm_mix, norm_ffn, norm_ple, w_ffn_gate, w_ffn_up, w_ffn_down, w_ple_proj, w_ple_gate, w_in_ab, w_out_ab, qn_moba, kn_moba, qn_dil, kn_dil, w_in_nsa, w_out_nsa, qn_nsa, kn_cmp, kn_slc, kn_win, cmp_k_pos, cmp_k_w1, cmp_k_b1, cmp_k_w2, cmp_k_b2, cmp_v_pos, cmp_v_w1, cmp_v_b1, cmp_v_w2, cmp_v_b2):
    B, S, D = x.shape
    depth = p.shape[0]
    assert D == D_MODEL and S == MOBA_NBLK * MOBA_BLOCK and (B * S) % ROW_TILE == 0 and B % ATTN0_ROWS == 0, (B, S, D)
    h = x.reshape(B * S, D)
    wg, wu, wd, wpg, wpp = (w.astype(BF16) for w in (w_ffn_gate, w_ffn_up, w_ffn_down, w_ple_gate, w_ple_proj))
    for i in range(depth):
        e = i // 2
        if i % 2 == 0:
            o = _mixer_ab(h, norm_mix[i], w_in_ab[e], qn_moba[e], kn_moba[e], qn_dil[e], kn_dil[e], rel_bias, B, S)
            w_out = w_out_ab[e]
        else:
            cmp_k = (cmp_k_pos[e], cmp_k_w1[e], cmp_k_b1[e], cmp_k_w2[e], cmp_k_b2[e])
            cmp_v = (cmp_v_pos[e], cmp_v_w1[e], cmp_v_b1[e], cmp_v_w2[e], cmp_v_b2[e])
            o = _mixer_nsa(h, norm_mix[i], w_in_nsa[e], qn_nsa[e], kn_cmp[e], kn_slc[e], kn_win[e],
                           cmp_k, cmp_v, rel_bias, B, S)
            w_out = w_out_nsa[e][_head_cols(_NSA_HEAD_ORDER), :]
        h = _post_attention(o, h, w_out.astype(BF16), norm_ffn[i], wg, wu, wd, norm_ple[i], wpg,
                            p.reshape(depth, B * S, -1), wpp, i)
    return h.reshape(B, S, D)
```
